```python
import jax, jax.numpy as jnp
from jax import lax
import numpy as np

D_MODEL = 1024
BATCH = 8
SEQ = 4096
DEPTH = 1

D_MIX = D_MODEL
SB_HEADS = 8
HEAD_DIM = 64
SB_WIDTH = SB_HEADS * HEAD_DIM
LRU_WIDTH = D_MIX - SB_WIDTH
LRU_BLOCKS = 8
LRU_BLOCK_DIM = LRU_WIDTH // LRU_BLOCKS
CONV_WIDTH = 4
LRU_C = 8.0
D_IN = 3 * SB_WIDTH + 2 * LRU_WIDTH
Q_BLOCK = 128
N_EXPERTS = 32
TOP_K = 4
D_EXPERT = D_MODEL
SWIGLU_LIMIT = 7.0
SWIGLU_ALPHA = 1.702
EXPERT_BLOCK = 256
EPS = 1e-6
N_MOD = 6

kernel_name = "hymba_stickbreak_rglru_moe_adaln"


def rmsnorm(x, g):
    x32 = x.astype(jnp.float32)
    y = x32 * lax.rsqrt(jnp.mean(x32 * x32, axis=-1, keepdims=True) + EPS)
    return (y * g.astype(jnp.float32)).astype(x.dtype)


def modulate(h, shift, scale):
    return h * (1.0 + scale[:, None, :]) + shift[:, None, :]


def stick_breaking_attention(q, k, v):
    B, S, H, Dh = q.shape
    n_blocks = S // Q_BLOCK
    qh = jnp.transpose(q, (0, 2, 1, 3)).astype(jnp.float32) * (Dh ** -0.5)
    kh = jnp.transpose(k, (0, 2, 1, 3)).astype(jnp.float32)
    vh = jnp.transpose(v, (0, 2, 1, 3)).astype(jnp.float32)
    k_pos = jnp.arange(S)

    def one_block(i):
        q_blk = lax.dynamic_slice_in_dim(qh, i * Q_BLOCK, Q_BLOCK, axis=2)
        z = jnp.einsum('bhqd,bhkd->bhqk', q_blk, kh)
        q_pos = i * Q_BLOCK + jnp.arange(Q_BLOCK)
        causal = k_pos[None, :] < q_pos[:, None]
        log_beta = jax.nn.log_sigmoid(z)
        log_keep = jnp.where(causal, log_beta - z, 0.0)
        later = lax.cumsum(log_keep, axis=3, reverse=True) - log_keep
        weights = jnp.where(causal, jnp.exp(log_beta + later), 0.0)
        return jnp.einsum('bhqk,bhkd->bhqd', weights, vh)

    out = lax.map(one_block, jnp.arange(n_blocks))
    out = jnp.transpose(out, (1, 0, 3, 2, 4)).reshape(B, S, H * Dh)
    return out.astype(q.dtype)


def rg_lru_branch(xr, gr, conv_w, conv_b, rg_w, rg_b, ig_w, ig_b, lru_lambda):
    B, S, W = xr.shape
    x32 = xr.astype(jnp.float32)
    xp = jnp.pad(x32, ((0, 0), (CONV_WIDTH - 1, 0), (0, 0)))
    cw = conv_w.astype(jnp.float32)
    xc = conv_b.astype(jnp.float32) + sum(cw[j] * xp[:, j:j + S] for j in range(CONV_WIDTH))
    xh = xc.reshape(B, S, LRU_BLOCKS, LRU_BLOCK_DIM)
    r = jax.nn.sigmoid(jnp.einsum('bshi,hij->bshj', xh, rg_w.astype(jnp.float32)).reshape(B, S, W)
                       + rg_b.astype(jnp.float32))
    ig = jax.nn.sigmoid(jnp.einsum('bshi,hij->bshj', xh, ig_w.astype(jnp.float32)).reshape(B, S, W)
                        + ig_b.astype(jnp.float32))
    log_a = -LRU_C * r * jax.nn.softplus(-lru_lambda.astype(jnp.float32))
    a = jnp.exp(log_a)
    b = jnp.sqrt(-jnp.expm1(2.0 * log_a)) * (ig * xc)

    def combine(left, right):
        a1, b1 = left
        a2, b2 = right
        return a1 * a2, a2 * b1 + b2

    _, h = lax.associative_scan(combine, (a, b), axis=1)
    y = h * jax.nn.gelu(gr.astype(jnp.float32))
    return y.astype(xr.dtype)


def hybrid_mixer(h, w_in, conv_w, conv_b, rg_w, rg_b, ig_w, ig_b, lru_lambda,
                 attn_out_g, lru_out_g, w_out):
    B, S, _ = h.shape
    proj = jnp.einsum('bsd,de->bse', h, w_in)
    q, k, v, xr, gr = jnp.split(
        proj, [SB_WIDTH, 2 * SB_WIDTH, 3 * SB_WIDTH, 3 * SB_WIDTH + LRU_WIDTH], axis=-1)
    q = q.reshape(B, S, SB_HEADS, HEAD_DIM)
    k = k.reshape(B, S, SB_HEADS, HEAD_DIM)
    v = v.reshape(B, S, SB_HEADS, HEAD_DIM)
    y_attn = stick_breaking_attention(q, k, v)
    y_lru = rg_lru_branch(xr, gr, conv_w, conv_b, rg_w, rg_b, ig_w, ig_b, lru_lambda)
    y = jnp.concatenate([rmsnorm(y_attn, attn_out_g), rmsnorm(y_lru, lru_out_g)], axis=-1)
    return jnp.einsum('bse,ed->bsd', y, w_out)


def moe_ffn(h, router_w, router_b, w_gate, b_gate, w_up, b_up, w_down, b_down):
    B, S, D = h.shape
    N = B * S
    xf = h.reshape(N, D)
    logits = xf.astype(jnp.float32) @ router_w.astype(jnp.float32) + router_b.astype(jnp.float32)
    top_val, top_idx = lax.top_k(logits, TOP_K)
    probs = jax.nn.softmax(top_val, axis=-1)
    n_assign = N * TOP_K
    flat_e = top_idx.reshape(-1).astype(jnp.int32)
    order = jnp.argsort(flat_e)
    e_sorted = flat_e[order]
    tok_sorted = (order // TOP_K).astype(jnp.int32)
    p_sorted = probs.reshape(-1)[order]
    counts = jnp.bincount(flat_e, length=N_EXPERTS).astype(jnp.int32)
    starts = jnp.cumsum(counts) - counts
    padded = ((counts + EXPERT_BLOCK - 1) // EXPERT_BLOCK) * EXPERT_BLOCK
    pad_ends = jnp.cumsum(padded)
    pad_starts = pad_ends - padded
    dest = pad_starts[e_sorted] + (jnp.arange(n_assign, dtype=jnp.int32) - starts[e_sorted])
    n_blocks = -(-n_assign // EXPERT_BLOCK) + N_EXPERTS
    cap = n_blocks * EXPERT_BLOCK
    tok_buf = jnp.zeros((cap,), jnp.int32).at[dest].set(tok_sorted)
    p_buf = jnp.zeros((cap,), jnp.float32).at[dest].set(p_sorted)
    block_start = jnp.arange(n_blocks, dtype=jnp.int32) * EXPERT_BLOCK
    block_e = jnp.minimum(jnp.searchsorted(pad_ends, block_start, side='right'),
                          N_EXPERTS - 1).astype(jnp.int32)

    def expert_block(args):
        tok, e = args
        xb = xf[tok]
        g = jnp.minimum(xb @ w_gate[e] + b_gate[e], SWIGLU_LIMIT)
        u = jnp.clip(xb @ w_up[e] + b_up[e], -SWIGLU_LIMIT, SWIGLU_LIMIT)
        act = (u + 1.0) * (g * jax.nn.sigmoid(SWIGLU_ALPHA * g))
        return act @ w_down[e] + b_down[e]

    y_blocks = lax.map(expert_block, (tok_buf.reshape(n_blocks, EXPERT_BLOCK), block_e))
    y = y_blocks.reshape(cap, D).astype(jnp.float32) * p_buf[:, None]
    out = jax.ops.segment_sum(y, tok_buf, num_segments=N)
    return out.reshape(B, S, D).astype(h.dtype)


def setup_inputs(seed: int = 0) -> dict:
    key = jax.random.key(seed)
    ks = jax.random.split(key, 32)

    def nrm(k, shape, scale):
        return jax.random.normal(k, shape, jnp.float32) * scale

    a_pow = jax.random.uniform(ks[12], (DEPTH, LRU_WIDTH), jnp.float32, 0.9, 0.999)
    a_base = a_pow ** (1.0 / LRU_C)
    return {
        "x": nrm(ks[0], (BATCH, SEQ, D_MODEL), 1.0),
        "c": nrm(ks[1], (BATCH, D_MODEL), 1.0),
        "ada_w": nrm(ks[2], (DEPTH, D_MODEL, N_MOD * D_MODEL), D_MODEL ** -0.5),
        "ada_b": nrm(ks[3], (DEPTH, N_MOD * D_MODEL), 0.02),
        "mix_norm_g": 1.0 + nrm(ks[4], (DEPTH, D_MODEL), 0.02),
        "w_in": nrm(ks[5], (DEPTH, D_MODEL, D_IN), D_MODEL ** -0.5),
        "conv_w": nrm(ks[6], (DEPTH, CONV_WIDTH, LRU_WIDTH), CONV_WIDTH ** -0.5),
        "conv_b": nrm(ks[7], (DEPTH, LRU_WIDTH), 0.02),
        "rg_w": nrm(ks[8], (DEPTH, LRU_BLOCKS, LRU_BLOCK_DIM, LRU_BLOCK_DIM), LRU_BLOCK_DIM ** -0.5),
        "rg_b": nrm(ks[9], (DEPTH, LRU_WIDTH), 0.02),
        "ig_w": nrm(ks[10], (DEPTH, LRU_BLOCKS, LRU_BLOCK_DIM, LRU_BLOCK_DIM), LRU_BLOCK_DIM ** -0.5),
        "ig_b": nrm(ks[11], (DEPTH, LRU_WIDTH), 0.02),
        "lru_lambda": jnp.log(a_base) - jnp.log1p(-a_base),
        "attn_out_g": 1.0 + nrm(ks[13], (DEPTH, SB_WIDTH), 0.02),
        "lru_out_g": 1.0 + nrm(ks[14], (DEPTH, LRU_WIDTH), 0.02),
        "w_out": nrm(ks[15], (DEPTH, D_MIX, D_MODEL), D_MIX ** -0.5),
        "ffn_norm_g": 1.0 + nrm(ks[16], (DEPTH, D_MODEL), 0.02),
        "router_w": nrm(ks[17], (DEPTH, D_MODEL, N_EXPERTS), D_MODEL ** -0.5),
        "router_b": nrm(ks[18], (DEPTH, N_EXPERTS), 0.01),
        "exp_w_gate": nrm(ks[19], (DEPTH, N_EXPERTS, D_MODEL, D_EXPERT), D_MODEL ** -0.5),
        "exp_b_gate": nrm(ks[20], (DEPTH, N_EXPERTS, D_EXPERT), 0.02),
        "exp_w_up": nrm(ks[21], (DEPTH, N_EXPERTS, D_MODEL, D_EXPERT), D_MODEL ** -0.5),
        "exp_b_up": nrm(ks[22], (DEPTH, N_EXPERTS, D_EXPERT), 0.02),
        "exp_w_down": nrm(ks[23], (DEPTH, N_EXPERTS, D_EXPERT, D_MODEL), D_EXPERT ** -0.5),
        "exp_b_down": nrm(ks[24], (DEPTH, N_EXPERTS, D_MODEL), 0.02),
        "final_norm_g": 1.0 + nrm(ks[25], (D_MODEL,), 0.02),
    }


def reference(x, c, ada_w, ada_b, mix_norm_g, w_in, conv_w, conv_b, rg_w, rg_b, ig_w, ig_b,
              lru_lambda, attn_out_g, lru_out_g, w_out, ffn_norm_g, router_w, router_b,
              exp_w_gate, exp_b_gate, exp_w_up, exp_b_up, exp_w_down, exp_b_down, final_norm_g):
    c_act = jax.nn.silu(c)
    for l in range(DEPTH):
        mod = jnp.einsum('bd,de->be', c_act, ada_w[l]) + ada_b[l]
        sh_m, sc_m, g_m, sh_f, sc_f, g_f = jnp.split(mod, N_MOD, axis=-1)
        h = modulate(rmsnorm(x, mix_norm_g[l]), sh_m, sc_m)
        x = x + g_m[:, None, :] * hybrid_mixer(
            h, w_in[l], conv_w[l], conv_b[l], rg_w[l], rg_b[l], ig_w[l], ig_b[l],
            lru_lambda[l], attn_out_g[l], lru_out_g[l], w_out[l])
        h = modulate(rmsnorm(x, ffn_norm_g[l]), sh_f, sc_f)
        x = x + g_f[:, None, :] * moe_ffn(
            h, router_w[l], router_b[l], exp_w_gate[l], exp_b_gate[l], exp_w_up[l], exp_b_up[l],
            exp_w_down[l], exp_b_down[l])
    return rmsnorm(x, final_norm_g)
```

```python
import functools

import jax
import jax.numpy as jnp
from jax import lax
from jax.experimental import pallas as pl
from jax.experimental.pallas import tpu as pltpu

F32 = jnp.float32
BF16 = jnp.bfloat16
HIGHEST = lax.Precision.HIGHEST

EPS = 1e-6
N_MOD = 6
SB_HEADS = 8
HEAD_DIM = 64
SB_WIDTH = SB_HEADS * HEAD_DIM
LRU_BLOCKS = 8
CONV_WIDTH = 4
LRU_C = 8.0
TOP_K = 4
SWIGLU_LIMIT = 7.0
SWIGLU_ALPHA = 1.702

LANES = 128
VMEM_LIMIT = 48 * 1024 * 1024

Q_BLOCK = 128
K_BLOCK = 128
EXPERT_BLOCK = 256
TOKEN_TILE = 256


def _params(sem):
    return pltpu.CompilerParams(dimension_semantics=sem, vmem_limit_bytes=VMEM_LIMIT)


def _ada_kernel(c_ref, w_ref, b_ref, o_ref):
    c = c_ref[...]
    ca = c * jax.nn.sigmoid(c)
    o_ref[...] = jnp.dot(ca, w_ref[...], precision=HIGHEST,
                         preferred_element_type=F32) + b_ref[...]


def _ada(c, ada_w, ada_b):
    B, D = c.shape
    E = ada_w.shape[1]
    tn = 1024
    return pl.pallas_call(
        _ada_kernel,
        grid=(E // tn,),
        in_specs=[pl.BlockSpec((B, D), lambda j: (0, 0)),
                  pl.BlockSpec((D, tn), lambda j: (0, j)),
                  pl.BlockSpec((1, tn), lambda j: (0, j))],
        out_specs=pl.BlockSpec((B, tn), lambda j: (0, j)),
        out_shape=jax.ShapeDtypeStruct((B, E), F32),
        compiler_params=_params(("arbitrary",)),
        name="ada",
    )(c, ada_w, ada_b.reshape(1, E))


def _rms(x, g):
    ms = jnp.mean(x * x, axis=-1, keepdims=True)
    return x * lax.rsqrt(ms + EPS) * g


def _inproj_kernel(x_ref, mod_ref, g_ref, w_ref, q_ref, k_ref, v_ref, xr_ref, gr_ref):
    x = x_ref[0]
    h = _rms(x, g_ref[...]) * (1.0 + mod_ref[0, 1:2, :]) + mod_ref[0, 0:1, :]
    hb = h.astype(BF16)
    W = SB_WIDTH

    def proj(c):
        return jnp.dot(hb, w_ref[:, c * W:(c + 1) * W], preferred_element_type=F32)

    q_ref[0] = (proj(0) * (HEAD_DIM ** -0.5)).astype(BF16)
    k_ref[0] = proj(1).astype(BF16)
    v_ref[0] = proj(2).astype(BF16)
    xr_ref[0] = proj(3)
    gr_ref[0] = proj(4)


def _inproj(x, mod3, g, w_in_b, ts=512):
    B, S, D = x.shape
    E = w_in_b.shape[1]
    W = SB_WIDTH
    row = pl.BlockSpec((1, ts, W), lambda b, s: (b, s, 0))
    return pl.pallas_call(
        _inproj_kernel,
        grid=(B, S // ts),
        in_specs=[pl.BlockSpec((1, ts, D), lambda b, s: (b, s, 0)),
                  pl.BlockSpec((1, N_MOD, D), lambda b, s: (b, 0, 0)),
                  pl.BlockSpec((1, D), lambda b, s: (0, 0)),
                  pl.BlockSpec((D, E), lambda b, s: (0, 0))],
        out_specs=[row, row, row, row, row],
        out_shape=[jax.ShapeDtypeStruct((B, S, W), BF16)] * 3
        + [jax.ShapeDtypeStruct((B, S, W), F32)] * 2,
        compiler_params=_params(("arbitrary", "arbitrary")),
        name="inproj",
    )(x, mod3, g.reshape(1, D), w_in_b)


def _attn_kernel(q_ref, k_ref, v_ref, o_ref):
    i = pl.program_id(2)
    QB, KB = Q_BLOCK, K_BLOCK
    q = q_ref[0]
    lane = lax.broadcasted_iota(jnp.int32, (QB, LANES), 1)
    zero = jnp.zeros_like(q)
    qh = (jnp.where(lane < HEAD_DIM, q, zero), jnp.where(lane >= HEAD_DIM, q, zero))

    uj = lax.broadcasted_iota(jnp.int32, (KB, KB + LANES), 0)
    us = lax.broadcasted_iota(jnp.int32, (KB, KB + LANES), 1)
    u = jnp.where((us >= KB) | (uj > us), 1.0, 0.0).astype(BF16)

    row = lax.broadcasted_iota(jnp.int32, (QB, KB), 0)
    col = lax.broadcasted_iota(jnp.int32, (QB, KB), 1)
    causal = col < row

    def block(j, state, diagonal):
        kblk = k_ref[0, pl.ds(pl.multiple_of(j * KB, KB), KB), :]
        vblk = v_ref[0, pl.ds(pl.multiple_of(j * KB, KB), KB), :]
        new = []
        for h in range(2):
            acc, carry = state[2 * h], state[2 * h + 1]
            z = lax.dot_general(qh[h], kblk, (((1,), (1,)), ((), ())),
                                preferred_element_type=F32)
            soft = jnp.log1p(jnp.exp(-jnp.abs(z)))
            log_beta = jnp.minimum(z, 0.0) - soft
            log_keep = -jnp.maximum(z, 0.0) - soft
            if diagonal:
                log_keep = jnp.where(causal, log_keep, 0.0)
            hi = log_keep.astype(BF16)
            lo = (log_keep - hi.astype(F32)).astype(BF16)
            sums = (jnp.dot(hi, u, preferred_element_type=F32)
                    + jnp.dot(lo, u, preferred_element_type=F32))
            w = jnp.exp(log_beta + sums[:, :KB] + carry)
            if diagonal:
                w = jnp.where(causal, w, 0.0)
            acc = acc + jnp.dot(w.astype(BF16), vblk, preferred_element_type=F32)
            carry = carry + sums[:, KB:]
            new += [acc, carry]
        return tuple(new)

    zeros = jnp.zeros((QB, LANES), F32)
    state = block(i, (zeros, zeros, zeros, zeros), True)
    state = lax.fori_loop(1, i + 1, lambda jj, st: block(i - jj, st, False), state)
    o_ref[0] = jnp.where(lane < HEAD_DIM, state[0], state[2])


def _attention(q, k, v):
    B, S, W = q.shape
    return pl.pallas_call(
        _attn_kernel,
        grid=(B, W // LANES, S // Q_BLOCK),
        in_specs=[pl.BlockSpec((1, Q_BLOCK, LANES), lambda b, p, i: (b, i, p)),
                  pl.BlockSpec((1, S, LANES), lambda b, p, i: (b, 0, p)),
                  pl.BlockSpec((1, S, LANES), lambda b, p, i: (b, 0, p))],
        out_specs=pl.BlockSpec((1, Q_BLOCK, LANES), lambda b, p, i: (b, i, p)),
        out_shape=jax.ShapeDtypeStruct((B, S, W), F32),
        compiler_params=_params(("arbitrary", "arbitrary", "arbitrary")),
        name="attn",
    )(q, k, v)


def _softplus(x):
    return jnp.maximum(x, 0.0) + jnp.log1p(jnp.exp(-jnp.abs(x)))


def _gelu_tanh(x):
    return 0.5 * x * (1.0 + jnp.tanh(0.7978845608028654 * (x + 0.044715 * x * x * x)))


def _lru_kernel(xr_ref, gr_ref, cw_ref, cb_ref, wr_ref, br_ref, wi_ref, bi_ref, lam_ref,
                g_ref, o_ref, xext, hc):
    TS = xr_ref.shape[1]
    PAD = 8

    @pl.when(pl.program_id(1) == 0)
    def _():
        xext[0:PAD, :] = jnp.zeros((PAD, xext.shape[1]), F32)
        hc[...] = jnp.zeros_like(hc)

    x = xr_ref[0]
    xext[PAD:PAD + TS, :] = x
    xc = cb_ref[...] + cw_ref[CONV_WIDTH - 1:CONV_WIDTH, :] * x
    for j in range(CONV_WIDTH - 1):
        back = CONV_WIDTH - 1 - j
        xc = xc + cw_ref[j:j + 1, :] * xext[PAD - back:PAD - back + TS, :]
    xext[0:PAD, :] = xext[TS:TS + PAD, :]

    xb = xc.astype(BF16)
    r = jax.nn.sigmoid(jnp.dot(xb, wr_ref[...], preferred_element_type=F32) + br_ref[...])
    ig = jax.nn.sigmoid(jnp.dot(xb, wi_ref[...], preferred_element_type=F32) + bi_ref[...])
    log_a = (-LRU_C) * r * _softplus(-lam_ref[...])
    a = jnp.exp(log_a)
    b = jnp.sqrt(-jnp.tanh(log_a) * (a * a + 1.0)) * (ig * xc)

    rows = lax.broadcasted_iota(jnp.int32, a.shape, 0)
    d = 1
    while d < TS:
        keep = rows >= d
        a_prev = jnp.where(keep, pltpu.roll(a, d, 0), 1.0)
        b_prev = jnp.where(keep, pltpu.roll(b, d, 0), 0.0)
        b = a * b_prev + b
        a = a * a_prev
        d *= 2
    h = a * hc[...] + b
    hc[...] = h[TS - 1:TS, :]

    y = h * _gelu_tanh(gr_ref[0])
    o_ref[0] = _rms(y, g_ref[...])


def _block_diag(w):
    H, I, J = w.shape
    eye = jnp.eye(H, dtype=w.dtype)
    return (w[:, :, None, :] * eye[:, None, :, None]).reshape(H * I, H * J)


def _lru(xr, gr, conv_w, conv_b, rg_w, rg_b, ig_w, ig_b, lam, g, ts=256):
    B, S, W = xr.shape
    row = pl.BlockSpec((1, ts, W), lambda b, s: (b, s, 0))
    vec = pl.BlockSpec((1, W), lambda b, s: (0, 0))
    mat = pl.BlockSpec((W, W), lambda b, s: (0, 0))
    return pl.pallas_call(
        _lru_kernel,
        grid=(B, S // ts),
        in_specs=[row, row, pl.BlockSpec((CONV_WIDTH, W), lambda b, s: (0, 0)), vec,
                  mat, vec, mat, vec, vec, vec],
        out_specs=row,
        out_shape=jax.ShapeDtypeStruct((B, S, W), F32),
        scratch_shapes=[pltpu.VMEM((ts + 8, W), F32), pltpu.VMEM((1, W), F32)],
        compiler_params=_params(("arbitrary", "arbitrary")),
        name="lru",
    )(xr, gr, conv_w, conv_b.reshape(1, W), _block_diag(rg_w).astype(BF16), rg_b.reshape(1, W),
      _block_diag(ig_w).astype(BF16), ig_b.reshape(1, W), lam.reshape(1, W), g.reshape(1, W))


def _outproj_kernel(x_ref, ya_ref, yl_ref, mod_ref, ga_ref, wo_ref, gf_ref, rw_ref, rb_ref,
                    x1_ref, h2_ref, idx_ref, prob_ref):
    W = SB_WIDTH
    ya = _rms(ya_ref[0], ga_ref[...]).astype(BF16)
    yl = yl_ref[0].astype(BF16)
    mix = (jnp.dot(ya, wo_ref[0:W, :], preferred_element_type=F32)
           + jnp.dot(yl, wo_ref[W:2 * W, :], preferred_element_type=F32))
    x1 = x_ref[0] + mod_ref[0, 2:3, :] * mix
    x1_ref[0] = x1
    h2 = _rms(x1, gf_ref[...]) * (1.0 + mod_ref[0, 4:5, :]) + mod_ref[0, 3:4, :]
    h2_ref[0] = h2

    logits = lax.dot_general(rw_ref[...], h2, (((1,), (1,)), ((), ())), precision=HIGHEST,
                             preferred_element_type=F32) + rb_ref[...]
    n_exp = logits.shape[0]
    eid = lax.broadcasted_iota(jnp.int32, logits.shape, 0)
    vals, idxs = [], []
    for _ in range(TOP_K):
        m = jnp.max(logits, axis=0, keepdims=True)
        sel = jnp.min(jnp.where(logits == m, eid, n_exp), axis=0, keepdims=True)
        vals.append(m)
        idxs.append(sel)
        logits = jnp.where(eid == sel, -jnp.inf, logits)
    es = [jnp.exp(vv - vals[0]) for vv in vals]
    inv = 1.0 / (es[0] + es[1] + es[2] + es[3])
    for r in range(TOP_K):
        idx_ref[0, r:r + 1, :] = idxs[r]
        prob_ref[0, r:r + 1, :] = es[r] * inv


def _outproj(x, ya, yl, mod3, ga, w_out_b, gf, router_wt, router_b, ts=512):
    B, S, D = x.shape
    W = ya.shape[2]
    NE = router_wt.shape[0]
    rowd = pl.BlockSpec((1, ts, D), lambda b, s: (b, s, 0))
    roww = pl.BlockSpec((1, ts, W), lambda b, s: (b, s, 0))
    sel = pl.BlockSpec((1, TOP_K, ts), lambda b, s: (b, 0, s))
    return pl.pallas_call(
        _outproj_kernel,
        grid=(B, S // ts),
        in_specs=[rowd, roww, roww,
                  pl.BlockSpec((1, N_MOD, D), lambda b, s: (b, 0, 0)),
                  pl.BlockSpec((1, W), lambda b, s: (0, 0)),
                  pl.BlockSpec((2 * W, D), lambda b, s: (0, 0)),
                  pl.BlockSpec((1, D), lambda b, s: (0, 0)),
                  pl.BlockSpec((NE, D), lambda b, s: (0, 0)),
                  pl.BlockSpec((NE, 1), lambda b, s: (0, 0))],
        out_specs=[rowd, rowd, sel, sel],
        out_shape=[jax.ShapeDtypeStruct((B, S, D), F32), jax.ShapeDtypeStruct((B, S, D), F32),
                   jax.ShapeDtypeStruct((B, TOP_K, S), jnp.int32),
                   jax.ShapeDtypeStruct((B, TOP_K, S), F32)],
        compiler_params=_params(("arbitrary", "arbitrary")),
        name="outproj",
    )(x, ya, yl, mod3, ga.reshape(1, W), w_out_b, gf.reshape(1, D), router_wt,
      router_b.reshape(NE, 1))


def _dispatch_kernel(pos_ref, fill_ref, h_ref, xs_ref, zbuf, sem, zsem):
    TT = h_ref.shape[0]
    TM = zbuf.shape[0]
    n_fill = fill_ref.shape[0]

    @pl.when(pl.program_id(0) == 0)
    def _():
        zbuf[...] = jnp.zeros_like(zbuf)

        def fill_copy(e):
            start = pl.multiple_of(jnp.maximum(fill_ref[e], 0), 8)
            return pltpu.make_async_copy(zbuf, xs_ref.at[pl.ds(start, TM), :], zsem)

        def start(e, c):
            @pl.when(fill_ref[e] >= 0)
            def _():
                fill_copy(e).start()
            return c

        def wait(e, c):
            @pl.when(fill_ref[e] >= 0)
            def _():
                fill_copy(e).wait()
            return c

        lax.fori_loop(0, n_fill, start, 0)
        lax.fori_loop(0, n_fill, wait, 0)

    def row_copy(t, r):
        return pltpu.make_async_copy(h_ref.at[pl.ds(t, 1), :],
                                     xs_ref.at[pl.ds(pos_ref[t * TOP_K + r], 1), :], sem)

    def issue(t, c):
        for r in range(TOP_K):
            row_copy(t, r).start()
        return c

    lax.fori_loop(0, TT, issue, 0)
    for r in range(TOP_K):
        pltpu.make_async_copy(h_ref, xs_ref.at[pl.ds(0, TT), :], sem).wait()


def _dispatch(h2f, pos_flat, fill_start, cap):
    N, D = h2f.shape
    TT = TOKEN_TILE
    return pl.pallas_call(
        _dispatch_kernel,
        grid=(N // TT,),
        in_specs=[pl.BlockSpec((TT * TOP_K,), lambda i: (i,), memory_space=pltpu.SMEM),
                  pl.BlockSpec(memory_space=pltpu.SMEM),
                  pl.BlockSpec((TT, D), lambda i: (i, 0))],
        out_specs=pl.BlockSpec(memory_space=pl.ANY),
        out_shape=jax.ShapeDtypeStruct((cap, D), F32),
        scratch_shapes=[pltpu.VMEM((EXPERT_BLOCK, D), F32), pltpu.SemaphoreType.DMA,
                        pltpu.SemaphoreType.DMA],
        compiler_params=_params(("arbitrary",)),
        name="dispatch",
    )(pos_flat, fill_start, h2f)


def _expert_kernel(be_ref, nu_ref, xs_ref, wg_ref, bg_ref, wu_ref, bu_ref, wd_ref, bd_ref,
                   ys_ref):
    @pl.when(pl.program_id(0) < nu_ref[0])
    def _():
        xb = xs_ref[...].astype(BF16)
        g = jnp.minimum(jnp.dot(xb, wg_ref[0], preferred_element_type=F32) + bg_ref[0],
                        SWIGLU_LIMIT)
        u = jnp.clip(jnp.dot(xb, wu_ref[0], preferred_element_type=F32) + bu_ref[0],
                     -SWIGLU_LIMIT, SWIGLU_LIMIT)
        act = (u + 1.0) * (g * jax.nn.sigmoid(SWIGLU_ALPHA * g))
        ys_ref[...] = jnp.dot(act.astype(BF16), wd_ref[0],
                              preferred_element_type=F32) + bd_ref[0]

    @pl.when(pl.program_id(0) >= nu_ref[0])
    def _():
        ys_ref[...] = jnp.zeros_like(ys_ref)


def _experts(xs, block_e, n_used, wg, bg, wu, bu, wd, bd):
    cap, D = xs.shape
    NE, _, DE = wg.shape
    TM = EXPERT_BLOCK
    n_blocks = cap // TM

    def rows(i, be, nu):
        return (jnp.minimum(i, nu[0] - 1), 0)

    def wsel(i, be, nu):
        return (be[i], 0, 0)

    grid_spec = pltpu.PrefetchScalarGridSpec(
        num_scalar_prefetch=2,
        grid=(n_blocks,),
        in_specs=[pl.BlockSpec((TM, D), rows),
                  pl.BlockSpec((1, D, DE), wsel), pl.BlockSpec((1, 1, DE), wsel),
                  pl.BlockSpec((1, D, DE), wsel), pl.BlockSpec((1, 1, DE), wsel),
                  pl.BlockSpec((1, DE, D), wsel), pl.BlockSpec((1, 1, D), wsel)],
        out_specs=pl.BlockSpec((TM, D), lambda i, be, nu: (i, 0)),
    )
    return pl.pallas_call(
        _expert_kernel,
        grid_spec=grid_spec,
        out_shape=jax.ShapeDtypeStruct((cap, D), F32),
        compiler_params=_params(("arbitrary",)),
        name="experts",
    )(block_e, n_used, xs, wg, bg.reshape(NE, 1, DE), wu, bu.reshape(NE, 1, DE), wd,
      bd.reshape(NE, 1, D))


def _combine_kernel(pos_ref, x1_ref, p_ref, mod_ref, g_ref, ys_ref, o_ref, buf, sem):
    TT = x1_ref.shape[0]

    def row_copy(t, r):
        return pltpu.make_async_copy(ys_ref.at[pl.ds(pos_ref[t * TOP_K + r], 1), :],
                                     buf.at[r, pl.ds(t, 1), :], sem)

    def issue(t, c):
        for r in range(TOP_K):
            row_copy(t, r).start()
        return c

    lax.fori_loop(0, TT, issue, 0)
    for r in range(TOP_K):
        pltpu.make_async_copy(ys_ref.at[pl.ds(0, TT), :], buf.at[r], sem).wait()

    p = p_ref[...]
    moe = p[:, 0:1] * buf[0]
    for r in range(1, TOP_K):
        moe = moe + p[:, r:r + 1] * buf[r]
    x2 = x1_ref[...] + mod_ref[0, 5:6, :] * moe
    o_ref[...] = _rms(x2, g_ref[...])


def _combine(x1f, probs, pos_flat, mod3, g, ys, tiles_per_batch):
    N, D = x1f.shape
    TT = TOKEN_TILE
    return pl.pallas_call(
        _combine_kernel,
        grid=(N // TT,),
        in_specs=[pl.BlockSpec((TT * TOP_K,), lambda i: (i,), memory_space=pltpu.SMEM),
                  pl.BlockSpec((TT, D), lambda i: (i, 0)),
                  pl.BlockSpec((TT, TOP_K), lambda i: (i, 0)),
                  pl.BlockSpec((1, N_MOD, D), lambda i: (i // tiles_per_batch, 0, 0)),
                  pl.BlockSpec((1, D), lambda i: (0, 0)),
                  pl.BlockSpec(memory_space=pl.ANY)],
        out_specs=pl.BlockSpec((TT, D), lambda i: (i, 0)),
        out_shape=jax.ShapeDtypeStruct((N, D), F32),
        scratch_shapes=[pltpu.VMEM((TOP_K, TT, D), F32), pltpu.SemaphoreType.DMA],
        compiler_params=_params(("arbitrary",)),
        name="combine",
    )(pos_flat, x1f, probs, mod3, g.reshape(1, D), ys)


def _routing(idx, n_exp):
    B, K, S = idx.shape
    TM = EXPERT_BLOCK
    n_assign = B * S * K
    e = jnp.transpose(idx, (0, 2, 1)).reshape(n_assign)
    onehot = (e[:, None] == jnp.arange(n_exp, dtype=jnp.int32)[None, :]).astype(jnp.int32)
    csum = jnp.cumsum(onehot, axis=0)
    rank = jnp.take_along_axis(csum, e[:, None], axis=1)[:, 0] - 1
    counts = csum[-1]
    padded = ((counts + TM - 1) // TM) * TM
    pad_ends = jnp.cumsum(padded)
    pad_starts = pad_ends - padded
    pos = (pad_starts[e] + rank).astype(jnp.int32)
    n_blocks = n_assign // TM + n_exp
    block_start = jnp.arange(n_blocks, dtype=jnp.int32) * TM
    block_e = jnp.minimum(jnp.searchsorted(pad_ends, block_start, side="right"),
                          n_exp - 1).astype(jnp.int32)
    n_used = (pad_ends[-1] // TM).astype(jnp.int32).reshape(1)
    tail_start = jnp.where(padded > 0, pad_ends - TM, -1)
    spare = pad_ends[-1] + jnp.arange(n_exp, dtype=jnp.int32) * TM
    fill_start = jnp.concatenate(
        [tail_start, jnp.where(spare < n_blocks * TM, spare, -1)]).astype(jnp.int32)
    return pos, block_e, n_used, fill_start, n_blocks * TM


def kernel(x, c, ada_w, ada_b, mix_norm_g, w_in, conv_w, conv_b, rg_w, rg_b, ig_w, ig_b,
           lru_lambda, attn_out_g, lru_out_g, w_out, ffn_norm_g, router_w, router_b,
           exp_w_gate, exp_b_gate, exp_w_up, exp_b_up, exp_w_down, exp_b_down, final_norm_g):
    B, S, D = x.shape
    depth = ada_w.shape[0]
    n_exp = router_w.shape[2]
    assert S % 512 == 0 and S % TOKEN_TILE == 0 and D % LANES == 0
    for l in range(depth):
        mod3 = _ada(c, ada_w[l], ada_b[l]).reshape(B, N_MOD, D)
        q, k, v, xr, gr = _inproj(x, mod3, mix_norm_g[l], w_in[l].astype(BF16))
        ya = _attention(q, k, v)
        yl = _lru(xr, gr, conv_w[l], conv_b[l], rg_w[l], rg_b[l], ig_w[l], ig_b[l],
                  lru_lambda[l], lru_out_g[l])
        x1, h2, idx, prob = _outproj(x, ya, yl, mod3, attn_out_g[l], w_out[l].astype(BF16),
                                     ffn_norm_g[l], router_w[l].T, router_b[l])
        pos, block_e, n_used, fill_start, cap = _routing(idx, n_exp)
        xs = _dispatch(h2.reshape(B * S, D), pos, fill_start, cap)
        ys = _experts(xs, block_e, n_used,
                      exp_w_gate[l].astype(BF16), exp_b_gate[l],
                      exp_w_up[l].astype(BF16), exp_b_up[l],
                      exp_w_down[l].astype(BF16), exp_b_down[l])
        probs = jnp.transpose(prob, (0, 2, 1)).reshape(B * S, TOP_K)
        assert depth == 1
        x = _combine(x1.reshape(B * S, D), probs, pos, mod3, final_norm_g, ys,
                     S // TOKEN_TILE).reshape(B, S, D)
    return x
```

```python
import functools

import jax
import jax.numpy as jnp
from jax import lax
from jax.experimental import pallas as pl
from jax.experimental.pallas import tpu as pltpu

F32 = jnp.float32
BF16 = jnp.bfloat16
HIGHEST = lax.Precision.HIGHEST

EPS = 1e-6
N_MOD = 6
SB_HEADS = 8
HEAD_DIM = 64
SB_WIDTH = SB_HEADS * HEAD_DIM
LRU_BLOCKS = 8
CONV_WIDTH = 4
LRU_C = 8.0
TOP_K = 4
SWIGLU_LIMIT = 7.0
SWIGLU_ALPHA = 1.702

LANES = 128
VMEM_LIMIT = 48 * 1024 * 1024

Q_BLOCK = 128
K_BLOCK = 128
ATTN_WINDOW_BLOCKS = 3
ATTN_UNDERFLOW_LOG = -110.0
EXPERT_BLOCK = 256
TOKEN_TILE = 256


def _params(sem):
    return pltpu.CompilerParams(dimension_semantics=sem, vmem_limit_bytes=VMEM_LIMIT)


def _ada_kernel(c_ref, w_ref, b_ref, o_ref):
    c = c_ref[...]
    ca = c * jax.nn.sigmoid(c)
    o_ref[...] = jnp.dot(ca, w_ref[...], precision=HIGHEST,
                         preferred_element_type=F32) + b_ref[...]


def _ada(c, ada_w, ada_b):
    B, D = c.shape
    E = ada_w.shape[1]
    tn = 1024
    return pl.pallas_call(
        _ada_kernel,
        grid=(E // tn,),
        in_specs=[pl.BlockSpec((B, D), lambda j: (0, 0)),
                  pl.BlockSpec((D, tn), lambda j: (0, j)),
                  pl.BlockSpec((1, tn), lambda j: (0, j))],
        out_specs=pl.BlockSpec((B, tn), lambda j: (0, j)),
        out_shape=jax.ShapeDtypeStruct((B, E), F32),
        compiler_params=_params(("arbitrary",)),
        name="ada",
    )(c, ada_w, ada_b.reshape(1, E))


def _rms(x, g):
    ms = jnp.mean(x * x, axis=-1, keepdims=True)
    return x * lax.rsqrt(ms + EPS) * g


def _inproj_kernel(x_ref, mod_ref, g_ref, w_ref, q_ref, k_ref, v_ref, xr_ref, gr_ref):
    x = x_ref[0]
    h = _rms(x, g_ref[...]) * (1.0 + mod_ref[0, 1:2, :]) + mod_ref[0, 0:1, :]
    hb = h.astype(BF16)
    W = SB_WIDTH

    def proj(c):
        return jnp.dot(hb, w_ref[:, c * W:(c + 1) * W], preferred_element_type=F32)

    q_ref[0] = (proj(0) * (HEAD_DIM ** -0.5)).astype(BF16)
    k_ref[0] = proj(1).astype(BF16)
    v_ref[0] = proj(2).astype(BF16)
    xr_ref[0] = proj(3)
    gr_ref[0] = proj(4)


def _inproj(x, mod3, g, w_in_b, ts=512):
    B, S, D = x.shape
    E = w_in_b.shape[1]
    W = SB_WIDTH
    row = pl.BlockSpec((1, ts, W), lambda b, s: (b, s, 0))
    return pl.pallas_call(
        _inproj_kernel,
        grid=(B, S // ts),
        in_specs=[pl.BlockSpec((1, ts, D), lambda b, s: (b, s, 0)),
                  pl.BlockSpec((1, N_MOD, D), lambda b, s: (b, 0, 0)),
                  pl.BlockSpec((1, D), lambda b, s: (0, 0)),
                  pl.BlockSpec((D, E), lambda b, s: (0, 0))],
        out_specs=[row, row, row, row, row],
        out_shape=[jax.ShapeDtypeStruct((B, S, W), BF16)] * 3
        + [jax.ShapeDtypeStruct((B, S, W), F32)] * 2,
        compiler_params=_params(("arbitrary", "arbitrary")),
        name="inproj",
    )(x, mod3, g.reshape(1, D), w_in_b)


def _attn_kernel(q_ref, k_ref, v_ref, o_ref):
    i = pl.program_id(2)
    QB, KB = Q_BLOCK, K_BLOCK
    q = q_ref[0]
    lane = lax.broadcasted_iota(jnp.int32, (QB, LANES), 1)
    zero = jnp.zeros_like(q)
    qs = jnp.concatenate([jnp.where(lane < HEAD_DIM, q, zero),
                          jnp.where(lane >= HEAD_DIM, q, zero)], axis=0)

    uj = lax.broadcasted_iota(jnp.int32, (2 * KB, KB + LANES), 0) & (KB - 1)
    us = lax.broadcasted_iota(jnp.int32, (2 * KB, KB + LANES), 1)
    u2 = jnp.where((us >= KB) | (uj > us), 1.0, 0.0).astype(BF16)

    row = lax.broadcasted_iota(jnp.int32, (2 * QB, KB), 0) & (QB - 1)
    col = lax.broadcasted_iota(jnp.int32, (2 * QB, KB), 1)
    causal = col < row

    def tiles(j_hi, n, diagonal, acc, carry):
        start = pl.multiple_of((j_hi - (n - 1)) * KB, KB)
        kw = k_ref[0, pl.ds(start, n * KB), :]
        vw = v_ref[0, pl.ds(start, n * KB), :]
        z = lax.dot_general(qs, kw, (((1,), (1,)), ((), ())), preferred_element_type=F32)
        soft = jnp.log(1.0 + jnp.exp(-jnp.abs(z)))
        log_beta = jnp.minimum(z, 0.0) - soft
        log_keep = -jnp.maximum(z, 0.0) - soft
        ws = [None] * n
        for c in reversed(range(n)):
            lk = log_keep[:, c * KB:(c + 1) * KB]
            masked = diagonal and c == n - 1
            if masked:
                lk = jnp.where(causal, lk, 0.0)
            hi = lk.astype(BF16)
            lo = (lk - hi.astype(F32)).astype(BF16)
            sums = jnp.dot(jnp.concatenate([hi, lo], axis=1), u2, preferred_element_type=F32)
            w = jnp.exp(log_beta[:, c * KB:(c + 1) * KB] + sums[:, :KB] + carry)
            if masked:
                w = jnp.where(causal, w, 0.0)
            ws[c] = w.astype(BF16)
            carry = carry + sums[:, KB:]
        w_all = ws[0] if n == 1 else jnp.concatenate(ws, axis=1)
        acc = acc + jnp.dot(w_all, vw, preferred_element_type=F32)
        return acc, carry

    zeros = jnp.zeros((2 * QB, LANES), F32)
    n_fast = ATTN_WINDOW_BLOCKS
    acc, carry = lax.cond(
        i >= n_fast - 1,
        lambda: tiles(i, n_fast, True, zeros, zeros),
        lambda: tiles(i, 1, True, zeros, zeros))
    j0 = jnp.where(i >= n_fast - 1, i - n_fast, i - 1)

    def more(st):
        j, _, _, cmax = st
        return (j >= 0) & (cmax >= ATTN_UNDERFLOW_LOG)

    def step(st):
        j, acc, carry, _ = st
        acc, carry = tiles(j, 1, False, acc, carry)
        return j - 1, acc, carry, jnp.max(carry)

    _, acc, _, _ = lax.while_loop(more, step, (j0, acc, carry, jnp.max(carry)))
    o_ref[0] = jnp.where(lane < HEAD_DIM, acc[:QB], acc[QB:])


def _attention(q, k, v):
    B, S, W = q.shape
    return pl.pallas_call(
        _attn_kernel,
        grid=(B, W // LANES, S // Q_BLOCK),
        in_specs=[pl.BlockSpec((1, Q_BLOCK, LANES), lambda b, p, i: (b, i, p)),
                  pl.BlockSpec((1, S, LANES), lambda b, p, i: (b, 0, p)),
                  pl.BlockSpec((1, S, LANES), lambda b, p, i: (b, 0, p))],
        out_specs=pl.BlockSpec((1, Q_BLOCK, LANES), lambda b, p, i: (b, i, p)),
        out_shape=jax.ShapeDtypeStruct((B, S, W), F32),
        compiler_params=_params(("arbitrary", "arbitrary", "arbitrary")),
        name="attn",
    )(q, k, v)


def _softplus(x):
    return jnp.maximum(x, 0.0) + jnp.log1p(jnp.exp(-jnp.abs(x)))


def _gelu_tanh(x):
    return 0.5 * x * (1.0 + jnp.tanh(0.7978845608028654 * (x + 0.044715 * x * x * x)))


def _lru_kernel(xr_ref, gr_ref, cw_ref, cb_ref, wr_ref, br_ref, wi_ref, bi_ref, lam_ref,
                g_ref, o_ref, xext, hc):
    TS = xr_ref.shape[1]
    PAD = 8

    @pl.when(pl.program_id(1) == 0)
    def _():
        xext[0:PAD, :] = jnp.zeros((PAD, xext.shape[1]), F32)
        hc[...] = jnp.zeros_like(hc)

    x = xr_ref[0]
    xext[PAD:PAD + TS, :] = x
    xc = cb_ref[...] + cw_ref[CONV_WIDTH - 1:CONV_WIDTH, :] * x
    for j in range(CONV_WIDTH - 1):
        back = CONV_WIDTH - 1 - j
        xc = xc + cw_ref[j:j + 1, :] * xext[PAD - back:PAD - back + TS, :]
    xext[0:PAD, :] = xext[TS:TS + PAD, :]

    xb = xc.astype(BF16)
    r = jax.nn.sigmoid(jnp.dot(xb, wr_ref[...], preferred_element_type=F32) + br_ref[...])
    ig = jax.nn.sigmoid(jnp.dot(xb, wi_ref[...], preferred_element_type=F32) + bi_ref[...])
    log_a = (-LRU_C) * r * _softplus(-lam_ref[...])
    a = jnp.exp(log_a)
    b = jnp.sqrt(-jnp.tanh(log_a) * (a * a + 1.0)) * (ig * xc)

    rows = lax.broadcasted_iota(jnp.int32, a.shape, 0)
    d = 1
    while d < TS:
        keep = rows >= d
        a_prev = jnp.where(keep, pltpu.roll(a, d, 0), 1.0)
        b_prev = jnp.where(keep, pltpu.roll(b, d, 0), 0.0)
        b = a * b_prev + b
        a = a * a_prev
        d *= 2
    h = a * hc[...] + b
    hc[...] = h[TS - 1:TS, :]

    y = h * _gelu_tanh(gr_ref[0])
    o_ref[0] = _rms(y, g_ref[...])


def _block_diag(w):
    H, I, J = w.shape
    eye = jnp.eye(H, dtype=w.dtype)
    return (w[:, :, None, :] * eye[:, None, :, None]).reshape(H * I, H * J)


def _lru(xr, gr, conv_w, conv_b, rg_w, rg_b, ig_w, ig_b, lam, g, ts=256):
    B, S, W = xr.shape
    row = pl.BlockSpec((1, ts, W), lambda b, s: (b, s, 0))
    vec = pl.BlockSpec((1, W), lambda b, s: (0, 0))
    mat = pl.BlockSpec((W, W), lambda b, s: (0, 0))
    return pl.pallas_call(
        _lru_kernel,
        grid=(B, S // ts),
        in_specs=[row, row, pl.BlockSpec((CONV_WIDTH, W), lambda b, s: (0, 0)), vec,
                  mat, vec, mat, vec, vec, vec],
        out_specs=row,
        out_shape=jax.ShapeDtypeStruct((B, S, W), F32),
        scratch_shapes=[pltpu.VMEM((ts + 8, W), F32), pltpu.VMEM((1, W), F32)],
        compiler_params=_params(("arbitrary", "arbitrary")),
        name="lru",
    )(xr, gr, conv_w, conv_b.reshape(1, W), _block_diag(rg_w).astype(BF16), rg_b.reshape(1, W),
      _block_diag(ig_w).astype(BF16), ig_b.reshape(1, W), lam.reshape(1, W), g.reshape(1, W))


def _outproj_kernel(x_ref, ya_ref, yl_ref, mod_ref, ga_ref, wo_ref, gf_ref, rw_ref, rb_ref,
                    x1_ref, h2_ref, idx_ref, prob_ref):
    W = SB_WIDTH
    ya = _rms(ya_ref[0], ga_ref[...]).astype(BF16)
    yl = yl_ref[0].astype(BF16)
    mix = (jnp.dot(ya, wo_ref[0:W, :], preferred_element_type=F32)
           + jnp.dot(yl, wo_ref[W:2 * W, :], preferred_element_type=F32))
    x1 = x_ref[0] + mod_ref[0, 2:3, :] * mix
    x1_ref[0] = x1
    h2 = _rms(x1, gf_ref[...]) * (1.0 + mod_ref[0, 4:5, :]) + mod_ref[0, 3:4, :]
    h2_ref[0] = h2

    logits = lax.dot_general(rw_ref[...], h2, (((1,), (1,)), ((), ())), precision=HIGHEST,
                             preferred_element_type=F32) + rb_ref[...]
    n_exp = logits.shape[0]
    eid = lax.broadcasted_iota(jnp.int32, logits.shape, 0)
    vals, idxs = [], []
    for _ in range(TOP_K):
        m = jnp.max(logits, axis=0, keepdims=True)
        sel = jnp.min(jnp.where(logits == m, eid, n_exp), axis=0, keepdims=True)
        vals.append(m)
        idxs.append(sel)
        logits = jnp.where(eid == sel, -jnp.inf, logits)
    es = [jnp.exp(vv - vals[0]) for vv in vals]
    inv = 1.0 / (es[0] + es[1] + es[2] + es[3])
    for r in range(TOP_K):
        idx_ref[0, r:r + 1, :] = idxs[r]
        prob_ref[0, r:r + 1, :] = es[r] * inv


def _outproj(x, ya, yl, mod3, ga, w_out_b, gf, router_wt, router_b, ts=512):
    B, S, D = x.shape
    W = ya.shape[2]
    NE = router_wt.shape[0]
    rowd = pl.BlockSpec((1, ts, D), lambda b, s: (b, s, 0))
    roww = pl.BlockSpec((1, ts, W), lambda b, s: (b, s, 0))
    sel = pl.BlockSpec((1, TOP_K, ts), lambda b, s: (b, 0, s))
    return pl.pallas_call(
        _outproj_kernel,
        grid=(B, S // ts),
        in_specs=[rowd, roww, roww,
                  pl.BlockSpec((1, N_MOD, D), lambda b, s: (b, 0, 0)),
                  pl.BlockSpec((1, W), lambda b, s: (0, 0)),
                  pl.BlockSpec((2 * W, D), lambda b, s: (0, 0)),
                  pl.BlockSpec((1, D), lambda b, s: (0, 0)),
                  pl.BlockSpec((NE, D), lambda b, s: (0, 0)),
                  pl.BlockSpec((NE, 1), lambda b, s: (0, 0))],
        out_specs=[rowd, rowd, sel, sel],
        out_shape=[jax.ShapeDtypeStruct((B, S, D), F32), jax.ShapeDtypeStruct((B, S, D), F32),
                   jax.ShapeDtypeStruct((B, TOP_K, S), jnp.int32),
                   jax.ShapeDtypeStruct((B, TOP_K, S), F32)],
        compiler_params=_params(("arbitrary", "arbitrary")),
        name="outproj",
    )(x, ya, yl, mod3, ga.reshape(1, W), w_out_b, gf.reshape(1, D), router_wt,
      router_b.reshape(NE, 1))


def _dispatch_kernel(pos_ref, fill_ref, h_ref, xs_ref, zbuf, sem, zsem):
    TT = h_ref.shape[0]
    TM = zbuf.shape[0]
    n_fill = fill_ref.shape[0]

    @pl.when(pl.program_id(0) == 0)
    def _():
        zbuf[...] = jnp.zeros_like(zbuf)

        def fill_copy(e):
            start = pl.multiple_of(jnp.maximum(fill_ref[e], 0), 8)
            return pltpu.make_async_copy(zbuf, xs_ref.at[pl.ds(start, TM), :], zsem)

        def start(e, c):
            @pl.when(fill_ref[e] >= 0)
            def _():
                fill_copy(e).start()
            return c

        def wait(e, c):
            @pl.when(fill_ref[e] >= 0)
            def _():
                fill_copy(e).wait()
            return c

        lax.fori_loop(0, n_fill, start, 0)
        lax.fori_loop(0, n_fill, wait, 0)

    def row_copy(t, r):
        return pltpu.make_async_copy(h_ref.at[pl.ds(t, 1), :],
                                     xs_ref.at[pl.ds(pos_ref[t * TOP_K + r], 1), :], sem)

    def issue(t, c):
        for r in range(TOP_K):
            row_copy(t, r).start()
        return c

    lax.fori_loop(0, TT, issue, 0)
    for r in range(TOP_K):
        pltpu.make_async_copy(h_ref, xs_ref.at[pl.ds(0, TT), :], sem).wait()


def _dispatch(h2f, pos_flat, fill_start, cap):
    N, D = h2f.shape
    TT = TOKEN_TILE
    return pl.pallas_call(
        _dispatch_kernel,
        grid=(N // TT,),
        in_specs=[pl.BlockSpec((TT * TOP_K,), lambda i: (i,), memory_space=pltpu.SMEM),
                  pl.BlockSpec(memory_space=pltpu.SMEM),
                  pl.BlockSpec((TT, D), lambda i: (i, 0))],
        out_specs=pl.BlockSpec(memory_space=pl.ANY),
        out_shape=jax.ShapeDtypeStruct((cap, D), F32),
        scratch_shapes=[pltpu.VMEM((EXPERT_BLOCK, D), F32), pltpu.SemaphoreType.DMA,
                        pltpu.SemaphoreType.DMA],
        compiler_params=_params(("arbitrary",)),
        name="dispatch",
    )(pos_flat, fill_start, h2f)


def _expert_kernel(be_ref, nu_ref, xs_ref, wg_ref, bg_ref, wu_ref, bu_ref, wd_ref, bd_ref,
                   ys_ref):
    @pl.when(pl.program_id(0) < nu_ref[0])
    def _():
        xb = xs_ref[...].astype(BF16)
        g = jnp.minimum(jnp.dot(xb, wg_ref[0], preferred_element_type=F32) + bg_ref[0],
                        SWIGLU_LIMIT)
        u = jnp.clip(jnp.dot(xb, wu_ref[0], preferred_element_type=F32) + bu_ref[0],
                     -SWIGLU_LIMIT, SWIGLU_LIMIT)
        act = (u + 1.0) * (g * jax.nn.sigmoid(SWIGLU_ALPHA * g))
        ys_ref[...] = jnp.dot(act.astype(BF16), wd_ref[0],
                              preferred_element_type=F32) + bd_ref[0]

    @pl.when(pl.program_id(0) >= nu_ref[0])
    def _():
        ys_ref[...] = jnp.zeros_like(ys_ref)


def _experts(xs, block_e, n_used, wg, bg, wu, bu, wd, bd):
    cap, D = xs.shape
    NE, _, DE = wg.shape
    TM = EXPERT_BLOCK
    n_blocks = cap // TM

    def rows(i, be, nu):
        return (jnp.minimum(i, nu[0] - 1), 0)

    def wsel(i, be, nu):
        return (be[i], 0, 0)

    grid_spec = pltpu.PrefetchScalarGridSpec(
        num_scalar_prefetch=2,
        grid=(n_blocks,),
        in_specs=[pl.BlockSpec((TM, D), rows),
                  pl.BlockSpec((1, D, DE), wsel), pl.BlockSpec((1, 1, DE), wsel),
                  pl.BlockSpec((1, D, DE), wsel), pl.BlockSpec((1, 1, DE), wsel),
                  pl.BlockSpec((1, DE, D), wsel), pl.BlockSpec((1, 1, D), wsel)],
        out_specs=pl.BlockSpec((TM, D), lambda i, be, nu: (i, 0)),
    )
    return pl.pallas_call(
        _expert_kernel,
        grid_spec=grid_spec,
        out_shape=jax.ShapeDtypeStruct((cap, D), F32),
        compiler_params=_params(("arbitrary",)),
        name="experts",
    )(block_e, n_used, xs, wg, bg.reshape(NE, 1, DE), wu, bu.reshape(NE, 1, DE), wd,
      bd.reshape(NE, 1, D))


def _combine_kernel(pos_ref, x1_ref, p_ref, mod_ref, g_ref, ys_ref, o_ref, buf, sem):
    TT = x1_ref.shape[0]

    def row_copy(t, r):
        return pltpu.make_async_copy(ys_ref.at[pl.ds(pos_ref[t * TOP_K + r], 1), :],
                                     buf.at[r, pl.ds(t, 1), :], sem)

    def issue(t, c):
        for r in range(TOP_K):
            row_copy(t, r).start()
        return c

    lax.fori_loop(0, TT, issue, 0)
    for r in range(TOP_K):
        pltpu.make_async_copy(ys_ref.at[pl.ds(0, TT), :], buf.at[r], sem).wait()

    p = p_ref[...]
    moe = p[:, 0:1] * buf[0]
    for r in range(1, TOP_K):
        moe = moe + p[:, r:r + 1] * buf[r]
    x2 = x1_ref[...] + mod_ref[0, 5:6, :] * moe
    o_ref[...] = _rms(x2, g_ref[...])


def _combine(x1f, probs, pos_flat, mod3, g, ys, tiles_per_batch):
    N, D = x1f.shape
    TT = TOKEN_TILE
    return pl.pallas_call(
        _combine_kernel,
        grid=(N // TT,),
        in_specs=[pl.BlockSpec((TT * TOP_K,), lambda i: (i,), memory_space=pltpu.SMEM),
                  pl.BlockSpec((TT, D), lambda i: (i, 0)),
                  pl.BlockSpec((TT, TOP_K), lambda i: (i, 0)),
                  pl.BlockSpec((1, N_MOD, D), lambda i: (i // tiles_per_batch, 0, 0)),
                  pl.BlockSpec((1, D), lambda i: (0, 0)),
                  pl.BlockSpec(memory_space=pl.ANY)],
        out_specs=pl.BlockSpec((TT, D), lambda i: (i, 0)),
        out_shape=jax.ShapeDtypeStruct((N, D), F32),
        scratch_shapes=[pltpu.VMEM((TOP_K, TT, D), F32), pltpu.SemaphoreType.DMA],
        compiler_params=_params(("arbitrary",)),
        name="combine",
    )(pos_flat, x1f, probs, mod3, g.reshape(1, D), ys)


def _routing(idx, n_exp):
    B, K, S = idx.shape
    TM = EXPERT_BLOCK
    n_assign = B * S * K
    e = jnp.transpose(idx, (0, 2, 1)).reshape(n_assign)
    onehot = (e[:, None] == jnp.arange(n_exp, dtype=jnp.int32)[None, :]).astype(jnp.int32)
    csum = jnp.cumsum(onehot, axis=0)
    rank = jnp.take_along_axis(csum, e[:, None], axis=1)[:, 0] - 1
    counts = csum[-1]
    padded = ((counts + TM - 1) // TM) * TM
    pad_ends = jnp.cumsum(padded)
    pad_starts = pad_ends - padded
    pos = (pad_starts[e] + rank).astype(jnp.int32)
    n_blocks = n_assign // TM + n_exp
    block_start = jnp.arange(n_blocks, dtype=jnp.int32) * TM
    block_e = jnp.minimum(jnp.searchsorted(pad_ends, block_start, side="right"),
                          n_exp - 1).astype(jnp.int32)
    n_used = (pad_ends[-1] // TM).astype(jnp.int32).reshape(1)
    tail_start = jnp.where(padded > 0, pad_ends - TM, -1)
    spare = pad_ends[-1] + jnp.arange(n_exp, dtype=jnp.int32) * TM
    fill_start = jnp.concatenate(
        [tail_start, jnp.where(spare < n_blocks * TM, spare, -1)]).astype(jnp.int32)
    return pos, block_e, n_used, fill_start, n_blocks * TM


def kernel(x, c, ada_w, ada_b, mix_norm_g, w_in, conv_w, conv_b, rg_w, rg_b, ig_w, ig_b,
           lru_lambda, attn_out_g, lru_out_g, w_out, ffn_norm_g, router_w, router_b,
           exp_w_gate, exp_b_gate, exp_w_up, exp_b_up, exp_w_down, exp_b_down, final_norm_g):
    B, S, D = x.shape
    depth = ada_w.shape[0]
    n_exp = router_w.shape[2]
    assert S % 512 == 0 and S % TOKEN_TILE == 0 and D % LANES == 0
    for l in range(depth):
        mod3 = _ada(c, ada_w[l], ada_b[l]).reshape(B, N_MOD, D)
        q, k, v, xr, gr = _inproj(x, mod3, mix_norm_g[l], w_in[l].astype(BF16))
        ya = _attention(q, k, v)
        yl = _lru(xr, gr, conv_w[l], conv_b[l], rg_w[l], rg_b[l], ig_w[l], ig_b[l],
                  lru_lambda[l], lru_out_g[l])
        x1, h2, idx, prob = _outproj(x, ya, yl, mod3, attn_out_g[l], w_out[l].astype(BF16),
                                     ffn_norm_g[l], router_w[l].T, router_b[l])
        pos, block_e, n_used, fill_start, cap = _routing(idx, n_exp)
        xs = _dispatch(h2.reshape(B * S, D), pos, fill_start, cap)
        ys = _experts(xs, block_e, n_used,
                      exp_w_gate[l].astype(BF16), exp_b_gate[l],
                      exp_w_up[l].astype(BF16), exp_b_up[l],
                      exp_w_down[l].astype(BF16), exp_b_down[l])
        probs = jnp.transpose(prob, (0, 2, 1)).reshape(B * S, TOP_K)
        assert depth == 1
        x = _combine(x1.reshape(B * S, D), probs, pos, mod3, final_norm_g, ys,
                     S // TOKEN_TILE).reshape(B, S, D)
    return x
```

```python
import functools

import jax
import jax.numpy as jnp
from jax import lax
from jax.experimental import pallas as pl
from jax.experimental.pallas import tpu as pltpu

F32 = jnp.float32
BF16 = jnp.bfloat16
HIGHEST = lax.Precision.HIGHEST

EPS = 1e-6
N_MOD = 6
SB_HEADS = 8
HEAD_DIM = 64
SB_WIDTH = SB_HEADS * HEAD_DIM
LRU_BLOCKS = 8
CONV_WIDTH = 4
LRU_C = 8.0
TOP_K = 4
SWIGLU_LIMIT = 7.0
SWIGLU_ALPHA = 1.702

LANES = 128
VMEM_LIMIT = 48 * 1024 * 1024

Q_BLOCK = 128
K_BLOCK = 128
ATTN_WINDOW_BLOCKS = 3
ATTN_UNDERFLOW_LOG = -110.0
EXPERT_BLOCK = 256
TOKEN_TILE = 256
ROW_DMA_UNROLL = 8


def _params(sem):
    return pltpu.CompilerParams(dimension_semantics=sem, vmem_limit_bytes=VMEM_LIMIT)


def _ada_kernel(c_ref, w_ref, b_ref, o_ref):
    c = c_ref[...]
    ca = c * jax.nn.sigmoid(c)
    o_ref[...] = jnp.dot(ca, w_ref[...], precision=HIGHEST,
                         preferred_element_type=F32) + b_ref[...]


def _ada(c, ada_w, ada_b):
    B, D = c.shape
    E = ada_w.shape[1]
    tn = 1024
    return pl.pallas_call(
        _ada_kernel,
        grid=(E // tn,),
        in_specs=[pl.BlockSpec((B, D), lambda j: (0, 0)),
                  pl.BlockSpec((D, tn), lambda j: (0, j)),
                  pl.BlockSpec((1, tn), lambda j: (0, j))],
        out_specs=pl.BlockSpec((B, tn), lambda j: (0, j)),
        out_shape=jax.ShapeDtypeStruct((B, E), F32),
        compiler_params=_params(("arbitrary",)),
        name="ada",
    )(c, ada_w, ada_b.reshape(1, E))


def _rms(x, g):
    ms = jnp.mean(x * x, axis=-1, keepdims=True)
    return x * lax.rsqrt(ms + EPS) * g


def _inproj_kernel(x_ref, mod_ref, g_ref, w_ref, q_ref, k_ref, v_ref, xr_ref, gr_ref):
    x = x_ref[0]
    h = _rms(x, g_ref[...]) * (1.0 + mod_ref[0, 1:2, :]) + mod_ref[0, 0:1, :]
    hb = h.astype(BF16)
    W = SB_WIDTH

    def proj(c):
        return jnp.dot(hb, w_ref[:, c * W:(c + 1) * W], preferred_element_type=F32)

    q_ref[0] = (proj(0) * (HEAD_DIM ** -0.5)).astype(BF16)
    k_ref[0] = proj(1).astype(BF16)
    v_ref[0] = proj(2).astype(BF16)
    xr_ref[0] = proj(3)
    gr_ref[0] = proj(4)


def _inproj(x, mod3, g, w_in_b, ts=512):
    B, S, D = x.shape
    E = w_in_b.shape[1]
    W = SB_WIDTH
    row = pl.BlockSpec((1, ts, W), lambda b, s: (b, s, 0))
    return pl.pallas_call(
        _inproj_kernel,
        grid=(B, S // ts),
        in_specs=[pl.BlockSpec((1, ts, D), lambda b, s: (b, s, 0)),
                  pl.BlockSpec((1, N_MOD, D), lambda b, s: (b, 0, 0)),
                  pl.BlockSpec((1, D), lambda b, s: (0, 0)),
                  pl.BlockSpec((D, E), lambda b, s: (0, 0))],
        out_specs=[row, row, row, row, row],
        out_shape=[jax.ShapeDtypeStruct((B, S, W), BF16)] * 3
        + [jax.ShapeDtypeStruct((B, S, W), F32)] * 2,
        compiler_params=_params(("arbitrary", "arbitrary")),
        name="inproj",
    )(x, mod3, g.reshape(1, D), w_in_b)


def _attn_kernel(q_ref, k_ref, v_ref, o_ref):
    i = pl.program_id(2)
    QB, KB = Q_BLOCK, K_BLOCK
    q = q_ref[0]
    lane = lax.broadcasted_iota(jnp.int32, (QB, LANES), 1)
    zero = jnp.zeros_like(q)
    qs = jnp.concatenate([jnp.where(lane < HEAD_DIM, q, zero),
                          jnp.where(lane >= HEAD_DIM, q, zero)], axis=0)

    uj = lax.broadcasted_iota(jnp.int32, (2 * KB, KB + LANES), 0) & (KB - 1)
    us = lax.broadcasted_iota(jnp.int32, (2 * KB, KB + LANES), 1)
    u2 = jnp.where((us >= KB) | (uj > us), 1.0, 0.0).astype(BF16)

    row = lax.broadcasted_iota(jnp.int32, (2 * QB, KB), 0) & (QB - 1)
    col = lax.broadcasted_iota(jnp.int32, (2 * QB, KB), 1)
    causal = col < row

    def tiles(j_hi, n, diagonal, acc, carry):
        start = pl.multiple_of((j_hi - (n - 1)) * KB, KB)
        kw = k_ref[0, pl.ds(start, n * KB), :]
        vw = v_ref[0, pl.ds(start, n * KB), :]
        z = lax.dot_general(qs, kw, (((1,), (1,)), ((), ())), preferred_element_type=F32)
        soft = jnp.log(1.0 + jnp.exp(-jnp.abs(z)))
        log_beta = jnp.minimum(z, 0.0) - soft
        log_keep = -jnp.maximum(z, 0.0) - soft
        ws = [None] * n
        for c in reversed(range(n)):
            lk = log_keep[:, c * KB:(c + 1) * KB]
            masked = diagonal and c == n - 1
            if masked:
                lk = jnp.where(causal, lk, 0.0)
            hi = lk.astype(BF16)
            lo = (lk - hi.astype(F32)).astype(BF16)
            sums = jnp.dot(jnp.concatenate([hi, lo], axis=1), u2, preferred_element_type=F32)
            w = jnp.exp(log_beta[:, c * KB:(c + 1) * KB] + sums[:, :KB] + carry)
            if masked:
                w = jnp.where(causal, w, 0.0)
            ws[c] = w.astype(BF16)
            carry = carry + sums[:, KB:]
        w_all = ws[0] if n == 1 else jnp.concatenate(ws, axis=1)
        acc = acc + jnp.dot(w_all, vw, preferred_element_type=F32)
        return acc, carry

    zeros = jnp.zeros((2 * QB, LANES), F32)
    n_fast = ATTN_WINDOW_BLOCKS
    acc, carry = lax.cond(
        i >= n_fast - 1,
        lambda: tiles(i, n_fast, True, zeros, zeros),
        lambda: tiles(i, 1, True, zeros, zeros))
    j0 = jnp.where(i >= n_fast - 1, i - n_fast, i - 1)

    def more(st):
        j, _, _, cmax = st
        return (j >= 0) & (cmax >= ATTN_UNDERFLOW_LOG)

    def step(st):
        j, acc, carry, _ = st
        acc, carry = tiles(j, 1, False, acc, carry)
        return j - 1, acc, carry, jnp.max(carry)

    _, acc, _, _ = lax.while_loop(more, step, (j0, acc, carry, jnp.max(carry)))
    o_ref[0] = jnp.where(lane < HEAD_DIM, acc[:QB], acc[QB:])


def _attention(q, k, v):
    B, S, W = q.shape
    return pl.pallas_call(
        _attn_kernel,
        grid=(B, W // LANES, S // Q_BLOCK),
        in_specs=[pl.BlockSpec((1, Q_BLOCK, LANES), lambda b, p, i: (b, i, p)),
                  pl.BlockSpec((1, S, LANES), lambda b, p, i: (b, 0, p)),
                  pl.BlockSpec((1, S, LANES), lambda b, p, i: (b, 0, p))],
        out_specs=pl.BlockSpec((1, Q_BLOCK, LANES), lambda b, p, i: (b, i, p)),
        out_shape=jax.ShapeDtypeStruct((B, S, W), F32),
        compiler_params=_params(("arbitrary", "arbitrary", "arbitrary")),
        name="attn",
    )(q, k, v)


def _softplus(x):
    return jnp.maximum(x, 0.0) + jnp.log1p(jnp.exp(-jnp.abs(x)))


def _gelu_tanh(x):
    return 0.5 * x * (1.0 + jnp.tanh(0.7978845608028654 * (x + 0.044715 * x * x * x)))


def _lru_kernel(xr_ref, gr_ref, cw_ref, cb_ref, wr_ref, br_ref, wi_ref, bi_ref, lam_ref,
                g_ref, o_ref, xext, hc):
    TS = xr_ref.shape[1]
    PAD = 8

    @pl.when(pl.program_id(1) == 0)
    def _():
        xext[0:PAD, :] = jnp.zeros((PAD, xext.shape[1]), F32)
        hc[...] = jnp.zeros_like(hc)

    x = xr_ref[0]
    xext[PAD:PAD + TS, :] = x
    xc = cb_ref[...] + cw_ref[CONV_WIDTH - 1:CONV_WIDTH, :] * x
    for j in range(CONV_WIDTH - 1):
        back = CONV_WIDTH - 1 - j
        xc = xc + cw_ref[j:j + 1, :] * xext[PAD - back:PAD - back + TS, :]
    xext[0:PAD, :] = xext[TS:TS + PAD, :]

    xb = xc.astype(BF16)
    r = jax.nn.sigmoid(jnp.dot(xb, wr_ref[...], preferred_element_type=F32) + br_ref[...])
    ig = jax.nn.sigmoid(jnp.dot(xb, wi_ref[...], preferred_element_type=F32) + bi_ref[...])
    log_a = (-LRU_C) * r * _softplus(-lam_ref[...])
    a = jnp.exp(log_a)
    b = jnp.sqrt(-jnp.tanh(log_a) * (a * a + 1.0)) * (ig * xc)

    rows = lax.broadcasted_iota(jnp.int32, a.shape, 0)
    d = 1
    while d < TS:
        keep = rows >= d
        a_prev = jnp.where(keep, pltpu.roll(a, d, 0), 1.0)
        b_prev = jnp.where(keep, pltpu.roll(b, d, 0), 0.0)
        b = a * b_prev + b
        a = a * a_prev
        d *= 2
    h = a * hc[...] + b
    hc[...] = h[TS - 1:TS, :]

    y = h * _gelu_tanh(gr_ref[0])
    o_ref[0] = _rms(y, g_ref[...])


def _block_diag(w):
    H, I, J = w.shape
    eye = jnp.eye(H, dtype=w.dtype)
    return (w[:, :, None, :] * eye[:, None, :, None]).reshape(H * I, H * J)


def _lru(xr, gr, conv_w, conv_b, rg_w, rg_b, ig_w, ig_b, lam, g, ts=256):
    B, S, W = xr.shape
    row = pl.BlockSpec((1, ts, W), lambda b, s: (b, s, 0))
    vec = pl.BlockSpec((1, W), lambda b, s: (0, 0))
    mat = pl.BlockSpec((W, W), lambda b, s: (0, 0))
    return pl.pallas_call(
        _lru_kernel,
        grid=(B, S // ts),
        in_specs=[row, row, pl.BlockSpec((CONV_WIDTH, W), lambda b, s: (0, 0)), vec,
                  mat, vec, mat, vec, vec, vec],
        out_specs=row,
        out_shape=jax.ShapeDtypeStruct((B, S, W), F32),
        scratch_shapes=[pltpu.VMEM((ts + 8, W), F32), pltpu.VMEM((1, W), F32)],
        compiler_params=_params(("arbitrary", "arbitrary")),
        name="lru",
    )(xr, gr, conv_w, conv_b.reshape(1, W), _block_diag(rg_w).astype(BF16), rg_b.reshape(1, W),
      _block_diag(ig_w).astype(BF16), ig_b.reshape(1, W), lam.reshape(1, W), g.reshape(1, W))


def _outproj_kernel(x_ref, ya_ref, yl_ref, mod_ref, ga_ref, wo_ref, gf_ref, rw_ref, rb_ref,
                    x1_ref, h2_ref, idx_ref, prob_ref, rank_ref, cnt_ref, tri, run):
    W = SB_WIDTH
    ya = _rms(ya_ref[0], ga_ref[...]).astype(BF16)
    yl = yl_ref[0].astype(BF16)
    mix = (jnp.dot(ya, wo_ref[0:W, :], preferred_element_type=F32)
           + jnp.dot(yl, wo_ref[W:2 * W, :], preferred_element_type=F32))
    x1 = x_ref[0] + mod_ref[0, 2:3, :] * mix
    x1_ref[0] = x1
    h2 = _rms(x1, gf_ref[...]) * (1.0 + mod_ref[0, 4:5, :]) + mod_ref[0, 3:4, :]
    h2_ref[0] = h2

    logits = lax.dot_general(rw_ref[...], h2, (((1,), (1,)), ((), ())), precision=HIGHEST,
                             preferred_element_type=F32) + rb_ref[...]
    n_exp = logits.shape[0]
    eid = lax.broadcasted_iota(jnp.int32, logits.shape, 0)
    vals, idxs = [], []
    for _ in range(TOP_K):
        m = jnp.max(logits, axis=0, keepdims=True)
        sel = jnp.min(jnp.where(logits == m, eid, n_exp), axis=0, keepdims=True)
        vals.append(m)
        idxs.append(sel)
        logits = jnp.where(eid == sel, -jnp.inf, logits)
    es = [jnp.exp(vv - vals[0]) for vv in vals]
    inv = 1.0 / (es[0] + es[1] + es[2] + es[3])

    first = (pl.program_id(0) == 0) & (pl.program_id(1) == 0)

    @pl.when(first)
    def _():
        ti = lax.broadcasted_iota(jnp.int32, tri.shape, 0)
        tj = lax.broadcasted_iota(jnp.int32, tri.shape, 1)
        tri[...] = jnp.where(ti < tj, 1.0, 0.0).astype(BF16)
        run[...] = jnp.zeros_like(run)

    hits = [eid == idxs[r] for r in range(TOP_K)]
    chosen = hits[0] | hits[1] | hits[2] | hits[3]
    cnt = jnp.where(chosen, 1.0, 0.0)
    before = jnp.dot(cnt.astype(BF16), tri[...], preferred_element_type=F32) + run[...]
    for r in range(TOP_K):
        idx_ref[0, r:r + 1, :] = idxs[r]
        prob_ref[0, r:r + 1, :] = es[r] * inv
        rank_ref[0, r:r + 1, :] = jnp.sum(jnp.where(hits[r], before, 0.0), axis=0,
                                          keepdims=True).astype(jnp.int32)
    run[...] = run[...] + jnp.sum(cnt, axis=1, keepdims=True)
    cnt_ref[...] = jnp.broadcast_to(run[...], cnt_ref.shape).astype(jnp.int32)


def _outproj(x, ya, yl, mod3, ga, w_out_b, gf, router_wt, router_b, ts=512):
    B, S, D = x.shape
    W = ya.shape[2]
    NE = router_wt.shape[0]
    rowd = pl.BlockSpec((1, ts, D), lambda b, s: (b, s, 0))
    roww = pl.BlockSpec((1, ts, W), lambda b, s: (b, s, 0))
    sel = pl.BlockSpec((1, TOP_K, ts), lambda b, s: (b, 0, s))
    return pl.pallas_call(
        _outproj_kernel,
        grid=(B, S // ts),
        in_specs=[rowd, roww, roww,
                  pl.BlockSpec((1, N_MOD, D), lambda b, s: (b, 0, 0)),
                  pl.BlockSpec((1, W), lambda b, s: (0, 0)),
                  pl.BlockSpec((2 * W, D), lambda b, s: (0, 0)),
                  pl.BlockSpec((1, D), lambda b, s: (0, 0)),
                  pl.BlockSpec((NE, D), lambda b, s: (0, 0)),
                  pl.BlockSpec((NE, 1), lambda b, s: (0, 0))],
        out_specs=[rowd, rowd, sel, sel, sel, pl.BlockSpec((NE, LANES), lambda b, s: (0, 0))],
        out_shape=[jax.ShapeDtypeStruct((B, S, D), F32), jax.ShapeDtypeStruct((B, S, D), F32),
                   jax.ShapeDtypeStruct((B, TOP_K, S), jnp.int32),
                   jax.ShapeDtypeStruct((B, TOP_K, S), F32),
                   jax.ShapeDtypeStruct((B, TOP_K, S), jnp.int32),
                   jax.ShapeDtypeStruct((NE, LANES), jnp.int32)],
        scratch_shapes=[pltpu.VMEM((ts, ts), BF16), pltpu.VMEM((NE, 1), F32)],
        compiler_params=_params(("arbitrary", "arbitrary")),
        name="outproj",
    )(x, ya, yl, mod3, ga.reshape(1, W), w_out_b, gf.reshape(1, D), router_wt,
      router_b.reshape(NE, 1))


def _dispatch_kernel(p0_ref, p1_ref, p2_ref, p3_ref, fill_ref, h_ref, xs_ref, zbuf, sem, zsem):
    TT = h_ref.shape[0]
    TM = zbuf.shape[0]
    n_fill = fill_ref.shape[0]

    @pl.when(pl.program_id(0) == 0)
    def _():
        zbuf[...] = jnp.zeros_like(zbuf)

        def fill_copy(e):
            start = pl.multiple_of(jnp.maximum(fill_ref[e], 0), 8)
            return pltpu.make_async_copy(zbuf, xs_ref.at[pl.ds(start, TM), :], zsem)

        def start(e, c):
            @pl.when(fill_ref[e] >= 0)
            def _():
                fill_copy(e).start()
            return c

        def wait(e, c):
            @pl.when(fill_ref[e] >= 0)
            def _():
                fill_copy(e).wait()
            return c

        lax.fori_loop(0, n_fill, start, 0)
        lax.fori_loop(0, n_fill, wait, 0)

    pos_refs = (p0_ref, p1_ref, p2_ref, p3_ref)

    def row_copy(t, r):
        return pltpu.make_async_copy(h_ref.at[pl.ds(t, 1), :],
                                     xs_ref.at[pl.ds(pos_refs[r][t], 1), :], sem)

    def issue(g, c):
        for tt in range(ROW_DMA_UNROLL):
            for r in range(TOP_K):
                row_copy(g * ROW_DMA_UNROLL + tt, r).start(priority=(tt * TOP_K + r) % 2)
        return c

    lax.fori_loop(0, TT // ROW_DMA_UNROLL, issue, 0)
    for r in range(TOP_K):
        pltpu.make_async_copy(h_ref, xs_ref.at[pl.ds(0, TT), :], sem).wait()


def _slot_specs(tiles_per_batch):
    def spec(r):
        return pl.BlockSpec(
            (TOKEN_TILE,),
            lambda i: ((i // tiles_per_batch * TOP_K + r) * tiles_per_batch
                       + i % tiles_per_batch,),
            memory_space=pltpu.SMEM)
    return [spec(r) for r in range(TOP_K)]


def _dispatch(h2f, pos_flat, fill_start, cap, tiles_per_batch):
    N, D = h2f.shape
    TT = TOKEN_TILE
    return pl.pallas_call(
        _dispatch_kernel,
        grid=(N // TT,),
        in_specs=_slot_specs(tiles_per_batch)
        + [pl.BlockSpec(memory_space=pltpu.SMEM), pl.BlockSpec((TT, D), lambda i: (i, 0))],
        out_specs=pl.BlockSpec(memory_space=pl.ANY),
        out_shape=jax.ShapeDtypeStruct((cap, D), F32),
        scratch_shapes=[pltpu.VMEM((EXPERT_BLOCK, D), F32), pltpu.SemaphoreType.DMA,
                        pltpu.SemaphoreType.DMA],
        compiler_params=_params(("arbitrary",)),
        name="dispatch",
    )(pos_flat, pos_flat, pos_flat, pos_flat, fill_start, h2f)


def _expert_kernel(be_ref, nu_ref, xs_ref, wg_ref, bg_ref, wu_ref, bu_ref, wd_ref, bd_ref,
                   ys_ref, wgb, wub, wdb):
    i = pl.program_id(0)

    @pl.when((i == 0) | (be_ref[i] != be_ref[jnp.maximum(i - 1, 0)]))
    def _():
        wgb[...] = wg_ref[0].astype(BF16)
        wub[...] = wu_ref[0].astype(BF16)
        wdb[...] = wd_ref[0].astype(BF16)

    @pl.when(i < nu_ref[0])
    def _():
        xb = xs_ref[...].astype(BF16)
        g = jnp.minimum(jnp.dot(xb, wgb[...], preferred_element_type=F32) + bg_ref[0],
                        SWIGLU_LIMIT)
        u = jnp.clip(jnp.dot(xb, wub[...], preferred_element_type=F32) + bu_ref[0],
                     -SWIGLU_LIMIT, SWIGLU_LIMIT)
        act = (u + 1.0) * (g * jax.nn.sigmoid(SWIGLU_ALPHA * g))
        ys_ref[...] = jnp.dot(act.astype(BF16), wdb[...],
                              preferred_element_type=F32) + bd_ref[0]

    @pl.when(i >= nu_ref[0])
    def _():
        ys_ref[...] = jnp.zeros_like(ys_ref)


def _experts(xs, block_e, n_used, wg, bg, wu, bu, wd, bd):
    cap, D = xs.shape
    NE, _, DE = wg.shape
    TM = EXPERT_BLOCK
    n_blocks = cap // TM

    def rows(i, be, nu):
        return (jnp.minimum(i, nu[0] - 1), 0)

    def wsel(i, be, nu):
        return (be[i], 0, 0)

    grid_spec = pltpu.PrefetchScalarGridSpec(
        num_scalar_prefetch=2,
        grid=(n_blocks,),
        in_specs=[pl.BlockSpec((TM, D), rows),
                  pl.BlockSpec((1, D, DE), wsel), pl.BlockSpec((1, 1, DE), wsel),
                  pl.BlockSpec((1, D, DE), wsel), pl.BlockSpec((1, 1, DE), wsel),
                  pl.BlockSpec((1, DE, D), wsel), pl.BlockSpec((1, 1, D), wsel)],
        out_specs=pl.BlockSpec((TM, D), lambda i, be, nu: (i, 0)),
        scratch_shapes=[pltpu.VMEM((D, DE), BF16), pltpu.VMEM((D, DE), BF16),
                        pltpu.VMEM((DE, D), BF16)],
    )
    return pl.pallas_call(
        _expert_kernel,
        grid_spec=grid_spec,
        out_shape=jax.ShapeDtypeStruct((cap, D), F32),
        compiler_params=_params(("arbitrary",)),
        name="experts",
    )(block_e, n_used, xs, wg, bg.reshape(NE, 1, DE), wu, bu.reshape(NE, 1, DE), wd,
      bd.reshape(NE, 1, D))


def _combine_kernel(c0_ref, c1_ref, c2_ref, c3_ref, n0_ref, n1_ref, n2_ref, n3_ref,
                    x1_ref, p_ref, mod_ref, g_ref, ys_ref, o_ref, buf, sem):
    TT = x1_ref.shape[0]
    i = pl.program_id(0)
    slot = i % 2

    def gather(pos_refs, s):
        def issue(g, c):
            for tt in range(ROW_DMA_UNROLL):
                t = g * ROW_DMA_UNROLL + tt
                for r in range(TOP_K):
                    pltpu.make_async_copy(
                        ys_ref.at[pl.ds(pos_refs[r][t], 1), :],
                        buf.at[s, r, pl.ds(t, 1), :],
                        sem.at[s]).start(priority=(tt * TOP_K + r) % 2)
            return c
        lax.fori_loop(0, TT // ROW_DMA_UNROLL, issue, 0)

    @pl.when(i == 0)
    def _():
        gather((c0_ref, c1_ref, c2_ref, c3_ref), 0)

    @pl.when(i + 1 < pl.num_programs(0))
    def _():
        gather((n0_ref, n1_ref, n2_ref, n3_ref), 1 - slot)

    for r in range(TOP_K):
        pltpu.make_async_copy(ys_ref.at[pl.ds(0, TT), :], buf.at[slot, r], sem.at[slot]).wait()

    p = p_ref[...]
    moe = p[:, 0:1] * buf[slot, 0]
    for r in range(1, TOP_K):
        moe = moe + p[:, r:r + 1] * buf[slot, r]
    x2 = x1_ref[...] + mod_ref[0, 5:6, :] * moe
    o_ref[...] = _rms(x2, g_ref[...])


def _combine(x1f, probs, pos_flat, mod3, g, ys, tiles_per_batch):
    N, D = x1f.shape
    TT = TOKEN_TILE
    n_tiles = N // TT

    def next_spec(r):
        def index(i):
            j = jnp.minimum(i + 1, n_tiles - 1)
            return ((j // tiles_per_batch * TOP_K + r) * tiles_per_batch + j % tiles_per_batch,)
        return pl.BlockSpec((TT,), index, memory_space=pltpu.SMEM)

    return pl.pallas_call(
        _combine_kernel,
        grid=(n_tiles,),
        in_specs=_slot_specs(tiles_per_batch) + [next_spec(r) for r in range(TOP_K)]
        + [pl.BlockSpec((TT, D), lambda i: (i, 0)),
           pl.BlockSpec((TT, TOP_K), lambda i: (i, 0)),
           pl.BlockSpec((1, N_MOD, D), lambda i: (i // tiles_per_batch, 0, 0)),
           pl.BlockSpec((1, D), lambda i: (0, 0)),
           pl.BlockSpec(memory_space=pl.ANY)],
        out_specs=pl.BlockSpec((TT, D), lambda i: (i, 0)),
        out_shape=jax.ShapeDtypeStruct((N, D), F32),
        scratch_shapes=[pltpu.VMEM((2, TOP_K, TT, D), F32), pltpu.SemaphoreType.DMA((2,))],
        compiler_params=_params(("arbitrary",)),
        name="combine",
    )(*([pos_flat] * (2 * TOP_K)), x1f, probs, mod3, g.reshape(1, D), ys)


def _routing(idx, rank, counts):
    B, K, S = idx.shape
    n_exp = counts.shape[0]
    TM = EXPERT_BLOCK
    n_assign = B * S * K
    padded = ((counts + TM - 1) // TM) * TM
    pad_ends = jnp.cumsum(padded)
    pad_starts = pad_ends - padded
    pos = (pad_starts[idx] + rank).astype(jnp.int32).reshape(n_assign)
    n_blocks = n_assign // TM + n_exp
    block_start = jnp.arange(n_blocks, dtype=jnp.int32) * TM
    block_e = jnp.minimum(jnp.searchsorted(pad_ends, block_start, side="right"),
                          n_exp - 1).astype(jnp.int32)
    n_used = (pad_ends[-1] // TM).astype(jnp.int32).reshape(1)
    tail_start = jnp.where(padded > 0, pad_ends - TM, -1)
    spare = pad_ends[-1] + jnp.arange(n_exp, dtype=jnp.int32) * TM
    fill_start = jnp.concatenate(
        [tail_start, jnp.where(spare < n_blocks * TM, spare, -1)]).astype(jnp.int32)
    return pos, block_e, n_used, fill_start, n_blocks * TM


def kernel(x, c, ada_w, ada_b, mix_norm_g, w_in, conv_w, conv_b, rg_w, rg_b, ig_w, ig_b,
           lru_lambda, attn_out_g, lru_out_g, w_out, ffn_norm_g, router_w, router_b,
           exp_w_gate, exp_b_gate, exp_w_up, exp_b_up, exp_w_down, exp_b_down, final_norm_g):
    B, S, D = x.shape
    depth = ada_w.shape[0]
    n_exp = router_w.shape[2]
    assert S % 512 == 0 and S % TOKEN_TILE == 0 and D % LANES == 0
    for l in range(depth):
        mod3 = _ada(c, ada_w[l], ada_b[l]).reshape(B, N_MOD, D)
        q, k, v, xr, gr = _inproj(x, mod3, mix_norm_g[l], w_in[l].astype(BF16))
        ya = _attention(q, k, v)
        yl = _lru(xr, gr, conv_w[l], conv_b[l], rg_w[l], rg_b[l], ig_w[l], ig_b[l],
                  lru_lambda[l], lru_out_g[l])
        x1, h2, idx, prob, rank, cnt = _outproj(
            x, ya, yl, mod3, attn_out_g[l], w_out[l].astype(BF16), ffn_norm_g[l],
            router_w[l].T, router_b[l])
        pos, block_e, n_used, fill_start, cap = _routing(idx, rank, cnt[:, 0])
        xs = _dispatch(h2.reshape(B * S, D), pos, fill_start, cap, S // TOKEN_TILE)
        ys = _experts(xs, block_e, n_used, exp_w_gate[l], exp_b_gate[l],
                      exp_w_up[l], exp_b_up[l], exp_w_down[l], exp_b_down[l])
        probs = jnp.transpose(prob, (0, 2, 1)).reshape(B * S, TOP_K)
        assert depth == 1
        x = _combine(x1.reshape(B * S, D), probs, pos, mod3, final_norm_g, ys,
                     S // TOKEN_TILE).reshape(B, S, D)
    return x
```

```python
import functools

import jax
import jax.numpy as jnp
from jax import lax
from jax.experimental import pallas as pl
from jax.experimental.pallas import tpu as pltpu

F32 = jnp.float32
BF16 = jnp.bfloat16
HIGHEST = lax.Precision.HIGHEST

EPS = 1e-6
N_MOD = 6
SB_HEADS = 8
HEAD_DIM = 64
SB_WIDTH = SB_HEADS * HEAD_DIM
LRU_BLOCKS = 8
CONV_WIDTH = 4
LRU_C = 8.0
TOP_K = 4
SWIGLU_LIMIT = 7.0
SWIGLU_ALPHA = 1.702

LANES = 128
VMEM_LIMIT = 48 * 1024 * 1024

Q_BLOCK = 128
K_BLOCK = 128
ATTN_WINDOW_BLOCKS = 3
ATTN_LOOP_BLOCKS = 2
ATTN_UNDERFLOW_LOG = -110.0
EXPERT_BLOCK = 256
TOKEN_TILE = 256
ROW_DMA_UNROLL = 8


def _params(sem):
    return pltpu.CompilerParams(dimension_semantics=sem, vmem_limit_bytes=VMEM_LIMIT)


def _ada_kernel(c_ref, w_ref, b_ref, o_ref):
    c = c_ref[...]
    ca = c * jax.nn.sigmoid(c)
    o_ref[...] = jnp.dot(ca, w_ref[...], precision=HIGHEST,
                         preferred_element_type=F32) + b_ref[...]


def _ada(c, ada_w, ada_b):
    B, D = c.shape
    E = ada_w.shape[1]
    tn = 1024
    return pl.pallas_call(
        _ada_kernel,
        grid=(E // tn,),
        in_specs=[pl.BlockSpec((B, D), lambda j: (0, 0)),
                  pl.BlockSpec((D, tn), lambda j: (0, j)),
                  pl.BlockSpec((1, tn), lambda j: (0, j))],
        out_specs=pl.BlockSpec((B, tn), lambda j: (0, j)),
        out_shape=jax.ShapeDtypeStruct((B, E), F32),
        compiler_params=_params(("arbitrary",)),
        name="ada",
    )(c, ada_w, ada_b.reshape(1, E))


def _rms(x, g):
    ms = jnp.mean(x * x, axis=-1, keepdims=True)
    return x * lax.rsqrt(ms + EPS) * g


def _inproj_kernel(x_ref, mod_ref, g_ref, w_ref, q_ref, k_ref, v_ref, xr_ref, gr_ref):
    x = x_ref[0]
    h = _rms(x, g_ref[...]) * (1.0 + mod_ref[0, 1:2, :]) + mod_ref[0, 0:1, :]
    hb = h.astype(BF16)
    W = SB_WIDTH

    def proj(c):
        return jnp.dot(hb, w_ref[:, c * W:(c + 1) * W], preferred_element_type=F32)

    q_ref[0] = (proj(0) * (HEAD_DIM ** -0.5)).astype(BF16)
    k_ref[0] = proj(1).astype(BF16)
    v_ref[0] = proj(2).astype(BF16)
    xr_ref[0] = proj(3)
    gr_ref[0] = proj(4)


def _inproj(x, mod3, g, w_in_b, ts=512):
    B, S, D = x.shape
    E = w_in_b.shape[1]
    W = SB_WIDTH
    row = pl.BlockSpec((1, ts, W), lambda b, s: (b, s, 0))
    return pl.pallas_call(
        _inproj_kernel,
        grid=(B, S // ts),
        in_specs=[pl.BlockSpec((1, ts, D), lambda b, s: (b, s, 0)),
                  pl.BlockSpec((1, N_MOD, D), lambda b, s: (b, 0, 0)),
                  pl.BlockSpec((1, D), lambda b, s: (0, 0)),
                  pl.BlockSpec((D, E), lambda b, s: (0, 0))],
        out_specs=[row, row, row, row, row],
        out_shape=[jax.ShapeDtypeStruct((B, S, W), BF16)] * 3
        + [jax.ShapeDtypeStruct((B, S, W), F32)] * 2,
        compiler_params=_params(("arbitrary", "arbitrary")),
        name="inproj",
    )(x, mod3, g.reshape(1, D), w_in_b)


def _attn_kernel(q_ref, k_ref, v_ref, o_ref):
    i = pl.program_id(1)
    QB, KB = Q_BLOCK, K_BLOCK
    n_pairs = q_ref.shape[2] // LANES
    lane = lax.broadcasted_iota(jnp.int32, (QB, LANES), 1)

    def stacked_q(p):
        q = q_ref[0, :, p * LANES:(p + 1) * LANES]
        zero = jnp.zeros_like(q)
        return jnp.concatenate([jnp.where(lane < HEAD_DIM, q, zero),
                                jnp.where(lane >= HEAD_DIM, q, zero)], axis=0)

    qs = [stacked_q(p) for p in range(n_pairs)]

    uj = lax.broadcasted_iota(jnp.int32, (2 * KB, KB + LANES), 0) & (KB - 1)
    us = lax.broadcasted_iota(jnp.int32, (2 * KB, KB + LANES), 1)
    u2 = jnp.where((us >= KB) | (uj > us), 1.0, 0.0).astype(BF16)

    row = lax.broadcasted_iota(jnp.int32, (2 * QB, KB), 0) & (QB - 1)
    col = lax.broadcasted_iota(jnp.int32, (2 * QB, KB), 1)
    causal = col < row

    def tiles(p, j_hi, n, diagonal, acc, carry):
        start = pl.multiple_of((j_hi - (n - 1)) * KB, KB)
        kw = k_ref[0, pl.ds(start, n * KB), p * LANES:(p + 1) * LANES]
        vw = v_ref[0, pl.ds(start, n * KB), p * LANES:(p + 1) * LANES]
        z = lax.dot_general(qs[p], kw, (((1,), (1,)), ((), ())), preferred_element_type=F32)
        soft = jnp.log(1.0 + jnp.exp(-jnp.abs(z)))
        log_beta = jnp.minimum(z, 0.0) - soft
        log_keep = log_beta - z
        ws = [None] * n
        for c in reversed(range(n)):
            lk = log_keep[:, c * KB:(c + 1) * KB]
            masked = diagonal and c == n - 1
            if masked:
                lk = jnp.where(causal, lk, 0.0)
            hi_f = lax.bitcast_convert_type(
                lax.bitcast_convert_type(lk, jnp.uint32) & jnp.uint32(0xFFFF0000), F32)
            hi_lo = jnp.concatenate([hi_f.astype(BF16), (lk - hi_f).astype(BF16)], axis=1)
            sums = jnp.dot(hi_lo, u2, preferred_element_type=F32)
            w = jnp.exp(log_beta[:, c * KB:(c + 1) * KB] + sums[:, :KB] + carry)
            if masked:
                w = jnp.where(causal, w, 0.0)
            ws[c] = w.astype(BF16)
            carry = carry + sums[:, KB:]
        w_all = ws[0] if n == 1 else jnp.concatenate(ws, axis=1)
        acc = acc + jnp.dot(w_all, vw, preferred_element_type=F32)
        return acc, carry

    def fold(j_hi, n, diagonal, state):
        out = []
        for p in range(n_pairs):
            out += tiles(p, j_hi, n, diagonal, state[2 * p], state[2 * p + 1])
        return tuple(out)

    def carry_max(state):
        m = state[1]
        for p in range(1, n_pairs):
            m = jnp.maximum(m, state[2 * p + 1])
        return jnp.max(m)

    zeros = jnp.zeros((2 * QB, LANES), F32)
    init = (zeros,) * (2 * n_pairs)
    n_first, n_loop = ATTN_WINDOW_BLOCKS, ATTN_LOOP_BLOCKS
    state = lax.cond(i >= n_first - 1,
                     lambda: fold(i, n_first, True, init),
                     lambda: fold(i, 1, True, init))
    j0 = jnp.where(i >= n_first - 1, i - n_first, i - 1)

    def loop(n, j, state):
        def more(st):
            return (st[0] >= n - 1) & (st[1] >= ATTN_UNDERFLOW_LOG)

        def step(st):
            new = fold(st[0], n, False, st[2])
            return st[0] - n, carry_max(new), new

        j, _, state = lax.while_loop(more, step, (j, carry_max(state), state))
        return j, state

    j1, state = loop(n_loop, j0, state)
    _, state = loop(1, j1, state)
    for p in range(n_pairs):
        acc = state[2 * p]
        o_ref[0, :, p * LANES:(p + 1) * LANES] = jnp.where(lane < HEAD_DIM, acc[:QB], acc[QB:])


def _attention(q, k, v):
    B, S, W = q.shape
    return pl.pallas_call(
        _attn_kernel,
        grid=(B, S // Q_BLOCK),
        in_specs=[pl.BlockSpec((1, Q_BLOCK, W), lambda b, i: (b, i, 0)),
                  pl.BlockSpec((1, S, W), lambda b, i: (b, 0, 0)),
                  pl.BlockSpec((1, S, W), lambda b, i: (b, 0, 0))],
        out_specs=pl.BlockSpec((1, Q_BLOCK, W), lambda b, i: (b, i, 0)),
        out_shape=jax.ShapeDtypeStruct((B, S, W), F32),
        compiler_params=_params(("arbitrary", "arbitrary")),
        name="attn",
    )(q, k, v)


def _softplus(x):
    return jnp.maximum(x, 0.0) + jnp.log1p(jnp.exp(-jnp.abs(x)))


def _gelu_tanh(x):
    return 0.5 * x * (1.0 + jnp.tanh(0.7978845608028654 * (x + 0.044715 * x * x * x)))


def _lru_kernel(xr_ref, gr_ref, cw_ref, cb_ref, wr_ref, br_ref, wi_ref, bi_ref, lam_ref,
                g_ref, o_ref, xext, hc):
    TS = xr_ref.shape[1]
    PAD = 8

    @pl.when(pl.program_id(1) == 0)
    def _():
        xext[0:PAD, :] = jnp.zeros((PAD, xext.shape[1]), F32)
        hc[...] = jnp.zeros_like(hc)

    x = xr_ref[0]
    xext[PAD:PAD + TS, :] = x
    xc = cb_ref[...] + cw_ref[CONV_WIDTH - 1:CONV_WIDTH, :] * x
    for j in range(CONV_WIDTH - 1):
        back = CONV_WIDTH - 1 - j
        xc = xc + cw_ref[j:j + 1, :] * xext[PAD - back:PAD - back + TS, :]
    xext[0:PAD, :] = xext[TS:TS + PAD, :]

    xb = xc.astype(BF16)
    r = jax.nn.sigmoid(jnp.dot(xb, wr_ref[...], preferred_element_type=F32) + br_ref[...])
    ig = jax.nn.sigmoid(jnp.dot(xb, wi_ref[...], preferred_element_type=F32) + bi_ref[...])
    log_a = (-LRU_C) * r * _softplus(-lam_ref[...])
    a = jnp.exp(log_a)
    b = jnp.sqrt(-jnp.tanh(log_a) * (a * a + 1.0)) * (ig * xc)

    rows = lax.broadcasted_iota(jnp.int32, a.shape, 0)
    d = 1
    while d < TS:
        keep = rows >= d
        a_prev = jnp.where(keep, pltpu.roll(a, d, 0), 1.0)
        b_prev = jnp.where(keep, pltpu.roll(b, d, 0), 0.0)
        b = a * b_prev + b
        a = a * a_prev
        d *= 2
    h = a * hc[...] + b
    hc[...] = h[TS - 1:TS, :]

    y = h * _gelu_tanh(gr_ref[0])
    o_ref[0] = _rms(y, g_ref[...])


def _block_diag(w):
    H, I, J = w.shape
    eye = jnp.eye(H, dtype=w.dtype)
    return (w[:, :, None, :] * eye[:, None, :, None]).reshape(H * I, H * J)


def _lru(xr, gr, conv_w, conv_b, rg_w, rg_b, ig_w, ig_b, lam, g, ts=256):
    B, S, W = xr.shape
    row = pl.BlockSpec((1, ts, W), lambda b, s: (b, s, 0))
    vec = pl.BlockSpec((1, W), lambda b, s: (0, 0))
    mat = pl.BlockSpec((W, W), lambda b, s: (0, 0))
    return pl.pallas_call(
        _lru_kernel,
        grid=(B, S // ts),
        in_specs=[row, row, pl.BlockSpec((CONV_WIDTH, W), lambda b, s: (0, 0)), vec,
                  mat, vec, mat, vec, vec, vec],
        out_specs=row,
        out_shape=jax.ShapeDtypeStruct((B, S, W), F32),
        scratch_shapes=[pltpu.VMEM((ts + 8, W), F32), pltpu.VMEM((1, W), F32)],
        compiler_params=_params(("arbitrary", "arbitrary")),
        name="lru",
    )(xr, gr, conv_w, conv_b.reshape(1, W), _block_diag(rg_w).astype(BF16), rg_b.reshape(1, W),
      _block_diag(ig_w).astype(BF16), ig_b.reshape(1, W), lam.reshape(1, W), g.reshape(1, W))


def _outproj_kernel(x_ref, ya_ref, yl_ref, mod_ref, ga_ref, wo_ref, gf_ref, rw_ref, rb_ref,
                    x1_ref, h2_ref, idx_ref, prob_ref, rank_ref, cnt_ref, tri, run):
    W = SB_WIDTH
    ya = _rms(ya_ref[0], ga_ref[...]).astype(BF16)
    yl = yl_ref[0].astype(BF16)
    mix = (jnp.dot(ya, wo_ref[0:W, :], preferred_element_type=F32)
           + jnp.dot(yl, wo_ref[W:2 * W, :], preferred_element_type=F32))
    x1 = x_ref[0] + mod_ref[0, 2:3, :] * mix
    x1_ref[0] = x1
    h2 = _rms(x1, gf_ref[...]) * (1.0 + mod_ref[0, 4:5, :]) + mod_ref[0, 3:4, :]
    h2_ref[0] = h2

    logits = lax.dot_general(rw_ref[...], h2, (((1,), (1,)), ((), ())), precision=HIGHEST,
                             preferred_element_type=F32) + rb_ref[...]
    n_exp = logits.shape[0]
    eid = lax.broadcasted_iota(jnp.int32, logits.shape, 0)
    vals, idxs = [], []
    for _ in range(TOP_K):
        m = jnp.max(logits, axis=0, keepdims=True)
        sel = jnp.min(jnp.where(logits == m, eid, n_exp), axis=0, keepdims=True)
        vals.append(m)
        idxs.append(sel)
        logits = jnp.where(eid == sel, -jnp.inf, logits)
    es = [jnp.exp(vv - vals[0]) for vv in vals]
    inv = 1.0 / (es[0] + es[1] + es[2] + es[3])

    first = (pl.program_id(0) == 0) & (pl.program_id(1) == 0)

    @pl.when(first)
    def _():
        ti = lax.broadcasted_iota(jnp.int32, tri.shape, 0)
        tj = lax.broadcasted_iota(jnp.int32, tri.shape, 1)
        tri[...] = jnp.where(ti < tj, 1.0, 0.0).astype(BF16)
        run[...] = jnp.zeros_like(run)

    hits = [eid == idxs[r] for r in range(TOP_K)]
    chosen = hits[0] | hits[1] | hits[2] | hits[3]
    cnt = jnp.where(chosen, 1.0, 0.0)
    before = jnp.dot(cnt.astype(BF16), tri[...], preferred_element_type=F32) + run[...]
    for r in range(TOP_K):
        idx_ref[0, r:r + 1, :] = idxs[r]
        prob_ref[0, r:r + 1, :] = es[r] * inv
        rank_ref[0, r:r + 1, :] = jnp.sum(jnp.where(hits[r], before, 0.0), axis=0,
                                          keepdims=True).astype(jnp.int32)
    run[...] = run[...] + jnp.sum(cnt, axis=1, keepdims=True)
    cnt_ref[...] = jnp.broadcast_to(run[...], cnt_ref.shape).astype(jnp.int32)


def _outproj(x, ya, yl, mod3, ga, w_out_b, gf, router_wt, router_b, ts=512):
    B, S, D = x.shape
    W = ya.shape[2]
    NE = router_wt.shape[0]
    rowd = pl.BlockSpec((1, ts, D), lambda b, s: (b, s, 0))
    roww = pl.BlockSpec((1, ts, W), lambda b, s: (b, s, 0))
    sel = pl.BlockSpec((1, TOP_K, ts), lambda b, s: (b, 0, s))
    return pl.pallas_call(
        _outproj_kernel,
        grid=(B, S // ts),
        in_specs=[rowd, roww, roww,
                  pl.BlockSpec((1, N_MOD, D), lambda b, s: (b, 0, 0)),
                  pl.BlockSpec((1, W), lambda b, s: (0, 0)),
                  pl.BlockSpec((2 * W, D), lambda b, s: (0, 0)),
                  pl.BlockSpec((1, D), lambda b, s: (0, 0)),
                  pl.BlockSpec((NE, D), lambda b, s: (0, 0)),
                  pl.BlockSpec((NE, 1), lambda b, s: (0, 0))],
        out_specs=[rowd, rowd, sel, sel, sel, pl.BlockSpec((NE, LANES), lambda b, s: (0, 0))],
        out_shape=[jax.ShapeDtypeStruct((B, S, D), F32), jax.ShapeDtypeStruct((B, S, D), F32),
                   jax.ShapeDtypeStruct((B, TOP_K, S), jnp.int32),
                   jax.ShapeDtypeStruct((B, TOP_K, S), F32),
                   jax.ShapeDtypeStruct((B, TOP_K, S), jnp.int32),
                   jax.ShapeDtypeStruct((NE, LANES), jnp.int32)],
        scratch_shapes=[pltpu.VMEM((ts, ts), BF16), pltpu.VMEM((NE, 1), F32)],
        compiler_params=_params(("arbitrary", "arbitrary")),
        name="outproj",
    )(x, ya, yl, mod3, ga.reshape(1, W), w_out_b, gf.reshape(1, D), router_wt,
      router_b.reshape(NE, 1))


def _dispatch_kernel(p0_ref, p1_ref, p2_ref, p3_ref, fill_ref, h_ref, xs_ref, zbuf, sem, zsem):
    TT = h_ref.shape[0]
    TM = zbuf.shape[0]
    n_fill = fill_ref.shape[0]

    @pl.when(pl.program_id(0) == 0)
    def _():
        zbuf[...] = jnp.zeros_like(zbuf)

        def fill_copy(e):
            start = pl.multiple_of(jnp.maximum(fill_ref[e], 0), 8)
            return pltpu.make_async_copy(zbuf, xs_ref.at[pl.ds(start, TM), :], zsem)

        def start(e, c):
            @pl.when(fill_ref[e] >= 0)
            def _():
                fill_copy(e).start()
            return c

        def wait(e, c):
            @pl.when(fill_ref[e] >= 0)
            def _():
                fill_copy(e).wait()
            return c

        lax.fori_loop(0, n_fill, start, 0)
        lax.fori_loop(0, n_fill, wait, 0)

    pos_refs = (p0_ref, p1_ref, p2_ref, p3_ref)

    def row_copy(t, r):
        return pltpu.make_async_copy(h_ref.at[pl.ds(t, 1), :],
                                     xs_ref.at[pl.ds(pos_refs[r][t], 1), :], sem)

    def issue(g, c):
        for tt in range(ROW_DMA_UNROLL):
            for r in range(TOP_K):
                row_copy(g * ROW_DMA_UNROLL + tt, r).start(priority=(tt * TOP_K + r) % 2)
        return c

    lax.fori_loop(0, TT // ROW_DMA_UNROLL, issue, 0)
    for r in range(TOP_K):
        pltpu.make_async_copy(h_ref, xs_ref.at[pl.ds(0, TT), :], sem).wait()


def _slot_specs(tiles_per_batch):
    def spec(r):
        return pl.BlockSpec(
            (TOKEN_TILE,),
            lambda i: ((i // tiles_per_batch * TOP_K + r) * tiles_per_batch
                       + i % tiles_per_batch,),
            memory_space=pltpu.SMEM)
    return [spec(r) for r in range(TOP_K)]


def _dispatch(h2f, pos_flat, fill_start, cap, tiles_per_batch):
    N, D = h2f.shape
    TT = TOKEN_TILE
    return pl.pallas_call(
        _dispatch_kernel,
        grid=(N // TT,),
        in_specs=_slot_specs(tiles_per_batch)
        + [pl.BlockSpec(memory_space=pltpu.SMEM), pl.BlockSpec((TT, D), lambda i: (i, 0))],
        out_specs=pl.BlockSpec(memory_space=pl.ANY),
        out_shape=jax.ShapeDtypeStruct((cap, D), F32),
        scratch_shapes=[pltpu.VMEM((EXPERT_BLOCK, D), F32), pltpu.SemaphoreType.DMA,
                        pltpu.SemaphoreType.DMA],
        compiler_params=_params(("arbitrary",)),
        name="dispatch",
    )(pos_flat, pos_flat, pos_flat, pos_flat, fill_start, h2f)


def _expert_kernel(be_ref, nu_ref, xs_ref, wg_ref, bg_ref, wu_ref, bu_ref, wd_ref, bd_ref,
                   ys_ref, wgb, wub, wdb):
    i = pl.program_id(0)

    @pl.when((i == 0) | (be_ref[i] != be_ref[jnp.maximum(i - 1, 0)]))
    def _():
        wgb[...] = wg_ref[0].astype(BF16)
        wub[...] = wu_ref[0].astype(BF16)
        wdb[...] = wd_ref[0].astype(BF16)

    @pl.when(i < nu_ref[0])
    def _():
        xb = xs_ref[...].astype(BF16)
        g = jnp.minimum(jnp.dot(xb, wgb[...], preferred_element_type=F32) + bg_ref[0],
                        SWIGLU_LIMIT)
        u = jnp.clip(jnp.dot(xb, wub[...], preferred_element_type=F32) + bu_ref[0],
                     -SWIGLU_LIMIT, SWIGLU_LIMIT)
        act = (u + 1.0) * (g * jax.nn.sigmoid(SWIGLU_ALPHA * g))
        ys_ref[...] = jnp.dot(act.astype(BF16), wdb[...],
                              preferred_element_type=F32) + bd_ref[0]

    @pl.when(i >= nu_ref[0])
    def _():
        ys_ref[...] = jnp.zeros_like(ys_ref)


def _experts(xs, block_e, n_used, wg, bg, wu, bu, wd, bd):
    cap, D = xs.shape
    NE, _, DE = wg.shape
    TM = EXPERT_BLOCK
    n_blocks = cap // TM

    def rows(i, be, nu):
        return (jnp.minimum(i, nu[0] - 1), 0)

    def wsel(i, be, nu):
        return (be[i], 0, 0)

    grid_spec = pltpu.PrefetchScalarGridSpec(
        num_scalar_prefetch=2,
        grid=(n_blocks,),
        in_specs=[pl.BlockSpec((TM, D), rows),
                  pl.BlockSpec((1, D, DE), wsel), pl.BlockSpec((1, 1, DE), wsel),
                  pl.BlockSpec((1, D, DE), wsel), pl.BlockSpec((1, 1, DE), wsel),
                  pl.BlockSpec((1, DE, D), wsel), pl.BlockSpec((1, 1, D), wsel)],
        out_specs=pl.BlockSpec((TM, D), lambda i, be, nu: (i, 0)),
        scratch_shapes=[pltpu.VMEM((D, DE), BF16), pltpu.VMEM((D, DE), BF16),
                        pltpu.VMEM((DE, D), BF16)],
    )
    return pl.pallas_call(
        _expert_kernel,
        grid_spec=grid_spec,
        out_shape=jax.ShapeDtypeStruct((cap, D), F32),
        compiler_params=_params(("arbitrary",)),
        name="experts",
    )(block_e, n_used, xs, wg, bg.reshape(NE, 1, DE), wu, bu.reshape(NE, 1, DE), wd,
      bd.reshape(NE, 1, D))


def _combine_kernel(c0_ref, c1_ref, c2_ref, c3_ref, n0_ref, n1_ref, n2_ref, n3_ref,
                    x1_ref, p_ref, mod_ref, g_ref, ys_ref, o_ref, buf, sem):
    TT = x1_ref.shape[0]
    i = pl.program_id(0)
    slot = i % 2

    def gather(pos_refs, s):
        def issue(g, c):
            for tt in range(ROW_DMA_UNROLL):
                t = g * ROW_DMA_UNROLL + tt
                for r in range(TOP_K):
                    pltpu.make_async_copy(
                        ys_ref.at[pl.ds(pos_refs[r][t], 1), :],
                        buf.at[s, r, pl.ds(t, 1), :],
                        sem.at[s]).start(priority=(tt * TOP_K + r) % 2)
            return c
        lax.fori_loop(0, TT // ROW_DMA_UNROLL, issue, 0)

    @pl.when(i == 0)
    def _():
        gather((c0_ref, c1_ref, c2_ref, c3_ref), 0)

    @pl.when(i + 1 < pl.num_programs(0))
    def _():
        gather((n0_ref, n1_ref, n2_ref, n3_ref), 1 - slot)

    for r in range(TOP_K):
        pltpu.make_async_copy(ys_ref.at[pl.ds(0, TT), :], buf.at[slot, r], sem.at[slot]).wait()

    p = p_ref[...]
    moe = p[:, 0:1] * buf[slot, 0]
    for r in range(1, TOP_K):
        moe = moe + p[:, r:r + 1] * buf[slot, r]
    x2 = x1_ref[...] + mod_ref[0, 5:6, :] * moe
    o_ref[...] = _rms(x2, g_ref[...])


def _combine(x1f, probs, pos_flat, mod3, g, ys, tiles_per_batch):
    N, D = x1f.shape
    TT = TOKEN_TILE
    n_tiles = N // TT

    def next_spec(r):
        def index(i):
            j = jnp.minimum(i + 1, n_tiles - 1)
            return ((j // tiles_per_batch * TOP_K + r) * tiles_per_batch + j % tiles_per_batch,)
        return pl.BlockSpec((TT,), index, memory_space=pltpu.SMEM)

    return pl.pallas_call(
        _combine_kernel,
        grid=(n_tiles,),
        in_specs=_slot_specs(tiles_per_batch) + [next_spec(r) for r in range(TOP_K)]
        + [pl.BlockSpec((TT, D), lambda i: (i, 0)),
           pl.BlockSpec((TT, TOP_K), lambda i: (i, 0)),
           pl.BlockSpec((1, N_MOD, D), lambda i: (i // tiles_per_batch, 0, 0)),
           pl.BlockSpec((1, D), lambda i: (0, 0)),
           pl.BlockSpec(memory_space=pl.ANY)],
        out_specs=pl.BlockSpec((TT, D), lambda i: (i, 0)),
        out_shape=jax.ShapeDtypeStruct((N, D), F32),
        scratch_shapes=[pltpu.VMEM((2, TOP_K, TT, D), F32), pltpu.SemaphoreType.DMA((2,))],
        compiler_params=_params(("arbitrary",)),
        name="combine",
    )(*([pos_flat] * (2 * TOP_K)), x1f, probs, mod3, g.reshape(1, D), ys)


def _routing(idx, rank, counts):
    B, K, S = idx.shape
    n_exp = counts.shape[0]
    TM = EXPERT_BLOCK
    n_assign = B * S * K
    padded = ((counts + TM - 1) // TM) * TM
    pad_ends = jnp.cumsum(padded)
    pad_starts = pad_ends - padded
    experts = jnp.arange(n_exp, dtype=jnp.int32).reshape(n_exp, 1, 1, 1)
    base = jnp.sum(jnp.where(idx[None] == experts, pad_starts.reshape(n_exp, 1, 1, 1), 0),
                   axis=0)
    pos = (base + rank).astype(jnp.int32).reshape(n_assign)
    n_blocks = n_assign // TM + n_exp
    block_start = jnp.arange(n_blocks, dtype=jnp.int32) * TM
    block_e = jnp.minimum(jnp.sum(pad_ends[:, None] <= block_start[None, :], axis=0),
                          n_exp - 1).astype(jnp.int32)
    n_used = (pad_ends[-1] // TM).astype(jnp.int32).reshape(1)
    tail_start = jnp.where(padded > 0, pad_ends - TM, -1)
    spare = pad_ends[-1] + jnp.arange(n_exp, dtype=jnp.int32) * TM
    fill_start = jnp.concatenate(
        [tail_start, jnp.where(spare < n_blocks * TM, spare, -1)]).astype(jnp.int32)
    return pos, block_e, n_used, fill_start, n_blocks * TM


def kernel(x, c, ada_w, ada_b, mix_norm_g, w_in, conv_w, conv_b, rg_w, rg_b, ig_w, ig_b,
           lru_lambda, attn_out_g, lru_out_g, w_out, ffn_norm_g, router_w, router_b,
           exp_w_gate, exp_b_gate, exp_w_up, exp_b_up, exp_w_down, exp_b_down, final_norm_g):
    B, S, D = x.shape
    depth = ada_w.shape[0]
    assert S % 512 == 0 and S % TOKEN_TILE == 0 and D % LANES == 0
    for l in range(depth):
        mod3 = _ada(c, ada_w[l], ada_b[l]).reshape(B, N_MOD, D)
        q, k, v, xr, gr = _inproj(x, mod3, mix_norm_g[l], w_in[l].astype(BF16))
        ya = _attention(q, k, v)
        yl = _lru(xr, gr, conv_w[l], conv_b[l], rg_w[l], rg_b[l], ig_w[l], ig_b[l],
                  lru_lambda[l], lru_out_g[l])
        x1, h2, idx, prob, rank, cnt = _outproj(
            x, ya, yl, mod3, attn_out_g[l], w_out[l].astype(BF16), ffn_norm_g[l],
            router_w[l].T, router_b[l])
        pos, block_e, n_used, fill_start, cap = _routing(idx, rank, cnt[:, 0])
        xs = _dispatch(h2.reshape(B * S, D), pos, fill_start, cap, S // TOKEN_TILE)
        ys = _experts(xs, block_e, n_used, exp_w_gate[l], exp_b_gate[l],
                      exp_w_up[l], exp_b_up[l], exp_w_down[l], exp_b_down[l])
        probs = jnp.transpose(prob, (0, 2, 1)).reshape(B * S, TOP_K)
        assert depth == 1
        x = _combine(x1.reshape(B * S, D), probs, pos, mod3, final_norm_g, ys,
                     S // TOKEN_TILE).reshape(B, S, D)
    return x
```

```python
import functools

import jax
import jax.numpy as jnp
from jax import lax
from jax.experimental import pallas as pl
from jax.experimental.pallas import tpu as pltpu

F32 = jnp.float32
BF16 = jnp.bfloat16
HIGHEST = lax.Precision.HIGHEST

EPS = 1e-6
N_MOD = 6
SB_HEADS = 8
HEAD_DIM = 64
SB_WIDTH = SB_HEADS * HEAD_DIM
LRU_BLOCKS = 8
CONV_WIDTH = 4
LRU_C = 8.0
TOP_K = 4
SWIGLU_LIMIT = 7.0
SWIGLU_ALPHA = 1.702
LOG2E = 1.4426950408889634

LANES = 128
SUBLANES = 8
VMEM_LIMIT = 48 * 1024 * 1024

Q_BLOCK = 128
K_BLOCK = 128
ATTN_WINDOW_BLOCKS = 4
ATTN_LOOP_BLOCKS = 4
ATTN_UNDERFLOW_LOG = -110.0
EXPERT_BLOCK = 256
TOKEN_TILE = 256
ROW_DMA_UNROLL = 8


def _params(sem):
    return pltpu.CompilerParams(dimension_semantics=sem, vmem_limit_bytes=VMEM_LIMIT)


def _ada_kernel(c_ref, w_ref, b_ref, o_ref):
    c = c_ref[...]
    ca = c * jax.nn.sigmoid(c)
    o_ref[...] = jnp.dot(ca, w_ref[...], precision=HIGHEST,
                         preferred_element_type=F32) + b_ref[...]


def _ada(c, ada_w, ada_b):
    B, D = c.shape
    E = ada_w.shape[1]
    tn = 1024
    return pl.pallas_call(
        _ada_kernel,
        grid=(E // tn,),
        in_specs=[pl.BlockSpec((B, D), lambda j: (0, 0)),
                  pl.BlockSpec((D, tn), lambda j: (0, j)),
                  pl.BlockSpec((1, tn), lambda j: (0, j))],
        out_specs=pl.BlockSpec((B, tn), lambda j: (0, j)),
        out_shape=jax.ShapeDtypeStruct((B, E), F32),
        compiler_params=_params(("arbitrary",)),
        name="ada",
    )(c, ada_w, ada_b.reshape(1, E))


def _rms(x, g):
    ms = jnp.mean(x * x, axis=-1, keepdims=True)
    return x * lax.rsqrt(ms + EPS) * g


def _inproj_kernel(x_ref, mod_ref, g_ref, w_ref, q_ref, k_ref, v_ref, xr_ref, gr_ref):
    x = x_ref[0]
    h = _rms(x, g_ref[...]) * (1.0 + mod_ref[0, 1:2, :]) + mod_ref[0, 0:1, :]
    hb = h.astype(BF16)
    W = SB_WIDTH

    def proj(c):
        return jnp.dot(hb, w_ref[:, c * W:(c + 1) * W], preferred_element_type=F32)

    q_ref[0] = (proj(0) * (HEAD_DIM ** -0.5)).astype(BF16)
    k_ref[0] = proj(1).astype(BF16)
    v_ref[0] = proj(2).astype(BF16)
    xr_ref[0] = proj(3)
    gr_ref[0] = proj(4)


def _inproj(x, mod3, g, w_in_b, ts=512):
    B, S, D = x.shape
    E = w_in_b.shape[1]
    W = SB_WIDTH
    row = pl.BlockSpec((1, ts, W), lambda b, s: (b, s, 0))
    return pl.pallas_call(
        _inproj_kernel,
        grid=(B, S // ts),
        in_specs=[pl.BlockSpec((1, ts, D), lambda b, s: (b, s, 0)),
                  pl.BlockSpec((1, N_MOD, D), lambda b, s: (b, 0, 0)),
                  pl.BlockSpec((1, D), lambda b, s: (0, 0)),
                  pl.BlockSpec((D, E), lambda b, s: (0, 0))],
        out_specs=[row, row, row, row, row],
        out_shape=[jax.ShapeDtypeStruct((B, S, W), BF16)] * 3
        + [jax.ShapeDtypeStruct((B, S, W), F32)] * 2,
        compiler_params=_params(("arbitrary", "arbitrary")),
        name="inproj",
    )(x, mod3, g.reshape(1, D), w_in_b)


def _attn_kernel(q_ref, k_ref, v_ref, o_ref, acc_ref, carry_ref):
    i = pl.program_id(1)
    QB, KB = Q_BLOCK, K_BLOCK
    n_pairs = q_ref.shape[2] // LANES
    lane = lax.broadcasted_iota(jnp.int32, (QB, LANES), 1)

    def stacked_q(p):
        q = q_ref[0, :, p * LANES:(p + 1) * LANES]
        zero = jnp.zeros_like(q)
        return jnp.concatenate([jnp.where(lane < HEAD_DIM, q, zero),
                                jnp.where(lane >= HEAD_DIM, q, zero)], axis=0)

    qs = [stacked_q(p) for p in range(n_pairs)]

    uj = lax.broadcasted_iota(jnp.int32, (2 * KB, KB + LANES), 0) & (KB - 1)
    us = lax.broadcasted_iota(jnp.int32, (2 * KB, KB + LANES), 1)
    u2 = jnp.where((us >= KB) | (uj > us), -1.0, 0.0).astype(BF16)

    row = lax.broadcasted_iota(jnp.int32, (2 * QB, KB), 0) & (QB - 1)
    col = lax.broadcasted_iota(jnp.int32, (2 * QB, KB), 1)
    causal = col < row

    def tiles(p, j_hi, n, diagonal, acc, carry):
        start = pl.multiple_of((j_hi - (n - 1)) * KB, KB)
        kw = k_ref[0, pl.ds(start, n * KB), p * LANES:(p + 1) * LANES]
        vw = v_ref[0, pl.ds(start, n * KB), p * LANES:(p + 1) * LANES]
        z = lax.dot_general(qs[p], kw, (((1,), (1,)), ((), ())), preferred_element_type=F32)
        softplus = jnp.maximum(z, 0.0) + jnp.log(1.0 + jnp.exp2(jnp.abs(z) * (-LOG2E)))
        log_beta = z - softplus
        ws = [None] * n
        for c in reversed(range(n)):
            sp = softplus[:, c * KB:(c + 1) * KB]
            masked = diagonal and c == n - 1
            if masked:
                sp = jnp.where(causal, sp, 0.0)
            hi_f = lax.bitcast_convert_type(
                lax.bitcast_convert_type(sp, jnp.uint32) & jnp.uint32(0xFFFF0000), F32)
            hi_lo = jnp.concatenate([hi_f.astype(BF16), (sp - hi_f).astype(BF16)], axis=1)
            sums = jnp.dot(hi_lo, u2, preferred_element_type=F32)
            w = jnp.exp2((log_beta[:, c * KB:(c + 1) * KB] + sums[:, :KB] + carry) * LOG2E)
            if masked:
                w = jnp.where(causal, w, 0.0)
            ws[c] = w.astype(BF16)
            carry = carry + sums[:, KB:]
        w_all = ws[0] if n == 1 else jnp.concatenate(ws, axis=1)
        acc = acc + jnp.dot(w_all, vw, preferred_element_type=F32)
        return acc, carry

    def fold(j_hi, n, diagonal):
        cmax = None
        for p in range(n_pairs):
            if diagonal:
                acc = carry = jnp.zeros((2 * QB, LANES), F32)
            else:
                acc, carry = acc_ref[p], carry_ref[p]
            acc, carry = tiles(p, j_hi, n, diagonal, acc, carry)
            acc_ref[p] = acc
            carry_ref[p] = carry
            cmax = carry if cmax is None else jnp.maximum(cmax, carry)
        return jnp.max(cmax)

    n_first, n_loop = ATTN_WINDOW_BLOCKS, ATTN_LOOP_BLOCKS
    cmax0 = lax.cond(i >= n_first - 1,
                     lambda: fold(i, n_first, True),
                     lambda: fold(i, 1, True))
    j0 = jnp.where(i >= n_first - 1, i - n_first, i - 1)

    def loop(n, j, cmax):
        def more(st):
            return (st[0] >= n - 1) & (st[1] >= ATTN_UNDERFLOW_LOG)

        def step(st):
            return st[0] - n, fold(st[0], n, False)

        return lax.while_loop(more, step, (j, cmax))

    j, cmax = j0, cmax0
    while n_loop >= 1:
        j, cmax = loop(n_loop, j, cmax)
        n_loop //= 2
    for p in range(n_pairs):
        acc = acc_ref[p]
        o_ref[0, :, p * LANES:(p + 1) * LANES] = jnp.where(lane < HEAD_DIM, acc[:QB], acc[QB:])


def _attention(q, k, v):
    B, S, W = q.shape
    return pl.pallas_call(
        _attn_kernel,
        grid=(B, S // Q_BLOCK),
        in_specs=[pl.BlockSpec((1, Q_BLOCK, W), lambda b, i: (b, i, 0)),
                  pl.BlockSpec((1, S, W), lambda b, i: (b, 0, 0)),
                  pl.BlockSpec((1, S, W), lambda b, i: (b, 0, 0))],
        out_specs=pl.BlockSpec((1, Q_BLOCK, W), lambda b, i: (b, i, 0)),
        out_shape=jax.ShapeDtypeStruct((B, S, W), F32),
        scratch_shapes=[pltpu.VMEM((W // LANES, 2 * Q_BLOCK, LANES), F32),
                        pltpu.VMEM((W // LANES, 2 * Q_BLOCK, LANES), F32)],
        compiler_params=_params(("arbitrary", "arbitrary")),
        name="attn",
    )(q, k, v)


def _softplus(x):
    return jnp.maximum(x, 0.0) + jnp.log1p(jnp.exp(-jnp.abs(x)))


def _gelu_tanh(x):
    return 0.5 * x * (1.0 + jnp.tanh(0.7978845608028654 * (x + 0.044715 * x * x * x)))


def _lru_kernel(xr_ref, gr_ref, cw_ref, cb_ref, wr_ref, br_ref, wi_ref, bi_ref, lam_ref,
                g_ref, o_ref, xext, hc):
    TS = xr_ref.shape[1]
    PAD = 8

    @pl.when(pl.program_id(1) == 0)
    def _():
        xext[0:PAD, :] = jnp.zeros((PAD, xext.shape[1]), F32)
        hc[...] = jnp.zeros_like(hc)

    x = xr_ref[0]
    xext[PAD:PAD + TS, :] = x
    xc = cb_ref[...] + cw_ref[CONV_WIDTH - 1:CONV_WIDTH, :] * x
    for j in range(CONV_WIDTH - 1):
        back = CONV_WIDTH - 1 - j
        xc = xc + cw_ref[j:j + 1, :] * xext[PAD - back:PAD - back + TS, :]
    xext[0:PAD, :] = xext[TS:TS + PAD, :]

    xb = xc.astype(BF16)
    r = jax.nn.sigmoid(jnp.dot(xb, wr_ref[...], preferred_element_type=F32) + br_ref[...])
    ig = jax.nn.sigmoid(jnp.dot(xb, wi_ref[...], preferred_element_type=F32) + bi_ref[...])
    log_a = (-LRU_C) * r * _softplus(-lam_ref[...])
    a = jnp.exp(log_a)
    b = jnp.sqrt(-jnp.tanh(log_a) * (a * a + 1.0)) * (ig * xc)

    rows = lax.broadcasted_iota(jnp.int32, a.shape, 0)
    d = 1
    while d < TS:
        keep = rows >= d
        a_prev = jnp.where(keep, pltpu.roll(a, d, 0), 1.0)
        b_prev = jnp.where(keep, pltpu.roll(b, d, 0), 0.0)
        b = a * b_prev + b
        a = a * a_prev
        d *= 2
    h = a * hc[...] + b
    hc[...] = h[TS - 1:TS, :]

    y = h * _gelu_tanh(gr_ref[0])
    o_ref[0] = _rms(y, g_ref[...])


def _block_diag(w):
    H, I, J = w.shape
    eye = jnp.eye(H, dtype=w.dtype)
    return (w[:, :, None, :] * eye[:, None, :, None]).reshape(H * I, H * J)


def _lru(xr, gr, conv_w, conv_b, rg_w, rg_b, ig_w, ig_b, lam, g, ts=256):
    B, S, W = xr.shape
    row = pl.BlockSpec((1, ts, W), lambda b, s: (b, s, 0))
    vec = pl.BlockSpec((1, W), lambda b, s: (0, 0))
    mat = pl.BlockSpec((W, W), lambda b, s: (0, 0))
    return pl.pallas_call(
        _lru_kernel,
        grid=(B, S // ts),
        in_specs=[row, row, pl.BlockSpec((CONV_WIDTH, W), lambda b, s: (0, 0)), vec,
                  mat, vec, mat, vec, vec, vec],
        out_specs=row,
        out_shape=jax.ShapeDtypeStruct((B, S, W), F32),
        scratch_shapes=[pltpu.VMEM((ts + 8, W), F32), pltpu.VMEM((1, W), F32)],
        compiler_params=_params(("arbitrary", "arbitrary")),
        name="lru",
    )(xr, gr, conv_w, conv_b.reshape(1, W), _block_diag(rg_w).astype(BF16), rg_b.reshape(1, W),
      _block_diag(ig_w).astype(BF16), ig_b.reshape(1, W), lam.reshape(1, W), g.reshape(1, W))


def _load_row_tiles(ref, n_rows, lead=()):
    chunks = [ref[lead + (pl.ds(c, n_rows, stride=SUBLANES), slice(None))]
              for c in range(SUBLANES)]
    return jnp.concatenate(chunks, axis=1)


def _store_row_tiles(ref, value):
    n_rows = value.shape[0]
    for c in range(SUBLANES):
        ref[pl.ds(c, n_rows, stride=SUBLANES), :] = value[:, c * LANES:(c + 1) * LANES]


def _outproj_kernel(x_ref, ya_ref, yl_ref, mod_ref, ga_ref, wo_ref, gf_ref, rw_ref, rb_ref,
                    x1_ref, h2_ref, idx_ref, prob_ref, rank_ref, cnt_ref, tri, run):
    W = SB_WIDTH
    ya = _rms(ya_ref[0], ga_ref[...]).astype(BF16)
    yl = yl_ref[0].astype(BF16)
    mix = (jnp.dot(ya, wo_ref[0:W, :], preferred_element_type=F32)
           + jnp.dot(yl, wo_ref[W:2 * W, :], preferred_element_type=F32))
    x1 = x_ref[0] + mod_ref[0, 2:3, :] * mix
    x1_ref[0] = x1
    h2 = _rms(x1, gf_ref[...]) * (1.0 + mod_ref[0, 4:5, :]) + mod_ref[0, 3:4, :]
    _store_row_tiles(h2_ref, h2)

    logits = lax.dot_general(rw_ref[...], h2, (((1,), (1,)), ((), ())), precision=HIGHEST,
                             preferred_element_type=F32) + rb_ref[...]
    n_exp = logits.shape[0]
    eid = lax.broadcasted_iota(jnp.int32, logits.shape, 0)
    vals, idxs = [], []
    for _ in range(TOP_K):
        m = jnp.max(logits, axis=0, keepdims=True)
        sel = jnp.min(jnp.where(logits == m, eid, n_exp), axis=0, keepdims=True)
        vals.append(m)
        idxs.append(sel)
        logits = jnp.where(eid == sel, -jnp.inf, logits)
    es = [jnp.exp(vv - vals[0]) for vv in vals]
    inv = 1.0 / (es[0] + es[1] + es[2] + es[3])

    first = (pl.program_id(0) == 0) & (pl.program_id(1) == 0)

    @pl.when(first)
    def _():
        ti = lax.broadcasted_iota(jnp.int32, tri.shape, 0)
        tj = lax.broadcasted_iota(jnp.int32, tri.shape, 1)
        tri[...] = jnp.where(ti < tj, 1.0, 0.0).astype(BF16)
        run[...] = jnp.zeros_like(run)

    hits = [eid == idxs[r] for r in range(TOP_K)]
    chosen = hits[0] | hits[1] | hits[2] | hits[3]
    cnt = jnp.where(chosen, 1.0, 0.0)
    before = jnp.dot(cnt.astype(BF16), tri[...], preferred_element_type=F32) + run[...]
    for r in range(TOP_K):
        idx_ref[0, r:r + 1, :] = idxs[r]
        prob_ref[0, r:r + 1, :] = es[r] * inv
        rank_ref[0, r:r + 1, :] = jnp.sum(jnp.where(hits[r], before, 0.0), axis=0,
                                          keepdims=True).astype(jnp.int32)
    run[...] = run[...] + jnp.sum(cnt, axis=1, keepdims=True)
    cnt_ref[...] = jnp.broadcast_to(run[...], cnt_ref.shape).astype(jnp.int32)


def _outproj(x, ya, yl, mod3, ga, w_out_b, gf, router_wt, router_b, ts=512):
    B, S, D = x.shape
    W = ya.shape[2]
    NE = router_wt.shape[0]
    rowd = pl.BlockSpec((1, ts, D), lambda b, s: (b, s, 0))
    roww = pl.BlockSpec((1, ts, W), lambda b, s: (b, s, 0))
    sel = pl.BlockSpec((1, TOP_K, ts), lambda b, s: (b, 0, s))
    return pl.pallas_call(
        _outproj_kernel,
        grid=(B, S // ts),
        in_specs=[rowd, roww, roww,
                  pl.BlockSpec((1, N_MOD, D), lambda b, s: (b, 0, 0)),
                  pl.BlockSpec((1, W), lambda b, s: (0, 0)),
                  pl.BlockSpec((2 * W, D), lambda b, s: (0, 0)),
                  pl.BlockSpec((1, D), lambda b, s: (0, 0)),
                  pl.BlockSpec((NE, D), lambda b, s: (0, 0)),
                  pl.BlockSpec((NE, 1), lambda b, s: (0, 0))],
        out_specs=[rowd,
                   pl.BlockSpec((ts * SUBLANES, LANES), lambda b, s: (b * (S // ts) + s, 0)),
                   sel, sel, sel, pl.BlockSpec((NE, LANES), lambda b, s: (0, 0))],
        out_shape=[jax.ShapeDtypeStruct((B, S, D), F32),
                   jax.ShapeDtypeStruct((B * S * SUBLANES, LANES), F32),
                   jax.ShapeDtypeStruct((B, TOP_K, S), jnp.int32),
                   jax.ShapeDtypeStruct((B, TOP_K, S), F32),
                   jax.ShapeDtypeStruct((B, TOP_K, S), jnp.int32),
                   jax.ShapeDtypeStruct((NE, LANES), jnp.int32)],
        scratch_shapes=[pltpu.VMEM((ts, ts), BF16), pltpu.VMEM((NE, 1), F32)],
        compiler_params=_params(("arbitrary", "arbitrary")),
        name="outproj",
    )(x, ya, yl, mod3, ga.reshape(1, W), w_out_b, gf.reshape(1, D), router_wt,
      router_b.reshape(NE, 1))


def _dispatch_kernel(p0_ref, p1_ref, p2_ref, p3_ref, fill_ref, h_ref, xs_ref, zbuf, sem, zsem):
    TT = h_ref.shape[0] // SUBLANES
    n_fill = fill_ref.shape[0]

    @pl.when(pl.program_id(0) == 0)
    def _():
        zbuf[...] = jnp.zeros_like(zbuf)

        def fill_copy(e):
            start = pl.multiple_of(jnp.maximum(fill_ref[e], 0), SUBLANES)
            return pltpu.make_async_copy(zbuf, xs_ref.at[pl.ds(start, zbuf.shape[0]), :], zsem)

        def start(e, c):
            @pl.when(fill_ref[e] >= 0)
            def _():
                fill_copy(e).start()
            return c

        def wait(e, c):
            @pl.when(fill_ref[e] >= 0)
            def _():
                fill_copy(e).wait()
            return c

        lax.fori_loop(0, n_fill, start, 0)
        lax.fori_loop(0, n_fill, wait, 0)

    pos_refs = (p0_ref, p1_ref, p2_ref, p3_ref)

    def row_copy(t, r):
        src = pl.multiple_of(t * SUBLANES, SUBLANES)
        dst = pl.multiple_of(pos_refs[r][t], SUBLANES)
        return pltpu.make_async_copy(h_ref.at[pl.ds(src, SUBLANES), :],
                                     xs_ref.at[pl.ds(dst, SUBLANES), :], sem)

    def issue(g, c):
        for tt in range(ROW_DMA_UNROLL):
            for r in range(TOP_K):
                row_copy(g * ROW_DMA_UNROLL + tt, r).start(priority=(tt * TOP_K + r) % 2)
        return c

    lax.fori_loop(0, TT // ROW_DMA_UNROLL, issue, 0)
    for r in range(TOP_K):
        pltpu.make_async_copy(h_ref, xs_ref.at[pl.ds(0, TT * SUBLANES), :], sem).wait()


def _slot_specs(tiles_per_batch):
    def spec(r):
        return pl.BlockSpec(
            (TOKEN_TILE,),
            lambda i: ((i // tiles_per_batch * TOP_K + r) * tiles_per_batch
                       + i % tiles_per_batch,),
            memory_space=pltpu.SMEM)
    return [spec(r) for r in range(TOP_K)]


def _dispatch(h2t, pos_flat, fill_start, cap, tiles_per_batch):
    TT = TOKEN_TILE
    n_tiles = h2t.shape[0] // (TT * SUBLANES)
    return pl.pallas_call(
        _dispatch_kernel,
        grid=(n_tiles,),
        in_specs=_slot_specs(tiles_per_batch)
        + [pl.BlockSpec(memory_space=pltpu.SMEM),
           pl.BlockSpec((TT * SUBLANES, LANES), lambda i: (i, 0))],
        out_specs=pl.BlockSpec(memory_space=pl.ANY),
        out_shape=jax.ShapeDtypeStruct((cap * SUBLANES, LANES), F32),
        scratch_shapes=[pltpu.VMEM((EXPERT_BLOCK * SUBLANES, LANES), F32),
                        pltpu.SemaphoreType.DMA, pltpu.SemaphoreType.DMA],
        compiler_params=_params(("arbitrary",)),
        name="dispatch",
    )(pos_flat, pos_flat, pos_flat, pos_flat, fill_start, h2t)


def _expert_kernel(be_ref, nu_ref, xs_ref, wg_ref, bg_ref, wu_ref, bu_ref, wd_ref, bd_ref,
                   ys_ref, wgb, wub, wdb):
    i = pl.program_id(0)

    @pl.when((i == 0) | (be_ref[i] != be_ref[jnp.maximum(i - 1, 0)]))
    def _():
        wgb[...] = wg_ref[0].astype(BF16)
        wub[...] = wu_ref[0].astype(BF16)
        wdb[...] = wd_ref[0].astype(BF16)

    @pl.when(i < nu_ref[0])
    def _():
        xb = _load_row_tiles(xs_ref, EXPERT_BLOCK).astype(BF16)
        g = jnp.minimum(jnp.dot(xb, wgb[...], preferred_element_type=F32) + bg_ref[0],
                        SWIGLU_LIMIT)
        u = jnp.clip(jnp.dot(xb, wub[...], preferred_element_type=F32) + bu_ref[0],
                     -SWIGLU_LIMIT, SWIGLU_LIMIT)
        act = (u + 1.0) * (g * jax.nn.sigmoid(SWIGLU_ALPHA * g))
        _store_row_tiles(ys_ref, jnp.dot(act.astype(BF16), wdb[...],
                                         preferred_element_type=F32) + bd_ref[0])

    @pl.when(i >= nu_ref[0])
    def _():
        ys_ref[...] = jnp.zeros_like(ys_ref)


def _experts(xs, block_e, n_used, wg, bg, wu, bu, wd, bd):
    NE, D, DE = wg.shape
    TM = EXPERT_BLOCK
    n_blocks = xs.shape[0] // (TM * SUBLANES)
    tile_rows = TM * SUBLANES

    def rows(i, be, nu):
        return (jnp.minimum(i, nu[0] - 1), 0)

    def wsel(i, be, nu):
        return (be[i], 0, 0)

    grid_spec = pltpu.PrefetchScalarGridSpec(
        num_scalar_prefetch=2,
        grid=(n_blocks,),
        in_specs=[pl.BlockSpec((tile_rows, LANES), rows),
                  pl.BlockSpec((1, D, DE), wsel), pl.BlockSpec((1, 1, DE), wsel),
                  pl.BlockSpec((1, D, DE), wsel), pl.BlockSpec((1, 1, DE), wsel),
                  pl.BlockSpec((1, DE, D), wsel), pl.BlockSpec((1, 1, D), wsel)],
        out_specs=pl.BlockSpec((tile_rows, LANES), lambda i, be, nu: (i, 0)),
        scratch_shapes=[pltpu.VMEM((D, DE), BF16), pltpu.VMEM((D, DE), BF16),
                        pltpu.VMEM((DE, D), BF16)],
    )
    return pl.pallas_call(
        _expert_kernel,
        grid_spec=grid_spec,
        out_shape=jax.ShapeDtypeStruct(xs.shape, F32),
        compiler_params=_params(("arbitrary",)),
        name="experts",
    )(block_e, n_used, xs, wg, bg.reshape(NE, 1, DE), wu, bu.reshape(NE, 1, DE), wd,
      bd.reshape(NE, 1, D))


def _combine_kernel(c0_ref, c1_ref, c2_ref, c3_ref, n0_ref, n1_ref, n2_ref, n3_ref,
                    x1_ref, p_ref, mod_ref, g_ref, ys_ref, o_ref, buf, sem):
    TT = x1_ref.shape[0]
    i = pl.program_id(0)

    def gather(pos_refs, s):
        def issue(g, c):
            for tt in range(ROW_DMA_UNROLL):
                t = g * ROW_DMA_UNROLL + tt
                for r in range(TOP_K):
                    pltpu.make_async_copy(
                        ys_ref.at[pl.ds(pl.multiple_of(pos_refs[r][t], SUBLANES), SUBLANES), :],
                        buf.at[s, r, pl.ds(pl.multiple_of(t * SUBLANES, SUBLANES), SUBLANES), :],
                        sem.at[s]).start(priority=(tt * TOP_K + r) % 2)
            return c
        lax.fori_loop(0, TT // ROW_DMA_UNROLL, issue, 0)

    @pl.when(i == 0)
    def _():
        gather((c0_ref, c1_ref, c2_ref, c3_ref), 0)

    def reduce_tile(s):
        @pl.when(i + 1 < pl.num_programs(0))
        def _():
            gather((n0_ref, n1_ref, n2_ref, n3_ref), 1 - s)

        for r in range(TOP_K):
            pltpu.make_async_copy(ys_ref.at[pl.ds(0, TT * SUBLANES), :], buf.at[s, r],
                                  sem.at[s]).wait()
        p = p_ref[...]
        moe = p[:, 0:1] * _load_row_tiles(buf, TT, (s, 0))
        for r in range(1, TOP_K):
            moe = moe + p[:, r:r + 1] * _load_row_tiles(buf, TT, (s, r))
        x2 = x1_ref[...] + mod_ref[0, 5:6, :] * moe
        o_ref[...] = _rms(x2, g_ref[...])

    for s in range(2):
        pl.when(i % 2 == s)(functools.partial(reduce_tile, s))


def _combine(x1f, probs, pos_flat, mod3, g, ys, tiles_per_batch):
    N, D = x1f.shape
    TT = TOKEN_TILE
    n_tiles = N // TT

    def next_spec(r):
        def index(i):
            j = jnp.minimum(i + 1, n_tiles - 1)
            return ((j // tiles_per_batch * TOP_K + r) * tiles_per_batch + j % tiles_per_batch,)
        return pl.BlockSpec((TT,), index, memory_space=pltpu.SMEM)

    return pl.pallas_call(
        _combine_kernel,
        grid=(n_tiles,),
        in_specs=_slot_specs(tiles_per_batch) + [next_spec(r) for r in range(TOP_K)]
        + [pl.BlockSpec((TT, D), lambda i: (i, 0)),
           pl.BlockSpec((TT, TOP_K), lambda i: (i, 0)),
           pl.BlockSpec((1, N_MOD, D), lambda i: (i // tiles_per_batch, 0, 0)),
           pl.BlockSpec((1, D), lambda i: (0, 0)),
           pl.BlockSpec(memory_space=pl.ANY)],
        out_specs=pl.BlockSpec((TT, D), lambda i: (i, 0)),
        out_shape=jax.ShapeDtypeStruct((N, D), F32),
        scratch_shapes=[pltpu.VMEM((2, TOP_K, TT * SUBLANES, LANES), F32),
                        pltpu.SemaphoreType.DMA((2,))],
        compiler_params=_params(("arbitrary",)),
        name="combine",
    )(*([pos_flat] * (2 * TOP_K)), x1f, probs, mod3, g.reshape(1, D), ys)


def _routing(idx, rank, counts):
    B, K, S = idx.shape
    n_exp = counts.shape[0]
    TM = EXPERT_BLOCK
    n_assign = B * S * K
    padded = ((counts + TM - 1) // TM) * TM
    pad_ends = jnp.cumsum(padded)
    pad_starts = pad_ends - padded
    experts = jnp.arange(n_exp, dtype=jnp.int32).reshape(n_exp, 1, 1, 1)
    base = jnp.sum(jnp.where(idx[None] == experts, pad_starts.reshape(n_exp, 1, 1, 1), 0),
                   axis=0)
    pos = ((base + rank) * SUBLANES).astype(jnp.int32).reshape(n_assign)
    n_blocks = n_assign // TM + n_exp
    block_start = jnp.arange(n_blocks, dtype=jnp.int32) * TM
    block_e = jnp.minimum(jnp.sum(pad_ends[:, None] <= block_start[None, :], axis=0),
                          n_exp - 1).astype(jnp.int32)
    n_used = (pad_ends[-1] // TM).astype(jnp.int32).reshape(1)
    tail_start = jnp.where(padded > 0, pad_ends - TM, -1)
    spare = pad_ends[-1] + jnp.arange(n_exp, dtype=jnp.int32) * TM
    fill_start = jnp.concatenate(
        [tail_start, jnp.where(spare < n_blocks * TM, spare, -1)]).astype(jnp.int32)
    return pos, block_e, n_used, fill_start * SUBLANES, n_blocks * TM


def kernel(x, c, ada_w, ada_b, mix_norm_g, w_in, conv_w, conv_b, rg_w, rg_b, ig_w, ig_b,
           lru_lambda, attn_out_g, lru_out_g, w_out, ffn_norm_g, router_w, router_b,
           exp_w_gate, exp_b_gate, exp_w_up, exp_b_up, exp_w_down, exp_b_down, final_norm_g):
    B, S, D = x.shape
    depth = ada_w.shape[0]
    assert S % 512 == 0 and S % TOKEN_TILE == 0 and D == SUBLANES * LANES
    for l in range(depth):
        mod3 = _ada(c, ada_w[l], ada_b[l]).reshape(B, N_MOD, D)
        q, k, v, xr, gr = _inproj(x, mod3, mix_norm_g[l], w_in[l].astype(BF16))
        ya = _attention(q, k, v)
        yl = _lru(xr, gr, conv_w[l], conv_b[l], rg_w[l], rg_b[l], ig_w[l], ig_b[l],
                  lru_lambda[l], lru_out_g[l])
        x1, h2, idx, prob, rank, cnt = _outproj(
            x, ya, yl, mod3, attn_out_g[l], w_out[l].astype(BF16), ffn_norm_g[l],
            router_w[l].T, router_b[l])
        pos, block_e, n_used, fill_start, cap = _routing(idx, rank, cnt[:, 0])
        xs = _dispatch(h2, pos, fill_start, cap, S // TOKEN_TILE)
        ys = _experts(xs, block_e, n_used, exp_w_gate[l], exp_b_gate[l],
                      exp_w_up[l], exp_b_up[l], exp_w_down[l], exp_b_down[l])
        probs = jnp.transpose(prob, (0, 2, 1)).reshape(B * S, TOP_K)
        assert depth == 1
        x = _combine(x1.reshape(B * S, D), probs, pos, mod3, final_norm_g, ys,
                     S // TOKEN_TILE).reshape(B, S, D)
    return x
```

```python
import functools

import jax
import jax.numpy as jnp
from jax import lax
from jax.experimental import pallas as pl
from jax.experimental.pallas import tpu as pltpu

F32 = jnp.float32
BF16 = jnp.bfloat16
HIGHEST = lax.Precision.HIGHEST

EPS = 1e-6
N_MOD = 6
SB_HEADS = 8
HEAD_DIM = 64
SB_WIDTH = SB_HEADS * HEAD_DIM
LRU_BLOCKS = 8
CONV_WIDTH = 4
LRU_C = 8.0
TOP_K = 4
SWIGLU_LIMIT = 7.0
SWIGLU_ALPHA = 1.702
LOG2E = 1.4426950408889634

LANES = 128
SUBLANES = 8
VMEM_LIMIT = 48 * 1024 * 1024

Q_BLOCK = 128
K_BLOCK = 128
ATTN_WINDOW_BLOCKS = 5
ATTN_LOOP_BLOCKS = 2
ATTN_UNDERFLOW_LOG = -110.0
EXPERT_BLOCK = 256
TOKEN_TILE = 256
ROW_DMA_UNROLL = 8


def _params(sem):
    return pltpu.CompilerParams(dimension_semantics=sem, vmem_limit_bytes=VMEM_LIMIT)


def _ada_kernel(c_ref, w_ref, b_ref, o_ref):
    c = c_ref[...]
    ca = c * jax.nn.sigmoid(c)
    o_ref[...] = jnp.dot(ca, w_ref[...], precision=HIGHEST,
                         preferred_element_type=F32) + b_ref[...]


def _ada(c, ada_w, ada_b):
    B, D = c.shape
    E = ada_w.shape[1]
    tn = 1024
    return pl.pallas_call(
        _ada_kernel,
        grid=(E // tn,),
        in_specs=[pl.BlockSpec((B, D), lambda j: (0, 0)),
                  pl.BlockSpec((D, tn), lambda j: (0, j)),
                  pl.BlockSpec((1, tn), lambda j: (0, j))],
        out_specs=pl.BlockSpec((B, tn), lambda j: (0, j)),
        out_shape=jax.ShapeDtypeStruct((B, E), F32),
        compiler_params=_params(("arbitrary",)),
        name="ada",
    )(c, ada_w, ada_b.reshape(1, E))


def _rms(x, g):
    ms = jnp.mean(x * x, axis=-1, keepdims=True)
    return x * lax.rsqrt(ms + EPS) * g


def _inproj_kernel(x_ref, mod_ref, g_ref, w_ref, q_ref, k_ref, v_ref, xr_ref, gr_ref):
    x = x_ref[0]
    h = _rms(x, g_ref[...]) * (1.0 + mod_ref[0, 1:2, :]) + mod_ref[0, 0:1, :]
    hb = h.astype(BF16)
    W = SB_WIDTH

    def proj(c):
        return jnp.dot(hb, w_ref[:, c * W:(c + 1) * W], preferred_element_type=F32)

    q_ref[0] = (proj(0) * (HEAD_DIM ** -0.5)).astype(BF16)
    k_ref[0] = proj(1).astype(BF16)
    v_ref[0] = proj(2).astype(BF16)
    xr_ref[0] = proj(3)
    gr_ref[0] = proj(4)


def _inproj(x, mod3, g, w_in_b, ts=512):
    B, S, D = x.shape
    E = w_in_b.shape[1]
    W = SB_WIDTH
    row = pl.BlockSpec((1, ts, W), lambda b, s: (b, s, 0))
    return pl.pallas_call(
        _inproj_kernel,
        grid=(B, S // ts),
        in_specs=[pl.BlockSpec((1, ts, D), lambda b, s: (b, s, 0)),
                  pl.BlockSpec((1, N_MOD, D), lambda b, s: (b, 0, 0)),
                  pl.BlockSpec((1, D), lambda b, s: (0, 0)),
                  pl.BlockSpec((D, E), lambda b, s: (0, 0))],
        out_specs=[row, row, row, row, row],
        out_shape=[jax.ShapeDtypeStruct((B, S, W), BF16)] * 3
        + [jax.ShapeDtypeStruct((B, S, W), F32)] * 2,
        compiler_params=_params(("arbitrary", "arbitrary")),
        name="inproj",
    )(x, mod3, g.reshape(1, D), w_in_b)


def _attn_kernel(q_ref, k_ref, v_ref, o_ref, acc_ref, carry_ref):
    i = pl.program_id(1)
    QB, KB = Q_BLOCK, K_BLOCK
    n_pairs = q_ref.shape[2] // LANES
    lane = lax.broadcasted_iota(jnp.int32, (QB, LANES), 1)

    def stacked_q(p):
        q = q_ref[0, :, p * LANES:(p + 1) * LANES]
        zero = jnp.zeros_like(q)
        return jnp.concatenate([jnp.where(lane < HEAD_DIM, q, zero),
                                jnp.where(lane >= HEAD_DIM, q, zero)], axis=0)

    qs = [stacked_q(p) for p in range(n_pairs)]

    uj = lax.broadcasted_iota(jnp.int32, (2 * KB, KB + LANES), 0) & (KB - 1)
    us = lax.broadcasted_iota(jnp.int32, (2 * KB, KB + LANES), 1)
    u2 = jnp.where((us >= KB) | (uj > us), -1.0, 0.0).astype(BF16)

    row = lax.broadcasted_iota(jnp.int32, (2 * QB, KB), 0) & (QB - 1)
    col = lax.broadcasted_iota(jnp.int32, (2 * QB, KB), 1)
    causal = col < row

    def tiles(p, j_hi, n, diagonal, acc, carry):
        start = pl.multiple_of((j_hi - (n - 1)) * KB, KB)
        kw = k_ref[0, pl.ds(start, n * KB), p * LANES:(p + 1) * LANES]
        vw = v_ref[0, pl.ds(start, n * KB), p * LANES:(p + 1) * LANES]
        z = lax.dot_general(qs[p], kw, (((1,), (1,)), ((), ())), preferred_element_type=F32)
        softplus = jnp.maximum(z, 0.0) + jnp.log(1.0 + jnp.exp2(jnp.abs(z) * (-LOG2E)))
        log_beta = z - softplus
        ws = [None] * n
        for c in reversed(range(n)):
            sp = softplus[:, c * KB:(c + 1) * KB]
            masked = diagonal and c == n - 1
            if masked:
                sp = jnp.where(causal, sp, 0.0)
            hi_f = lax.bitcast_convert_type(
                lax.bitcast_convert_type(sp, jnp.uint32) & jnp.uint32(0xFFFF0000), F32)
            hi_lo = jnp.concatenate([hi_f.astype(BF16), (sp - hi_f).astype(BF16)], axis=1)
            sums = jnp.dot(hi_lo, u2, preferred_element_type=F32)
            w = jnp.exp2((log_beta[:, c * KB:(c + 1) * KB] + sums[:, :KB] + carry) * LOG2E)
            if masked:
                w = jnp.where(causal, w, 0.0)
            ws[c] = w.astype(BF16)
            carry = carry + sums[:, KB:]
        w_all = ws[0] if n == 1 else jnp.concatenate(ws, axis=1)
        acc = acc + jnp.dot(w_all, vw, preferred_element_type=F32)
        return acc, carry

    def fold(j_hi, n, diagonal):
        cmax = None
        for p in range(n_pairs):
            if diagonal:
                acc = carry = jnp.zeros((2 * QB, LANES), F32)
            else:
                acc, carry = acc_ref[p], carry_ref[p]
            acc, carry = tiles(p, j_hi, n, diagonal, acc, carry)
            acc_ref[p] = acc
            carry_ref[p] = carry
            cmax = carry if cmax is None else jnp.maximum(cmax, carry)
        return jnp.max(cmax)

    n_first, n_loop = ATTN_WINDOW_BLOCKS, ATTN_LOOP_BLOCKS
    n0 = jnp.minimum(i + 1, n_first)
    cmax0 = lax.switch(n0 - 1, [functools.partial(fold, i, n, True)
                                for n in range(1, n_first + 1)])
    j0 = i - n0

    def loop(n, j, cmax):
        def more(st):
            return (st[0] >= n - 1) & (st[1] >= ATTN_UNDERFLOW_LOG)

        def step(st):
            return st[0] - n, fold(st[0], n, False)

        return lax.while_loop(more, step, (j, cmax))

    j, cmax = j0, cmax0
    while n_loop >= 1:
        j, cmax = loop(n_loop, j, cmax)
        n_loop //= 2
    for p in range(n_pairs):
        acc = acc_ref[p]
        o_ref[0, :, p * LANES:(p + 1) * LANES] = jnp.where(lane < HEAD_DIM, acc[:QB], acc[QB:])


def _attention(q, k, v):
    B, S, W = q.shape
    return pl.pallas_call(
        _attn_kernel,
        grid=(B, S // Q_BLOCK),
        in_specs=[pl.BlockSpec((1, Q_BLOCK, W), lambda b, i: (b, i, 0)),
                  pl.BlockSpec((1, S, W), lambda b, i: (b, 0, 0)),
                  pl.BlockSpec((1, S, W), lambda b, i: (b, 0, 0))],
        out_specs=pl.BlockSpec((1, Q_BLOCK, W), lambda b, i: (b, i, 0)),
        out_shape=jax.ShapeDtypeStruct((B, S, W), F32),
        scratch_shapes=[pltpu.VMEM((W // LANES, 2 * Q_BLOCK, LANES), F32),
                        pltpu.VMEM((W // LANES, 2 * Q_BLOCK, LANES), F32)],
        compiler_params=_params(("arbitrary", "arbitrary")),
        name="attn",
    )(q, k, v)


def _softplus(x):
    return jnp.maximum(x, 0.0) + jnp.log1p(jnp.exp(-jnp.abs(x)))


def _gelu_tanh(x):
    return 0.5 * x * (1.0 + jnp.tanh(0.7978845608028654 * (x + 0.044715 * x * x * x)))


def _lru_kernel(xr_ref, gr_ref, cw_ref, cb_ref, wr_ref, br_ref, wi_ref, bi_ref, lam_ref,
                g_ref, o_ref, xext, hc):
    TS = xr_ref.shape[1]
    PAD = 8

    @pl.when(pl.program_id(1) == 0)
    def _():
        xext[0:PAD, :] = jnp.zeros((PAD, xext.shape[1]), F32)
        hc[...] = jnp.zeros_like(hc)

    x = xr_ref[0]
    xext[PAD:PAD + TS, :] = x
    xc = cb_ref[...] + cw_ref[CONV_WIDTH - 1:CONV_WIDTH, :] * x
    for j in range(CONV_WIDTH - 1):
        back = CONV_WIDTH - 1 - j
        xc = xc + cw_ref[j:j + 1, :] * xext[PAD - back:PAD - back + TS, :]
    xext[0:PAD, :] = xext[TS:TS + PAD, :]

    xb = xc.astype(BF16)
    r = jax.nn.sigmoid(jnp.dot(xb, wr_ref[...], preferred_element_type=F32) + br_ref[...])
    ig = jax.nn.sigmoid(jnp.dot(xb, wi_ref[...], preferred_element_type=F32) + bi_ref[...])
    log_a = (-LRU_C) * r * _softplus(-lam_ref[...])
    a = jnp.exp(log_a)
    b = jnp.sqrt(-jnp.tanh(log_a) * (a * a + 1.0)) * (ig * xc)

    rows = lax.broadcasted_iota(jnp.int32, a.shape, 0)
    d = 1
    while d < TS:
        keep = rows >= d
        a_prev = jnp.where(keep, pltpu.roll(a, d, 0), 1.0)
        b_prev = jnp.where(keep, pltpu.roll(b, d, 0), 0.0)
        b = a * b_prev + b
        a = a * a_prev
        d *= 2
    h = a * hc[...] + b
    hc[...] = h[TS - 1:TS, :]

    y = h * _gelu_tanh(gr_ref[0])
    o_ref[0] = _rms(y, g_ref[...])


def _block_diag(w):
    H, I, J = w.shape
    eye = jnp.eye(H, dtype=w.dtype)
    return (w[:, :, None, :] * eye[:, None, :, None]).reshape(H * I, H * J)


def _lru(xr, gr, conv_w, conv_b, rg_w, rg_b, ig_w, ig_b, lam, g, ts=256):
    B, S, W = xr.shape
    row = pl.BlockSpec((1, ts, W), lambda b, s: (b, s, 0))
    vec = pl.BlockSpec((1, W), lambda b, s: (0, 0))
    mat = pl.BlockSpec((W, W), lambda b, s: (0, 0))
    return pl.pallas_call(
        _lru_kernel,
        grid=(B, S // ts),
        in_specs=[row, row, pl.BlockSpec((CONV_WIDTH, W), lambda b, s: (0, 0)), vec,
                  mat, vec, mat, vec, vec, vec],
        out_specs=row,
        out_shape=jax.ShapeDtypeStruct((B, S, W), F32),
        scratch_shapes=[pltpu.VMEM((ts + 8, W), F32), pltpu.VMEM((1, W), F32)],
        compiler_params=_params(("arbitrary", "arbitrary")),
        name="lru",
    )(xr, gr, conv_w, conv_b.reshape(1, W), _block_diag(rg_w).astype(BF16), rg_b.reshape(1, W),
      _block_diag(ig_w).astype(BF16), ig_b.reshape(1, W), lam.reshape(1, W), g.reshape(1, W))


def _load_row_tiles(ref, n_rows, lead=()):
    chunks = [ref[lead + (pl.ds(c, n_rows, stride=SUBLANES), slice(None))]
              for c in range(SUBLANES)]
    return jnp.concatenate(chunks, axis=1)


def _store_row_tiles(ref, value):
    n_rows = value.shape[0]
    for c in range(SUBLANES):
        ref[pl.ds(c, n_rows, stride=SUBLANES), :] = value[:, c * LANES:(c + 1) * LANES]


def _outproj_kernel(x_ref, ya_ref, yl_ref, mod_ref, ga_ref, wo_ref, gf_ref, rw_ref, rb_ref,
                    x1_ref, h2_ref, idx_ref, prob_ref, rank_ref, cnt_ref, tri, run):
    W = SB_WIDTH
    ya = _rms(ya_ref[0], ga_ref[...]).astype(BF16)
    yl = yl_ref[0].astype(BF16)
    mix = (jnp.dot(ya, wo_ref[0:W, :], preferred_element_type=F32)
           + jnp.dot(yl, wo_ref[W:2 * W, :], preferred_element_type=F32))
    x1 = x_ref[0] + mod_ref[0, 2:3, :] * mix
    x1_ref[0] = x1
    h2 = _rms(x1, gf_ref[...]) * (1.0 + mod_ref[0, 4:5, :]) + mod_ref[0, 3:4, :]
    _store_row_tiles(h2_ref, h2)

    logits = lax.dot_general(rw_ref[...], h2, (((1,), (1,)), ((), ())), precision=HIGHEST,
                             preferred_element_type=F32) + rb_ref[...]
    n_exp = logits.shape[0]
    eid = lax.broadcasted_iota(jnp.int32, logits.shape, 0)
    vals, idxs = [], []
    for _ in range(TOP_K):
        m = jnp.max(logits, axis=0, keepdims=True)
        sel = jnp.min(jnp.where(logits == m, eid, n_exp), axis=0, keepdims=True)
        vals.append(m)
        idxs.append(sel)
        logits = jnp.where(eid == sel, -jnp.inf, logits)
    es = [jnp.exp(vv - vals[0]) for vv in vals]
    inv = 1.0 / (es[0] + es[1] + es[2] + es[3])

    first = (pl.program_id(0) == 0) & (pl.program_id(1) == 0)

    @pl.when(first)
    def _():
        ti = lax.broadcasted_iota(jnp.int32, tri.shape, 0)
        tj = lax.broadcasted_iota(jnp.int32, tri.shape, 1)
        tri[...] = jnp.where(ti < tj, 1.0, 0.0).astype(BF16)
        run[...] = jnp.zeros_like(run)

    hits = [eid == idxs[r] for r in range(TOP_K)]
    chosen = hits[0] | hits[1] | hits[2] | hits[3]
    cnt = jnp.where(chosen, 1.0, 0.0)
    before = jnp.dot(cnt.astype(BF16), tri[...], preferred_element_type=F32) + run[...]
    for r in range(TOP_K):
        idx_ref[0, r:r + 1, :] = idxs[r]
        prob_ref[0, r:r + 1, :] = es[r] * inv
        rank_ref[0, r:r + 1, :] = jnp.sum(jnp.where(hits[r], before, 0.0), axis=0,
                                          keepdims=True).astype(jnp.int32)
    run[...] = run[...] + jnp.sum(cnt, axis=1, keepdims=True)
    cnt_ref[...] = jnp.broadcast_to(run[...], cnt_ref.shape).astype(jnp.int32)


def _outproj(x, ya, yl, mod3, ga, w_out_b, gf, router_wt, router_b, ts=512):
    B, S, D = x.shape
    W = ya.shape[2]
    NE = router_wt.shape[0]
    rowd = pl.BlockSpec((1, ts, D), lambda b, s: (b, s, 0))
    roww = pl.BlockSpec((1, ts, W), lambda b, s: (b, s, 0))
    sel = pl.BlockSpec((1, TOP_K, ts), lambda b, s: (b, 0, s))
    return pl.pallas_call(
        _outproj_kernel,
        grid=(B, S // ts),
        in_specs=[rowd, roww, roww,
                  pl.BlockSpec((1, N_MOD, D), lambda b, s: (b, 0, 0)),
                  pl.BlockSpec((1, W), lambda b, s: (0, 0)),
                  pl.BlockSpec((2 * W, D), lambda b, s: (0, 0)),
                  pl.BlockSpec((1, D), lambda b, s: (0, 0)),
                  pl.BlockSpec((NE, D), lambda b, s: (0, 0)),
                  pl.BlockSpec((NE, 1), lambda b, s: (0, 0))],
        out_specs=[rowd,
                   pl.BlockSpec((ts * SUBLANES, LANES), lambda b, s: (b * (S // ts) + s, 0)),
                   sel, sel, sel, pl.BlockSpec((NE, LANES), lambda b, s: (0, 0))],
        out_shape=[jax.ShapeDtypeStruct((B, S, D), F32),
                   jax.ShapeDtypeStruct((B * S * SUBLANES, LANES), F32),
                   jax.ShapeDtypeStruct((B, TOP_K, S), jnp.int32),
                   jax.ShapeDtypeStruct((B, TOP_K, S), F32),
                   jax.ShapeDtypeStruct((B, TOP_K, S), jnp.int32),
                   jax.ShapeDtypeStruct((NE, LANES), jnp.int32)],
        scratch_shapes=[pltpu.VMEM((ts, ts), BF16), pltpu.VMEM((NE, 1), F32)],
        compiler_params=_params(("arbitrary", "arbitrary")),
        name="outproj",
    )(x, ya, yl, mod3, ga.reshape(1, W), w_out_b, gf.reshape(1, D), router_wt,
      router_b.reshape(NE, 1))


def _dispatch_kernel(p0_ref, p1_ref, p2_ref, p3_ref, fill_ref, h_ref, xs_ref, zbuf, sem, zsem):
    TT = h_ref.shape[0] // SUBLANES
    n_fill = fill_ref.shape[0]

    @pl.when(pl.program_id(0) == 0)
    def _():
        zbuf[...] = jnp.zeros_like(zbuf)

        def fill_copy(e):
            start = pl.multiple_of(jnp.maximum(fill_ref[e], 0), SUBLANES)
            return pltpu.make_async_copy(zbuf, xs_ref.at[pl.ds(start, zbuf.shape[0]), :], zsem)

        def start(e, c):
            @pl.when(fill_ref[e] >= 0)
            def _():
                fill_copy(e).start()
            return c

        def wait(e, c):
            @pl.when(fill_ref[e] >= 0)
            def _():
                fill_copy(e).wait()
            return c

        lax.fori_loop(0, n_fill, start, 0)
        lax.fori_loop(0, n_fill, wait, 0)

    pos_refs = (p0_ref, p1_ref, p2_ref, p3_ref)

    def row_copy(t, r):
        src = pl.multiple_of(t * SUBLANES, SUBLANES)
        dst = pl.multiple_of(pos_refs[r][t], SUBLANES)
        return pltpu.make_async_copy(h_ref.at[pl.ds(src, SUBLANES), :],
                                     xs_ref.at[pl.ds(dst, SUBLANES), :], sem)

    def issue(g, c):
        for tt in range(ROW_DMA_UNROLL):
            for r in range(TOP_K):
                row_copy(g * ROW_DMA_UNROLL + tt, r).start(priority=(tt * TOP_K + r) % 2)
        return c

    lax.fori_loop(0, TT // ROW_DMA_UNROLL, issue, 0)
    for r in range(TOP_K):
        pltpu.make_async_copy(h_ref, xs_ref.at[pl.ds(0, TT * SUBLANES), :], sem).wait()


def _slot_specs(tiles_per_batch):
    def spec(r):
        return pl.BlockSpec(
            (TOKEN_TILE,),
            lambda i: ((i // tiles_per_batch * TOP_K + r) * tiles_per_batch
                       + i % tiles_per_batch,),
            memory_space=pltpu.SMEM)
    return [spec(r) for r in range(TOP_K)]


def _dispatch(h2t, pos_flat, fill_start, cap, tiles_per_batch):
    TT = TOKEN_TILE
    n_tiles = h2t.shape[0] // (TT * SUBLANES)
    return pl.pallas_call(
        _dispatch_kernel,
        grid=(n_tiles,),
        in_specs=_slot_specs(tiles_per_batch)
        + [pl.BlockSpec(memory_space=pltpu.SMEM),
           pl.BlockSpec((TT * SUBLANES, LANES), lambda i: (i, 0))],
        out_specs=pl.BlockSpec(memory_space=pl.ANY),
        out_shape=jax.ShapeDtypeStruct((cap * SUBLANES, LANES), F32),
        scratch_shapes=[pltpu.VMEM((EXPERT_BLOCK * SUBLANES, LANES), F32),
                        pltpu.SemaphoreType.DMA, pltpu.SemaphoreType.DMA],
        compiler_params=_params(("arbitrary",)),
        name="dispatch",
    )(pos_flat, pos_flat, pos_flat, pos_flat, fill_start, h2t)


def _expert_kernel(first_ref, count_ref, xs_ref, wg_ref, bg_ref, wu_ref, bu_ref, wd_ref,
                   bd_ref, ys_ref, wgb, wub, wdb, xbuf, ybuf, xsem, ysem):
    e = pl.program_id(0)
    n_exp = pl.num_programs(0)
    rows = xbuf.shape[1]
    first, count = first_ref[e], count_ref[e]

    wgb[...] = wg_ref[0].astype(BF16)
    wub[...] = wu_ref[0].astype(BF16)
    wdb[...] = wd_ref[0].astype(BF16)

    def block(ref, j):
        return ref.at[pl.ds(pl.multiple_of((first + j) * rows, rows), rows), :]

    def x_copy(j, slot):
        return pltpu.make_async_copy(block(xs_ref, j), xbuf.at[slot], xsem.at[slot])

    def y_copy(j, slot):
        return pltpu.make_async_copy(ybuf.at[slot], block(ys_ref, j), ysem.at[slot])

    @pl.when(count > 0)
    def _():
        x_copy(0, 0).start()

    def step(j, c):
        slot = j % 2
        x_copy(j, slot).wait()

        @pl.when(j + 1 < count)
        def _():
            x_copy(j + 1, 1 - slot).start()

        @pl.when(j >= 2)
        def _():
            y_copy(j - 2, slot).wait()

        xb = _load_row_tiles(xbuf, EXPERT_BLOCK, (slot,)).astype(BF16)
        g = jnp.minimum(jnp.dot(xb, wgb[...], preferred_element_type=F32) + bg_ref[0],
                        SWIGLU_LIMIT)
        u = jnp.clip(jnp.dot(xb, wub[...], preferred_element_type=F32) + bu_ref[0],
                     -SWIGLU_LIMIT, SWIGLU_LIMIT)
        act = (u + 1.0) * (g * jax.nn.sigmoid(SWIGLU_ALPHA * g))
        _store_row_tiles(ybuf.at[slot], jnp.dot(act.astype(BF16), wdb[...],
                                                preferred_element_type=F32) + bd_ref[0])
        y_copy(j, slot).start()
        return c

    lax.fori_loop(0, count, step, 0)

    @pl.when(count >= 2)
    def _():
        y_copy(count - 2, count % 2).wait()

    @pl.when(count >= 1)
    def _():
        y_copy(count - 1, (count - 1) % 2).wait()

    @pl.when(e == n_exp - 1)
    def _():
        spare_first, spare_count = first_ref[n_exp], count_ref[n_exp]
        ybuf[0] = jnp.zeros(ybuf.shape[1:], F32)

        def spare_copy(j):
            dst = pl.ds(pl.multiple_of((spare_first + j) * rows, rows), rows)
            return pltpu.make_async_copy(ybuf.at[0], ys_ref.at[dst, :], ysem.at[0])

        def start(j, c):
            spare_copy(j).start()
            return c

        def wait(j, c):
            spare_copy(j).wait()
            return c

        lax.fori_loop(0, spare_count, start, 0)
        lax.fori_loop(0, spare_count, wait, 0)


def _experts(xs, first_block, block_count, wg, bg, wu, bu, wd, bd):
    NE, D, DE = wg.shape
    tile_rows = EXPERT_BLOCK * SUBLANES

    def wsel(e, first, count):
        return (e, 0, 0)

    grid_spec = pltpu.PrefetchScalarGridSpec(
        num_scalar_prefetch=2,
        grid=(NE,),
        in_specs=[pl.BlockSpec(memory_space=pl.ANY),
                  pl.BlockSpec((1, D, DE), wsel), pl.BlockSpec((1, 1, DE), wsel),
                  pl.BlockSpec((1, D, DE), wsel), pl.BlockSpec((1, 1, DE), wsel),
                  pl.BlockSpec((1, DE, D), wsel), pl.BlockSpec((1, 1, D), wsel)],
        out_specs=pl.BlockSpec(memory_space=pl.ANY),
        scratch_shapes=[pltpu.VMEM((D, DE), BF16), pltpu.VMEM((D, DE), BF16),
                        pltpu.VMEM((DE, D), BF16),
                        pltpu.VMEM((2, tile_rows, LANES), F32),
                        pltpu.VMEM((2, tile_rows, LANES), F32),
                        pltpu.SemaphoreType.DMA((2,)), pltpu.SemaphoreType.DMA((2,))],
    )
    return pl.pallas_call(
        _expert_kernel,
        grid_spec=grid_spec,
        out_shape=jax.ShapeDtypeStruct(xs.shape, F32),
        compiler_params=_params(("arbitrary",)),
        name="experts",
    )(first_block, block_count, xs, wg, bg.reshape(NE, 1, DE), wu, bu.reshape(NE, 1, DE), wd,
      bd.reshape(NE, 1, D))


def _combine_kernel(c0_ref, c1_ref, c2_ref, c3_ref, n0_ref, n1_ref, n2_ref, n3_ref,
                    x1_ref, p_ref, mod_ref, g_ref, ys_ref, o_ref, buf, sem):
    TT = x1_ref.shape[0]
    i = pl.program_id(0)

    def gather(pos_refs, s):
        def issue(g, c):
            for tt in range(ROW_DMA_UNROLL):
                t = g * ROW_DMA_UNROLL + tt
                for r in range(TOP_K):
                    pltpu.make_async_copy(
                        ys_ref.at[pl.ds(pl.multiple_of(pos_refs[r][t], SUBLANES), SUBLANES), :],
                        buf.at[s, r, pl.ds(pl.multiple_of(t * SUBLANES, SUBLANES), SUBLANES), :],
                        sem.at[s]).start(priority=(tt * TOP_K + r) % 2)
            return c
        lax.fori_loop(0, TT // ROW_DMA_UNROLL, issue, 0)

    @pl.when(i == 0)
    def _():
        gather((c0_ref, c1_ref, c2_ref, c3_ref), 0)

    def reduce_tile(s):
        @pl.when(i + 1 < pl.num_programs(0))
        def _():
            gather((n0_ref, n1_ref, n2_ref, n3_ref), 1 - s)

        for r in range(TOP_K):
            pltpu.make_async_copy(ys_ref.at[pl.ds(0, TT * SUBLANES), :], buf.at[s, r],
                                  sem.at[s]).wait()
        p = p_ref[...]
        moe = p[:, 0:1] * _load_row_tiles(buf, TT, (s, 0))
        for r in range(1, TOP_K):
            moe = moe + p[:, r:r + 1] * _load_row_tiles(buf, TT, (s, r))
        x2 = x1_ref[...] + mod_ref[0, 5:6, :] * moe
        o_ref[...] = _rms(x2, g_ref[...])

    for s in range(2):
        pl.when(i % 2 == s)(functools.partial(reduce_tile, s))


def _combine(x1f, probs, pos_flat, mod3, g, ys, tiles_per_batch):
    N, D = x1f.shape
    TT = TOKEN_TILE
    n_tiles = N // TT

    def next_spec(r):
        def index(i):
            j = jnp.minimum(i + 1, n_tiles - 1)
            return ((j // tiles_per_batch * TOP_K + r) * tiles_per_batch + j % tiles_per_batch,)
        return pl.BlockSpec((TT,), index, memory_space=pltpu.SMEM)

    return pl.pallas_call(
        _combine_kernel,
        grid=(n_tiles,),
        in_specs=_slot_specs(tiles_per_batch) + [next_spec(r) for r in range(TOP_K)]
        + [pl.BlockSpec((TT, D), lambda i: (i, 0)),
           pl.BlockSpec((TT, TOP_K), lambda i: (i, 0)),
           pl.BlockSpec((1, N_MOD, D), lambda i: (i // tiles_per_batch, 0, 0)),
           pl.BlockSpec((1, D), lambda i: (0, 0)),
           pl.BlockSpec(memory_space=pl.ANY)],
        out_specs=pl.BlockSpec((TT, D), lambda i: (i, 0)),
        out_shape=jax.ShapeDtypeStruct((N, D), F32),
        scratch_shapes=[pltpu.VMEM((2, TOP_K, TT * SUBLANES, LANES), F32),
                        pltpu.SemaphoreType.DMA((2,))],
        compiler_params=_params(("arbitrary",)),
        name="combine",
    )(*([pos_flat] * (2 * TOP_K)), x1f, probs, mod3, g.reshape(1, D), ys)


def _routing(idx, rank, counts):
    B, K, S = idx.shape
    n_exp = counts.shape[0]
    TM = EXPERT_BLOCK
    n_assign = B * S * K
    padded = ((counts + TM - 1) // TM) * TM
    pad_ends = jnp.cumsum(padded)
    pad_starts = pad_ends - padded
    experts = jnp.arange(n_exp, dtype=jnp.int32).reshape(n_exp, 1, 1, 1)
    base = jnp.sum(jnp.where(idx[None] == experts, pad_starts.reshape(n_exp, 1, 1, 1), 0),
                   axis=0)
    pos = ((base + rank) * SUBLANES).astype(jnp.int32).reshape(n_assign)
    n_blocks = n_assign // TM + n_exp
    first_block = jnp.concatenate([pad_starts, pad_ends[-1:]]) // TM
    block_count = jnp.concatenate([padded // TM, n_blocks - pad_ends[-1:] // TM])
    tail_start = jnp.where(padded > 0, pad_ends - TM, -1)
    spare = pad_ends[-1] + jnp.arange(n_exp, dtype=jnp.int32) * TM
    fill_start = jnp.concatenate(
        [tail_start, jnp.where(spare < n_blocks * TM, spare, -1)]).astype(jnp.int32)
    return (pos, first_block.astype(jnp.int32), block_count.astype(jnp.int32),
            fill_start * SUBLANES, n_blocks * TM)


def kernel(x, c, ada_w, ada_b, mix_norm_g, w_in, conv_w, conv_b, rg_w, rg_b, ig_w, ig_b,
           lru_lambda, attn_out_g, lru_out_g, w_out, ffn_norm_g, router_w, router_b,
           exp_w_gate, exp_b_gate, exp_w_up, exp_b_up, exp_w_down, exp_b_down, final_norm_g):
    B, S, D = x.shape
    depth = ada_w.shape[0]
    assert S % 512 == 0 and S % TOKEN_TILE == 0 and D == SUBLANES * LANES
    for l in range(depth):
        mod3 = _ada(c, ada_w[l], ada_b[l]).reshape(B, N_MOD, D)
        q, k, v, xr, gr = _inproj(x, mod3, mix_norm_g[l], w_in[l].astype(BF16))
        ya = _attention(q, k, v)
        yl = _lru(xr, gr, conv_w[l], conv_b[l], rg_w[l], rg_b[l], ig_w[l], ig_b[l],
                  lru_lambda[l], lru_out_g[l])
        x1, h2, idx, prob, rank, cnt = _outproj(
            x, ya, yl, mod3, attn_out_g[l], w_out[l].astype(BF16), ffn_norm_g[l],
            router_w[l].T, router_b[l])
        pos, first_block, block_count, fill_start, cap = _routing(idx, rank, cnt[:, 0])
        xs = _dispatch(h2, pos, fill_start, cap, S // TOKEN_TILE)
        ys = _experts(xs, first_block, block_count, exp_w_gate[l], exp_b_gate[l],
                      exp_w_up[l], exp_b_up[l], exp_w_down[l], exp_b_down[l])
        probs = jnp.transpose(prob, (0, 2, 1)).reshape(B * S, TOP_K)
        assert depth == 1
        x = _combine(x1.reshape(B * S, D), probs, pos, mod3, final_norm_g, ys,
                     S // TOKEN_TILE).reshape(B, S, D)
    return x
```

```python
import functools

import jax
import jax.numpy as jnp
from jax import lax
from jax.experimental import pallas as pl
from jax.experimental.pallas import tpu as pltpu

F32 = jnp.float32
BF16 = jnp.bfloat16
HIGHEST = lax.Precision.HIGHEST

EPS = 1e-6
N_MOD = 6
SB_HEADS = 8
HEAD_DIM = 64
SB_WIDTH = SB_HEADS * HEAD_DIM
LRU_BLOCKS = 8
CONV_WIDTH = 4
LRU_C = 8.0
TOP_K = 4
SWIGLU_LIMIT = 7.0
SWIGLU_ALPHA = 1.702
LOG2E = 1.4426950408889634

LANES = 128
SUBLANES = 8
VMEM_LIMIT = 56 * 1024 * 1024

Q_BLOCK = 128
K_BLOCK = 128
ATTN_WINDOW_BLOCKS = 5
ATTN_LOOP_BLOCKS = 2
ATTN_UNDERFLOW_LOG = -110.0
EXPERT_BLOCK = 512
TOKEN_TILE = 256
ROW_DMA_UNROLL = 8


def _params(sem):
    return pltpu.CompilerParams(dimension_semantics=sem, vmem_limit_bytes=VMEM_LIMIT)


def _ada_kernel(c_ref, w_ref, b_ref, o_ref):
    c = c_ref[...]
    ca = c * jax.nn.sigmoid(c)
    o_ref[...] = jnp.dot(ca, w_ref[...], precision=HIGHEST,
                         preferred_element_type=F32) + b_ref[...]


def _ada(c, ada_w, ada_b):
    B, D = c.shape
    E = ada_w.shape[1]
    tn = 1024
    return pl.pallas_call(
        _ada_kernel,
        grid=(E // tn,),
        in_specs=[pl.BlockSpec((B, D), lambda j: (0, 0)),
                  pl.BlockSpec((D, tn), lambda j: (0, j)),
                  pl.BlockSpec((1, tn), lambda j: (0, j))],
        out_specs=pl.BlockSpec((B, tn), lambda j: (0, j)),
        out_shape=jax.ShapeDtypeStruct((B, E), F32),
        compiler_params=_params(("arbitrary",)),
        name="ada",
    )(c, ada_w, ada_b.reshape(1, E))


def _rms(x, g):
    ms = jnp.mean(x * x, axis=-1, keepdims=True)
    return x * lax.rsqrt(ms + EPS) * g


def _inproj_kernel(x_ref, mod_ref, g_ref, w_ref, q_ref, k_ref, v_ref, xr_ref, gr_ref):
    x = x_ref[0]
    h = _rms(x, g_ref[...]) * (1.0 + mod_ref[0, 1:2, :]) + mod_ref[0, 0:1, :]
    hb = h.astype(BF16)
    W = SB_WIDTH

    def proj(c):
        return jnp.dot(hb, w_ref[:, c * W:(c + 1) * W], preferred_element_type=F32)

    q_ref[0] = (proj(0) * (HEAD_DIM ** -0.5)).astype(BF16)
    k_ref[0] = proj(1).astype(BF16)
    v_ref[0] = proj(2).astype(BF16)
    xr_ref[0] = proj(3)
    gr_ref[0] = proj(4)


def _inproj(x, mod3, g, w_in_b, ts=512):
    B, S, D = x.shape
    E = w_in_b.shape[1]
    W = SB_WIDTH
    row = pl.BlockSpec((1, ts, W), lambda b, s: (b, s, 0))
    return pl.pallas_call(
        _inproj_kernel,
        grid=(B, S // ts),
        in_specs=[pl.BlockSpec((1, ts, D), lambda b, s: (b, s, 0)),
                  pl.BlockSpec((1, N_MOD, D), lambda b, s: (b, 0, 0)),
                  pl.BlockSpec((1, D), lambda b, s: (0, 0)),
                  pl.BlockSpec((D, E), lambda b, s: (0, 0))],
        out_specs=[row, row, row, row, row],
        out_shape=[jax.ShapeDtypeStruct((B, S, W), BF16)] * 3
        + [jax.ShapeDtypeStruct((B, S, W), F32)] * 2,
        compiler_params=_params(("arbitrary", "arbitrary")),
        name="inproj",
    )(x, mod3, g.reshape(1, D), w_in_b)


def _attn_kernel(q_ref, k_ref, v_ref, o_ref, acc_ref, carry_ref):
    i = pl.program_id(1)
    QB, KB = Q_BLOCK, K_BLOCK
    n_pairs = q_ref.shape[2] // LANES
    lane = lax.broadcasted_iota(jnp.int32, (QB, LANES), 1)

    def stacked_q(p):
        q = q_ref[0, :, p * LANES:(p + 1) * LANES]
        zero = jnp.zeros_like(q)
        return jnp.concatenate([jnp.where(lane < HEAD_DIM, q, zero),
                                jnp.where(lane >= HEAD_DIM, q, zero)], axis=0)

    qs = [stacked_q(p) for p in range(n_pairs)]

    uj = lax.broadcasted_iota(jnp.int32, (2 * KB, KB + LANES), 0) & (KB - 1)
    us = lax.broadcasted_iota(jnp.int32, (2 * KB, KB + LANES), 1)
    u2 = jnp.where((us >= KB) | (uj > us), -1.0, 0.0).astype(BF16)

    row = lax.broadcasted_iota(jnp.int32, (2 * QB, KB), 0) & (QB - 1)
    col = lax.broadcasted_iota(jnp.int32, (2 * QB, KB), 1)
    causal = col < row

    def tiles(p, j_hi, n, diagonal, acc, carry):
        start = pl.multiple_of((j_hi - (n - 1)) * KB, KB)
        kw = k_ref[0, pl.ds(start, n * KB), p * LANES:(p + 1) * LANES]
        vw = v_ref[0, pl.ds(start, n * KB), p * LANES:(p + 1) * LANES]
        z = lax.dot_general(qs[p], kw, (((1,), (1,)), ((), ())), preferred_element_type=F32)
        softplus = jnp.maximum(z, 0.0) + jnp.log(1.0 + jnp.exp2(jnp.abs(z) * (-LOG2E)))
        log_beta = z - softplus
        ws = [None] * n
        for c in reversed(range(n)):
            sp = softplus[:, c * KB:(c + 1) * KB]
            masked = diagonal and c == n - 1
            if masked:
                sp = jnp.where(causal, sp, 0.0)
            hi_f = lax.bitcast_convert_type(
                lax.bitcast_convert_type(sp, jnp.uint32) & jnp.uint32(0xFFFF0000), F32)
            hi_lo = jnp.concatenate([hi_f.astype(BF16), (sp - hi_f).astype(BF16)], axis=1)
            sums = jnp.dot(hi_lo, u2, preferred_element_type=F32)
            w = jnp.exp2((log_beta[:, c * KB:(c + 1) * KB] + sums[:, :KB] + carry) * LOG2E)
            if masked:
                w = jnp.where(causal, w, 0.0)
            ws[c] = w.astype(BF16)
            carry = carry + sums[:, KB:]
        w_all = ws[0] if n == 1 else jnp.concatenate(ws, axis=1)
        acc = acc + jnp.dot(w_all, vw, preferred_element_type=F32)
        return acc, carry

    def fold(j_hi, n, diagonal):
        cmax = None
        for p in range(n_pairs):
            if diagonal:
                acc = carry = jnp.zeros((2 * QB, LANES), F32)
            else:
                acc, carry = acc_ref[p], carry_ref[p]
            acc, carry = tiles(p, j_hi, n, diagonal, acc, carry)
            acc_ref[p] = acc
            carry_ref[p] = carry
            cmax = carry if cmax is None else jnp.maximum(cmax, carry)
        return jnp.max(cmax)

    n_first, n_loop = ATTN_WINDOW_BLOCKS, ATTN_LOOP_BLOCKS
    n0 = jnp.minimum(i + 1, n_first)
    cmax0 = lax.switch(n0 - 1, [functools.partial(fold, i, n, True)
                                for n in range(1, n_first + 1)])
    j0 = i - n0

    def loop(n, j, cmax):
        def more(st):
            return (st[0] >= n - 1) & (st[1] >= ATTN_UNDERFLOW_LOG)

        def step(st):
            return st[0] - n, fold(st[0], n, False)

        return lax.while_loop(more, step, (j, cmax))

    j, cmax = j0, cmax0
    while n_loop >= 1:
        j, cmax = loop(n_loop, j, cmax)
        n_loop //= 2
    for p in range(n_pairs):
        acc = acc_ref[p]
        o_ref[0, :, p * LANES:(p + 1) * LANES] = jnp.where(lane < HEAD_DIM, acc[:QB], acc[QB:])


def _attention(q, k, v):
    B, S, W = q.shape
    return pl.pallas_call(
        _attn_kernel,
        grid=(B, S // Q_BLOCK),
        in_specs=[pl.BlockSpec((1, Q_BLOCK, W), lambda b, i: (b, i, 0)),
                  pl.BlockSpec((1, S, W), lambda b, i: (b, 0, 0)),
                  pl.BlockSpec((1, S, W), lambda b, i: (b, 0, 0))],
        out_specs=pl.BlockSpec((1, Q_BLOCK, W), lambda b, i: (b, i, 0)),
        out_shape=jax.ShapeDtypeStruct((B, S, W), F32),
        scratch_shapes=[pltpu.VMEM((W // LANES, 2 * Q_BLOCK, LANES), F32),
                        pltpu.VMEM((W // LANES, 2 * Q_BLOCK, LANES), F32)],
        compiler_params=_params(("arbitrary", "arbitrary")),
        name="attn",
    )(q, k, v)


def _softplus(x):
    return jnp.maximum(x, 0.0) + jnp.log1p(jnp.exp(-jnp.abs(x)))


def _gelu_tanh(x):
    return 0.5 * x * (1.0 + jnp.tanh(0.7978845608028654 * (x + 0.044715 * x * x * x)))


def _lru_kernel(xr_ref, gr_ref, cw_ref, cb_ref, wr_ref, br_ref, wi_ref, bi_ref, lam_ref,
                g_ref, o_ref, xext, hc):
    TS = xr_ref.shape[1]
    PAD = 8

    @pl.when(pl.program_id(1) == 0)
    def _():
        xext[0:PAD, :] = jnp.zeros((PAD, xext.shape[1]), F32)
        hc[...] = jnp.zeros_like(hc)

    x = xr_ref[0]
    xext[PAD:PAD + TS, :] = x
    xc = cb_ref[...] + cw_ref[CONV_WIDTH - 1:CONV_WIDTH, :] * x
    for j in range(CONV_WIDTH - 1):
        back = CONV_WIDTH - 1 - j
        xc = xc + cw_ref[j:j + 1, :] * xext[PAD - back:PAD - back + TS, :]
    xext[0:PAD, :] = xext[TS:TS + PAD, :]

    xb = xc.astype(BF16)
    r = jax.nn.sigmoid(jnp.dot(xb, wr_ref[...], preferred_element_type=F32) + br_ref[...])
    ig = jax.nn.sigmoid(jnp.dot(xb, wi_ref[...], preferred_element_type=F32) + bi_ref[...])
    log_a = (-LRU_C) * r * _softplus(-lam_ref[...])
    a = jnp.exp(log_a)
    b = jnp.sqrt(-jnp.tanh(log_a) * (a * a + 1.0)) * (ig * xc)

    rows = lax.broadcasted_iota(jnp.int32, a.shape, 0)
    d = 1
    while d < TS:
        keep = rows >= d
        a_prev = jnp.where(keep, pltpu.roll(a, d, 0), 1.0)
        b_prev = jnp.where(keep, pltpu.roll(b, d, 0), 0.0)
        b = a * b_prev + b
        a = a * a_prev
        d *= 2
    h = a * hc[...] + b
    hc[...] = h[TS - 1:TS, :]

    y = h * _gelu_tanh(gr_ref[0])
    o_ref[0] = _rms(y, g_ref[...])


def _block_diag(w):
    H, I, J = w.shape
    eye = jnp.eye(H, dtype=w.dtype)
    return (w[:, :, None, :] * eye[:, None, :, None]).reshape(H * I, H * J)


def _lru(xr, gr, conv_w, conv_b, rg_w, rg_b, ig_w, ig_b, lam, g, ts=256):
    B, S, W = xr.shape
    row = pl.BlockSpec((1, ts, W), lambda b, s: (b, s, 0))
    vec = pl.BlockSpec((1, W), lambda b, s: (0, 0))
    mat = pl.BlockSpec((W, W), lambda b, s: (0, 0))
    return pl.pallas_call(
        _lru_kernel,
        grid=(B, S // ts),
        in_specs=[row, row, pl.BlockSpec((CONV_WIDTH, W), lambda b, s: (0, 0)), vec,
                  mat, vec, mat, vec, vec, vec],
        out_specs=row,
        out_shape=jax.ShapeDtypeStruct((B, S, W), F32),
        scratch_shapes=[pltpu.VMEM((ts + 8, W), F32), pltpu.VMEM((1, W), F32)],
        compiler_params=_params(("arbitrary", "arbitrary")),
        name="lru",
    )(xr, gr, conv_w, conv_b.reshape(1, W), _block_diag(rg_w).astype(BF16), rg_b.reshape(1, W),
      _block_diag(ig_w).astype(BF16), ig_b.reshape(1, W), lam.reshape(1, W), g.reshape(1, W))


def _load_row_tiles(ref, n_rows, lead=()):
    chunks = [ref[lead + (pl.ds(c, n_rows, stride=SUBLANES), slice(None))]
              for c in range(SUBLANES)]
    return jnp.concatenate(chunks, axis=1)


def _store_row_tiles(ref, value):
    n_rows = value.shape[0]
    for c in range(SUBLANES):
        ref[pl.ds(c, n_rows, stride=SUBLANES), :] = value[:, c * LANES:(c + 1) * LANES]


def _outproj_kernel(x_ref, ya_ref, yl_ref, mod_ref, ga_ref, wo_ref, gf_ref, rw_ref, rb_ref,
                    x1_ref, h2_ref, idx_ref, prob_ref, rank_ref, cnt_ref, tri, run):
    W = SB_WIDTH
    ya = _rms(ya_ref[0], ga_ref[...]).astype(BF16)
    yl = yl_ref[0].astype(BF16)
    mix = (jnp.dot(ya, wo_ref[0:W, :], preferred_element_type=F32)
           + jnp.dot(yl, wo_ref[W:2 * W, :], preferred_element_type=F32))
    x1 = x_ref[0] + mod_ref[0, 2:3, :] * mix
    x1_ref[0] = x1
    h2 = _rms(x1, gf_ref[...]) * (1.0 + mod_ref[0, 4:5, :]) + mod_ref[0, 3:4, :]
    _store_row_tiles(h2_ref, h2)

    logits = lax.dot_general(rw_ref[...], h2, (((1,), (1,)), ((), ())), precision=HIGHEST,
                             preferred_element_type=F32) + rb_ref[...]
    n_exp = logits.shape[0]
    eid = lax.broadcasted_iota(jnp.int32, logits.shape, 0)
    vals, idxs = [], []
    for _ in range(TOP_K):
        m = jnp.max(logits, axis=0, keepdims=True)
        sel = jnp.min(jnp.where(logits == m, eid, n_exp), axis=0, keepdims=True)
        vals.append(m)
        idxs.append(sel)
        logits = jnp.where(eid == sel, -jnp.inf, logits)
    es = [jnp.exp(vv - vals[0]) for vv in vals]
    inv = 1.0 / (es[0] + es[1] + es[2] + es[3])

    first = (pl.program_id(0) == 0) & (pl.program_id(1) == 0)

    @pl.when(first)
    def _():
        ti = lax.broadcasted_iota(jnp.int32, tri.shape, 0)
        tj = lax.broadcasted_iota(jnp.int32, tri.shape, 1)
        tri[...] = jnp.where(ti < tj, 1.0, 0.0).astype(BF16)
        run[...] = jnp.zeros_like(run)

    hits = [eid == idxs[r] for r in range(TOP_K)]
    chosen = hits[0] | hits[1] | hits[2] | hits[3]
    cnt = jnp.where(chosen, 1.0, 0.0)
    before = jnp.dot(cnt.astype(BF16), tri[...], preferred_element_type=F32) + run[...]
    for r in range(TOP_K):
        idx_ref[0, r:r + 1, :] = idxs[r]
        prob_ref[0, r:r + 1, :] = es[r] * inv
        rank_ref[0, r:r + 1, :] = jnp.sum(jnp.where(hits[r], before, 0.0), axis=0,
                                          keepdims=True).astype(jnp.int32)
    run[...] = run[...] + jnp.sum(cnt, axis=1, keepdims=True)
    cnt_ref[...] = jnp.broadcast_to(run[...], cnt_ref.shape).astype(jnp.int32)


def _outproj(x, ya, yl, mod3, ga, w_out_b, gf, router_wt, router_b, ts=512):
    B, S, D = x.shape
    W = ya.shape[2]
    NE = router_wt.shape[0]
    rowd = pl.BlockSpec((1, ts, D), lambda b, s: (b, s, 0))
    roww = pl.BlockSpec((1, ts, W), lambda b, s: (b, s, 0))
    sel = pl.BlockSpec((1, TOP_K, ts), lambda b, s: (b, 0, s))
    return pl.pallas_call(
        _outproj_kernel,
        grid=(B, S // ts),
        in_specs=[rowd, roww, roww,
                  pl.BlockSpec((1, N_MOD, D), lambda b, s: (b, 0, 0)),
                  pl.BlockSpec((1, W), lambda b, s: (0, 0)),
                  pl.BlockSpec((2 * W, D), lambda b, s: (0, 0)),
                  pl.BlockSpec((1, D), lambda b, s: (0, 0)),
                  pl.BlockSpec((NE, D), lambda b, s: (0, 0)),
                  pl.BlockSpec((NE, 1), lambda b, s: (0, 0))],
        out_specs=[rowd,
                   pl.BlockSpec((ts * SUBLANES, LANES), lambda b, s: (b * (S // ts) + s, 0)),
                   sel, sel, sel, pl.BlockSpec((NE, LANES), lambda b, s: (0, 0))],
        out_shape=[jax.ShapeDtypeStruct((B, S, D), F32),
                   jax.ShapeDtypeStruct((B * S * SUBLANES, LANES), F32),
                   jax.ShapeDtypeStruct((B, TOP_K, S), jnp.int32),
                   jax.ShapeDtypeStruct((B, TOP_K, S), F32),
                   jax.ShapeDtypeStruct((B, TOP_K, S), jnp.int32),
                   jax.ShapeDtypeStruct((NE, LANES), jnp.int32)],
        scratch_shapes=[pltpu.VMEM((ts, ts), BF16), pltpu.VMEM((NE, 1), F32)],
        compiler_params=_params(("arbitrary", "arbitrary")),
        name="outproj",
    )(x, ya, yl, mod3, ga.reshape(1, W), w_out_b, gf.reshape(1, D), router_wt,
      router_b.reshape(NE, 1))


def _dispatch_kernel(p0_ref, p1_ref, p2_ref, p3_ref, fill_ref, h_ref, xs_ref, zbuf, sem, zsem):
    TT = h_ref.shape[0] // SUBLANES
    n_fill = fill_ref.shape[0]

    @pl.when(pl.program_id(0) == 0)
    def _():
        zbuf[...] = jnp.zeros_like(zbuf)

        def fill_copy(e):
            start = pl.multiple_of(jnp.maximum(fill_ref[e], 0), SUBLANES)
            return pltpu.make_async_copy(zbuf, xs_ref.at[pl.ds(start, zbuf.shape[0]), :], zsem)

        def start(e, c):
            @pl.when(fill_ref[e] >= 0)
            def _():
                fill_copy(e).start()
            return c

        def wait(e, c):
            @pl.when(fill_ref[e] >= 0)
            def _():
                fill_copy(e).wait()
            return c

        lax.fori_loop(0, n_fill, start, 0)
        lax.fori_loop(0, n_fill, wait, 0)

    pos_refs = (p0_ref, p1_ref, p2_ref, p3_ref)

    def row_copy(t, r):
        src = pl.multiple_of(t * SUBLANES, SUBLANES)
        dst = pl.multiple_of(pos_refs[r][t], SUBLANES)
        return pltpu.make_async_copy(h_ref.at[pl.ds(src, SUBLANES), :],
                                     xs_ref.at[pl.ds(dst, SUBLANES), :], sem)

    def issue(g, c):
        for tt in range(ROW_DMA_UNROLL):
            for r in range(TOP_K):
                row_copy(g * ROW_DMA_UNROLL + tt, r).start(priority=(tt * TOP_K + r) % 2)
        return c

    lax.fori_loop(0, TT // ROW_DMA_UNROLL, issue, 0)
    for r in range(TOP_K):
        pltpu.make_async_copy(h_ref, xs_ref.at[pl.ds(0, TT * SUBLANES), :], sem).wait()


def _slot_specs(tiles_per_batch):
    def spec(r):
        return pl.BlockSpec(
            (TOKEN_TILE,),
            lambda i: ((i // tiles_per_batch * TOP_K + r) * tiles_per_batch
                       + i % tiles_per_batch,),
            memory_space=pltpu.SMEM)
    return [spec(r) for r in range(TOP_K)]


def _dispatch(h2t, pos_flat, fill_start, cap, tiles_per_batch):
    TT = TOKEN_TILE
    n_tiles = h2t.shape[0] // (TT * SUBLANES)
    return pl.pallas_call(
        _dispatch_kernel,
        grid=(n_tiles,),
        in_specs=_slot_specs(tiles_per_batch)
        + [pl.BlockSpec(memory_space=pltpu.SMEM),
           pl.BlockSpec((TT * SUBLANES, LANES), lambda i: (i, 0))],
        out_specs=pl.BlockSpec(memory_space=pl.ANY),
        out_shape=jax.ShapeDtypeStruct((cap * SUBLANES, LANES), F32),
        scratch_shapes=[pltpu.VMEM((EXPERT_BLOCK * SUBLANES, LANES), F32),
                        pltpu.SemaphoreType.DMA, pltpu.SemaphoreType.DMA],
        compiler_params=_params(("arbitrary",)),
        name="dispatch",
    )(pos_flat, pos_flat, pos_flat, pos_flat, fill_start, h2t)


def _expert_kernel(first_ref, count_ref, xs_ref, wg_ref, bg_ref, wu_ref, bu_ref, wd_ref,
                   bd_ref, ys_ref, wgb, wub, wdb, xbuf, ybuf, xsem, ysem):
    e = pl.program_id(0)
    n_exp = pl.num_programs(0)
    rows = xbuf.shape[1]
    first, count = first_ref[e], count_ref[e]

    wgb[...] = wg_ref[0].astype(BF16)
    wub[...] = wu_ref[0].astype(BF16)
    wdb[...] = wd_ref[0].astype(BF16)

    def block(ref, j):
        return ref.at[pl.ds(pl.multiple_of((first + j) * rows, rows), rows), :]

    def x_copy(j, slot):
        return pltpu.make_async_copy(block(xs_ref, j), xbuf.at[slot], xsem.at[slot])

    def y_copy(j, slot):
        return pltpu.make_async_copy(ybuf.at[slot], block(ys_ref, j), ysem.at[slot])

    @pl.when(count > 0)
    def _():
        x_copy(0, 0).start()

    def step(j, c):
        slot = j % 2
        x_copy(j, slot).wait()

        @pl.when(j + 1 < count)
        def _():
            x_copy(j + 1, 1 - slot).start()

        @pl.when(j >= 2)
        def _():
            y_copy(j - 2, slot).wait()

        xb = _load_row_tiles(xbuf, EXPERT_BLOCK, (slot,)).astype(BF16)
        g = jnp.minimum(jnp.dot(xb, wgb[...], preferred_element_type=F32) + bg_ref[0],
                        SWIGLU_LIMIT)
        u = jnp.clip(jnp.dot(xb, wub[...], preferred_element_type=F32) + bu_ref[0],
                     -SWIGLU_LIMIT, SWIGLU_LIMIT)
        act = (u + 1.0) * (g * jax.nn.sigmoid(SWIGLU_ALPHA * g))
        _store_row_tiles(ybuf.at[slot], jnp.dot(act.astype(BF16), wdb[...],
                                                preferred_element_type=F32) + bd_ref[0])
        y_copy(j, slot).start()
        return c

    lax.fori_loop(0, count, step, 0)

    @pl.when(count >= 2)
    def _():
        y_copy(count - 2, count % 2).wait()

    @pl.when(count >= 1)
    def _():
        y_copy(count - 1, (count - 1) % 2).wait()

    @pl.when(e == n_exp - 1)
    def _():
        spare_first, spare_count = first_ref[n_exp], count_ref[n_exp]
        ybuf[0] = jnp.zeros(ybuf.shape[1:], F32)

        def spare_copy(j):
            dst = pl.ds(pl.multiple_of((spare_first + j) * rows, rows), rows)
            return pltpu.make_async_copy(ybuf.at[0], ys_ref.at[dst, :], ysem.at[0])

        def start(j, c):
            spare_copy(j).start()
            return c

        def wait(j, c):
            spare_copy(j).wait()
            return c

        lax.fori_loop(0, spare_count, start, 0)
        lax.fori_loop(0, spare_count, wait, 0)


def _experts(xs, first_block, block_count, wg, bg, wu, bu, wd, bd):
    NE, D, DE = wg.shape
    tile_rows = EXPERT_BLOCK * SUBLANES

    def wsel(e, first, count):
        return (e, 0, 0)

    grid_spec = pltpu.PrefetchScalarGridSpec(
        num_scalar_prefetch=2,
        grid=(NE,),
        in_specs=[pl.BlockSpec(memory_space=pl.ANY),
                  pl.BlockSpec((1, D, DE), wsel), pl.BlockSpec((1, 1, DE), wsel),
                  pl.BlockSpec((1, D, DE), wsel), pl.BlockSpec((1, 1, DE), wsel),
                  pl.BlockSpec((1, DE, D), wsel), pl.BlockSpec((1, 1, D), wsel)],
        out_specs=pl.BlockSpec(memory_space=pl.ANY),
        scratch_shapes=[pltpu.VMEM((D, DE), BF16), pltpu.VMEM((D, DE), BF16),
                        pltpu.VMEM((DE, D), BF16),
                        pltpu.VMEM((2, tile_rows, LANES), F32),
                        pltpu.VMEM((2, tile_rows, LANES), F32),
                        pltpu.SemaphoreType.DMA((2,)), pltpu.SemaphoreType.DMA((2,))],
    )
    return pl.pallas_call(
        _expert_kernel,
        grid_spec=grid_spec,
        out_shape=jax.ShapeDtypeStruct(xs.shape, F32),
        compiler_params=_params(("arbitrary",)),
        name="experts",
    )(first_block, block_count, xs, wg, bg.reshape(NE, 1, DE), wu, bu.reshape(NE, 1, DE), wd,
      bd.reshape(NE, 1, D))


def _combine_kernel(c0_ref, c1_ref, c2_ref, c3_ref, n0_ref, n1_ref, n2_ref, n3_ref,
                    x1_ref, p_ref, mod_ref, g_ref, ys_ref, o_ref, buf, sem):
    TT = x1_ref.shape[0]
    i = pl.program_id(0)

    def gather(pos_refs, s):
        def issue(g, c):
            for tt in range(ROW_DMA_UNROLL):
                t = g * ROW_DMA_UNROLL + tt
                for r in range(TOP_K):
                    pltpu.make_async_copy(
                        ys_ref.at[pl.ds(pl.multiple_of(pos_refs[r][t], SUBLANES), SUBLANES), :],
                        buf.at[s, r, pl.ds(pl.multiple_of(t * SUBLANES, SUBLANES), SUBLANES), :],
                        sem.at[s]).start(priority=(tt * TOP_K + r) % 2)
            return c
        lax.fori_loop(0, TT // ROW_DMA_UNROLL, issue, 0)

    @pl.when(i == 0)
    def _():
        gather((c0_ref, c1_ref, c2_ref, c3_ref), 0)

    def reduce_tile(s):
        @pl.when(i + 1 < pl.num_programs(0))
        def _():
            gather((n0_ref, n1_ref, n2_ref, n3_ref), 1 - s)

        for r in range(TOP_K):
            pltpu.make_async_copy(ys_ref.at[pl.ds(0, TT * SUBLANES), :], buf.at[s, r],
                                  sem.at[s]).wait()
        p = p_ref[...]
        moe = p[:, 0:1] * _load_row_tiles(buf, TT, (s, 0))
        for r in range(1, TOP_K):
            moe = moe + p[:, r:r + 1] * _load_row_tiles(buf, TT, (s, r))
        x2 = x1_ref[...] + mod_ref[0, 5:6, :] * moe
        o_ref[...] = _rms(x2, g_ref[...])

    for s in range(2):
        pl.when(i % 2 == s)(functools.partial(reduce_tile, s))


def _combine(x1f, probs, pos_flat, mod3, g, ys, tiles_per_batch):
    N, D = x1f.shape
    TT = TOKEN_TILE
    n_tiles = N // TT

    def next_spec(r):
        def index(i):
            j = jnp.minimum(i + 1, n_tiles - 1)
            return ((j // tiles_per_batch * TOP_K + r) * tiles_per_batch + j % tiles_per_batch,)
        return pl.BlockSpec((TT,), index, memory_space=pltpu.SMEM)

    return pl.pallas_call(
        _combine_kernel,
        grid=(n_tiles,),
        in_specs=_slot_specs(tiles_per_batch) + [next_spec(r) for r in range(TOP_K)]
        + [pl.BlockSpec((TT, D), lambda i: (i, 0)),
           pl.BlockSpec((TT, TOP_K), lambda i: (i, 0)),
           pl.BlockSpec((1, N_MOD, D), lambda i: (i // tiles_per_batch, 0, 0)),
           pl.BlockSpec((1, D), lambda i: (0, 0)),
           pl.BlockSpec(memory_space=pl.ANY)],
        out_specs=pl.BlockSpec((TT, D), lambda i: (i, 0)),
        out_shape=jax.ShapeDtypeStruct((N, D), F32),
        scratch_shapes=[pltpu.VMEM((2, TOP_K, TT * SUBLANES, LANES), F32),
                        pltpu.SemaphoreType.DMA((2,))],
        compiler_params=_params(("arbitrary",)),
        name="combine",
    )(*([pos_flat] * (2 * TOP_K)), x1f, probs, mod3, g.reshape(1, D), ys)


def _routing(idx, rank, counts):
    B, K, S = idx.shape
    n_exp = counts.shape[0]
    TM = EXPERT_BLOCK
    n_assign = B * S * K
    padded = ((counts + TM - 1) // TM) * TM
    pad_ends = jnp.cumsum(padded)
    pad_starts = pad_ends - padded
    experts = jnp.arange(n_exp, dtype=jnp.int32).reshape(n_exp, 1, 1, 1)
    base = jnp.sum(jnp.where(idx[None] == experts, pad_starts.reshape(n_exp, 1, 1, 1), 0),
                   axis=0)
    pos = ((base + rank) * SUBLANES).astype(jnp.int32).reshape(n_assign)
    n_blocks = n_assign // TM + n_exp
    first_block = jnp.concatenate([pad_starts, pad_ends[-1:]]) // TM
    block_count = jnp.concatenate([padded // TM, n_blocks - pad_ends[-1:] // TM])
    tail_start = jnp.where(padded > 0, pad_ends - TM, -1)
    spare = pad_ends[-1] + jnp.arange(n_exp, dtype=jnp.int32) * TM
    fill_start = jnp.concatenate(
        [tail_start, jnp.where(spare < n_blocks * TM, spare, -1)]).astype(jnp.int32)
    return (pos, first_block.astype(jnp.int32), block_count.astype(jnp.int32),
            fill_start * SUBLANES, n_blocks * TM)


def kernel(x, c, ada_w, ada_b, mix_norm_g, w_in, conv_w, conv_b, rg_w, rg_b, ig_w, ig_b,
           lru_lambda, attn_out_g, lru_out_g, w_out, ffn_norm_g, router_w, router_b,
           exp_w_gate, exp_b_gate, exp_w_up, exp_b_up, exp_w_down, exp_b_down, final_norm_g):
    B, S, D = x.shape
    depth = ada_w.shape[0]
    assert S % 512 == 0 and S % TOKEN_TILE == 0 and D == SUBLANES * LANES
    for l in range(depth):
        mod3 = _ada(c, ada_w[l], ada_b[l]).reshape(B, N_MOD, D)
        q, k, v, xr, gr = _inproj(x, mod3, mix_norm_g[l], w_in[l].astype(BF16))
        ya = _attention(q, k, v)
        yl = _lru(xr, gr, conv_w[l], conv_b[l], rg_w[l], rg_b[l], ig_w[l], ig_b[l],
                  lru_lambda[l], lru_out_g[l])
        x1, h2, idx, prob, rank, cnt = _outproj(
            x, ya, yl, mod3, attn_out_g[l], w_out[l].astype(BF16), ffn_norm_g[l],
            router_w[l].T, router_b[l])
        pos, first_block, block_count, fill_start, cap = _routing(idx, rank, cnt[:, 0])
        xs = _dispatch(h2, pos, fill_start, cap, S // TOKEN_TILE)
        ys = _experts(xs, first_block, block_count, exp_w_gate[l], exp_b_gate[l],
                      exp_w_up[l], exp_b_up[l], exp_w_down[l], exp_b_down[l])
        probs = jnp.transpose(prob, (0, 2, 1)).reshape(B * S, TOP_K)
        assert depth == 1
        x = _combine(x1.reshape(B * S, D), probs, pos, mod3, final_norm_g, ys,
                     S // TOKEN_TILE).reshape(B, S, D)
    return x
```

```python
import functools

import jax
import jax.numpy as jnp
from jax import lax
from jax.experimental import pallas as pl
from jax.experimental.pallas import tpu as pltpu

F32 = jnp.float32
BF16 = jnp.bfloat16
HIGHEST = lax.Precision.HIGHEST

EPS = 1e-6
N_MOD = 6
SB_HEADS = 8
HEAD_DIM = 64
SB_WIDTH = SB_HEADS * HEAD_DIM
LRU_BLOCKS = 8
CONV_WIDTH = 4
LRU_C = 8.0
TOP_K = 4
SWIGLU_LIMIT = 7.0
SWIGLU_ALPHA = 1.702
LOG2E = 1.4426950408889634

LANES = 128
SUBLANES = 8
VMEM_LIMIT = 56 * 1024 * 1024

Q_BLOCK = 128
K_BLOCK = 128
ATTN_WINDOW_BLOCKS = 5
ATTN_LOOP_BLOCKS = 2
ATTN_UNDERFLOW_LOG = -110.0
EXPERT_BLOCK = 512
TOKEN_TILE = 256
ROW_DMA_UNROLL = 8
LRU_SCAN_ROWS = 256


def _params(sem):
    return pltpu.CompilerParams(dimension_semantics=sem, vmem_limit_bytes=VMEM_LIMIT)


def _ada_kernel(c_ref, w_ref, b_ref, o_ref):
    c = c_ref[...]
    ca = c * jax.nn.sigmoid(c)
    o_ref[...] = jnp.dot(ca, w_ref[...], precision=HIGHEST,
                         preferred_element_type=F32) + b_ref[...]


def _ada(c, ada_w, ada_b):
    B, D = c.shape
    E = ada_w.shape[1]
    tn = 1024
    return pl.pallas_call(
        _ada_kernel,
        grid=(E // tn,),
        in_specs=[pl.BlockSpec((B, D), lambda j: (0, 0)),
                  pl.BlockSpec((D, tn), lambda j: (0, j)),
                  pl.BlockSpec((1, tn), lambda j: (0, j))],
        out_specs=pl.BlockSpec((B, tn), lambda j: (0, j)),
        out_shape=jax.ShapeDtypeStruct((B, E), F32),
        compiler_params=_params(("arbitrary",)),
        name="ada",
    )(c, ada_w, ada_b.reshape(1, E))


def _rms(x, g):
    ms = jnp.mean(x * x, axis=-1, keepdims=True)
    return x * lax.rsqrt(ms + EPS) * g


def _softplus(x):
    return jnp.maximum(x, 0.0) + jnp.log1p(jnp.exp(-jnp.abs(x)))


def _gelu_tanh(x):
    return 0.5 * x * (1.0 + jnp.tanh(0.7978845608028654 * (x + 0.044715 * x * x * x)))


LRU_PAD = 8


def _lru_tile(x, gate, cw_ref, cb_ref, wr_ref, br_ref, wi_ref, bi_ref, lam_ref, g_ref,
              xext, hc):
    T = x.shape[0]
    xext[LRU_PAD:LRU_PAD + T, :] = x
    xc = cb_ref[...] + cw_ref[CONV_WIDTH - 1:CONV_WIDTH, :] * x
    for j in range(CONV_WIDTH - 1):
        back = CONV_WIDTH - 1 - j
        xc = xc + cw_ref[j:j + 1, :] * xext[LRU_PAD - back:LRU_PAD - back + T, :]
    xext[0:LRU_PAD, :] = xext[T:T + LRU_PAD, :]

    xb = xc.astype(BF16)
    r = jax.nn.sigmoid(jnp.dot(xb, wr_ref[...], preferred_element_type=F32) + br_ref[...])
    ig = jax.nn.sigmoid(jnp.dot(xb, wi_ref[...], preferred_element_type=F32) + bi_ref[...])
    log_a = (-LRU_C) * r * _softplus(-lam_ref[...])
    a = jnp.exp(log_a)
    b = jnp.sqrt(-jnp.tanh(log_a) * (a * a + 1.0)) * (ig * xc)

    G = T // SUBLANES
    a = a.reshape(G, SUBLANES, -1)
    b = b.reshape(G, SUBLANES, -1)
    rows = lax.broadcasted_iota(jnp.int32, a.shape, 1)
    d = 1
    while d < SUBLANES:
        keep = rows >= d
        a_prev = jnp.where(keep, pltpu.roll(a, d, 1), 1.0)
        b_prev = jnp.where(keep, pltpu.roll(b, d, 1), 0.0)
        b = a * b_prev + b
        a = a * a_prev
        d *= 2
    prev = hc[...]
    groups = []
    for g in range(G):
        hg = a[g] * prev + b[g]
        prev = hg[SUBLANES - 1:SUBLANES, :]
        groups.append(hg)
    hc[...] = prev
    h = jnp.concatenate(groups, axis=0)
    return _rms(h * _gelu_tanh(gate), g_ref[...])


def _inproj_kernel(x_ref, mod_ref, g_ref, w_ref, cw_ref, cb_ref, wr_ref, br_ref, wi_ref,
                   bi_ref, lam_ref, gl_ref, q_ref, k_ref, v_ref, yl_ref, xext, hc):
    @pl.when(pl.program_id(1) == 0)
    def _():
        xext[0:LRU_PAD, :] = jnp.zeros((LRU_PAD, xext.shape[1]), F32)
        hc[...] = jnp.zeros_like(hc)

    x = x_ref[0]
    h = _rms(x, g_ref[...]) * (1.0 + mod_ref[0, 1:2, :]) + mod_ref[0, 0:1, :]
    hb = h.astype(BF16)
    W = SB_WIDTH

    def proj(c):
        return jnp.dot(hb, w_ref[:, c * W:(c + 1) * W], preferred_element_type=F32)

    xr, gr = proj(3), proj(4)
    T = LRU_SCAN_ROWS
    for t in range(x.shape[0] // T):
        rows = slice(t * T, (t + 1) * T)
        yl_ref[0, rows, :] = _lru_tile(xr[rows], gr[rows], cw_ref, cb_ref, wr_ref, br_ref,
                                       wi_ref, bi_ref, lam_ref, gl_ref, xext, hc)
    q_ref[0] = (proj(0) * (HEAD_DIM ** -0.5)).astype(BF16)
    k_ref[0] = proj(1).astype(BF16)
    v_ref[0] = proj(2).astype(BF16)


def _block_diag(w):
    H, I, J = w.shape
    eye = jnp.eye(H, dtype=w.dtype)
    return (w[:, :, None, :] * eye[:, None, :, None]).reshape(H * I, H * J)


def _inproj(x, mod3, g, w_in_b, conv_w, conv_b, rg_w, rg_b, ig_w, ig_b, lam, g_lru, ts=512):
    B, S, D = x.shape
    E = w_in_b.shape[1]
    W = SB_WIDTH
    row = pl.BlockSpec((1, ts, W), lambda b, s: (b, s, 0))
    vec = pl.BlockSpec((1, W), lambda b, s: (0, 0))
    mat = pl.BlockSpec((W, W), lambda b, s: (0, 0))
    return pl.pallas_call(
        _inproj_kernel,
        grid=(B, S // ts),
        in_specs=[pl.BlockSpec((1, ts, D), lambda b, s: (b, s, 0)),
                  pl.BlockSpec((1, N_MOD, D), lambda b, s: (b, 0, 0)),
                  pl.BlockSpec((1, D), lambda b, s: (0, 0)),
                  pl.BlockSpec((D, E), lambda b, s: (0, 0)),
                  pl.BlockSpec((CONV_WIDTH, W), lambda b, s: (0, 0)), vec,
                  mat, vec, mat, vec, vec, vec],
        out_specs=[row, row, row, row],
        out_shape=[jax.ShapeDtypeStruct((B, S, W), BF16)] * 3
        + [jax.ShapeDtypeStruct((B, S, W), F32)],
        scratch_shapes=[pltpu.VMEM((LRU_SCAN_ROWS + LRU_PAD, W), F32), pltpu.VMEM((1, W), F32)],
        compiler_params=_params(("arbitrary", "arbitrary")),
        name="inproj",
    )(x, mod3, g.reshape(1, D), w_in_b, conv_w, conv_b.reshape(1, W),
      _block_diag(rg_w).astype(BF16), rg_b.reshape(1, W), _block_diag(ig_w).astype(BF16),
      ig_b.reshape(1, W), lam.reshape(1, W), g_lru.reshape(1, W))


def _attn_kernel(q_ref, k_ref, v_ref, o_ref, acc_ref, carry_ref):
    i = pl.program_id(1)
    QB, KB = Q_BLOCK, K_BLOCK
    n_pairs = q_ref.shape[2] // LANES
    lane = lax.broadcasted_iota(jnp.int32, (QB, LANES), 1)

    def stacked_q(p):
        q = q_ref[0, :, p * LANES:(p + 1) * LANES]
        zero = jnp.zeros_like(q)
        return jnp.concatenate([jnp.where(lane < HEAD_DIM, q, zero),
                                jnp.where(lane >= HEAD_DIM, q, zero)], axis=0)

    qs = [stacked_q(p) for p in range(n_pairs)]

    uj = lax.broadcasted_iota(jnp.int32, (2 * KB, KB + LANES), 0) & (KB - 1)
    us = lax.broadcasted_iota(jnp.int32, (2 * KB, KB + LANES), 1)
    u2 = jnp.where((us >= KB) | (uj > us), -1.0, 0.0).astype(BF16)

    row = lax.broadcasted_iota(jnp.int32, (2 * QB, KB), 0) & (QB - 1)
    col = lax.broadcasted_iota(jnp.int32, (2 * QB, KB), 1)
    causal = col < row

    def tiles(p, j_hi, n, diagonal, acc, carry):
        start = pl.multiple_of((j_hi - (n - 1)) * KB, KB)
        kw = k_ref[0, pl.ds(start, n * KB), p * LANES:(p + 1) * LANES]
        vw = v_ref[0, pl.ds(start, n * KB), p * LANES:(p + 1) * LANES]
        z = lax.dot_general(qs[p], kw, (((1,), (1,)), ((), ())), preferred_element_type=F32)
        softplus = jnp.maximum(z, 0.0) + jnp.log(1.0 + jnp.exp2(jnp.abs(z) * (-LOG2E)))
        log_beta = z - softplus
        ws = [None] * n
        for c in reversed(range(n)):
            sp = softplus[:, c * KB:(c + 1) * KB]
            masked = diagonal and c == n - 1
            if masked:
                sp = jnp.where(causal, sp, 0.0)
            hi_f = lax.bitcast_convert_type(
                lax.bitcast_convert_type(sp, jnp.uint32) & jnp.uint32(0xFFFF0000), F32)
            hi_lo = jnp.concatenate([hi_f.astype(BF16), (sp - hi_f).astype(BF16)], axis=1)
            sums = jnp.dot(hi_lo, u2, preferred_element_type=F32)
            w = jnp.exp2((log_beta[:, c * KB:(c + 1) * KB] + sums[:, :KB] + carry) * LOG2E)
            if masked:
                w = jnp.where(causal, w, 0.0)
            ws[c] = w.astype(BF16)
            carry = carry + sums[:, KB:]
        w_all = ws[0] if n == 1 else jnp.concatenate(ws, axis=1)
        acc = acc + jnp.dot(w_all, vw, preferred_element_type=F32)
        return acc, carry

    def fold(j_hi, n, diagonal):
        cmax = None
        for p in range(n_pairs):
            if diagonal:
                acc = carry = jnp.zeros((2 * QB, LANES), F32)
            else:
                acc, carry = acc_ref[p], carry_ref[p]
            acc, carry = tiles(p, j_hi, n, diagonal, acc, carry)
            acc_ref[p] = acc
            carry_ref[p] = carry
            cmax = carry if cmax is None else jnp.maximum(cmax, carry)
        return jnp.max(cmax)

    n_first, n_loop = ATTN_WINDOW_BLOCKS, ATTN_LOOP_BLOCKS
    n0 = jnp.minimum(i + 1, n_first)
    cmax0 = lax.switch(n0 - 1, [functools.partial(fold, i, n, True)
                                for n in range(1, n_first + 1)])
    j0 = i - n0

    def loop(n, j, cmax):
        def more(st):
            return (st[0] >= n - 1) & (st[1] >= ATTN_UNDERFLOW_LOG)

        def step(st):
            return st[0] - n, fold(st[0], n, False)

        return lax.while_loop(more, step, (j, cmax))

    j, cmax = j0, cmax0
    while n_loop >= 1:
        j, cmax = loop(n_loop, j, cmax)
        n_loop //= 2
    for p in range(n_pairs):
        acc = acc_ref[p]
        o_ref[0, :, p * LANES:(p + 1) * LANES] = jnp.where(lane < HEAD_DIM, acc[:QB], acc[QB:])


def _attention(q, k, v):
    B, S, W = q.shape
    return pl.pallas_call(
        _attn_kernel,
        grid=(B, S // Q_BLOCK),
        in_specs=[pl.BlockSpec((1, Q_BLOCK, W), lambda b, i: (b, i, 0)),
                  pl.BlockSpec((1, S, W), lambda b, i: (b, 0, 0)),
                  pl.BlockSpec((1, S, W), lambda b, i: (b, 0, 0))],
        out_specs=pl.BlockSpec((1, Q_BLOCK, W), lambda b, i: (b, i, 0)),
        out_shape=jax.ShapeDtypeStruct((B, S, W), F32),
        scratch_shapes=[pltpu.VMEM((W // LANES, 2 * Q_BLOCK, LANES), F32),
                        pltpu.VMEM((W // LANES, 2 * Q_BLOCK, LANES), F32)],
        compiler_params=_params(("arbitrary", "arbitrary")),
        name="attn",
    )(q, k, v)


def _load_row_tiles(ref, n_rows, lead=()):
    chunks = [ref[lead + (pl.ds(c, n_rows, stride=SUBLANES), slice(None))]
              for c in range(SUBLANES)]
    return jnp.concatenate(chunks, axis=1)


def _store_row_tiles(ref, value):
    n_rows = value.shape[0]
    for c in range(SUBLANES):
        ref[pl.ds(c, n_rows, stride=SUBLANES), :] = value[:, c * LANES:(c + 1) * LANES]


def _outproj_kernel(x_ref, ya_ref, yl_ref, mod_ref, ga_ref, wo_ref, gf_ref, rw_ref, rb_ref,
                    x1_ref, h2_ref, idx_ref, prob_ref, rank_ref, cnt_ref, tri, run):
    W = SB_WIDTH
    ya = _rms(ya_ref[0], ga_ref[...]).astype(BF16)
    yl = yl_ref[0].astype(BF16)
    mix = (jnp.dot(ya, wo_ref[0:W, :], preferred_element_type=F32)
           + jnp.dot(yl, wo_ref[W:2 * W, :], preferred_element_type=F32))
    x1 = x_ref[0] + mod_ref[0, 2:3, :] * mix
    x1_ref[0] = x1
    h2 = _rms(x1, gf_ref[...]) * (1.0 + mod_ref[0, 4:5, :]) + mod_ref[0, 3:4, :]
    _store_row_tiles(h2_ref, h2)

    logits = lax.dot_general(rw_ref[...], h2, (((1,), (1,)), ((), ())), precision=HIGHEST,
                             preferred_element_type=F32) + rb_ref[...]
    n_exp = logits.shape[0]
    eid = lax.broadcasted_iota(jnp.int32, logits.shape, 0)
    vals, idxs = [], []
    for _ in range(TOP_K):
        m = jnp.max(logits, axis=0, keepdims=True)
        sel = jnp.min(jnp.where(logits == m, eid, n_exp), axis=0, keepdims=True)
        vals.append(m)
        idxs.append(sel)
        logits = jnp.where(eid == sel, -jnp.inf, logits)
    es = [jnp.exp(vv - vals[0]) for vv in vals]
    inv = 1.0 / (es[0] + es[1] + es[2] + es[3])

    first = (pl.program_id(0) == 0) & (pl.program_id(1) == 0)

    @pl.when(first)
    def _():
        ti = lax.broadcasted_iota(jnp.int32, tri.shape, 0)
        tj = lax.broadcasted_iota(jnp.int32, tri.shape, 1)
        tri[...] = jnp.where(ti < tj, 1.0, 0.0).astype(BF16)
        run[...] = jnp.zeros_like(run)

    hits = [eid == idxs[r] for r in range(TOP_K)]
    chosen = hits[0] | hits[1] | hits[2] | hits[3]
    cnt = jnp.where(chosen, 1.0, 0.0)
    before = jnp.dot(cnt.astype(BF16), tri[...], preferred_element_type=F32) + run[...]
    for r in range(TOP_K):
        idx_ref[0, r:r + 1, :] = idxs[r]
        prob_ref[0, r:r + 1, :] = es[r] * inv
        rank_ref[0, r:r + 1, :] = jnp.sum(jnp.where(hits[r], before, 0.0), axis=0,
                                          keepdims=True).astype(jnp.int32)
    run[...] = run[...] + jnp.sum(cnt, axis=1, keepdims=True)
    cnt_ref[...] = jnp.broadcast_to(run[...], cnt_ref.shape).astype(jnp.int32)


def _outproj(x, ya, yl, mod3, ga, w_out_b, gf, router_wt, router_b, ts=512):
    B, S, D = x.shape
    W = ya.shape[2]
    NE = router_wt.shape[0]
    rowd = pl.BlockSpec((1, ts, D), lambda b, s: (b, s, 0))
    roww = pl.BlockSpec((1, ts, W), lambda b, s: (b, s, 0))
    sel = pl.BlockSpec((1, TOP_K, ts), lambda b, s: (b, 0, s))
    return pl.pallas_call(
        _outproj_kernel,
        grid=(B, S // ts),
        in_specs=[rowd, roww, roww,
                  pl.BlockSpec((1, N_MOD, D), lambda b, s: (b, 0, 0)),
                  pl.BlockSpec((1, W), lambda b, s: (0, 0)),
                  pl.BlockSpec((2 * W, D), lambda b, s: (0, 0)),
                  pl.BlockSpec((1, D), lambda b, s: (0, 0)),
                  pl.BlockSpec((NE, D), lambda b, s: (0, 0)),
                  pl.BlockSpec((NE, 1), lambda b, s: (0, 0))],
        out_specs=[rowd,
                   pl.BlockSpec((ts * SUBLANES, LANES), lambda b, s: (b * (S // ts) + s, 0)),
                   sel, sel, sel, pl.BlockSpec((NE, LANES), lambda b, s: (0, 0))],
        out_shape=[jax.ShapeDtypeStruct((B, S, D), F32),
                   jax.ShapeDtypeStruct((B * S * SUBLANES, LANES), F32),
                   jax.ShapeDtypeStruct((B, TOP_K, S), jnp.int32),
                   jax.ShapeDtypeStruct((B, TOP_K, S), F32),
                   jax.ShapeDtypeStruct((B, TOP_K, S), jnp.int32),
                   jax.ShapeDtypeStruct((NE, LANES), jnp.int32)],
        scratch_shapes=[pltpu.VMEM((ts, ts), BF16), pltpu.VMEM((NE, 1), F32)],
        compiler_params=_params(("arbitrary", "arbitrary")),
        name="outproj",
    )(x, ya, yl, mod3, ga.reshape(1, W), w_out_b, gf.reshape(1, D), router_wt,
      router_b.reshape(NE, 1))


def _dispatch_kernel(p0_ref, p1_ref, p2_ref, p3_ref, fill_ref, h_ref, xs_ref, zbuf, sem, zsem):
    TT = h_ref.shape[0] // SUBLANES
    n_fill = fill_ref.shape[0]

    @pl.when(pl.program_id(0) == 0)
    def _():
        zbuf[...] = jnp.zeros_like(zbuf)

        def fill_copy(e):
            start = pl.multiple_of(jnp.maximum(fill_ref[e], 0), SUBLANES)
            return pltpu.make_async_copy(zbuf, xs_ref.at[pl.ds(start, zbuf.shape[0]), :], zsem)

        def start(e, c):
            @pl.when(fill_ref[e] >= 0)
            def _():
                fill_copy(e).start()
            return c

        def wait(e, c):
            @pl.when(fill_ref[e] >= 0)
            def _():
                fill_copy(e).wait()
            return c

        lax.fori_loop(0, n_fill, start, 0)
        lax.fori_loop(0, n_fill, wait, 0)

    pos_refs = (p0_ref, p1_ref, p2_ref, p3_ref)

    def row_copy(t, r):
        src = pl.multiple_of(t * SUBLANES, SUBLANES)
        dst = pl.multiple_of(pos_refs[r][t], SUBLANES)
        return pltpu.make_async_copy(h_ref.at[pl.ds(src, SUBLANES), :],
                                     xs_ref.at[pl.ds(dst, SUBLANES), :], sem)

    def issue(g, c):
        for tt in range(ROW_DMA_UNROLL):
            for r in range(TOP_K):
                row_copy(g * ROW_DMA_UNROLL + tt, r).start(priority=(tt * TOP_K + r) % 2)
        return c

    lax.fori_loop(0, TT // ROW_DMA_UNROLL, issue, 0)
    for r in range(TOP_K):
        pltpu.make_async_copy(h_ref, xs_ref.at[pl.ds(0, TT * SUBLANES), :], sem).wait()


def _slot_specs(tiles_per_batch):
    def spec(r):
        return pl.BlockSpec(
            (TOKEN_TILE,),
            lambda i: ((i // tiles_per_batch * TOP_K + r) * tiles_per_batch
                       + i % tiles_per_batch,),
            memory_space=pltpu.SMEM)
    return [spec(r) for r in range(TOP_K)]


def _dispatch(h2t, pos_flat, fill_start, cap, tiles_per_batch):
    TT = TOKEN_TILE
    n_tiles = h2t.shape[0] // (TT * SUBLANES)
    return pl.pallas_call(
        _dispatch_kernel,
        grid=(n_tiles,),
        in_specs=_slot_specs(tiles_per_batch)
        + [pl.BlockSpec(memory_space=pltpu.SMEM),
           pl.BlockSpec((TT * SUBLANES, LANES), lambda i: (i, 0))],
        out_specs=pl.BlockSpec(memory_space=pl.ANY),
        out_shape=jax.ShapeDtypeStruct((cap * SUBLANES, LANES), F32),
        scratch_shapes=[pltpu.VMEM((EXPERT_BLOCK * SUBLANES, LANES), F32),
                        pltpu.SemaphoreType.DMA, pltpu.SemaphoreType.DMA],
        compiler_params=_params(("arbitrary",)),
        name="dispatch",
    )(pos_flat, pos_flat, pos_flat, pos_flat, fill_start, h2t)


def _expert_kernel(first_ref, count_ref, xs_ref, wg_ref, bg_ref, wu_ref, bu_ref, wd_ref,
                   bd_ref, ys_ref, wgb, wub, wdb, xbuf, ybuf, xsem, ysem):
    e = pl.program_id(0)
    n_exp = pl.num_programs(0)
    rows = xbuf.shape[1]
    first, count = first_ref[e], count_ref[e]

    wgb[...] = wg_ref[0].astype(BF16)
    wub[...] = wu_ref[0].astype(BF16)
    wdb[...] = wd_ref[0].astype(BF16)

    def block(ref, j):
        return ref.at[pl.ds(pl.multiple_of((first + j) * rows, rows), rows), :]

    def x_copy(j, slot):
        return pltpu.make_async_copy(block(xs_ref, j), xbuf.at[slot], xsem.at[slot])

    def y_copy(j, slot):
        return pltpu.make_async_copy(ybuf.at[slot], block(ys_ref, j), ysem.at[slot])

    @pl.when(count > 0)
    def _():
        x_copy(0, 0).start()

    def step(j, c):
        slot = j % 2
        x_copy(j, slot).wait()

        @pl.when(j + 1 < count)
        def _():
            x_copy(j + 1, 1 - slot).start()

        @pl.when(j >= 2)
        def _():
            y_copy(j - 2, slot).wait()

        xb = _load_row_tiles(xbuf, EXPERT_BLOCK, (slot,)).astype(BF16)
        g = jnp.minimum(jnp.dot(xb, wgb[...], preferred_element_type=F32) + bg_ref[0],
                        SWIGLU_LIMIT)
        u = jnp.clip(jnp.dot(xb, wub[...], preferred_element_type=F32) + bu_ref[0],
                     -SWIGLU_LIMIT, SWIGLU_LIMIT)
        act = (u + 1.0) * (g * jax.nn.sigmoid(SWIGLU_ALPHA * g))
        _store_row_tiles(ybuf.at[slot], jnp.dot(act.astype(BF16), wdb[...],
                                                preferred_element_type=F32) + bd_ref[0])
        y_copy(j, slot).start()
        return c

    lax.fori_loop(0, count, step, 0)

    @pl.when(count >= 2)
    def _():
        y_copy(count - 2, count % 2).wait()

    @pl.when(count >= 1)
    def _():
        y_copy(count - 1, (count - 1) % 2).wait()

    @pl.when(e == n_exp - 1)
    def _():
        spare_first, spare_count = first_ref[n_exp], count_ref[n_exp]
        ybuf[0] = jnp.zeros(ybuf.shape[1:], F32)

        def spare_copy(j):
            dst = pl.ds(pl.multiple_of((spare_first + j) * rows, rows), rows)
            return pltpu.make_async_copy(ybuf.at[0], ys_ref.at[dst, :], ysem.at[0])

        def start(j, c):
            spare_copy(j).start()
            return c

        def wait(j, c):
            spare_copy(j).wait()
            return c

        lax.fori_loop(0, spare_count, start, 0)
        lax.fori_loop(0, spare_count, wait, 0)


def _experts(xs, first_block, block_count, wg, bg, wu, bu, wd, bd):
    NE, D, DE = wg.shape
    tile_rows = EXPERT_BLOCK * SUBLANES

    def wsel(e, first, count):
        return (e, 0, 0)

    grid_spec = pltpu.PrefetchScalarGridSpec(
        num_scalar_prefetch=2,
        grid=(NE,),
        in_specs=[pl.BlockSpec(memory_space=pl.ANY),
                  pl.BlockSpec((1, D, DE), wsel), pl.BlockSpec((1, 1, DE), wsel),
                  pl.BlockSpec((1, D, DE), wsel), pl.BlockSpec((1, 1, DE), wsel),
                  pl.BlockSpec((1, DE, D), wsel), pl.BlockSpec((1, 1, D), wsel)],
        out_specs=pl.BlockSpec(memory_space=pl.ANY),
        scratch_shapes=[pltpu.VMEM((D, DE), BF16), pltpu.VMEM((D, DE), BF16),
                        pltpu.VMEM((DE, D), BF16),
                        pltpu.VMEM((2, tile_rows, LANES), F32),
                        pltpu.VMEM((2, tile_rows, LANES), F32),
                        pltpu.SemaphoreType.DMA((2,)), pltpu.SemaphoreType.DMA((2,))],
    )
    return pl.pallas_call(
        _expert_kernel,
        grid_spec=grid_spec,
        out_shape=jax.ShapeDtypeStruct(xs.shape, F32),
        compiler_params=_params(("arbitrary",)),
        name="experts",
    )(first_block, block_count, xs, wg, bg.reshape(NE, 1, DE), wu, bu.reshape(NE, 1, DE), wd,
      bd.reshape(NE, 1, D))


def _combine_kernel(c0_ref, c1_ref, c2_ref, c3_ref, n0_ref, n1_ref, n2_ref, n3_ref,
                    x1_ref, p_ref, mod_ref, g_ref, ys_ref, o_ref, buf, sem):
    TT = x1_ref.shape[0]
    i = pl.program_id(0)

    def gather(pos_refs, s):
        def issue(g, c):
            for tt in range(ROW_DMA_UNROLL):
                t = g * ROW_DMA_UNROLL + tt
                for r in range(TOP_K):
                    pltpu.make_async_copy(
                        ys_ref.at[pl.ds(pl.multiple_of(pos_refs[r][t], SUBLANES), SUBLANES), :],
                        buf.at[s, r, pl.ds(pl.multiple_of(t * SUBLANES, SUBLANES), SUBLANES), :],
                        sem.at[s]).start(priority=(tt * TOP_K + r) % 2)
            return c
        lax.fori_loop(0, TT // ROW_DMA_UNROLL, issue, 0)

    @pl.when(i == 0)
    def _():
        gather((c0_ref, c1_ref, c2_ref, c3_ref), 0)

    def reduce_tile(s):
        @pl.when(i + 1 < pl.num_programs(0))
        def _():
            gather((n0_ref, n1_ref, n2_ref, n3_ref), 1 - s)

        for r in range(TOP_K):
            pltpu.make_async_copy(ys_ref.at[pl.ds(0, TT * SUBLANES), :], buf.at[s, r],
                                  sem.at[s]).wait()
        p = p_ref[...]
        moe = p[:, 0:1] * _load_row_tiles(buf, TT, (s, 0))
        for r in range(1, TOP_K):
            moe = moe + p[:, r:r + 1] * _load_row_tiles(buf, TT, (s, r))
        x2 = x1_ref[...] + mod_ref[0, 5:6, :] * moe
        o_ref[...] = _rms(x2, g_ref[...])

    for s in range(2):
        pl.when(i % 2 == s)(functools.partial(reduce_tile, s))


def _combine(x1f, probs, pos_flat, mod3, g, ys, tiles_per_batch):
    N, D = x1f.shape
    TT = TOKEN_TILE
    n_tiles = N // TT

    def next_spec(r):
        def index(i):
            j = jnp.minimum(i + 1, n_tiles - 1)
            return ((j // tiles_per_batch * TOP_K + r) * tiles_per_batch + j % tiles_per_batch,)
        return pl.BlockSpec((TT,), index, memory_space=pltpu.SMEM)

    return pl.pallas_call(
        _combine_kernel,
        grid=(n_tiles,),
        in_specs=_slot_specs(tiles_per_batch) + [next_spec(r) for r in range(TOP_K)]
        + [pl.BlockSpec((TT, D), lambda i: (i, 0)),
           pl.BlockSpec((TT, TOP_K), lambda i: (i, 0)),
           pl.BlockSpec((1, N_MOD, D), lambda i: (i // tiles_per_batch, 0, 0)),
           pl.BlockSpec((1, D), lambda i: (0, 0)),
           pl.BlockSpec(memory_space=pl.ANY)],
        out_specs=pl.BlockSpec((TT, D), lambda i: (i, 0)),
        out_shape=jax.ShapeDtypeStruct((N, D), F32),
        scratch_shapes=[pltpu.VMEM((2, TOP_K, TT * SUBLANES, LANES), F32),
                        pltpu.SemaphoreType.DMA((2,))],
        compiler_params=_params(("arbitrary",)),
        name="combine",
    )(*([pos_flat] * (2 * TOP_K)), x1f, probs, mod3, g.reshape(1, D), ys)


def _routing(idx, rank, counts):
    B, K, S = idx.shape
    n_exp = counts.shape[0]
    TM = EXPERT_BLOCK
    n_assign = B * S * K
    padded = ((counts + TM - 1) // TM) * TM
    pad_ends = jnp.cumsum(padded)
    pad_starts = pad_ends - padded
    experts = jnp.arange(n_exp, dtype=jnp.int32).reshape(n_exp, 1, 1, 1)
    base = jnp.sum(jnp.where(idx[None] == experts, pad_starts.reshape(n_exp, 1, 1, 1), 0),
                   axis=0)
    pos = ((base + rank) * SUBLANES).astype(jnp.int32).reshape(n_assign)
    n_blocks = n_assign // TM + n_exp
    first_block = jnp.concatenate([pad_starts, pad_ends[-1:]]) // TM
    block_count = jnp.concatenate([padded // TM, n_blocks - pad_ends[-1:] // TM])
    tail_start = jnp.where(padded > 0, pad_ends - TM, -1)
    spare = pad_ends[-1] + jnp.arange(n_exp, dtype=jnp.int32) * TM
    fill_start = jnp.concatenate(
        [tail_start, jnp.where(spare < n_blocks * TM, spare, -1)]).astype(jnp.int32)
    return (pos, first_block.astype(jnp.int32), block_count.astype(jnp.int32),
            fill_start * SUBLANES, n_blocks * TM)


def kernel(x, c, ada_w, ada_b, mix_norm_g, w_in, conv_w, conv_b, rg_w, rg_b, ig_w, ig_b,
           lru_lambda, attn_out_g, lru_out_g, w_out, ffn_norm_g, router_w, router_b,
           exp_w_gate, exp_b_gate, exp_w_up, exp_b_up, exp_w_down, exp_b_down, final_norm_g):
    B, S, D = x.shape
    depth = ada_w.shape[0]
    assert S % 512 == 0 and S % TOKEN_TILE == 0 and D == SUBLANES * LANES
    for l in range(depth):
        mod3 = _ada(c, ada_w[l], ada_b[l]).reshape(B, N_MOD, D)
        q, k, v, yl = _inproj(x, mod3, mix_norm_g[l], w_in[l].astype(BF16), conv_w[l],
                              conv_b[l], rg_w[l], rg_b[l], ig_w[l], ig_b[l], lru_lambda[l],
                              lru_out_g[l])
        ya = _attention(q, k, v)
        x1, h2, idx, prob, rank, cnt = _outproj(
            x, ya, yl, mod3, attn_out_g[l], w_out[l].astype(BF16), ffn_norm_g[l],
            router_w[l].T, router_b[l])
        pos, first_block, block_count, fill_start, cap = _routing(idx, rank, cnt[:, 0])
        xs = _dispatch(h2, pos, fill_start, cap, S // TOKEN_TILE)
        ys = _experts(xs, first_block, block_count, exp_w_gate[l], exp_b_gate[l],
                      exp_w_up[l], exp_b_up[l], exp_w_down[l], exp_b_down[l])
        probs = jnp.transpose(prob, (0, 2, 1)).reshape(B * S, TOP_K)
        assert depth == 1
        x = _combine(x1.reshape(B * S, D), probs, pos, mod3, final_norm_g, ys,
                     S // TOKEN_TILE).reshape(B, S, D)
    return x
```

```python
import functools

import jax
import jax.numpy as jnp
from jax import lax
from jax.experimental import pallas as pl
from jax.experimental.pallas import tpu as pltpu

F32 = jnp.float32
BF16 = jnp.bfloat16
HIGHEST = lax.Precision.HIGHEST

EPS = 1e-6
N_MOD = 6
SB_HEADS = 8
HEAD_DIM = 64
SB_WIDTH = SB_HEADS * HEAD_DIM
LRU_BLOCKS = 8
CONV_WIDTH = 4
LRU_C = 8.0
TOP_K = 4
SWIGLU_LIMIT = 7.0
SWIGLU_ALPHA = 1.702
LOG2E = 1.4426950408889634

LANES = 128
SUBLANES = 8
VMEM_LIMIT = 56 * 1024 * 1024

Q_BLOCK = 128
K_BLOCK = 128
ATTN_WINDOW_BLOCKS = 5
ATTN_LOOP_BLOCKS = 2
ATTN_UNDERFLOW_LOG = -110.0
EXPERT_BLOCK = 512
TOKEN_TILE = 256
ROW_DMA_UNROLL = 8
LRU_SCAN_ROWS = 256


def _params(sem):
    return pltpu.CompilerParams(dimension_semantics=sem, vmem_limit_bytes=VMEM_LIMIT)


def _ada_kernel(c_ref, w_ref, b_ref, o_ref):
    c = c_ref[...]
    ca = c * jax.nn.sigmoid(c)
    o_ref[...] = jnp.dot(ca, w_ref[...], precision=HIGHEST,
                         preferred_element_type=F32) + b_ref[...]


def _ada(c, ada_w, ada_b):
    B, D = c.shape
    E = ada_w.shape[1]
    tn = 1024
    return pl.pallas_call(
        _ada_kernel,
        grid=(E // tn,),
        in_specs=[pl.BlockSpec((B, D), lambda j: (0, 0)),
                  pl.BlockSpec((D, tn), lambda j: (0, j)),
                  pl.BlockSpec((1, tn), lambda j: (0, j))],
        out_specs=pl.BlockSpec((B, tn), lambda j: (0, j)),
        out_shape=jax.ShapeDtypeStruct((B, E), F32),
        compiler_params=_params(("arbitrary",)),
        name="ada",
    )(c, ada_w, ada_b.reshape(1, E))


def _rms(x, g):
    ms = jnp.mean(x * x, axis=-1, keepdims=True)
    return x * lax.rsqrt(ms + EPS) * g


def _softplus(x):
    return jnp.maximum(x, 0.0) + jnp.log1p(jnp.exp(-jnp.abs(x)))


def _gelu_tanh(x):
    return 0.5 * x * (1.0 + jnp.tanh(0.7978845608028654 * (x + 0.044715 * x * x * x)))


LRU_PAD = 8


def _lru_tile(x, gate, cw_ref, cb_ref, wr_ref, br_ref, wi_ref, bi_ref, lam_ref, g_ref,
              xext, hc):
    T = x.shape[0]
    xext[LRU_PAD:LRU_PAD + T, :] = x
    xc = cb_ref[...] + cw_ref[CONV_WIDTH - 1:CONV_WIDTH, :] * x
    for j in range(CONV_WIDTH - 1):
        back = CONV_WIDTH - 1 - j
        xc = xc + cw_ref[j:j + 1, :] * xext[LRU_PAD - back:LRU_PAD - back + T, :]
    xext[0:LRU_PAD, :] = xext[T:T + LRU_PAD, :]

    xb = xc.astype(BF16)
    r = jax.nn.sigmoid(jnp.dot(xb, wr_ref[...], preferred_element_type=F32) + br_ref[...])
    ig = jax.nn.sigmoid(jnp.dot(xb, wi_ref[...], preferred_element_type=F32) + bi_ref[...])
    log_a = (-LRU_C) * r * _softplus(-lam_ref[...])
    a = jnp.exp(log_a)
    b = jnp.sqrt(-jnp.tanh(log_a) * (a * a + 1.0)) * (ig * xc)

    G = T // SUBLANES
    a = a.reshape(G, SUBLANES, -1)
    b = b.reshape(G, SUBLANES, -1)
    rows = lax.broadcasted_iota(jnp.int32, a.shape, 1)
    d = 1
    while d < SUBLANES:
        keep = rows >= d
        a_prev = jnp.where(keep, pltpu.roll(a, d, 1), 1.0)
        b_prev = jnp.where(keep, pltpu.roll(b, d, 1), 0.0)
        b = a * b_prev + b
        a = a * a_prev
        d *= 2
    prev = hc[...]
    groups = []
    for g in range(G):
        hg = a[g] * prev + b[g]
        prev = hg[SUBLANES - 1:SUBLANES, :]
        groups.append(hg)
    hc[...] = prev
    h = jnp.concatenate(groups, axis=0)
    return _rms(h * _gelu_tanh(gate), g_ref[...])


def _inproj_kernel(x_ref, mod_ref, g_ref, w_ref, cw_ref, cb_ref, wr_ref, br_ref, wi_ref,
                   bi_ref, lam_ref, gl_ref, q_ref, k_ref, v_ref, yl_ref, xext, hc):
    @pl.when(pl.program_id(1) == 0)
    def _():
        xext[0:LRU_PAD, :] = jnp.zeros((LRU_PAD, xext.shape[1]), F32)
        hc[...] = jnp.zeros_like(hc)

    x = x_ref[0]
    h = _rms(x, g_ref[...]) * (1.0 + mod_ref[0, 1:2, :]) + mod_ref[0, 0:1, :]
    hb = h.astype(BF16)
    W = SB_WIDTH

    def proj(c):
        return jnp.dot(hb, w_ref[:, c * W:(c + 1) * W], preferred_element_type=F32)

    xr, gr = proj(3), proj(4)
    T = LRU_SCAN_ROWS
    for t in range(x.shape[0] // T):
        rows = slice(t * T, (t + 1) * T)
        yl_ref[0, rows, :] = _lru_tile(xr[rows], gr[rows], cw_ref, cb_ref, wr_ref, br_ref,
                                       wi_ref, bi_ref, lam_ref, gl_ref, xext, hc)
    q_ref[0] = (proj(0) * (HEAD_DIM ** -0.5)).astype(BF16)
    k_ref[0] = proj(1).astype(BF16)
    v_ref[0] = proj(2).astype(BF16)


def _block_diag(w):
    H, I, J = w.shape
    eye = jnp.eye(H, dtype=w.dtype)
    return (w[:, :, None, :] * eye[:, None, :, None]).reshape(H * I, H * J)


def _inproj(x, mod3, g, w_in_b, conv_w, conv_b, rg_w, rg_b, ig_w, ig_b, lam, g_lru, ts=512):
    B, S, D = x.shape
    E = w_in_b.shape[1]
    W = SB_WIDTH
    row = pl.BlockSpec((1, ts, W), lambda b, s: (b, s, 0))
    vec = pl.BlockSpec((1, W), lambda b, s: (0, 0))
    mat = pl.BlockSpec((W, W), lambda b, s: (0, 0))
    return pl.pallas_call(
        _inproj_kernel,
        grid=(B, S // ts),
        in_specs=[pl.BlockSpec((1, ts, D), lambda b, s: (b, s, 0)),
                  pl.BlockSpec((1, N_MOD, D), lambda b, s: (b, 0, 0)),
                  pl.BlockSpec((1, D), lambda b, s: (0, 0)),
                  pl.BlockSpec((D, E), lambda b, s: (0, 0)),
                  pl.BlockSpec((CONV_WIDTH, W), lambda b, s: (0, 0)), vec,
                  mat, vec, mat, vec, vec, vec],
        out_specs=[row, row, row, row],
        out_shape=[jax.ShapeDtypeStruct((B, S, W), BF16)] * 3
        + [jax.ShapeDtypeStruct((B, S, W), F32)],
        scratch_shapes=[pltpu.VMEM((LRU_SCAN_ROWS + LRU_PAD, W), F32), pltpu.VMEM((1, W), F32)],
        compiler_params=_params(("arbitrary", "arbitrary")),
        name="inproj",
    )(x, mod3, g.reshape(1, D), w_in_b, conv_w, conv_b.reshape(1, W),
      _block_diag(rg_w).astype(BF16), rg_b.reshape(1, W), _block_diag(ig_w).astype(BF16),
      ig_b.reshape(1, W), lam.reshape(1, W), g_lru.reshape(1, W))


def _attn_kernel(q_ref, k_ref, v_ref, o_ref, acc_ref, carry_ref):
    i = pl.program_id(1)
    QB, KB = Q_BLOCK, K_BLOCK
    n_pairs = q_ref.shape[2] // LANES
    lane = lax.broadcasted_iota(jnp.int32, (QB, LANES), 1)

    def stacked_q(p):
        q = q_ref[0, :, p * LANES:(p + 1) * LANES]
        zero = jnp.zeros_like(q)
        return jnp.concatenate([jnp.where(lane < HEAD_DIM, q, zero),
                                jnp.where(lane >= HEAD_DIM, q, zero)], axis=0)

    qs = [stacked_q(p) for p in range(n_pairs)]

    uj = lax.broadcasted_iota(jnp.int32, (2 * KB, KB + LANES), 0) & (KB - 1)
    us = lax.broadcasted_iota(jnp.int32, (2 * KB, KB + LANES), 1)
    u2 = jnp.where((us >= KB) | (uj > us), -1.0, 0.0).astype(BF16)

    row = lax.broadcasted_iota(jnp.int32, (2 * QB, KB), 0) & (QB - 1)
    col = lax.broadcasted_iota(jnp.int32, (2 * QB, KB), 1)
    causal = col < row

    def tiles(p, j_hi, n, diagonal, acc, carry):
        start = pl.multiple_of((j_hi - (n - 1)) * KB, KB)
        kw = k_ref[0, pl.ds(start, n * KB), p * LANES:(p + 1) * LANES]
        vw = v_ref[0, pl.ds(start, n * KB), p * LANES:(p + 1) * LANES]
        z = lax.dot_general(qs[p], kw, (((1,), (1,)), ((), ())), preferred_element_type=F32)
        softplus = jnp.maximum(z, 0.0) + jnp.log(1.0 + jnp.exp2(jnp.abs(z) * (-LOG2E)))
        log_beta = z - softplus
        ws = [None] * n
        for c in reversed(range(n)):
            sp = softplus[:, c * KB:(c + 1) * KB]
            masked = diagonal and c == n - 1
            if masked:
                sp = jnp.where(causal, sp, 0.0)
            hi_f = lax.bitcast_convert_type(
                lax.bitcast_convert_type(sp, jnp.uint32) & jnp.uint32(0xFFFF0000), F32)
            hi_lo = jnp.concatenate([hi_f.astype(BF16), (sp - hi_f).astype(BF16)], axis=1)
            sums = jnp.dot(hi_lo, u2, preferred_element_type=F32)
            w = jnp.exp2((log_beta[:, c * KB:(c + 1) * KB] + sums[:, :KB] + carry) * LOG2E)
            if masked:
                w = jnp.where(causal, w, 0.0)
            ws[c] = w.astype(BF16)
            carry = carry + sums[:, KB:]
        w_all = ws[0] if n == 1 else jnp.concatenate(ws, axis=1)
        acc = acc + jnp.dot(w_all, vw, preferred_element_type=F32)
        return acc, carry

    def fold(j_hi, n, diagonal):
        cmax = None
        for p in range(n_pairs):
            if diagonal:
                acc = carry = jnp.zeros((2 * QB, LANES), F32)
            else:
                acc, carry = acc_ref[p], carry_ref[p]
            acc, carry = tiles(p, j_hi, n, diagonal, acc, carry)
            acc_ref[p] = acc
            carry_ref[p] = carry
            cmax = carry if cmax is None else jnp.maximum(cmax, carry)
        return jnp.max(cmax)

    n_first, n_loop = ATTN_WINDOW_BLOCKS, ATTN_LOOP_BLOCKS
    n0 = jnp.minimum(i + 1, n_first)
    cmax0 = lax.switch(n0 - 1, [functools.partial(fold, i, n, True)
                                for n in range(1, n_first + 1)])
    j0 = i - n0

    def loop(n, j, cmax):
        def more(st):
            return (st[0] >= n - 1) & (st[1] >= ATTN_UNDERFLOW_LOG)

        def step(st):
            return st[0] - n, fold(st[0], n, False)

        return lax.while_loop(more, step, (j, cmax))

    j, cmax = j0, cmax0
    while n_loop >= 1:
        j, cmax = loop(n_loop, j, cmax)
        n_loop //= 2
    for p in range(n_pairs):
        acc = acc_ref[p]
        o_ref[0, :, p * LANES:(p + 1) * LANES] = jnp.where(lane < HEAD_DIM, acc[:QB], acc[QB:])


def _attention(q, k, v):
    B, S, W = q.shape
    return pl.pallas_call(
        _attn_kernel,
        grid=(B, S // Q_BLOCK),
        in_specs=[pl.BlockSpec((1, Q_BLOCK, W), lambda b, i: (b, i, 0)),
                  pl.BlockSpec((1, S, W), lambda b, i: (b, 0, 0)),
                  pl.BlockSpec((1, S, W), lambda b, i: (b, 0, 0))],
        out_specs=pl.BlockSpec((1, Q_BLOCK, W), lambda b, i: (b, i, 0)),
        out_shape=jax.ShapeDtypeStruct((B, S, W), F32),
        scratch_shapes=[pltpu.VMEM((W // LANES, 2 * Q_BLOCK, LANES), F32),
                        pltpu.VMEM((W // LANES, 2 * Q_BLOCK, LANES), F32)],
        compiler_params=_params(("arbitrary", "arbitrary")),
        name="attn",
    )(q, k, v)


def _load_row_tiles(ref, n_rows, lead=()):
    chunks = [ref[lead + (pl.ds(c, n_rows, stride=SUBLANES), slice(None))]
              for c in range(SUBLANES)]
    return jnp.concatenate(chunks, axis=1)


def _store_row_tiles(ref, value):
    n_rows = value.shape[0]
    for c in range(SUBLANES):
        ref[pl.ds(c, n_rows, stride=SUBLANES), :] = value[:, c * LANES:(c + 1) * LANES]


PACK_ROWS = SUBLANES // 2


def _store_packed_rows(ref, value):
    n_rows, d = value.shape
    bits = lax.bitcast_convert_type(value.astype(BF16).astype(F32), jnp.uint32)
    words = (bits[:, :d // 2] >> 16) | bits[:, d // 2:]
    for c in range(PACK_ROWS):
        ref[pl.ds(c, n_rows, stride=PACK_ROWS), :] = words[:, c * LANES:(c + 1) * LANES]


def _load_packed_rows(ref, n_rows, lead=()):
    lo, hi = [], []
    for c in range(PACK_ROWS):
        w = ref[lead + (pl.ds(c, n_rows, stride=PACK_ROWS), slice(None))]
        lo.append(lax.bitcast_convert_type(w << 16, F32).astype(BF16))
        hi.append(lax.bitcast_convert_type(w & jnp.uint32(0xFFFF0000), F32).astype(BF16))
    return jnp.concatenate(lo + hi, axis=1)


def _outproj_kernel(x_ref, ya_ref, yl_ref, mod_ref, ga_ref, wo_ref, gf_ref, rw_ref, rb_ref,
                    x1_ref, h2_ref, idx_ref, prob_ref, rank_ref, cnt_ref, tri, run):
    W = SB_WIDTH
    ya = _rms(ya_ref[0], ga_ref[...]).astype(BF16)
    yl = yl_ref[0].astype(BF16)
    mix = (jnp.dot(ya, wo_ref[0:W, :], preferred_element_type=F32)
           + jnp.dot(yl, wo_ref[W:2 * W, :], preferred_element_type=F32))
    x1 = x_ref[0] + mod_ref[0, 2:3, :] * mix
    x1_ref[0] = x1
    h2 = _rms(x1, gf_ref[...]) * (1.0 + mod_ref[0, 4:5, :]) + mod_ref[0, 3:4, :]
    _store_packed_rows(h2_ref, h2)

    logits = lax.dot_general(rw_ref[...], h2, (((1,), (1,)), ((), ())), precision=HIGHEST,
                             preferred_element_type=F32) + rb_ref[...]
    n_exp = logits.shape[0]
    eid = lax.broadcasted_iota(jnp.int32, logits.shape, 0)
    vals, idxs = [], []
    for _ in range(TOP_K):
        m = jnp.max(logits, axis=0, keepdims=True)
        sel = jnp.min(jnp.where(logits == m, eid, n_exp), axis=0, keepdims=True)
        vals.append(m)
        idxs.append(sel)
        logits = jnp.where(eid == sel, -jnp.inf, logits)
    es = [jnp.exp(vv - vals[0]) for vv in vals]
    inv = 1.0 / (es[0] + es[1] + es[2] + es[3])

    first = (pl.program_id(0) == 0) & (pl.program_id(1) == 0)

    @pl.when(first)
    def _():
        ti = lax.broadcasted_iota(jnp.int32, tri.shape, 0)
        tj = lax.broadcasted_iota(jnp.int32, tri.shape, 1)
        tri[...] = jnp.where(ti < tj, 1.0, 0.0).astype(BF16)
        run[...] = jnp.zeros_like(run)

    hits = [eid == idxs[r] for r in range(TOP_K)]
    chosen = hits[0] | hits[1] | hits[2] | hits[3]
    cnt = jnp.where(chosen, 1.0, 0.0)
    before = jnp.dot(cnt.astype(BF16), tri[...], preferred_element_type=F32) + run[...]
    for r in range(TOP_K):
        idx_ref[0, r:r + 1, :] = idxs[r]
        prob_ref[0, r:r + 1, :] = es[r] * inv
        rank_ref[0, r:r + 1, :] = jnp.sum(jnp.where(hits[r], before, 0.0), axis=0,
                                          keepdims=True).astype(jnp.int32)
    run[...] = run[...] + jnp.sum(cnt, axis=1, keepdims=True)
    cnt_ref[...] = jnp.broadcast_to(run[...], cnt_ref.shape).astype(jnp.int32)


def _outproj(x, ya, yl, mod3, ga, w_out_b, gf, router_wt, router_b, ts=512):
    B, S, D = x.shape
    W = ya.shape[2]
    NE = router_wt.shape[0]
    rowd = pl.BlockSpec((1, ts, D), lambda b, s: (b, s, 0))
    roww = pl.BlockSpec((1, ts, W), lambda b, s: (b, s, 0))
    sel = pl.BlockSpec((1, TOP_K, ts), lambda b, s: (b, 0, s))
    return pl.pallas_call(
        _outproj_kernel,
        grid=(B, S // ts),
        in_specs=[rowd, roww, roww,
                  pl.BlockSpec((1, N_MOD, D), lambda b, s: (b, 0, 0)),
                  pl.BlockSpec((1, W), lambda b, s: (0, 0)),
                  pl.BlockSpec((2 * W, D), lambda b, s: (0, 0)),
                  pl.BlockSpec((1, D), lambda b, s: (0, 0)),
                  pl.BlockSpec((NE, D), lambda b, s: (0, 0)),
                  pl.BlockSpec((NE, 1), lambda b, s: (0, 0))],
        out_specs=[rowd,
                   pl.BlockSpec((ts * PACK_ROWS, LANES), lambda b, s: (b * (S // ts) + s, 0)),
                   sel, sel, sel, pl.BlockSpec((NE, LANES), lambda b, s: (0, 0))],
        out_shape=[jax.ShapeDtypeStruct((B, S, D), F32),
                   jax.ShapeDtypeStruct((B * S * PACK_ROWS, LANES), jnp.uint32),
                   jax.ShapeDtypeStruct((B, TOP_K, S), jnp.int32),
                   jax.ShapeDtypeStruct((B, TOP_K, S), F32),
                   jax.ShapeDtypeStruct((B, TOP_K, S), jnp.int32),
                   jax.ShapeDtypeStruct((NE, LANES), jnp.int32)],
        scratch_shapes=[pltpu.VMEM((ts, ts), BF16), pltpu.VMEM((NE, 1), F32)],
        compiler_params=_params(("arbitrary", "arbitrary")),
        name="outproj",
    )(x, ya, yl, mod3, ga.reshape(1, W), w_out_b, gf.reshape(1, D), router_wt,
      router_b.reshape(NE, 1))


def _dispatch_kernel(p0_ref, p1_ref, p2_ref, p3_ref, fill_ref, h_ref, xs_ref, zbuf, sem, zsem):
    TT = h_ref.shape[0] // PACK_ROWS
    n_fill = fill_ref.shape[0]

    @pl.when(pl.program_id(0) == 0)
    def _():
        zbuf[...] = jnp.zeros_like(zbuf)

        def fill_copy(e):
            start = pl.multiple_of(jnp.maximum(fill_ref[e], 0), PACK_ROWS)
            return pltpu.make_async_copy(zbuf, xs_ref.at[pl.ds(start, zbuf.shape[0]), :], zsem)

        def start(e, c):
            @pl.when(fill_ref[e] >= 0)
            def _():
                fill_copy(e).start()
            return c

        def wait(e, c):
            @pl.when(fill_ref[e] >= 0)
            def _():
                fill_copy(e).wait()
            return c

        lax.fori_loop(0, n_fill, start, 0)
        lax.fori_loop(0, n_fill, wait, 0)

    pos_refs = (p0_ref, p1_ref, p2_ref, p3_ref)

    def row_copy(t, r):
        src = pl.multiple_of(t * PACK_ROWS, PACK_ROWS)
        dst = pl.multiple_of(pos_refs[r][t], PACK_ROWS)
        return pltpu.make_async_copy(h_ref.at[pl.ds(src, PACK_ROWS), :],
                                     xs_ref.at[pl.ds(dst, PACK_ROWS), :], sem)

    def issue(g, c):
        for tt in range(ROW_DMA_UNROLL):
            for r in range(TOP_K):
                row_copy(g * ROW_DMA_UNROLL + tt, r).start(priority=(tt * TOP_K + r) % 2)
        return c

    lax.fori_loop(0, TT // ROW_DMA_UNROLL, issue, 0)
    for r in range(TOP_K):
        pltpu.make_async_copy(h_ref, xs_ref.at[pl.ds(0, TT * PACK_ROWS), :], sem).wait()


def _slot_specs(tiles_per_batch):
    def spec(r):
        return pl.BlockSpec(
            (TOKEN_TILE,),
            lambda i: ((i // tiles_per_batch * TOP_K + r) * tiles_per_batch
                       + i % tiles_per_batch,),
            memory_space=pltpu.SMEM)
    return [spec(r) for r in range(TOP_K)]


def _dispatch(h2t, pos_flat, fill_start, cap, tiles_per_batch):
    TT = TOKEN_TILE
    n_tiles = h2t.shape[0] // (TT * PACK_ROWS)
    return pl.pallas_call(
        _dispatch_kernel,
        grid=(n_tiles,),
        in_specs=_slot_specs(tiles_per_batch)
        + [pl.BlockSpec(memory_space=pltpu.SMEM),
           pl.BlockSpec((TT * PACK_ROWS, LANES), lambda i: (i, 0))],
        out_specs=pl.BlockSpec(memory_space=pl.ANY),
        out_shape=jax.ShapeDtypeStruct((cap * PACK_ROWS, LANES), jnp.uint32),
        scratch_shapes=[pltpu.VMEM((EXPERT_BLOCK * PACK_ROWS, LANES), jnp.uint32),
                        pltpu.SemaphoreType.DMA, pltpu.SemaphoreType.DMA],
        compiler_params=_params(("arbitrary",)),
        name="dispatch",
    )(pos_flat, pos_flat, pos_flat, pos_flat, fill_start, h2t)


def _expert_kernel(first_ref, count_ref, xs_ref, wg_ref, bg_ref, wu_ref, bu_ref, wd_ref,
                   bd_ref, ys_ref, wgb, wub, wdb, xbuf, ybuf, xsem, ysem):
    e = pl.program_id(0)
    n_exp = pl.num_programs(0)
    rows = ybuf.shape[1]
    first, count = first_ref[e], count_ref[e]

    wgb[...] = wg_ref[0].astype(BF16)
    wub[...] = wu_ref[0].astype(BF16)
    wdb[...] = wd_ref[0].astype(BF16)

    def block(ref, buf, j):
        n = buf.shape[1]
        return ref.at[pl.ds(pl.multiple_of((first + j) * n, n), n), :]

    def x_copy(j, slot):
        return pltpu.make_async_copy(block(xs_ref, xbuf, j), xbuf.at[slot], xsem.at[slot])

    def y_copy(j, slot):
        return pltpu.make_async_copy(ybuf.at[slot], block(ys_ref, ybuf, j), ysem.at[slot])

    @pl.when(count > 0)
    def _():
        x_copy(0, 0).start()

    def step(j, c):
        slot = j % 2
        x_copy(j, slot).wait()

        @pl.when(j + 1 < count)
        def _():
            x_copy(j + 1, 1 - slot).start()

        @pl.when(j >= 2)
        def _():
            y_copy(j - 2, slot).wait()

        xb = _load_packed_rows(xbuf, EXPERT_BLOCK, (slot,))
        g = jnp.minimum(jnp.dot(xb, wgb[...], preferred_element_type=F32) + bg_ref[0],
                        SWIGLU_LIMIT)
        u = jnp.clip(jnp.dot(xb, wub[...], preferred_element_type=F32) + bu_ref[0],
                     -SWIGLU_LIMIT, SWIGLU_LIMIT)
        act = (u + 1.0) * (g * jax.nn.sigmoid(SWIGLU_ALPHA * g))
        _store_row_tiles(ybuf.at[slot], jnp.dot(act.astype(BF16), wdb[...],
                                                preferred_element_type=F32) + bd_ref[0])
        y_copy(j, slot).start()
        return c

    lax.fori_loop(0, count, step, 0)

    @pl.when(count >= 2)
    def _():
        y_copy(count - 2, count % 2).wait()

    @pl.when(count >= 1)
    def _():
        y_copy(count - 1, (count - 1) % 2).wait()

    @pl.when(e == n_exp - 1)
    def _():
        spare_first, spare_count = first_ref[n_exp], count_ref[n_exp]
        ybuf[0] = jnp.zeros(ybuf.shape[1:], F32)

        def spare_copy(j):
            dst = pl.ds(pl.multiple_of((spare_first + j) * rows, rows), rows)
            return pltpu.make_async_copy(ybuf.at[0], ys_ref.at[dst, :], ysem.at[0])

        def start(j, c):
            spare_copy(j).start()
            return c

        def wait(j, c):
            spare_copy(j).wait()
            return c

        lax.fori_loop(0, spare_count, start, 0)
        lax.fori_loop(0, spare_count, wait, 0)


def _experts(xs, first_block, block_count, wg, bg, wu, bu, wd, bd):
    NE, D, DE = wg.shape
    cap = xs.shape[0] // PACK_ROWS

    def wsel(e, first, count):
        return (e, 0, 0)

    grid_spec = pltpu.PrefetchScalarGridSpec(
        num_scalar_prefetch=2,
        grid=(NE,),
        in_specs=[pl.BlockSpec(memory_space=pl.ANY),
                  pl.BlockSpec((1, D, DE), wsel), pl.BlockSpec((1, 1, DE), wsel),
                  pl.BlockSpec((1, D, DE), wsel), pl.BlockSpec((1, 1, DE), wsel),
                  pl.BlockSpec((1, DE, D), wsel), pl.BlockSpec((1, 1, D), wsel)],
        out_specs=pl.BlockSpec(memory_space=pl.ANY),
        scratch_shapes=[pltpu.VMEM((D, DE), BF16), pltpu.VMEM((D, DE), BF16),
                        pltpu.VMEM((DE, D), BF16),
                        pltpu.VMEM((2, EXPERT_BLOCK * PACK_ROWS, LANES), jnp.uint32),
                        pltpu.VMEM((2, EXPERT_BLOCK * SUBLANES, LANES), F32),
                        pltpu.SemaphoreType.DMA((2,)), pltpu.SemaphoreType.DMA((2,))],
    )
    return pl.pallas_call(
        _expert_kernel,
        grid_spec=grid_spec,
        out_shape=jax.ShapeDtypeStruct((cap * SUBLANES, LANES), F32),
        compiler_params=_params(("arbitrary",)),
        name="experts",
    )(first_block, block_count, xs, wg, bg.reshape(NE, 1, DE), wu, bu.reshape(NE, 1, DE), wd,
      bd.reshape(NE, 1, D))


def _combine_kernel(c0_ref, c1_ref, c2_ref, c3_ref, n0_ref, n1_ref, n2_ref, n3_ref,
                    x1_ref, p_ref, mod_ref, g_ref, ys_ref, o_ref, buf, sem):
    TT = x1_ref.shape[0]
    i = pl.program_id(0)

    def gather(pos_refs, s):
        def issue(g, c):
            for tt in range(ROW_DMA_UNROLL):
                t = g * ROW_DMA_UNROLL + tt
                for r in range(TOP_K):
                    pltpu.make_async_copy(
                        ys_ref.at[pl.ds(pl.multiple_of(pos_refs[r][t], SUBLANES), SUBLANES), :],
                        buf.at[s, r, pl.ds(pl.multiple_of(t * SUBLANES, SUBLANES), SUBLANES), :],
                        sem.at[s]).start(priority=(tt * TOP_K + r) % 2)
            return c
        lax.fori_loop(0, TT // ROW_DMA_UNROLL, issue, 0)

    @pl.when(i == 0)
    def _():
        gather((c0_ref, c1_ref, c2_ref, c3_ref), 0)

    def reduce_tile(s):
        @pl.when(i + 1 < pl.num_programs(0))
        def _():
            gather((n0_ref, n1_ref, n2_ref, n3_ref), 1 - s)

        for r in range(TOP_K):
            pltpu.make_async_copy(ys_ref.at[pl.ds(0, TT * SUBLANES), :], buf.at[s, r],
                                  sem.at[s]).wait()
        p = p_ref[...]
        moe = p[:, 0:1] * _load_row_tiles(buf, TT, (s, 0))
        for r in range(1, TOP_K):
            moe = moe + p[:, r:r + 1] * _load_row_tiles(buf, TT, (s, r))
        x2 = x1_ref[...] + mod_ref[0, 5:6, :] * moe
        o_ref[...] = _rms(x2, g_ref[...])

    for s in range(2):
        pl.when(i % 2 == s)(functools.partial(reduce_tile, s))


def _combine(x1f, probs, pos_flat, mod3, g, ys, tiles_per_batch):
    N, D = x1f.shape
    TT = TOKEN_TILE
    n_tiles = N // TT

    def next_spec(r):
        def index(i):
            j = jnp.minimum(i + 1, n_tiles - 1)
            return ((j // tiles_per_batch * TOP_K + r) * tiles_per_batch + j % tiles_per_batch,)
        return pl.BlockSpec((TT,), index, memory_space=pltpu.SMEM)

    return pl.pallas_call(
        _combine_kernel,
        grid=(n_tiles,),
        in_specs=_slot_specs(tiles_per_batch) + [next_spec(r) for r in range(TOP_K)]
        + [pl.BlockSpec((TT, D), lambda i: (i, 0)),
           pl.BlockSpec((TT, TOP_K), lambda i: (i, 0)),
           pl.BlockSpec((1, N_MOD, D), lambda i: (i // tiles_per_batch, 0, 0)),
           pl.BlockSpec((1, D), lambda i: (0, 0)),
           pl.BlockSpec(memory_space=pl.ANY)],
        out_specs=pl.BlockSpec((TT, D), lambda i: (i, 0)),
        out_shape=jax.ShapeDtypeStruct((N, D), F32),
        scratch_shapes=[pltpu.VMEM((2, TOP_K, TT * SUBLANES, LANES), F32),
                        pltpu.SemaphoreType.DMA((2,))],
        compiler_params=_params(("arbitrary",)),
        name="combine",
    )(*([pos_flat] * (2 * TOP_K)), x1f, probs, mod3, g.reshape(1, D), ys)


def _routing(idx, rank, counts):
    B, K, S = idx.shape
    n_exp = counts.shape[0]
    TM = EXPERT_BLOCK
    n_assign = B * S * K
    padded = ((counts + TM - 1) // TM) * TM
    pad_ends = jnp.cumsum(padded)
    pad_starts = pad_ends - padded
    experts = jnp.arange(n_exp, dtype=jnp.int32).reshape(n_exp, 1, 1, 1)
    base = jnp.sum(jnp.where(idx[None] == experts, pad_starts.reshape(n_exp, 1, 1, 1), 0),
                   axis=0)
    pos = (base + rank).astype(jnp.int32).reshape(n_assign)
    n_blocks = n_assign // TM + n_exp
    first_block = jnp.concatenate([pad_starts, pad_ends[-1:]]) // TM
    block_count = jnp.concatenate([padded // TM, n_blocks - pad_ends[-1:] // TM])
    tail_start = jnp.where(padded > 0, pad_ends - TM, -1)
    spare = pad_ends[-1] + jnp.arange(n_exp, dtype=jnp.int32) * TM
    fill_start = jnp.concatenate(
        [tail_start, jnp.where(spare < n_blocks * TM, spare, -1)]).astype(jnp.int32)
    return (pos, first_block.astype(jnp.int32), block_count.astype(jnp.int32),
            fill_start, n_blocks * TM)


def kernel(x, c, ada_w, ada_b, mix_norm_g, w_in, conv_w, conv_b, rg_w, rg_b, ig_w, ig_b,
           lru_lambda, attn_out_g, lru_out_g, w_out, ffn_norm_g, router_w, router_b,
           exp_w_gate, exp_b_gate, exp_w_up, exp_b_up, exp_w_down, exp_b_down, final_norm_g):
    B, S, D = x.shape
    depth = ada_w.shape[0]
    assert S % 512 == 0 and S % TOKEN_TILE == 0 and D == SUBLANES * LANES
    for l in range(depth):
        mod3 = _ada(c, ada_w[l], ada_b[l]).reshape(B, N_MOD, D)
        q, k, v, yl = _inproj(x, mod3, mix_norm_g[l], w_in[l].astype(BF16), conv_w[l],
                              conv_b[l], rg_w[l], rg_b[l], ig_w[l], ig_b[l], lru_lambda[l],
                              lru_out_g[l])
        ya = _attention(q, k, v)
        x1, h2, idx, prob, rank, cnt = _outproj(
            x, ya, yl, mod3, attn_out_g[l], w_out[l].astype(BF16), ffn_norm_g[l],
            router_w[l].T, router_b[l])
        pos, first_block, block_count, fill_start, cap = _routing(idx, rank, cnt[:, 0])
        xs = _dispatch(h2, pos * PACK_ROWS, fill_start * PACK_ROWS, cap, S // TOKEN_TILE)
        ys = _experts(xs, first_block, block_count, exp_w_gate[l], exp_b_gate[l],
                      exp_w_up[l], exp_b_up[l], exp_w_down[l], exp_b_down[l])
        probs = jnp.transpose(prob, (0, 2, 1)).reshape(B * S, TOP_K)
        assert depth == 1
        x = _combine(x1.reshape(B * S, D), probs, pos * SUBLANES, mod3, final_norm_g, ys,
                     S // TOKEN_TILE).reshape(B, S, D)
    return x
```

```python
import functools

import jax
import jax.numpy as jnp
from jax import lax
from jax.experimental import pallas as pl
from jax.experimental.pallas import tpu as pltpu

F32 = jnp.float32
BF16 = jnp.bfloat16
HIGHEST = lax.Precision.HIGHEST

EPS = 1e-6
N_MOD = 6
SB_HEADS = 8
HEAD_DIM = 64
SB_WIDTH = SB_HEADS * HEAD_DIM
LRU_BLOCKS = 8
CONV_WIDTH = 4
LRU_C = 8.0
TOP_K = 4
SWIGLU_LIMIT = 7.0
SWIGLU_ALPHA = 1.702
LOG2E = 1.4426950408889634

LANES = 128
SUBLANES = 8
VMEM_LIMIT = 56 * 1024 * 1024

Q_BLOCK = 128
K_BLOCK = 128
ATTN_WINDOW_BLOCKS = 5
ATTN_LOOP_BLOCKS = 2
ATTN_UNDERFLOW_LOG = -110.0
EXPERT_BLOCK = 512
TOKEN_TILE = 256
ROW_DMA_UNROLL = 8
LRU_SCAN_ROWS = 256


def _params(sem):
    return pltpu.CompilerParams(dimension_semantics=sem, vmem_limit_bytes=VMEM_LIMIT)


def _ada_kernel(c_ref, w_ref, b_ref, o_ref):
    c = c_ref[...]
    ca = c * jax.nn.sigmoid(c)
    o_ref[...] = jnp.dot(ca, w_ref[...], precision=HIGHEST,
                         preferred_element_type=F32) + b_ref[...]


def _ada(c, ada_w, ada_b):
    B, D = c.shape
    E = ada_w.shape[1]
    tn = 1024
    return pl.pallas_call(
        _ada_kernel,
        grid=(E // tn,),
        in_specs=[pl.BlockSpec((B, D), lambda j: (0, 0)),
                  pl.BlockSpec((D, tn), lambda j: (0, j)),
                  pl.BlockSpec((1, tn), lambda j: (0, j))],
        out_specs=pl.BlockSpec((B, tn), lambda j: (0, j)),
        out_shape=jax.ShapeDtypeStruct((B, E), F32),
        compiler_params=_params(("arbitrary",)),
        name="ada",
    )(c, ada_w, ada_b.reshape(1, E))


def _rms(x, g):
    ms = jnp.mean(x * x, axis=-1, keepdims=True)
    return x * lax.rsqrt(ms + EPS) * g


def _softplus(x):
    return jnp.maximum(x, 0.0) + jnp.log1p(jnp.exp(-jnp.abs(x)))


def _gelu_tanh(x):
    return 0.5 * x * (1.0 + jnp.tanh(0.7978845608028654 * (x + 0.044715 * x * x * x)))


LRU_PAD = 8


def _lru_tile(x, gate, cw_ref, cb_ref, wr_ref, br_ref, wi_ref, bi_ref, lam_ref, g_ref,
              xext, hc):
    T = x.shape[0]
    xext[LRU_PAD:LRU_PAD + T, :] = x
    xc = cb_ref[...] + cw_ref[CONV_WIDTH - 1:CONV_WIDTH, :] * x
    for j in range(CONV_WIDTH - 1):
        back = CONV_WIDTH - 1 - j
        xc = xc + cw_ref[j:j + 1, :] * xext[LRU_PAD - back:LRU_PAD - back + T, :]
    xext[0:LRU_PAD, :] = xext[T:T + LRU_PAD, :]

    xb = xc.astype(BF16)
    r = jax.nn.sigmoid(jnp.dot(xb, wr_ref[...], preferred_element_type=F32) + br_ref[...])
    ig = jax.nn.sigmoid(jnp.dot(xb, wi_ref[...], preferred_element_type=F32) + bi_ref[...])
    log_a = (-LRU_C) * r * _softplus(-lam_ref[...])
    a = jnp.exp(log_a)
    b = jnp.sqrt(-jnp.tanh(log_a) * (a * a + 1.0)) * (ig * xc)

    G = T // SUBLANES
    a = a.reshape(G, SUBLANES, -1)
    b = b.reshape(G, SUBLANES, -1)
    rows = lax.broadcasted_iota(jnp.int32, a.shape, 1)
    d = 1
    while d < SUBLANES:
        keep = rows >= d
        a_prev = jnp.where(keep, pltpu.roll(a, d, 1), 1.0)
        b_prev = jnp.where(keep, pltpu.roll(b, d, 1), 0.0)
        b = a * b_prev + b
        a = a * a_prev
        d *= 2
    prev = hc[...]
    groups = []
    for g in range(G):
        hg = a[g] * prev + b[g]
        prev = hg[SUBLANES - 1:SUBLANES, :]
        groups.append(hg)
    hc[...] = prev
    h = jnp.concatenate(groups, axis=0)
    return _rms(h * _gelu_tanh(gate), g_ref[...])


def _inproj_kernel(x_ref, mod_ref, g_ref, w_ref, cw_ref, cb_ref, wr_ref, br_ref, wi_ref,
                   bi_ref, lam_ref, gl_ref, q_ref, k_ref, v_ref, yl_ref, xext, hc):
    @pl.when(pl.program_id(1) == 0)
    def _():
        xext[0:LRU_PAD, :] = jnp.zeros((LRU_PAD, xext.shape[1]), F32)
        hc[...] = jnp.zeros_like(hc)

    x = x_ref[0]
    h = _rms(x, g_ref[...]) * (1.0 + mod_ref[0, 1:2, :]) + mod_ref[0, 0:1, :]
    hb = h.astype(BF16)
    W = SB_WIDTH

    def proj(c):
        return jnp.dot(hb, w_ref[:, c * W:(c + 1) * W], preferred_element_type=F32)

    xr, gr = proj(3), proj(4)
    T = LRU_SCAN_ROWS
    for t in range(x.shape[0] // T):
        rows = slice(t * T, (t + 1) * T)
        yl_ref[0, rows, :] = _lru_tile(xr[rows], gr[rows], cw_ref, cb_ref, wr_ref, br_ref,
                                       wi_ref, bi_ref, lam_ref, gl_ref, xext, hc)
    q_ref[0] = (proj(0) * (HEAD_DIM ** -0.5)).astype(BF16)
    k_ref[0] = proj(1).astype(BF16)
    v_ref[0] = proj(2).astype(BF16)


def _block_diag(w):
    H, I, J = w.shape
    eye = jnp.eye(H, dtype=w.dtype)
    return (w[:, :, None, :] * eye[:, None, :, None]).reshape(H * I, H * J)


def _inproj(x, mod3, g, w_in_b, conv_w, conv_b, rg_w, rg_b, ig_w, ig_b, lam, g_lru, ts=512):
    B, S, D = x.shape
    E = w_in_b.shape[1]
    W = SB_WIDTH
    row = pl.BlockSpec((1, ts, W), lambda b, s: (b, s, 0))
    vec = pl.BlockSpec((1, W), lambda b, s: (0, 0))
    mat = pl.BlockSpec((W, W), lambda b, s: (0, 0))
    return pl.pallas_call(
        _inproj_kernel,
        grid=(B, S // ts),
        in_specs=[pl.BlockSpec((1, ts, D), lambda b, s: (b, s, 0)),
                  pl.BlockSpec((1, N_MOD, D), lambda b, s: (b, 0, 0)),
                  pl.BlockSpec((1, D), lambda b, s: (0, 0)),
                  pl.BlockSpec((D, E), lambda b, s: (0, 0)),
                  pl.BlockSpec((CONV_WIDTH, W), lambda b, s: (0, 0)), vec,
                  mat, vec, mat, vec, vec, vec],
        out_specs=[row, row, row, row],
        out_shape=[jax.ShapeDtypeStruct((B, S, W), BF16)] * 3
        + [jax.ShapeDtypeStruct((B, S, W), F32)],
        scratch_shapes=[pltpu.VMEM((LRU_SCAN_ROWS + LRU_PAD, W), F32), pltpu.VMEM((1, W), F32)],
        compiler_params=_params(("arbitrary", "arbitrary")),
        name="inproj",
    )(x, mod3, g.reshape(1, D), w_in_b, conv_w, conv_b.reshape(1, W),
      _block_diag(rg_w).astype(BF16), rg_b.reshape(1, W), _block_diag(ig_w).astype(BF16),
      ig_b.reshape(1, W), lam.reshape(1, W), g_lru.reshape(1, W))


def _attn_kernel(q_ref, k_ref, v_ref, o_ref, acc_ref, carry_ref):
    i = pl.program_id(1)
    QB, KB = Q_BLOCK, K_BLOCK
    n_pairs = q_ref.shape[2] // LANES
    lane = lax.broadcasted_iota(jnp.int32, (QB, LANES), 1)

    def stacked_q(p):
        q = q_ref[0, :, p * LANES:(p + 1) * LANES]
        zero = jnp.zeros_like(q)
        return jnp.concatenate([jnp.where(lane < HEAD_DIM, q, zero),
                                jnp.where(lane >= HEAD_DIM, q, zero)], axis=0)

    qs = [stacked_q(p) for p in range(n_pairs)]

    uj = lax.broadcasted_iota(jnp.int32, (2 * KB, KB + LANES), 0) & (KB - 1)
    us = lax.broadcasted_iota(jnp.int32, (2 * KB, KB + LANES), 1)
    u2 = jnp.where((us >= KB) | (uj > us), -1.0, 0.0).astype(BF16)

    row = lax.broadcasted_iota(jnp.int32, (2 * QB, KB), 0) & (QB - 1)
    col = lax.broadcasted_iota(jnp.int32, (2 * QB, KB), 1)
    causal = col < row

    def tiles(p, j_hi, n, diagonal, acc, carry):
        start = pl.multiple_of((j_hi - (n - 1)) * KB, KB)
        kw = k_ref[0, pl.ds(start, n * KB), p * LANES:(p + 1) * LANES]
        vw = v_ref[0, pl.ds(start, n * KB), p * LANES:(p + 1) * LANES]
        z = lax.dot_general(qs[p], kw, (((1,), (1,)), ((), ())), preferred_element_type=F32)
        softplus = jnp.maximum(z, 0.0) + jnp.log(1.0 + jnp.exp2(jnp.abs(z) * (-LOG2E)))
        log_beta = z - softplus
        ws = [None] * n
        for c in reversed(range(n)):
            sp = softplus[:, c * KB:(c + 1) * KB]
            masked = diagonal and c == n - 1
            if masked:
                sp = jnp.where(causal, sp, 0.0)
            hi_f = lax.bitcast_convert_type(
                lax.bitcast_convert_type(sp, jnp.uint32) & jnp.uint32(0xFFFF0000), F32)
            hi_lo = jnp.concatenate([hi_f.astype(BF16), (sp - hi_f).astype(BF16)], axis=1)
            sums = jnp.dot(hi_lo, u2, preferred_element_type=F32)
            w = jnp.exp2((log_beta[:, c * KB:(c + 1) * KB] + sums[:, :KB] + carry) * LOG2E)
            if masked:
                w = jnp.where(causal, w, 0.0)
            ws[c] = w.astype(BF16)
            carry = carry + sums[:, KB:]
        w_all = ws[0] if n == 1 else jnp.concatenate(ws, axis=1)
        acc = acc + jnp.dot(w_all, vw, preferred_element_type=F32)
        return acc, carry

    def fold(j_hi, n, diagonal):
        cmax = None
        for p in range(n_pairs):
            if diagonal:
                acc = carry = jnp.zeros((2 * QB, LANES), F32)
            else:
                acc, carry = acc_ref[p], carry_ref[p]
            acc, carry = tiles(p, j_hi, n, diagonal, acc, carry)
            acc_ref[p] = acc
            carry_ref[p] = carry
            cmax = carry if cmax is None else jnp.maximum(cmax, carry)
        return jnp.max(cmax)

    n_first, n_loop = ATTN_WINDOW_BLOCKS, ATTN_LOOP_BLOCKS
    n0 = jnp.minimum(i + 1, n_first)
    cmax0 = lax.switch(n0 - 1, [functools.partial(fold, i, n, True)
                                for n in range(1, n_first + 1)])
    j0 = i - n0

    def loop(n, j, cmax):
        def more(st):
            return (st[0] >= n - 1) & (st[1] >= ATTN_UNDERFLOW_LOG)

        def step(st):
            return st[0] - n, fold(st[0], n, False)

        return lax.while_loop(more, step, (j, cmax))

    j, cmax = j0, cmax0
    while n_loop >= 1:
        j, cmax = loop(n_loop, j, cmax)
        n_loop //= 2
    for p in range(n_pairs):
        acc = acc_ref[p]
        o_ref[0, :, p * LANES:(p + 1) * LANES] = jnp.where(lane < HEAD_DIM, acc[:QB], acc[QB:])


def _attention(q, k, v):
    B, S, W = q.shape
    return pl.pallas_call(
        _attn_kernel,
        grid=(B, S // Q_BLOCK),
        in_specs=[pl.BlockSpec((1, Q_BLOCK, W), lambda b, i: (b, i, 0)),
                  pl.BlockSpec((1, S, W), lambda b, i: (b, 0, 0)),
                  pl.BlockSpec((1, S, W), lambda b, i: (b, 0, 0))],
        out_specs=pl.BlockSpec((1, Q_BLOCK, W), lambda b, i: (b, i, 0)),
        out_shape=jax.ShapeDtypeStruct((B, S, W), F32),
        scratch_shapes=[pltpu.VMEM((W // LANES, 2 * Q_BLOCK, LANES), F32),
                        pltpu.VMEM((W // LANES, 2 * Q_BLOCK, LANES), F32)],
        compiler_params=_params(("arbitrary", "arbitrary")),
        name="attn",
    )(q, k, v)


def _load_row_tiles(ref, n_rows, lead=()):
    chunks = [ref[lead + (pl.ds(c, n_rows, stride=SUBLANES), slice(None))]
              for c in range(SUBLANES)]
    return jnp.concatenate(chunks, axis=1)


def _store_row_tiles(ref, value):
    n_rows = value.shape[0]
    for c in range(SUBLANES):
        ref[pl.ds(c, n_rows, stride=SUBLANES), :] = value[:, c * LANES:(c + 1) * LANES]


PACK_ROWS = SUBLANES // 2


def _store_packed_rows(ref, value):
    n_rows, d = value.shape
    bits = lax.bitcast_convert_type(value.astype(BF16).astype(F32), jnp.uint32)
    words = (bits[:, :d // 2] >> 16) | bits[:, d // 2:]
    for c in range(PACK_ROWS):
        ref[pl.ds(c, n_rows, stride=PACK_ROWS), :] = words[:, c * LANES:(c + 1) * LANES]


def _load_packed_rows(ref, n_rows, lead=()):
    lo, hi = [], []
    for c in range(PACK_ROWS):
        w = ref[lead + (pl.ds(c, n_rows, stride=PACK_ROWS), slice(None))]
        lo.append(lax.bitcast_convert_type(w << 16, F32).astype(BF16))
        hi.append(lax.bitcast_convert_type(w & jnp.uint32(0xFFFF0000), F32).astype(BF16))
    return jnp.concatenate(lo + hi, axis=1)


def _outproj_kernel(x_ref, ya_ref, yl_ref, mod_ref, ga_ref, wo_ref, gf_ref, rw_ref, rb_ref,
                    x1_ref, h2_ref, idx_ref, prob_ref, rank_ref, cnt_ref, tri, run):
    W = SB_WIDTH
    ya = _rms(ya_ref[0], ga_ref[...]).astype(BF16)
    yl = yl_ref[0].astype(BF16)
    mix = (jnp.dot(ya, wo_ref[0:W, :], preferred_element_type=F32)
           + jnp.dot(yl, wo_ref[W:2 * W, :], preferred_element_type=F32))
    x1 = x_ref[0] + mod_ref[0, 2:3, :] * mix
    x1_ref[0] = x1
    h2 = _rms(x1, gf_ref[...]) * (1.0 + mod_ref[0, 4:5, :]) + mod_ref[0, 3:4, :]
    _store_packed_rows(h2_ref, h2)

    logits = lax.dot_general(rw_ref[...], h2, (((1,), (1,)), ((), ())), precision=HIGHEST,
                             preferred_element_type=F32) + rb_ref[...]
    n_exp = logits.shape[0]
    eid = lax.broadcasted_iota(jnp.int32, logits.shape, 0)
    vals, idxs = [], []
    for _ in range(TOP_K):
        m = jnp.max(logits, axis=0, keepdims=True)
        sel = jnp.min(jnp.where(logits == m, eid, n_exp), axis=0, keepdims=True)
        vals.append(m)
        idxs.append(sel)
        logits = jnp.where(eid == sel, -jnp.inf, logits)
    es = [jnp.exp(vv - vals[0]) for vv in vals]
    inv = 1.0 / (es[0] + es[1] + es[2] + es[3])

    first = (pl.program_id(0) == 0) & (pl.program_id(1) == 0)

    @pl.when(first)
    def _():
        ti = lax.broadcasted_iota(jnp.int32, tri.shape, 0)
        tj = lax.broadcasted_iota(jnp.int32, tri.shape, 1)
        tri[...] = jnp.where(ti < tj, 1.0, 0.0).astype(BF16)
        run[...] = jnp.zeros_like(run)

    hits = [eid == idxs[r] for r in range(TOP_K)]
    chosen = hits[0] | hits[1] | hits[2] | hits[3]
    cnt = jnp.where(chosen, 1.0, 0.0)
    before = jnp.dot(cnt.astype(BF16), tri[...], preferred_element_type=F32) + run[...]
    for r in range(TOP_K):
        idx_ref[0, r:r + 1, :] = idxs[r]
        prob_ref[0, r:r + 1, :] = es[r] * inv
        rank_ref[0, r:r + 1, :] = jnp.sum(jnp.where(hits[r], before, 0.0), axis=0,
                                          keepdims=True).astype(jnp.int32)
    run[...] = run[...] + jnp.sum(cnt, axis=1, keepdims=True)
    cnt_ref[...] = jnp.broadcast_to(run[...], cnt_ref.shape).astype(jnp.int32)


def _outproj(x, ya, yl, mod3, ga, w_out_b, gf, router_wt, router_b, ts=512):
    B, S, D = x.shape
    W = ya.shape[2]
    NE = router_wt.shape[0]
    rowd = pl.BlockSpec((1, ts, D), lambda b, s: (b, s, 0))
    roww = pl.BlockSpec((1, ts, W), lambda b, s: (b, s, 0))
    sel = pl.BlockSpec((1, TOP_K, ts), lambda b, s: (b, 0, s))
    return pl.pallas_call(
        _outproj_kernel,
        grid=(B, S // ts),
        in_specs=[rowd, roww, roww,
                  pl.BlockSpec((1, N_MOD, D), lambda b, s: (b, 0, 0)),
                  pl.BlockSpec((1, W), lambda b, s: (0, 0)),
                  pl.BlockSpec((2 * W, D), lambda b, s: (0, 0)),
                  pl.BlockSpec((1, D), lambda b, s: (0, 0)),
                  pl.BlockSpec((NE, D), lambda b, s: (0, 0)),
                  pl.BlockSpec((NE, 1), lambda b, s: (0, 0))],
        out_specs=[rowd,
                   pl.BlockSpec((ts * PACK_ROWS, LANES), lambda b, s: (b * (S // ts) + s, 0)),
                   sel, sel, sel, pl.BlockSpec((NE, LANES), lambda b, s: (0, 0))],
        out_shape=[jax.ShapeDtypeStruct((B, S, D), F32),
                   jax.ShapeDtypeStruct((B * S * PACK_ROWS, LANES), jnp.uint32),
                   jax.ShapeDtypeStruct((B, TOP_K, S), jnp.int32),
                   jax.ShapeDtypeStruct((B, TOP_K, S), F32),
                   jax.ShapeDtypeStruct((B, TOP_K, S), jnp.int32),
                   jax.ShapeDtypeStruct((NE, LANES), jnp.int32)],
        scratch_shapes=[pltpu.VMEM((ts, ts), BF16), pltpu.VMEM((NE, 1), F32)],
        compiler_params=_params(("arbitrary", "arbitrary")),
        name="outproj",
    )(x, ya, yl, mod3, ga.reshape(1, W), w_out_b, gf.reshape(1, D), router_wt,
      router_b.reshape(NE, 1))


def _dispatch_kernel(p0_ref, p1_ref, p2_ref, p3_ref, fill_ref, h_ref, xs_ref, zbuf, sem, zsem):
    TT = h_ref.shape[0] // PACK_ROWS
    n_fill = fill_ref.shape[0]

    @pl.when(pl.program_id(0) == 0)
    def _():
        zbuf[...] = jnp.zeros_like(zbuf)

        def fill_copy(e):
            start = pl.multiple_of(jnp.maximum(fill_ref[e], 0), PACK_ROWS)
            return pltpu.make_async_copy(zbuf, xs_ref.at[pl.ds(start, zbuf.shape[0]), :], zsem)

        def start(e, c):
            @pl.when(fill_ref[e] >= 0)
            def _():
                fill_copy(e).start()
            return c

        def wait(e, c):
            @pl.when(fill_ref[e] >= 0)
            def _():
                fill_copy(e).wait()
            return c

        lax.fori_loop(0, n_fill, start, 0)
        lax.fori_loop(0, n_fill, wait, 0)

    pos_refs = (p0_ref, p1_ref, p2_ref, p3_ref)

    def row_copy(t, r):
        src = pl.multiple_of(t * PACK_ROWS, PACK_ROWS)
        dst = pl.multiple_of(pos_refs[r][t], PACK_ROWS)
        return pltpu.make_async_copy(h_ref.at[pl.ds(src, PACK_ROWS), :],
                                     xs_ref.at[pl.ds(dst, PACK_ROWS), :], sem)

    def issue(g, c):
        for tt in range(ROW_DMA_UNROLL):
            for r in range(TOP_K):
                row_copy(g * ROW_DMA_UNROLL + tt, r).start(priority=(tt * TOP_K + r) % 2)
        return c

    lax.fori_loop(0, TT // ROW_DMA_UNROLL, issue, 0)
    for r in range(TOP_K):
        pltpu.make_async_copy(h_ref, xs_ref.at[pl.ds(0, TT * PACK_ROWS), :], sem).wait()


def _slot_specs(tiles_per_batch):
    def spec(r):
        return pl.BlockSpec(
            (TOKEN_TILE,),
            lambda i: ((i // tiles_per_batch * TOP_K + r) * tiles_per_batch
                       + i % tiles_per_batch,),
            memory_space=pltpu.SMEM)
    return [spec(r) for r in range(TOP_K)]


def _dispatch(h2t, pos_flat, fill_start, cap, tiles_per_batch):
    TT = TOKEN_TILE
    n_tiles = h2t.shape[0] // (TT * PACK_ROWS)
    return pl.pallas_call(
        _dispatch_kernel,
        grid=(n_tiles,),
        in_specs=_slot_specs(tiles_per_batch)
        + [pl.BlockSpec(memory_space=pltpu.SMEM),
           pl.BlockSpec((TT * PACK_ROWS, LANES), lambda i: (i, 0))],
        out_specs=pl.BlockSpec(memory_space=pl.ANY),
        out_shape=jax.ShapeDtypeStruct((cap * PACK_ROWS, LANES), jnp.uint32),
        scratch_shapes=[pltpu.VMEM((EXPERT_BLOCK * PACK_ROWS, LANES), jnp.uint32),
                        pltpu.SemaphoreType.DMA, pltpu.SemaphoreType.DMA],
        compiler_params=_params(("arbitrary",)),
        name="dispatch",
    )(pos_flat, pos_flat, pos_flat, pos_flat, fill_start, h2t)


def _expert_kernel(first_ref, count_ref, xs_ref, wg_ref, bg_ref, wu_ref, bu_ref, wd_ref,
                   bd_ref, ys_ref, wgb, wub, wdb, xbuf, ybuf, xsem, ysem):
    e = pl.program_id(0)
    n_exp = pl.num_programs(0)
    rows = ybuf.shape[1]
    first, count = first_ref[e], count_ref[e]

    wgb[...] = wg_ref[0].astype(BF16)
    wub[...] = wu_ref[0].astype(BF16)
    wdb[...] = wd_ref[0].astype(BF16)

    def block(ref, buf, j):
        n = buf.shape[1]
        return ref.at[pl.ds(pl.multiple_of((first + j) * n, n), n), :]

    def x_copy(j, slot):
        return pltpu.make_async_copy(block(xs_ref, xbuf, j), xbuf.at[slot], xsem.at[slot])

    def y_copy(j, slot):
        return pltpu.make_async_copy(ybuf.at[slot], block(ys_ref, ybuf, j), ysem.at[slot])

    @pl.when(count > 0)
    def _():
        x_copy(0, 0).start()

    def step(j, c):
        slot = j % 2
        x_copy(j, slot).wait()

        @pl.when(j + 1 < count)
        def _():
            x_copy(j + 1, 1 - slot).start()

        @pl.when(j >= 2)
        def _():
            y_copy(j - 2, slot).wait()

        xb = _load_packed_rows(xbuf, EXPERT_BLOCK, (slot,))
        g = jnp.minimum(jnp.dot(xb, wgb[...], preferred_element_type=F32) + bg_ref[0],
                        SWIGLU_LIMIT)
        u = jnp.clip(jnp.dot(xb, wub[...], preferred_element_type=F32) + bu_ref[0],
                     -SWIGLU_LIMIT, SWIGLU_LIMIT)
        act = (u + 1.0) * (g * jax.nn.sigmoid(SWIGLU_ALPHA * g))
        _store_row_tiles(ybuf.at[slot], jnp.dot(act.astype(BF16), wdb[...],
                                                preferred_element_type=F32) + bd_ref[0])
        y_copy(j, slot).start()
        return c

    lax.fori_loop(0, count, step, 0)

    @pl.when(count >= 2)
    def _():
        y_copy(count - 2, count % 2).wait()

    @pl.when(count >= 1)
    def _():
        y_copy(count - 1, (count - 1) % 2).wait()

    @pl.when(e == n_exp - 1)
    def _():
        spare_first, spare_count = first_ref[n_exp], count_ref[n_exp]
        ybuf[0] = jnp.zeros(ybuf.shape[1:], F32)

        def spare_copy(j):
            dst = pl.ds(pl.multiple_of((spare_first + j) * rows, rows), rows)
            return pltpu.make_async_copy(ybuf.at[0], ys_ref.at[dst, :], ysem.at[0])

        def start(j, c):
            spare_copy(j).start()
            return c

        def wait(j, c):
            spare_copy(j).wait()
            return c

        lax.fori_loop(0, spare_count, start, 0)
        lax.fori_loop(0, spare_count, wait, 0)


def _experts(xs, first_block, block_count, wg, bg, wu, bu, wd, bd):
    NE, D, DE = wg.shape
    cap = xs.shape[0] // PACK_ROWS

    def wsel(e, first, count):
        return (e, 0, 0)

    grid_spec = pltpu.PrefetchScalarGridSpec(
        num_scalar_prefetch=2,
        grid=(NE,),
        in_specs=[pl.BlockSpec(memory_space=pl.ANY),
                  pl.BlockSpec((1, D, DE), wsel), pl.BlockSpec((1, 1, DE), wsel),
                  pl.BlockSpec((1, D, DE), wsel), pl.BlockSpec((1, 1, DE), wsel),
                  pl.BlockSpec((1, DE, D), wsel), pl.BlockSpec((1, 1, D), wsel)],
        out_specs=pl.BlockSpec(memory_space=pl.ANY),
        scratch_shapes=[pltpu.VMEM((D, DE), BF16), pltpu.VMEM((D, DE), BF16),
                        pltpu.VMEM((DE, D), BF16),
                        pltpu.VMEM((2, EXPERT_BLOCK * PACK_ROWS, LANES), jnp.uint32),
                        pltpu.VMEM((2, EXPERT_BLOCK * SUBLANES, LANES), F32),
                        pltpu.SemaphoreType.DMA((2,)), pltpu.SemaphoreType.DMA((2,))],
    )
    return pl.pallas_call(
        _expert_kernel,
        grid_spec=grid_spec,
        out_shape=jax.ShapeDtypeStruct((cap * SUBLANES, LANES), F32),
        compiler_params=_params(("arbitrary",)),
        name="experts",
    )(first_block, block_count, xs, wg, bg.reshape(NE, 1, DE), wu, bu.reshape(NE, 1, DE), wd,
      bd.reshape(NE, 1, D))


def _combine_kernel(c0_ref, c1_ref, c2_ref, c3_ref, n0_ref, n1_ref, n2_ref, n3_ref,
                    x1_ref, p_ref, mod_ref, g_ref, ys_ref, o_ref, buf, sem):
    TT = x1_ref.shape[0]
    i = pl.program_id(0)

    def gather(pos_refs, s):
        def issue(g, c):
            for tt in range(SUBLANES):
                t = g * SUBLANES + tt
                for r in range(TOP_K):
                    pltpu.make_async_copy(
                        ys_ref.at[pl.ds(pl.multiple_of(pos_refs[r][t], SUBLANES), SUBLANES), :],
                        buf.at[s, r, g, :, tt, :],
                        sem.at[s]).start(priority=(tt * TOP_K + r) % 2)
            return c
        lax.fori_loop(0, TT // SUBLANES, issue, 0)

    def rows_of(s, r):
        chunks = [buf[s, r, :, c, :, :].reshape(TT, LANES) for c in range(SUBLANES)]
        return jnp.concatenate(chunks, axis=1)

    @pl.when(i == 0)
    def _():
        gather((c0_ref, c1_ref, c2_ref, c3_ref), 0)

    def reduce_tile(s):
        @pl.when(i + 1 < pl.num_programs(0))
        def _():
            gather((n0_ref, n1_ref, n2_ref, n3_ref), 1 - s)

        for r in range(TOP_K):
            pltpu.make_async_copy(buf.at[s, r], buf.at[s, r], sem.at[s]).wait()
        p = p_ref[...]
        moe = p[:, 0:1] * rows_of(s, 0)
        for r in range(1, TOP_K):
            moe = moe + p[:, r:r + 1] * rows_of(s, r)
        x2 = x1_ref[...] + mod_ref[0, 5:6, :] * moe
        o_ref[...] = _rms(x2, g_ref[...])

    for s in range(2):
        pl.when(i % 2 == s)(functools.partial(reduce_tile, s))


def _combine(x1f, probs, pos_flat, mod3, g, ys, tiles_per_batch):
    N, D = x1f.shape
    TT = TOKEN_TILE
    n_tiles = N // TT

    def next_spec(r):
        def index(i):
            j = jnp.minimum(i + 1, n_tiles - 1)
            return ((j // tiles_per_batch * TOP_K + r) * tiles_per_batch + j % tiles_per_batch,)
        return pl.BlockSpec((TT,), index, memory_space=pltpu.SMEM)

    return pl.pallas_call(
        _combine_kernel,
        grid=(n_tiles,),
        in_specs=_slot_specs(tiles_per_batch) + [next_spec(r) for r in range(TOP_K)]
        + [pl.BlockSpec((TT, D), lambda i: (i, 0)),
           pl.BlockSpec((TT, TOP_K), lambda i: (i, 0)),
           pl.BlockSpec((1, N_MOD, D), lambda i: (i // tiles_per_batch, 0, 0)),
           pl.BlockSpec((1, D), lambda i: (0, 0)),
           pl.BlockSpec(memory_space=pl.ANY)],
        out_specs=pl.BlockSpec((TT, D), lambda i: (i, 0)),
        out_shape=jax.ShapeDtypeStruct((N, D), F32),
        scratch_shapes=[pltpu.VMEM((2, TOP_K, TT // SUBLANES, SUBLANES, SUBLANES, LANES), F32),
                        pltpu.SemaphoreType.DMA((2,))],
        compiler_params=_params(("arbitrary",)),
        name="combine",
    )(*([pos_flat] * (2 * TOP_K)), x1f, probs, mod3, g.reshape(1, D), ys)


def _routing(idx, rank, counts):
    B, K, S = idx.shape
    n_exp = counts.shape[0]
    TM = EXPERT_BLOCK
    n_assign = B * S * K
    padded = ((counts + TM - 1) // TM) * TM
    pad_ends = jnp.cumsum(padded)
    pad_starts = pad_ends - padded
    experts = jnp.arange(n_exp, dtype=jnp.int32).reshape(n_exp, 1, 1, 1)
    base = jnp.sum(jnp.where(idx[None] == experts, pad_starts.reshape(n_exp, 1, 1, 1), 0),
                   axis=0)
    pos = (base + rank).astype(jnp.int32).reshape(n_assign)
    n_blocks = n_assign // TM + n_exp
    first_block = jnp.concatenate([pad_starts, pad_ends[-1:]]) // TM
    block_count = jnp.concatenate([padded // TM, n_blocks - pad_ends[-1:] // TM])
    tail_start = jnp.where(padded > 0, pad_ends - TM, -1)
    spare = pad_ends[-1] + jnp.arange(n_exp, dtype=jnp.int32) * TM
    fill_start = jnp.concatenate(
        [tail_start, jnp.where(spare < n_blocks * TM, spare, -1)]).astype(jnp.int32)
    return (pos, first_block.astype(jnp.int32), block_count.astype(jnp.int32),
            fill_start, n_blocks * TM)


def kernel(x, c, ada_w, ada_b, mix_norm_g, w_in, conv_w, conv_b, rg_w, rg_b, ig_w, ig_b,
           lru_lambda, attn_out_g, lru_out_g, w_out, ffn_norm_g, router_w, router_b,
           exp_w_gate, exp_b_gate, exp_w_up, exp_b_up, exp_w_down, exp_b_down, final_norm_g):
    B, S, D = x.shape
    depth = ada_w.shape[0]
    assert S % 512 == 0 and S % TOKEN_TILE == 0 and D == SUBLANES * LANES
    for l in range(depth):
        mod3 = _ada(c, ada_w[l], ada_b[l]).reshape(B, N_MOD, D)
        q, k, v, yl = _inproj(x, mod3, mix_norm_g[l], w_in[l].astype(BF16), conv_w[l],
                              conv_b[l], rg_w[l], rg_b[l], ig_w[l], ig_b[l], lru_lambda[l],
                              lru_out_g[l])
        ya = _attention(q, k, v)
        x1, h2, idx, prob, rank, cnt = _outproj(
            x, ya, yl, mod3, attn_out_g[l], w_out[l].astype(BF16), ffn_norm_g[l],
            router_w[l].T, router_b[l])
        pos, first_block, block_count, fill_start, cap = _routing(idx, rank, cnt[:, 0])
        xs = _dispatch(h2, pos * PACK_ROWS, fill_start * PACK_ROWS, cap, S // TOKEN_TILE)
        ys = _experts(xs, first_block, block_count, exp_w_gate[l], exp_b_gate[l],
                      exp_w_up[l], exp_b_up[l], exp_w_down[l], exp_b_down[l])
        probs = jnp.transpose(prob, (0, 2, 1)).reshape(B * S, TOP_K)
        assert depth == 1
        x = _combine(x1.reshape(B * S, D), probs, pos * SUBLANES, mod3, final_norm_g, ys,
                     S // TOKEN_TILE).reshape(B, S, D)
    return x
```

```python
import functools

import jax
import jax.numpy as jnp
from jax import lax
from jax.experimental import pallas as pl
from jax.experimental.pallas import tpu as pltpu

F32 = jnp.float32
BF16 = jnp.bfloat16
HIGHEST = lax.Precision.HIGHEST

EPS = 1e-6
N_MOD = 6
SB_HEADS = 8
HEAD_DIM = 64
SB_WIDTH = SB_HEADS * HEAD_DIM
LRU_BLOCKS = 8
CONV_WIDTH = 4
LRU_C = 8.0
TOP_K = 4
SWIGLU_LIMIT = 7.0
SWIGLU_ALPHA = 1.702
LOG2E = 1.4426950408889634

LANES = 128
SUBLANES = 8
VMEM_LIMIT = 56 * 1024 * 1024

Q_BLOCK = 128
K_BLOCK = 128
ATTN_WINDOW_BLOCKS = 5
ATTN_LOOP_BLOCKS = 2
ATTN_UNDERFLOW_LOG = -110.0
EXPERT_BLOCK = 512
TOKEN_TILE = 256
ROW_DMA_UNROLL = 8
LRU_SCAN_ROWS = 256


def _params(sem):
    return pltpu.CompilerParams(dimension_semantics=sem, vmem_limit_bytes=VMEM_LIMIT)


def _ada_kernel(c_ref, w_ref, b_ref, o_ref):
    c = c_ref[...]
    ca = c * jax.nn.sigmoid(c)
    o_ref[...] = jnp.dot(ca, w_ref[...], precision=HIGHEST,
                         preferred_element_type=F32) + b_ref[...]


def _ada(c, ada_w, ada_b):
    B, D = c.shape
    E = ada_w.shape[1]
    tn = 1024
    return pl.pallas_call(
        _ada_kernel,
        grid=(E // tn,),
        in_specs=[pl.BlockSpec((B, D), lambda j: (0, 0)),
                  pl.BlockSpec((D, tn), lambda j: (0, j)),
                  pl.BlockSpec((1, tn), lambda j: (0, j))],
        out_specs=pl.BlockSpec((B, tn), lambda j: (0, j)),
        out_shape=jax.ShapeDtypeStruct((B, E), F32),
        compiler_params=_params(("arbitrary",)),
        name="ada",
    )(c, ada_w, ada_b.reshape(1, E))


def _rms(x, g):
    ms = jnp.mean(x * x, axis=-1, keepdims=True)
    return x * lax.rsqrt(ms + EPS) * g


def _softplus(x):
    return jnp.maximum(x, 0.0) + jnp.log1p(jnp.exp(-jnp.abs(x)))


def _gelu_tanh(x):
    return 0.5 * x * (1.0 + jnp.tanh(0.7978845608028654 * (x + 0.044715 * x * x * x)))


LRU_PAD = 8


def _lru_tile(x, gate, cw_ref, cb_ref, wr_ref, br_ref, wi_ref, bi_ref, lam_ref, g_ref,
              xext, hc):
    T = x.shape[0]
    xext[LRU_PAD:LRU_PAD + T, :] = x
    xc = cb_ref[...] + cw_ref[CONV_WIDTH - 1:CONV_WIDTH, :] * x
    for j in range(CONV_WIDTH - 1):
        back = CONV_WIDTH - 1 - j
        xc = xc + cw_ref[j:j + 1, :] * xext[LRU_PAD - back:LRU_PAD - back + T, :]
    xext[0:LRU_PAD, :] = xext[T:T + LRU_PAD, :]

    xb = xc.astype(BF16)
    r = jax.nn.sigmoid(jnp.dot(xb, wr_ref[...], preferred_element_type=F32) + br_ref[...])
    ig = jax.nn.sigmoid(jnp.dot(xb, wi_ref[...], preferred_element_type=F32) + bi_ref[...])
    log_a = (-LRU_C) * r * _softplus(-lam_ref[...])
    a = jnp.exp(log_a)
    b = jnp.sqrt(-jnp.tanh(log_a) * (a * a + 1.0)) * (ig * xc)

    G = T // SUBLANES
    a = a.reshape(G, SUBLANES, -1)
    b = b.reshape(G, SUBLANES, -1)
    rows = lax.broadcasted_iota(jnp.int32, a.shape, 1)
    d = 1
    while d < SUBLANES:
        keep = rows >= d
        a_prev = jnp.where(keep, pltpu.roll(a, d, 1), 1.0)
        b_prev = jnp.where(keep, pltpu.roll(b, d, 1), 0.0)
        b = a * b_prev + b
        a = a * a_prev
        d *= 2
    prev = hc[...]
    groups = []
    for g in range(G):
        hg = a[g] * prev + b[g]
        prev = hg[SUBLANES - 1:SUBLANES, :]
        groups.append(hg)
    hc[...] = prev
    h = jnp.concatenate(groups, axis=0)
    return _rms(h * _gelu_tanh(gate), g_ref[...])


def _inproj_kernel(x_ref, mod_ref, g_ref, w_ref, cw_ref, cb_ref, wr_ref, br_ref, wi_ref,
                   bi_ref, lam_ref, gl_ref, q_ref, k_ref, v_ref, yl_ref, xext, hc):
    @pl.when(pl.program_id(1) == 0)
    def _():
        xext[0:LRU_PAD, :] = jnp.zeros((LRU_PAD, xext.shape[1]), F32)
        hc[...] = jnp.zeros_like(hc)

    x = x_ref[0]
    h = _rms(x, g_ref[...]) * (1.0 + mod_ref[0, 1:2, :]) + mod_ref[0, 0:1, :]
    hb = h.astype(BF16)
    W = SB_WIDTH

    def proj(c):
        return jnp.dot(hb, w_ref[:, c * W:(c + 1) * W], preferred_element_type=F32)

    xr, gr = proj(3), proj(4)
    T = LRU_SCAN_ROWS
    for t in range(x.shape[0] // T):
        rows = slice(t * T, (t + 1) * T)
        yl_ref[0, rows, :] = _lru_tile(xr[rows], gr[rows], cw_ref, cb_ref, wr_ref, br_ref,
                                       wi_ref, bi_ref, lam_ref, gl_ref, xext, hc)
    q_ref[0] = (proj(0) * (HEAD_DIM ** -0.5)).astype(BF16)
    k_ref[0] = proj(1).astype(BF16)
    v_ref[0] = proj(2).astype(BF16)


def _block_diag(w):
    H, I, J = w.shape
    eye = jnp.eye(H, dtype=w.dtype)
    return (w[:, :, None, :] * eye[:, None, :, None]).reshape(H * I, H * J)


def _inproj(x, mod3, g, w_in_b, conv_w, conv_b, rg_w, rg_b, ig_w, ig_b, lam, g_lru, ts=512):
    B, S, D = x.shape
    E = w_in_b.shape[1]
    W = SB_WIDTH
    row = pl.BlockSpec((1, ts, W), lambda b, s: (b, s, 0))
    vec = pl.BlockSpec((1, W), lambda b, s: (0, 0))
    mat = pl.BlockSpec((W, W), lambda b, s: (0, 0))
    return pl.pallas_call(
        _inproj_kernel,
        grid=(B, S // ts),
        in_specs=[pl.BlockSpec((1, ts, D), lambda b, s: (b, s, 0)),
                  pl.BlockSpec((1, N_MOD, D), lambda b, s: (b, 0, 0)),
                  pl.BlockSpec((1, D), lambda b, s: (0, 0)),
                  pl.BlockSpec((D, E), lambda b, s: (0, 0)),
                  pl.BlockSpec((CONV_WIDTH, W), lambda b, s: (0, 0)), vec,
                  mat, vec, mat, vec, vec, vec],
        out_specs=[row, row, row, row],
        out_shape=[jax.ShapeDtypeStruct((B, S, W), BF16)] * 3
        + [jax.ShapeDtypeStruct((B, S, W), F32)],
        scratch_shapes=[pltpu.VMEM((LRU_SCAN_ROWS + LRU_PAD, W), F32), pltpu.VMEM((1, W), F32)],
        compiler_params=_params(("arbitrary", "arbitrary")),
        name="inproj",
    )(x, mod3, g.reshape(1, D), w_in_b, conv_w, conv_b.reshape(1, W),
      _block_diag(rg_w).astype(BF16), rg_b.reshape(1, W), _block_diag(ig_w).astype(BF16),
      ig_b.reshape(1, W), lam.reshape(1, W), g_lru.reshape(1, W))


def _attn_kernel(q_ref, k_ref, v_ref, o_ref, acc_ref, carry_ref):
    i = pl.program_id(1)
    QB, KB = Q_BLOCK, K_BLOCK
    n_pairs = q_ref.shape[2] // LANES
    lane = lax.broadcasted_iota(jnp.int32, (QB, LANES), 1)

    def stacked_q(p):
        q = q_ref[0, :, p * LANES:(p + 1) * LANES]
        zero = jnp.zeros_like(q)
        return jnp.concatenate([jnp.where(lane < HEAD_DIM, q, zero),
                                jnp.where(lane >= HEAD_DIM, q, zero)], axis=0)

    qs = [stacked_q(p) for p in range(n_pairs)]

    uj = lax.broadcasted_iota(jnp.int32, (2 * KB, KB + LANES), 0) & (KB - 1)
    us = lax.broadcasted_iota(jnp.int32, (2 * KB, KB + LANES), 1)
    u2 = jnp.where((us >= KB) | (uj > us), -1.0, 0.0).astype(BF16)

    row = lax.broadcasted_iota(jnp.int32, (2 * QB, KB), 0) & (QB - 1)
    col = lax.broadcasted_iota(jnp.int32, (2 * QB, KB), 1)
    causal = col < row

    def tiles(p, j_hi, n, diagonal, acc, carry):
        start = pl.multiple_of((j_hi - (n - 1)) * KB, KB)
        kw = k_ref[0, pl.ds(start, n * KB), p * LANES:(p + 1) * LANES]
        vw = v_ref[0, pl.ds(start, n * KB), p * LANES:(p + 1) * LANES]
        z = lax.dot_general(qs[p], kw, (((1,), (1,)), ((), ())), preferred_element_type=F32)
        softplus = jnp.maximum(z, 0.0) + jnp.log(1.0 + jnp.exp2(jnp.abs(z) * (-LOG2E)))
        log_beta = z - softplus
        ws = [None] * n
        for c in reversed(range(n)):
            sp = softplus[:, c * KB:(c + 1) * KB]
            masked = diagonal and c == n - 1
            if masked:
                sp = jnp.where(causal, sp, 0.0)
            hi_f = lax.bitcast_convert_type(
                lax.bitcast_convert_type(sp, jnp.uint32) & jnp.uint32(0xFFFF0000), F32)
            hi_lo = jnp.concatenate([hi_f.astype(BF16), (sp - hi_f).astype(BF16)], axis=1)
            sums = jnp.dot(hi_lo, u2, preferred_element_type=F32)
            w = jnp.exp2((log_beta[:, c * KB:(c + 1) * KB] + sums[:, :KB] + carry) * LOG2E)
            if masked:
                w = jnp.where(causal, w, 0.0)
            ws[c] = w.astype(BF16)
            carry = carry + sums[:, KB:]
        w_all = ws[0] if n == 1 else jnp.concatenate(ws, axis=1)
        acc = acc + jnp.dot(w_all, vw, preferred_element_type=F32)
        return acc, carry

    def fold(j_hi, n, diagonal):
        cmax = None
        for p in range(n_pairs):
            if diagonal:
                acc = carry = jnp.zeros((2 * QB, LANES), F32)
            else:
                acc, carry = acc_ref[p], carry_ref[p]
            acc, carry = tiles(p, j_hi, n, diagonal, acc, carry)
            acc_ref[p] = acc
            carry_ref[p] = carry
            cmax = carry if cmax is None else jnp.maximum(cmax, carry)
        return jnp.max(cmax)

    n_first, n_loop = ATTN_WINDOW_BLOCKS, ATTN_LOOP_BLOCKS
    n0 = jnp.minimum(i + 1, n_first)
    cmax0 = lax.switch(n0 - 1, [functools.partial(fold, i, n, True)
                                for n in range(1, n_first + 1)])
    j0 = i - n0

    def loop(n, j, cmax):
        def more(st):
            return (st[0] >= n - 1) & (st[1] >= ATTN_UNDERFLOW_LOG)

        def step(st):
            return st[0] - n, fold(st[0], n, False)

        return lax.while_loop(more, step, (j, cmax))

    j, cmax = j0, cmax0
    while n_loop >= 1:
        j, cmax = loop(n_loop, j, cmax)
        n_loop //= 2
    for p in range(n_pairs):
        acc = acc_ref[p]
        o_ref[0, :, p * LANES:(p + 1) * LANES] = jnp.where(lane < HEAD_DIM, acc[:QB], acc[QB:])


def _attention(q, k, v):
    B, S, W = q.shape
    return pl.pallas_call(
        _attn_kernel,
        grid=(B, S // Q_BLOCK),
        in_specs=[pl.BlockSpec((1, Q_BLOCK, W), lambda b, i: (b, i, 0)),
                  pl.BlockSpec((1, S, W), lambda b, i: (b, 0, 0)),
                  pl.BlockSpec((1, S, W), lambda b, i: (b, 0, 0))],
        out_specs=pl.BlockSpec((1, Q_BLOCK, W), lambda b, i: (b, i, 0)),
        out_shape=jax.ShapeDtypeStruct((B, S, W), F32),
        scratch_shapes=[pltpu.VMEM((W // LANES, 2 * Q_BLOCK, LANES), F32),
                        pltpu.VMEM((W // LANES, 2 * Q_BLOCK, LANES), F32)],
        compiler_params=_params(("arbitrary", "arbitrary")),
        name="attn",
    )(q, k, v)


def _load_row_tiles(ref, n_rows, lead=()):
    chunks = [ref[lead + (pl.ds(c, n_rows, stride=SUBLANES), slice(None))]
              for c in range(SUBLANES)]
    return jnp.concatenate(chunks, axis=1)


def _store_row_tiles(ref, value):
    n_rows = value.shape[0]
    for c in range(SUBLANES):
        ref[pl.ds(c, n_rows, stride=SUBLANES), :] = value[:, c * LANES:(c + 1) * LANES]


PACK_ROWS = SUBLANES // 2


def _store_packed_rows(ref, value):
    n_rows, d = value.shape
    bits = lax.bitcast_convert_type(value.astype(BF16).astype(F32), jnp.uint32)
    words = (bits[:, :d // 2] >> 16) | bits[:, d // 2:]
    for c in range(PACK_ROWS):
        ref[pl.ds(c, n_rows, stride=PACK_ROWS), :] = words[:, c * LANES:(c + 1) * LANES]


def _load_packed_rows(ref, n_rows, lead=()):
    lo, hi = [], []
    for c in range(PACK_ROWS):
        w = ref[lead + (pl.ds(c, n_rows, stride=PACK_ROWS), slice(None))]
        lo.append(lax.bitcast_convert_type(w << 16, F32).astype(BF16))
        hi.append(lax.bitcast_convert_type(w & jnp.uint32(0xFFFF0000), F32).astype(BF16))
    return jnp.concatenate(lo + hi, axis=1)


def _outproj_kernel(x_ref, ya_ref, yl_ref, mod_ref, ga_ref, wo_ref, gf_ref, rw_ref, rb_ref,
                    x1_ref, h2_ref, idx_ref, prob_ref, rank_ref, cnt_ref, tri, run):
    W = SB_WIDTH
    ya = _rms(ya_ref[0], ga_ref[...]).astype(BF16)
    yl = yl_ref[0].astype(BF16)
    mix = (jnp.dot(ya, wo_ref[0:W, :], preferred_element_type=F32)
           + jnp.dot(yl, wo_ref[W:2 * W, :], preferred_element_type=F32))
    x1 = x_ref[0] + mod_ref[0, 2:3, :] * mix
    x1_ref[0] = x1
    h2 = _rms(x1, gf_ref[...]) * (1.0 + mod_ref[0, 4:5, :]) + mod_ref[0, 3:4, :]
    _store_packed_rows(h2_ref, h2)

    logits = lax.dot_general(rw_ref[...], h2, (((1,), (1,)), ((), ())), precision=HIGHEST,
                             preferred_element_type=F32) + rb_ref[...]
    n_exp = logits.shape[0]
    eid = lax.broadcasted_iota(jnp.int32, logits.shape, 0)
    vals, idxs = [], []
    for _ in range(TOP_K):
        m = jnp.max(logits, axis=0, keepdims=True)
        sel = jnp.min(jnp.where(logits == m, eid, n_exp), axis=0, keepdims=True)
        vals.append(m)
        idxs.append(sel)
        logits = jnp.where(eid == sel, -jnp.inf, logits)
    es = [jnp.exp(vv - vals[0]) for vv in vals]
    inv = 1.0 / (es[0] + es[1] + es[2] + es[3])

    first = (pl.program_id(0) == 0) & (pl.program_id(1) == 0)

    @pl.when(first)
    def _():
        ti = lax.broadcasted_iota(jnp.int32, tri.shape, 0)
        tj = lax.broadcasted_iota(jnp.int32, tri.shape, 1)
        tri[...] = jnp.where(ti < tj, 1.0, 0.0).astype(BF16)
        run[...] = jnp.zeros_like(run)

    hits = [eid == idxs[r] for r in range(TOP_K)]
    chosen = hits[0] | hits[1] | hits[2] | hits[3]
    cnt = jnp.where(chosen, 1.0, 0.0)
    before = jnp.dot(cnt.astype(BF16), tri[...], preferred_element_type=F32) + run[...]
    for r in range(TOP_K):
        idx_ref[0, r:r + 1, :] = idxs[r]
        prob_ref[0, r:r + 1, :] = es[r] * inv
        rank_ref[0, r:r + 1, :] = jnp.sum(jnp.where(hits[r], before, 0.0), axis=0,
                                          keepdims=True).astype(jnp.int32)
    run[...] = run[...] + jnp.sum(cnt, axis=1, keepdims=True)
    cnt_ref[...] = jnp.broadcast_to(run[...], cnt_ref.shape).astype(jnp.int32)


def _outproj(x, ya, yl, mod3, ga, w_out_b, gf, router_wt, router_b, ts=512):
    B, S, D = x.shape
    W = ya.shape[2]
    NE = router_wt.shape[0]
    rowd = pl.BlockSpec((1, ts, D), lambda b, s: (b, s, 0))
    roww = pl.BlockSpec((1, ts, W), lambda b, s: (b, s, 0))
    sel = pl.BlockSpec((1, TOP_K, ts), lambda b, s: (b, 0, s))
    return pl.pallas_call(
        _outproj_kernel,
        grid=(B, S // ts),
        in_specs=[rowd, roww, roww,
                  pl.BlockSpec((1, N_MOD, D), lambda b, s: (b, 0, 0)),
                  pl.BlockSpec((1, W), lambda b, s: (0, 0)),
                  pl.BlockSpec((2 * W, D), lambda b, s: (0, 0)),
                  pl.BlockSpec((1, D), lambda b, s: (0, 0)),
                  pl.BlockSpec((NE, D), lambda b, s: (0, 0)),
                  pl.BlockSpec((NE, 1), lambda b, s: (0, 0))],
        out_specs=[rowd,
                   pl.BlockSpec((ts * PACK_ROWS, LANES), lambda b, s: (b * (S // ts) + s, 0)),
                   sel, sel, sel, pl.BlockSpec((NE, LANES), lambda b, s: (0, 0))],
        out_shape=[jax.ShapeDtypeStruct((B, S, D), F32),
                   jax.ShapeDtypeStruct((B * S * PACK_ROWS, LANES), jnp.uint32),
                   jax.ShapeDtypeStruct((B, TOP_K, S), jnp.int32),
                   jax.ShapeDtypeStruct((B, TOP_K, S), F32),
                   jax.ShapeDtypeStruct((B, TOP_K, S), jnp.int32),
                   jax.ShapeDtypeStruct((NE, LANES), jnp.int32)],
        scratch_shapes=[pltpu.VMEM((ts, ts), BF16), pltpu.VMEM((NE, 1), F32)],
        compiler_params=_params(("arbitrary", "arbitrary")),
        name="outproj",
    )(x, ya, yl, mod3, ga.reshape(1, W), w_out_b, gf.reshape(1, D), router_wt,
      router_b.reshape(NE, 1))


def _slot_specs(tiles_per_batch):
    def spec(r):
        return pl.BlockSpec(
            (TOKEN_TILE,),
            lambda i: ((i // tiles_per_batch * TOP_K + r) * tiles_per_batch
                       + i % tiles_per_batch,),
            memory_space=pltpu.SMEM)
    return [spec(r) for r in range(TOP_K)]


def _expert_kernel(first_ref, count_ref, tok_ref, h_ref, wg_ref, bg_ref, wu_ref, bu_ref,
                   wd_ref, bd_ref, ys_ref, wgb, wub, wdb, xbuf, ybuf, xsem, ysem):
    e = pl.program_id(0)
    n_exp = pl.num_programs(0)
    TM = EXPERT_BLOCK
    n_blocks = tok_ref.shape[0] // TM
    rows = ybuf.shape[1]
    first, count = first_ref[e], count_ref[e]

    wgb[...] = wg_ref[0].astype(BF16)
    wub[...] = wu_ref[0].astype(BF16)
    wdb[...] = wd_ref[0].astype(BF16)

    def gather_row(b, i, slot):
        src = pl.multiple_of(tok_ref[b * TM + i], PACK_ROWS)
        return pltpu.make_async_copy(
            h_ref.at[pl.ds(src, PACK_ROWS), :],
            xbuf.at[slot, pl.ds(pl.multiple_of(i * PACK_ROWS, PACK_ROWS), PACK_ROWS), :],
            xsem.at[slot])

    def x_wait(slot):
        pltpu.make_async_copy(h_ref.at[pl.ds(0, TM * PACK_ROWS), :], xbuf.at[slot],
                              xsem.at[slot]).wait()

    def y_copy(j, slot):
        dst = pl.ds(pl.multiple_of((first + j) * rows, rows), rows)
        return pltpu.make_async_copy(ybuf.at[slot], ys_ref.at[dst, :], ysem.at[slot])

    @pl.when(e == 0)
    def _():
        def issue(g, c):
            for u in range(ROW_DMA_UNROLL):
                gather_row(0, g * ROW_DMA_UNROLL + u, 0).start(priority=u % 2)
            return c
        lax.fori_loop(0, TM // ROW_DMA_UNROLL, issue, 0)

    def step(j, c):
        b = first + j
        xslot = b % 2
        slot = j % 2
        x_wait(xslot)

        @pl.when(j >= 2)
        def _():
            y_copy(j - 2, slot).wait()

        nb = jnp.minimum(b + 1, n_blocks - 1)
        for i in range(TM):
            gather_row(nb, i, 1 - xslot).start(priority=i % 2)

        xb = _load_packed_rows(xbuf, EXPERT_BLOCK, (xslot,))
        g = jnp.minimum(jnp.dot(xb, wgb[...], preferred_element_type=F32) + bg_ref[0],
                        SWIGLU_LIMIT)
        u = jnp.clip(jnp.dot(xb, wub[...], preferred_element_type=F32) + bu_ref[0],
                     -SWIGLU_LIMIT, SWIGLU_LIMIT)
        act = (u + 1.0) * (g * jax.nn.sigmoid(SWIGLU_ALPHA * g))
        _store_row_tiles(ybuf.at[slot], jnp.dot(act.astype(BF16), wdb[...],
                                                preferred_element_type=F32) + bd_ref[0])
        y_copy(j, slot).start()
        return c

    lax.fori_loop(0, count, step, 0)

    @pl.when(count >= 2)
    def _():
        y_copy(count - 2, count % 2).wait()

    @pl.when(count >= 1)
    def _():
        y_copy(count - 1, (count - 1) % 2).wait()

    @pl.when(e == n_exp - 1)
    def _():
        x_wait((first + count) % 2)
        spare_first, spare_count = first_ref[n_exp], count_ref[n_exp]
        ybuf[0] = jnp.zeros(ybuf.shape[1:], F32)

        def spare_copy(j):
            dst = pl.ds(pl.multiple_of((spare_first + j) * rows, rows), rows)
            return pltpu.make_async_copy(ybuf.at[0], ys_ref.at[dst, :], ysem.at[0])

        def start(j, c):
            spare_copy(j).start()
            return c

        def wait(j, c):
            spare_copy(j).wait()
            return c

        lax.fori_loop(0, spare_count, start, 0)
        lax.fori_loop(0, spare_count, wait, 0)


def _experts(h2t, first_block, block_count, tok, wg, bg, wu, bu, wd, bd):
    NE, D, DE = wg.shape
    cap = tok.shape[0]

    def wsel(e, first, count, tok):
        return (e, 0, 0)

    grid_spec = pltpu.PrefetchScalarGridSpec(
        num_scalar_prefetch=3,
        grid=(NE,),
        in_specs=[pl.BlockSpec(memory_space=pl.ANY),
                  pl.BlockSpec((1, D, DE), wsel), pl.BlockSpec((1, 1, DE), wsel),
                  pl.BlockSpec((1, D, DE), wsel), pl.BlockSpec((1, 1, DE), wsel),
                  pl.BlockSpec((1, DE, D), wsel), pl.BlockSpec((1, 1, D), wsel)],
        out_specs=pl.BlockSpec(memory_space=pl.ANY),
        scratch_shapes=[pltpu.VMEM((D, DE), BF16), pltpu.VMEM((D, DE), BF16),
                        pltpu.VMEM((DE, D), BF16),
                        pltpu.VMEM((2, EXPERT_BLOCK * PACK_ROWS, LANES), jnp.uint32),
                        pltpu.VMEM((2, EXPERT_BLOCK * SUBLANES, LANES), F32),
                        pltpu.SemaphoreType.DMA((2,)), pltpu.SemaphoreType.DMA((2,))],
    )
    return pl.pallas_call(
        _expert_kernel,
        grid_spec=grid_spec,
        out_shape=jax.ShapeDtypeStruct((cap * SUBLANES, LANES), F32),
        compiler_params=_params(("arbitrary",)),
        name="experts",
    )(first_block, block_count, tok, h2t, wg, bg.reshape(NE, 1, DE), wu, bu.reshape(NE, 1, DE),
      wd, bd.reshape(NE, 1, D))


def _combine_kernel(c0_ref, c1_ref, c2_ref, c3_ref, n0_ref, n1_ref, n2_ref, n3_ref,
                    x1_ref, p_ref, mod_ref, g_ref, ys_ref, o_ref, buf, sem):
    TT = x1_ref.shape[0]
    i = pl.program_id(0)

    def gather(pos_refs, s):
        def issue(g, c):
            for tt in range(ROW_DMA_UNROLL):
                t = g * ROW_DMA_UNROLL + tt
                for r in range(TOP_K):
                    pltpu.make_async_copy(
                        ys_ref.at[pl.ds(pl.multiple_of(pos_refs[r][t], SUBLANES), SUBLANES), :],
                        buf.at[s, r, pl.ds(pl.multiple_of(t * SUBLANES, SUBLANES), SUBLANES), :],
                        sem.at[s]).start(priority=(tt * TOP_K + r) % 2)
            return c
        lax.fori_loop(0, TT // ROW_DMA_UNROLL, issue, 0)

    @pl.when(i == 0)
    def _():
        gather((c0_ref, c1_ref, c2_ref, c3_ref), 0)

    def reduce_tile(s):
        @pl.when(i + 1 < pl.num_programs(0))
        def _():
            gather((n0_ref, n1_ref, n2_ref, n3_ref), 1 - s)

        for r in range(TOP_K):
            pltpu.make_async_copy(ys_ref.at[pl.ds(0, TT * SUBLANES), :], buf.at[s, r],
                                  sem.at[s]).wait()
        p = p_ref[...]
        moe = p[:, 0:1] * _load_row_tiles(buf, TT, (s, 0))
        for r in range(1, TOP_K):
            moe = moe + p[:, r:r + 1] * _load_row_tiles(buf, TT, (s, r))
        x2 = x1_ref[...] + mod_ref[0, 5:6, :] * moe
        o_ref[...] = _rms(x2, g_ref[...])

    for s in range(2):
        pl.when(i % 2 == s)(functools.partial(reduce_tile, s))


def _combine(x1f, probs, pos_flat, mod3, g, ys, tiles_per_batch):
    N, D = x1f.shape
    TT = TOKEN_TILE
    n_tiles = N // TT

    def next_spec(r):
        def index(i):
            j = jnp.minimum(i + 1, n_tiles - 1)
            return ((j // tiles_per_batch * TOP_K + r) * tiles_per_batch + j % tiles_per_batch,)
        return pl.BlockSpec((TT,), index, memory_space=pltpu.SMEM)

    return pl.pallas_call(
        _combine_kernel,
        grid=(n_tiles,),
        in_specs=_slot_specs(tiles_per_batch) + [next_spec(r) for r in range(TOP_K)]
        + [pl.BlockSpec((TT, D), lambda i: (i, 0)),
           pl.BlockSpec((TT, TOP_K), lambda i: (i, 0)),
           pl.BlockSpec((1, N_MOD, D), lambda i: (i // tiles_per_batch, 0, 0)),
           pl.BlockSpec((1, D), lambda i: (0, 0)),
           pl.BlockSpec(memory_space=pl.ANY)],
        out_specs=pl.BlockSpec((TT, D), lambda i: (i, 0)),
        out_shape=jax.ShapeDtypeStruct((N, D), F32),
        scratch_shapes=[pltpu.VMEM((2, TOP_K, TT * SUBLANES, LANES), F32),
                        pltpu.SemaphoreType.DMA((2,))],
        compiler_params=_params(("arbitrary",)),
        name="combine",
    )(*([pos_flat] * (2 * TOP_K)), x1f, probs, mod3, g.reshape(1, D), ys)


def _routing(idx, rank, counts):
    B, K, S = idx.shape
    n_exp = counts.shape[0]
    TM = EXPERT_BLOCK
    n_assign = B * S * K
    padded = ((counts + TM - 1) // TM) * TM
    pad_ends = jnp.cumsum(padded)
    pad_starts = pad_ends - padded
    experts = jnp.arange(n_exp, dtype=jnp.int32).reshape(n_exp, 1, 1, 1)
    base = jnp.sum(jnp.where(idx[None] == experts, pad_starts.reshape(n_exp, 1, 1, 1), 0),
                   axis=0)
    pos = (base + rank).astype(jnp.int32).reshape(n_assign)
    n_blocks = n_assign // TM + n_exp
    first_block = jnp.concatenate([pad_starts, pad_ends[-1:]]) // TM
    block_count = jnp.concatenate([padded // TM, n_blocks - pad_ends[-1:] // TM])
    shift = n_assign.bit_length()
    low_mask = (1 << shift) - 1
    b_i = lax.broadcasted_iota(jnp.int32, idx.shape, 0)
    k_i = lax.broadcasted_iota(jnp.int32, idx.shape, 1)
    s_i = lax.broadcasted_iota(jnp.int32, idx.shape, 2)
    real = (idx << shift) | ((b_i * S + s_i) * K + k_i)
    filler_id = jnp.arange(n_blocks * TM - n_assign, dtype=jnp.int32)
    filler_end = jnp.cumsum(padded - counts)
    filler_e = jnp.sum(filler_end[:, None] <= filler_id[None, :], axis=0).astype(jnp.int32)
    keys = jnp.concatenate([real.reshape(n_assign), (filler_e << shift) | low_mask])
    low = jnp.sort(keys) & low_mask
    tok = jnp.where(low < n_assign, low // K, 0).astype(jnp.int32)
    return pos, first_block.astype(jnp.int32), block_count.astype(jnp.int32), tok


def kernel(x, c, ada_w, ada_b, mix_norm_g, w_in, conv_w, conv_b, rg_w, rg_b, ig_w, ig_b,
           lru_lambda, attn_out_g, lru_out_g, w_out, ffn_norm_g, router_w, router_b,
           exp_w_gate, exp_b_gate, exp_w_up, exp_b_up, exp_w_down, exp_b_down, final_norm_g):
    B, S, D = x.shape
    depth = ada_w.shape[0]
    assert S % 512 == 0 and S % TOKEN_TILE == 0 and D == SUBLANES * LANES
    for l in range(depth):
        mod3 = _ada(c, ada_w[l], ada_b[l]).reshape(B, N_MOD, D)
        q, k, v, yl = _inproj(x, mod3, mix_norm_g[l], w_in[l].astype(BF16), conv_w[l],
                              conv_b[l], rg_w[l], rg_b[l], ig_w[l], ig_b[l], lru_lambda[l],
                              lru_out_g[l])
        ya = _attention(q, k, v)
        x1, h2, idx, prob, rank, cnt = _outproj(
            x, ya, yl, mod3, attn_out_g[l], w_out[l].astype(BF16), ffn_norm_g[l],
            router_w[l].T, router_b[l])
        pos, first_block, block_count, tok = _routing(idx, rank, cnt[:, 0])
        ys = _experts(h2, first_block, block_count, tok * PACK_ROWS, exp_w_gate[l],
                      exp_b_gate[l], exp_w_up[l], exp_b_up[l], exp_w_down[l], exp_b_down[l])
        probs = jnp.transpose(prob, (0, 2, 1)).reshape(B * S, TOP_K)
        assert depth == 1
        x = _combine(x1.reshape(B * S, D), probs, pos * SUBLANES, mod3, final_norm_g, ys,
                     S // TOKEN_TILE).reshape(B, S, D)
    return x
```

```python
import functools

import jax
import jax.numpy as jnp
from jax import lax
from jax.experimental import pallas as pl
from jax.experimental.pallas import tpu as pltpu

F32 = jnp.float32
BF16 = jnp.bfloat16
HIGHEST = lax.Precision.HIGHEST

EPS = 1e-6
N_MOD = 6
SB_HEADS = 8
HEAD_DIM = 64
SB_WIDTH = SB_HEADS * HEAD_DIM
LRU_BLOCKS = 8
CONV_WIDTH = 4
LRU_C = 8.0
TOP_K = 4
SWIGLU_LIMIT = 7.0
SWIGLU_ALPHA = 1.702
LOG2E = 1.4426950408889634

LANES = 128
SUBLANES = 8
VMEM_LIMIT = 56 * 1024 * 1024

Q_BLOCK = 128
K_BLOCK = 128
ATTN_WINDOW_BLOCKS = 5
ATTN_LOOP_BLOCKS = 2
ATTN_UNDERFLOW_LOG = -110.0
EXPERT_BLOCK = 512
TOKEN_TILE = 256
ROW_DMA_UNROLL = 8
LRU_SCAN_ROWS = 256


def _params(sem):
    return pltpu.CompilerParams(dimension_semantics=sem, vmem_limit_bytes=VMEM_LIMIT)


def _ada_kernel(c_ref, w_ref, b_ref, o_ref):
    c = c_ref[...]
    ca = c * jax.nn.sigmoid(c)
    o_ref[...] = jnp.dot(ca, w_ref[...], precision=HIGHEST,
                         preferred_element_type=F32) + b_ref[...]


def _ada(c, ada_w, ada_b):
    B, D = c.shape
    E = ada_w.shape[1]
    tn = 1024
    return pl.pallas_call(
        _ada_kernel,
        grid=(E // tn,),
        in_specs=[pl.BlockSpec((B, D), lambda j: (0, 0)),
                  pl.BlockSpec((D, tn), lambda j: (0, j)),
                  pl.BlockSpec((1, tn), lambda j: (0, j))],
        out_specs=pl.BlockSpec((B, tn), lambda j: (0, j)),
        out_shape=jax.ShapeDtypeStruct((B, E), F32),
        compiler_params=_params(("arbitrary",)),
        name="ada",
    )(c, ada_w, ada_b.reshape(1, E))


def _rms(x, g):
    ms = jnp.mean(x * x, axis=-1, keepdims=True)
    return x * lax.rsqrt(ms + EPS) * g


def _softplus(x):
    return jnp.maximum(x, 0.0) + jnp.log1p(jnp.exp(-jnp.abs(x)))


def _gelu_tanh(x):
    return 0.5 * x * (1.0 + jnp.tanh(0.7978845608028654 * (x + 0.044715 * x * x * x)))


LRU_PAD = 8


def _lru_tile(x, gate, cw_ref, cb_ref, wr_ref, br_ref, wi_ref, bi_ref, lam_ref, g_ref,
              xext, hc):
    T = x.shape[0]
    xext[LRU_PAD:LRU_PAD + T, :] = x
    xc = cb_ref[...] + cw_ref[CONV_WIDTH - 1:CONV_WIDTH, :] * x
    for j in range(CONV_WIDTH - 1):
        back = CONV_WIDTH - 1 - j
        xc = xc + cw_ref[j:j + 1, :] * xext[LRU_PAD - back:LRU_PAD - back + T, :]
    xext[0:LRU_PAD, :] = xext[T:T + LRU_PAD, :]

    xb = xc.astype(BF16)
    r = jax.nn.sigmoid(jnp.dot(xb, wr_ref[...], preferred_element_type=F32) + br_ref[...])
    ig = jax.nn.sigmoid(jnp.dot(xb, wi_ref[...], preferred_element_type=F32) + bi_ref[...])
    log_a = (-LRU_C) * r * _softplus(-lam_ref[...])
    a = jnp.exp(log_a)
    b = jnp.sqrt(-jnp.tanh(log_a) * (a * a + 1.0)) * (ig * xc)

    G = T // SUBLANES
    a = a.reshape(G, SUBLANES, -1)
    b = b.reshape(G, SUBLANES, -1)
    rows = lax.broadcasted_iota(jnp.int32, a.shape, 1)
    d = 1
    while d < SUBLANES:
        keep = rows >= d
        a_prev = jnp.where(keep, pltpu.roll(a, d, 1), 1.0)
        b_prev = jnp.where(keep, pltpu.roll(b, d, 1), 0.0)
        b = a * b_prev + b
        a = a * a_prev
        d *= 2
    prev = hc[...]
    groups = []
    for g in range(G):
        hg = a[g] * prev + b[g]
        prev = hg[SUBLANES - 1:SUBLANES, :]
        groups.append(hg)
    hc[...] = prev
    h = jnp.concatenate(groups, axis=0)
    return _rms(h * _gelu_tanh(gate), g_ref[...])


def _inproj_kernel(x_ref, mod_ref, g_ref, w_ref, cw_ref, cb_ref, wr_ref, br_ref, wi_ref,
                   bi_ref, lam_ref, gl_ref, q_ref, k_ref, v_ref, yl_ref, xext, hc):
    @pl.when(pl.program_id(1) == 0)
    def _():
        xext[0:LRU_PAD, :] = jnp.zeros((LRU_PAD, xext.shape[1]), F32)
        hc[...] = jnp.zeros_like(hc)

    x = x_ref[0]
    h = _rms(x, g_ref[...]) * (1.0 + mod_ref[0, 1:2, :]) + mod_ref[0, 0:1, :]
    hb = h.astype(BF16)
    W = SB_WIDTH

    def proj(c):
        return jnp.dot(hb, w_ref[:, c * W:(c + 1) * W], preferred_element_type=F32)

    xr, gr = proj(3), proj(4)
    T = LRU_SCAN_ROWS
    for t in range(x.shape[0] // T):
        rows = slice(t * T, (t + 1) * T)
        yl_ref[0, rows, :] = _lru_tile(xr[rows], gr[rows], cw_ref, cb_ref, wr_ref, br_ref,
                                       wi_ref, bi_ref, lam_ref, gl_ref, xext, hc)
    q_ref[0] = (proj(0) * (HEAD_DIM ** -0.5)).astype(BF16)
    k_ref[0] = proj(1).astype(BF16)
    v_ref[0] = proj(2).astype(BF16)


def _block_diag(w):
    H, I, J = w.shape
    eye = jnp.eye(H, dtype=w.dtype)
    return (w[:, :, None, :] * eye[:, None, :, None]).reshape(H * I, H * J)


def _inproj(x, mod3, g, w_in_b, conv_w, conv_b, rg_w, rg_b, ig_w, ig_b, lam, g_lru, ts=512):
    B, S, D = x.shape
    E = w_in_b.shape[1]
    W = SB_WIDTH
    row = pl.BlockSpec((1, ts, W), lambda b, s: (b, s, 0))
    vec = pl.BlockSpec((1, W), lambda b, s: (0, 0))
    mat = pl.BlockSpec((W, W), lambda b, s: (0, 0))
    return pl.pallas_call(
        _inproj_kernel,
        grid=(B, S // ts),
        in_specs=[pl.BlockSpec((1, ts, D), lambda b, s: (b, s, 0)),
                  pl.BlockSpec((1, N_MOD, D), lambda b, s: (b, 0, 0)),
                  pl.BlockSpec((1, D), lambda b, s: (0, 0)),
                  pl.BlockSpec((D, E), lambda b, s: (0, 0)),
                  pl.BlockSpec((CONV_WIDTH, W), lambda b, s: (0, 0)), vec,
                  mat, vec, mat, vec, vec, vec],
        out_specs=[row, row, row, row],
        out_shape=[jax.ShapeDtypeStruct((B, S, W), BF16)] * 3
        + [jax.ShapeDtypeStruct((B, S, W), F32)],
        scratch_shapes=[pltpu.VMEM((LRU_SCAN_ROWS + LRU_PAD, W), F32), pltpu.VMEM((1, W), F32)],
        compiler_params=_params(("arbitrary", "arbitrary")),
        name="inproj",
    )(x, mod3, g.reshape(1, D), w_in_b, conv_w, conv_b.reshape(1, W),
      _block_diag(rg_w).astype(BF16), rg_b.reshape(1, W), _block_diag(ig_w).astype(BF16),
      ig_b.reshape(1, W), lam.reshape(1, W), g_lru.reshape(1, W))


def _attn_kernel(q_ref, k_ref, v_ref, o_ref, acc_ref, carry_ref):
    i = pl.program_id(1)
    QB, KB = Q_BLOCK, K_BLOCK
    n_pairs = q_ref.shape[2] // LANES
    lane = lax.broadcasted_iota(jnp.int32, (QB, LANES), 1)

    def stacked_q(p):
        q = q_ref[0, :, p * LANES:(p + 1) * LANES]
        zero = jnp.zeros_like(q)
        return jnp.concatenate([jnp.where(lane < HEAD_DIM, q, zero),
                                jnp.where(lane >= HEAD_DIM, q, zero)], axis=0)

    qs = [stacked_q(p) for p in range(n_pairs)]

    uj = lax.broadcasted_iota(jnp.int32, (2 * KB, KB + LANES), 0) & (KB - 1)
    us = lax.broadcasted_iota(jnp.int32, (2 * KB, KB + LANES), 1)
    u2 = jnp.where((us >= KB) | (uj > us), -1.0, 0.0).astype(BF16)

    row = lax.broadcasted_iota(jnp.int32, (2 * QB, KB), 0) & (QB - 1)
    col = lax.broadcasted_iota(jnp.int32, (2 * QB, KB), 1)
    causal = col < row

    def tiles(p, j_hi, n, diagonal, acc, carry):
        start = pl.multiple_of((j_hi - (n - 1)) * KB, KB)
        kw = k_ref[0, pl.ds(start, n * KB), p * LANES:(p + 1) * LANES]
        vw = v_ref[0, pl.ds(start, n * KB), p * LANES:(p + 1) * LANES]
        z = lax.dot_general(qs[p], kw, (((1,), (1,)), ((), ())), preferred_element_type=F32)
        softplus = jnp.maximum(z, 0.0) + jnp.log(1.0 + jnp.exp2(jnp.abs(z) * (-LOG2E)))
        log_beta = z - softplus
        ws = [None] * n
        for c in reversed(range(n)):
            sp = softplus[:, c * KB:(c + 1) * KB]
            masked = diagonal and c == n - 1
            if masked:
                sp = jnp.where(causal, sp, 0.0)
            hi_f = lax.bitcast_convert_type(
                lax.bitcast_convert_type(sp, jnp.uint32) & jnp.uint32(0xFFFF0000), F32)
            hi_lo = jnp.concatenate([hi_f.astype(BF16), (sp - hi_f).astype(BF16)], axis=1)
            sums = jnp.dot(hi_lo, u2, preferred_element_type=F32)
            w = jnp.exp2((log_beta[:, c * KB:(c + 1) * KB] + sums[:, :KB] + carry) * LOG2E)
            if masked:
                w = jnp.where(causal, w, 0.0)
            ws[c] = w.astype(BF16)
            carry = carry + sums[:, KB:]
        w_all = ws[0] if n == 1 else jnp.concatenate(ws, axis=1)
        acc = acc + jnp.dot(w_all, vw, preferred_element_type=F32)
        return acc, carry

    def fold(j_hi, n, diagonal):
        cmax = None
        for p in range(n_pairs):
            if diagonal:
                acc = carry = jnp.zeros((2 * QB, LANES), F32)
            else:
                acc, carry = acc_ref[p], carry_ref[p]
            acc, carry = tiles(p, j_hi, n, diagonal, acc, carry)
            acc_ref[p] = acc
            carry_ref[p] = carry
            cmax = carry if cmax is None else jnp.maximum(cmax, carry)
        return jnp.max(cmax)

    n_first, n_loop = ATTN_WINDOW_BLOCKS, ATTN_LOOP_BLOCKS
    n0 = jnp.minimum(i + 1, n_first)
    cmax0 = lax.switch(n0 - 1, [functools.partial(fold, i, n, True)
                                for n in range(1, n_first + 1)])
    j0 = i - n0

    def loop(n, j, cmax):
        def more(st):
            return (st[0] >= n - 1) & (st[1] >= ATTN_UNDERFLOW_LOG)

        def step(st):
            return st[0] - n, fold(st[0], n, False)

        return lax.while_loop(more, step, (j, cmax))

    j, cmax = j0, cmax0
    while n_loop >= 1:
        j, cmax = loop(n_loop, j, cmax)
        n_loop //= 2
    for p in range(n_pairs):
        acc = acc_ref[p]
        o_ref[0, :, p * LANES:(p + 1) * LANES] = jnp.where(lane < HEAD_DIM, acc[:QB], acc[QB:])


def _attention(q, k, v):
    B, S, W = q.shape
    return pl.pallas_call(
        _attn_kernel,
        grid=(B, S // Q_BLOCK),
        in_specs=[pl.BlockSpec((1, Q_BLOCK, W), lambda b, i: (b, i, 0)),
                  pl.BlockSpec((1, S, W), lambda b, i: (b, 0, 0)),
                  pl.BlockSpec((1, S, W), lambda b, i: (b, 0, 0))],
        out_specs=pl.BlockSpec((1, Q_BLOCK, W), lambda b, i: (b, i, 0)),
        out_shape=jax.ShapeDtypeStruct((B, S, W), F32),
        scratch_shapes=[pltpu.VMEM((W // LANES, 2 * Q_BLOCK, LANES), F32),
                        pltpu.VMEM((W // LANES, 2 * Q_BLOCK, LANES), F32)],
        compiler_params=_params(("arbitrary", "arbitrary")),
        name="attn",
    )(q, k, v)


def _load_row_tiles(ref, n_rows, lead=()):
    chunks = [ref[lead + (pl.ds(c, n_rows, stride=SUBLANES), slice(None))]
              for c in range(SUBLANES)]
    return jnp.concatenate(chunks, axis=1)


def _store_row_tiles(ref, value):
    n_rows = value.shape[0]
    for c in range(SUBLANES):
        ref[pl.ds(c, n_rows, stride=SUBLANES), :] = value[:, c * LANES:(c + 1) * LANES]


PACK_ROWS = SUBLANES // 2


def _store_packed_rows(ref, value):
    n_rows, d = value.shape
    bits = lax.bitcast_convert_type(value.astype(BF16).astype(F32), jnp.uint32)
    words = (bits[:, :d // 2] >> 16) | bits[:, d // 2:]
    for c in range(PACK_ROWS):
        ref[pl.ds(c, n_rows, stride=PACK_ROWS), :] = words[:, c * LANES:(c + 1) * LANES]


def _load_packed_rows(ref, n_rows, lead=()):
    lo, hi = [], []
    for c in range(PACK_ROWS):
        w = ref[lead + (pl.ds(c, n_rows, stride=PACK_ROWS), slice(None))]
        lo.append(lax.bitcast_convert_type(w << 16, F32).astype(BF16))
        hi.append(lax.bitcast_convert_type(w & jnp.uint32(0xFFFF0000), F32).astype(BF16))
    return jnp.concatenate(lo + hi, axis=1)


def _outproj_kernel(x_ref, ya_ref, yl_ref, mod_ref, ga_ref, wo_ref, gf_ref, rw_ref, rb_ref,
                    x1_ref, h2_ref, idx_ref, prob_ref, rank_ref, cnt_ref, tri, run):
    W = SB_WIDTH
    ya = _rms(ya_ref[0], ga_ref[...]).astype(BF16)
    yl = yl_ref[0].astype(BF16)
    mix = (jnp.dot(ya, wo_ref[0:W, :], preferred_element_type=F32)
           + jnp.dot(yl, wo_ref[W:2 * W, :], preferred_element_type=F32))
    x1 = x_ref[0] + mod_ref[0, 2:3, :] * mix
    x1_ref[0] = x1
    h2 = _rms(x1, gf_ref[...]) * (1.0 + mod_ref[0, 4:5, :]) + mod_ref[0, 3:4, :]
    _store_packed_rows(h2_ref, h2)

    logits = lax.dot_general(rw_ref[...], h2, (((1,), (1,)), ((), ())), precision=HIGHEST,
                             preferred_element_type=F32) + rb_ref[...]
    n_exp = logits.shape[0]
    eid = lax.broadcasted_iota(jnp.int32, logits.shape, 0)
    vals, idxs = [], []
    for _ in range(TOP_K):
        m = jnp.max(logits, axis=0, keepdims=True)
        sel = jnp.min(jnp.where(logits == m, eid, n_exp), axis=0, keepdims=True)
        vals.append(m)
        idxs.append(sel)
        logits = jnp.where(eid == sel, -jnp.inf, logits)
    es = [jnp.exp(vv - vals[0]) for vv in vals]
    inv = 1.0 / (es[0] + es[1] + es[2] + es[3])

    first = (pl.program_id(0) == 0) & (pl.program_id(1) == 0)

    @pl.when(first)
    def _():
        ti = lax.broadcasted_iota(jnp.int32, tri.shape, 0)
        tj = lax.broadcasted_iota(jnp.int32, tri.shape, 1)
        tri[...] = jnp.where(ti < tj, 1.0, 0.0).astype(BF16)
        run[...] = jnp.zeros_like(run)

    hits = [eid == idxs[r] for r in range(TOP_K)]
    chosen = hits[0] | hits[1] | hits[2] | hits[3]
    cnt = jnp.where(chosen, 1.0, 0.0)
    before = jnp.dot(cnt.astype(BF16), tri[...], preferred_element_type=F32) + run[...]
    for r in range(TOP_K):
        idx_ref[0, r:r + 1, :] = idxs[r]
        prob_ref[0, r:r + 1, :] = es[r] * inv
        rank_ref[0, r:r + 1, :] = jnp.sum(jnp.where(hits[r], before, 0.0), axis=0,
                                          keepdims=True).astype(jnp.int32)
    run[...] = run[...] + jnp.sum(cnt, axis=1, keepdims=True)
    cnt_ref[...] = jnp.broadcast_to(run[...], cnt_ref.shape).astype(jnp.int32)


def _outproj(x, ya, yl, mod3, ga, w_out_b, gf, router_wt, router_b, ts=512):
    B, S, D = x.shape
    W = ya.shape[2]
    NE = router_wt.shape[0]
    rowd = pl.BlockSpec((1, ts, D), lambda b, s: (b, s, 0))
    roww = pl.BlockSpec((1, ts, W), lambda b, s: (b, s, 0))
    sel = pl.BlockSpec((1, TOP_K, ts), lambda b, s: (b, 0, s))
    return pl.pallas_call(
        _outproj_kernel,
        grid=(B, S // ts),
        in_specs=[rowd, roww, roww,
                  pl.BlockSpec((1, N_MOD, D), lambda b, s: (b, 0, 0)),
                  pl.BlockSpec((1, W), lambda b, s: (0, 0)),
                  pl.BlockSpec((2 * W, D), lambda b, s: (0, 0)),
                  pl.BlockSpec((1, D), lambda b, s: (0, 0)),
                  pl.BlockSpec((NE, D), lambda b, s: (0, 0)),
                  pl.BlockSpec((NE, 1), lambda b, s: (0, 0))],
        out_specs=[rowd,
                   pl.BlockSpec((ts * PACK_ROWS, LANES), lambda b, s: (b * (S // ts) + s, 0)),
                   sel, sel, sel, pl.BlockSpec((NE, LANES), lambda b, s: (0, 0))],
        out_shape=[jax.ShapeDtypeStruct((B, S, D), F32),
                   jax.ShapeDtypeStruct((B * S * PACK_ROWS, LANES), jnp.uint32),
                   jax.ShapeDtypeStruct((B, TOP_K, S), jnp.int32),
                   jax.ShapeDtypeStruct((B, TOP_K, S), F32),
                   jax.ShapeDtypeStruct((B, TOP_K, S), jnp.int32),
                   jax.ShapeDtypeStruct((NE, LANES), jnp.int32)],
        scratch_shapes=[pltpu.VMEM((ts, ts), BF16), pltpu.VMEM((NE, 1), F32)],
        compiler_params=_params(("arbitrary", "arbitrary")),
        name="outproj",
    )(x, ya, yl, mod3, ga.reshape(1, W), w_out_b, gf.reshape(1, D), router_wt,
      router_b.reshape(NE, 1))


def _dispatch_kernel(p0_ref, p1_ref, p2_ref, p3_ref, fill_ref, h_ref, xs_ref, zbuf, sem, zsem):
    TT = h_ref.shape[0] // PACK_ROWS
    n_fill = fill_ref.shape[0]

    @pl.when(pl.program_id(0) == 0)
    def _():
        zbuf[...] = jnp.zeros_like(zbuf)

        def fill_copy(e):
            start = pl.multiple_of(jnp.maximum(fill_ref[e], 0), PACK_ROWS)
            return pltpu.make_async_copy(zbuf, xs_ref.at[pl.ds(start, zbuf.shape[0]), :], zsem)

        def start(e, c):
            @pl.when(fill_ref[e] >= 0)
            def _():
                fill_copy(e).start()
            return c

        def wait(e, c):
            @pl.when(fill_ref[e] >= 0)
            def _():
                fill_copy(e).wait()
            return c

        lax.fori_loop(0, n_fill, start, 0)
        lax.fori_loop(0, n_fill, wait, 0)

    pos_refs = (p0_ref, p1_ref, p2_ref, p3_ref)

    def row_copy(t, r):
        src = pl.multiple_of(t * PACK_ROWS, PACK_ROWS)
        dst = pl.multiple_of(pos_refs[r][t], PACK_ROWS)
        return pltpu.make_async_copy(h_ref.at[pl.ds(src, PACK_ROWS), :],
                                     xs_ref.at[pl.ds(dst, PACK_ROWS), :], sem)

    def issue(g, c):
        for tt in range(ROW_DMA_UNROLL):
            for r in range(TOP_K):
                row_copy(g * ROW_DMA_UNROLL + tt, r).start(priority=(tt * TOP_K + r) % 2)
        return c

    lax.fori_loop(0, TT // ROW_DMA_UNROLL, issue, 0)
    for r in range(TOP_K):
        pltpu.make_async_copy(h_ref, xs_ref.at[pl.ds(0, TT * PACK_ROWS), :], sem).wait()


def _slot_specs(tiles_per_batch):
    def spec(r):
        return pl.BlockSpec(
            (TOKEN_TILE,),
            lambda i: ((i // tiles_per_batch * TOP_K + r) * tiles_per_batch
                       + i % tiles_per_batch,),
            memory_space=pltpu.SMEM)
    return [spec(r) for r in range(TOP_K)]


def _dispatch(h2t, pos_flat, fill_start, cap, tiles_per_batch):
    TT = TOKEN_TILE
    n_tiles = h2t.shape[0] // (TT * PACK_ROWS)
    return pl.pallas_call(
        _dispatch_kernel,
        grid=(n_tiles,),
        in_specs=_slot_specs(tiles_per_batch)
        + [pl.BlockSpec(memory_space=pltpu.SMEM),
           pl.BlockSpec((TT * PACK_ROWS, LANES), lambda i: (i, 0))],
        out_specs=pl.BlockSpec(memory_space=pl.ANY),
        out_shape=jax.ShapeDtypeStruct((cap * PACK_ROWS, LANES), jnp.uint32),
        scratch_shapes=[pltpu.VMEM((EXPERT_BLOCK * PACK_ROWS, LANES), jnp.uint32),
                        pltpu.SemaphoreType.DMA, pltpu.SemaphoreType.DMA],
        compiler_params=_params(("arbitrary",)),
        name="dispatch",
    )(pos_flat, pos_flat, pos_flat, pos_flat, fill_start, h2t)


def _expert_kernel(first_ref, count_ref, xs_ref, wg_ref, bg_ref, wu_ref, bu_ref, wd_ref,
                   bd_ref, ys_ref, wgb, wub, wdb, xbuf, ybuf, xsem, ysem):
    e = pl.program_id(0)
    n_exp = pl.num_programs(0)
    rows = ybuf.shape[1]
    first, count = first_ref[e], count_ref[e]

    wgb[...] = wg_ref[0].astype(BF16)
    wub[...] = wu_ref[0].astype(BF16)
    wdb[...] = wd_ref[0].astype(BF16)

    def block(ref, buf, j):
        n = buf.shape[1]
        return ref.at[pl.ds(pl.multiple_of((first + j) * n, n), n), :]

    def x_copy(j, slot):
        return pltpu.make_async_copy(block(xs_ref, xbuf, j), xbuf.at[slot], xsem.at[slot])

    def y_copy(j, slot):
        return pltpu.make_async_copy(ybuf.at[slot], block(ys_ref, ybuf, j), ysem.at[slot])

    @pl.when(count > 0)
    def _():
        x_copy(0, 0).start(priority=1)

    def step(j, c):
        slot = j % 2
        x_copy(j, slot).wait()

        @pl.when(j + 1 < count)
        def _():
            x_copy(j + 1, 1 - slot).start(priority=1)

        @pl.when(j >= 2)
        def _():
            y_copy(j - 2, slot).wait()

        xb = _load_packed_rows(xbuf, EXPERT_BLOCK, (slot,))
        g = jnp.minimum(jnp.dot(xb, wgb[...], preferred_element_type=F32) + bg_ref[0],
                        SWIGLU_LIMIT)
        u = jnp.clip(jnp.dot(xb, wub[...], preferred_element_type=F32) + bu_ref[0],
                     -SWIGLU_LIMIT, SWIGLU_LIMIT)
        act = (u + 1.0) * (g * jax.nn.sigmoid(SWIGLU_ALPHA * g))
        _store_row_tiles(ybuf.at[slot], jnp.dot(act.astype(BF16), wdb[...],
                                                preferred_element_type=F32) + bd_ref[0])
        y_copy(j, slot).start(priority=1)
        return c

    lax.fori_loop(0, count, step, 0)

    @pl.when(count >= 2)
    def _():
        y_copy(count - 2, count % 2).wait()

    @pl.when(count >= 1)
    def _():
        y_copy(count - 1, (count - 1) % 2).wait()

    @pl.when(e == n_exp - 1)
    def _():
        spare_first, spare_count = first_ref[n_exp], count_ref[n_exp]
        ybuf[0] = jnp.zeros(ybuf.shape[1:], F32)

        def spare_copy(j):
            dst = pl.ds(pl.multiple_of((spare_first + j) * rows, rows), rows)
            return pltpu.make_async_copy(ybuf.at[0], ys_ref.at[dst, :], ysem.at[0])

        def start(j, c):
            spare_copy(j).start()
            return c

        def wait(j, c):
            spare_copy(j).wait()
            return c

        lax.fori_loop(0, spare_count, start, 0)
        lax.fori_loop(0, spare_count, wait, 0)


def _experts(xs, first_block, block_count, wg, bg, wu, bu, wd, bd):
    NE, D, DE = wg.shape
    cap = xs.shape[0] // PACK_ROWS

    def wsel(e, first, count):
        return (e, 0, 0)

    grid_spec = pltpu.PrefetchScalarGridSpec(
        num_scalar_prefetch=2,
        grid=(NE,),
        in_specs=[pl.BlockSpec(memory_space=pl.ANY),
                  pl.BlockSpec((1, D, DE), wsel), pl.BlockSpec((1, 1, DE), wsel),
                  pl.BlockSpec((1, D, DE), wsel), pl.BlockSpec((1, 1, DE), wsel),
                  pl.BlockSpec((1, DE, D), wsel), pl.BlockSpec((1, 1, D), wsel)],
        out_specs=pl.BlockSpec(memory_space=pl.ANY),
        scratch_shapes=[pltpu.VMEM((D, DE), BF16), pltpu.VMEM((D, DE), BF16),
                        pltpu.VMEM((DE, D), BF16),
                        pltpu.VMEM((2, EXPERT_BLOCK * PACK_ROWS, LANES), jnp.uint32),
                        pltpu.VMEM((2, EXPERT_BLOCK * SUBLANES, LANES), F32),
                        pltpu.SemaphoreType.DMA((2,)), pltpu.SemaphoreType.DMA((2,))],
    )
    return pl.pallas_call(
        _expert_kernel,
        grid_spec=grid_spec,
        out_shape=jax.ShapeDtypeStruct((cap * SUBLANES, LANES), F32),
        compiler_params=_params(("arbitrary",)),
        name="experts",
    )(first_block, block_count, xs, wg, bg.reshape(NE, 1, DE), wu, bu.reshape(NE, 1, DE), wd,
      bd.reshape(NE, 1, D))


def _combine_kernel(c0_ref, c1_ref, c2_ref, c3_ref, n0_ref, n1_ref, n2_ref, n3_ref,
                    x1_ref, p_ref, mod_ref, g_ref, ys_ref, o_ref, buf, sem):
    TT = x1_ref.shape[0]
    i = pl.program_id(0)

    def gather(pos_refs, s):
        def issue(g, c):
            for tt in range(ROW_DMA_UNROLL):
                t = g * ROW_DMA_UNROLL + tt
                for r in range(TOP_K):
                    pltpu.make_async_copy(
                        ys_ref.at[pl.ds(pl.multiple_of(pos_refs[r][t], SUBLANES), SUBLANES), :],
                        buf.at[s, r, pl.ds(pl.multiple_of(t * SUBLANES, SUBLANES), SUBLANES), :],
                        sem.at[s]).start(priority=(tt * TOP_K + r) % 2)
            return c
        lax.fori_loop(0, TT // ROW_DMA_UNROLL, issue, 0)

    @pl.when(i == 0)
    def _():
        gather((c0_ref, c1_ref, c2_ref, c3_ref), 0)

    def reduce_tile(s):
        @pl.when(i + 1 < pl.num_programs(0))
        def _():
            gather((n0_ref, n1_ref, n2_ref, n3_ref), 1 - s)

        for r in range(TOP_K):
            pltpu.make_async_copy(ys_ref.at[pl.ds(0, TT * SUBLANES), :], buf.at[s, r],
                                  sem.at[s]).wait()
        p = p_ref[...]
        moe = p[:, 0:1] * _load_row_tiles(buf, TT, (s, 0))
        for r in range(1, TOP_K):
            moe = moe + p[:, r:r + 1] * _load_row_tiles(buf, TT, (s, r))
        x2 = x1_ref[...] + mod_ref[0, 5:6, :] * moe
        o_ref[...] = _rms(x2, g_ref[...])

    for s in range(2):
        pl.when(i % 2 == s)(functools.partial(reduce_tile, s))


def _combine(x1f, probs, pos_flat, mod3, g, ys, tiles_per_batch):
    N, D = x1f.shape
    TT = TOKEN_TILE
    n_tiles = N // TT

    def next_spec(r):
        def index(i):
            j = jnp.minimum(i + 1, n_tiles - 1)
            return ((j // tiles_per_batch * TOP_K + r) * tiles_per_batch + j % tiles_per_batch,)
        return pl.BlockSpec((TT,), index, memory_space=pltpu.SMEM)

    return pl.pallas_call(
        _combine_kernel,
        grid=(n_tiles,),
        in_specs=_slot_specs(tiles_per_batch) + [next_spec(r) for r in range(TOP_K)]
        + [pl.BlockSpec((TT, D), lambda i: (i, 0)),
           pl.BlockSpec((TT, TOP_K), lambda i: (i, 0)),
           pl.BlockSpec((1, N_MOD, D), lambda i: (i // tiles_per_batch, 0, 0)),
           pl.BlockSpec((1, D), lambda i: (0, 0)),
           pl.BlockSpec(memory_space=pl.ANY)],
        out_specs=pl.BlockSpec((TT, D), lambda i: (i, 0)),
        out_shape=jax.ShapeDtypeStruct((N, D), F32),
        scratch_shapes=[pltpu.VMEM((2, TOP_K, TT * SUBLANES, LANES), F32),
                        pltpu.SemaphoreType.DMA((2,))],
        compiler_params=_params(("arbitrary",)),
        name="combine",
    )(*([pos_flat] * (2 * TOP_K)), x1f, probs, mod3, g.reshape(1, D), ys)


def _routing(idx, rank, counts):
    B, K, S = idx.shape
    n_exp = counts.shape[0]
    TM = EXPERT_BLOCK
    n_assign = B * S * K
    padded = ((counts + TM - 1) // TM) * TM
    pad_ends = jnp.cumsum(padded)
    pad_starts = pad_ends - padded
    experts = jnp.arange(n_exp, dtype=jnp.int32).reshape(n_exp, 1, 1, 1)
    base = jnp.sum(jnp.where(idx[None] == experts, pad_starts.reshape(n_exp, 1, 1, 1), 0),
                   axis=0)
    pos = (base + rank).astype(jnp.int32).reshape(n_assign)
    n_blocks = n_assign // TM + n_exp
    first_block = jnp.concatenate([pad_starts, pad_ends[-1:]]) // TM
    block_count = jnp.concatenate([padded // TM, n_blocks - pad_ends[-1:] // TM])
    tail_start = jnp.where(padded > 0, pad_ends - TM, -1)
    spare = pad_ends[-1] + jnp.arange(n_exp, dtype=jnp.int32) * TM
    fill_start = jnp.concatenate(
        [tail_start, jnp.where(spare < n_blocks * TM, spare, -1)]).astype(jnp.int32)
    return (pos, first_block.astype(jnp.int32), block_count.astype(jnp.int32),
            fill_start, n_blocks * TM)


def kernel(x, c, ada_w, ada_b, mix_norm_g, w_in, conv_w, conv_b, rg_w, rg_b, ig_w, ig_b,
           lru_lambda, attn_out_g, lru_out_g, w_out, ffn_norm_g, router_w, router_b,
           exp_w_gate, exp_b_gate, exp_w_up, exp_b_up, exp_w_down, exp_b_down, final_norm_g):
    B, S, D = x.shape
    depth = ada_w.shape[0]
    assert S % 512 == 0 and S % TOKEN_TILE == 0 and D == SUBLANES * LANES
    for l in range(depth):
        mod3 = _ada(c, ada_w[l], ada_b[l]).reshape(B, N_MOD, D)
        q, k, v, yl = _inproj(x, mod3, mix_norm_g[l], w_in[l].astype(BF16), conv_w[l],
                              conv_b[l], rg_w[l], rg_b[l], ig_w[l], ig_b[l], lru_lambda[l],
                              lru_out_g[l])
        ya = _attention(q, k, v)
        x1, h2, idx, prob, rank, cnt = _outproj(
            x, ya, yl, mod3, attn_out_g[l], w_out[l].astype(BF16), ffn_norm_g[l],
            router_w[l].T, router_b[l])
        pos, first_block, block_count, fill_start, cap = _routing(idx, rank, cnt[:, 0])
        xs = _dispatch(h2, pos * PACK_ROWS, fill_start * PACK_ROWS, cap, S // TOKEN_TILE)
        ys = _experts(xs, first_block, block_count, exp_w_gate[l], exp_b_gate[l],
                      exp_w_up[l], exp_b_up[l], exp_w_down[l], exp_b_down[l])
        probs = jnp.transpose(prob, (0, 2, 1)).reshape(B * S, TOP_K)
        assert depth == 1
        x = _combine(x1.reshape(B * S, D), probs, pos * SUBLANES, mod3, final_norm_g, ys,
                     S // TOKEN_TILE).reshape(B, S, D)
    return x
```

```python
import functools

import jax
import jax.numpy as jnp
from jax import lax
from jax.experimental import pallas as pl
from jax.experimental.pallas import tpu as pltpu

F32 = jnp.float32
BF16 = jnp.bfloat16
HIGHEST = lax.Precision.HIGHEST

EPS = 1e-6
N_MOD = 6
SB_HEADS = 8
HEAD_DIM = 64
SB_WIDTH = SB_HEADS * HEAD_DIM
LRU_BLOCKS = 8
CONV_WIDTH = 4
LRU_C = 8.0
TOP_K = 4
SWIGLU_LIMIT = 7.0
SWIGLU_ALPHA = 1.702
LOG2E = 1.4426950408889634

LANES = 128
SUBLANES = 8
VMEM_LIMIT = 56 * 1024 * 1024

Q_BLOCK = 128
K_BLOCK = 128
ATTN_WINDOW_BLOCKS = 5
ATTN_LOOP_BLOCKS = 2
ATTN_UNDERFLOW_LOG = -110.0
EXPERT_BLOCK = 512
TOKEN_TILE = 256
ROW_DMA_UNROLL = 8
LRU_SCAN_ROWS = 256


def _params(sem):
    return pltpu.CompilerParams(dimension_semantics=sem, vmem_limit_bytes=VMEM_LIMIT)


def _ada_kernel(c_ref, w_ref, b_ref, o_ref):
    c = c_ref[...]
    ca = c * jax.nn.sigmoid(c)
    o_ref[...] = jnp.dot(ca, w_ref[...], precision=HIGHEST,
                         preferred_element_type=F32) + b_ref[...]


def _ada(c, ada_w, ada_b):
    B, D = c.shape
    E = ada_w.shape[1]
    tn = 1024
    return pl.pallas_call(
        _ada_kernel,
        grid=(E // tn,),
        in_specs=[pl.BlockSpec((B, D), lambda j: (0, 0)),
                  pl.BlockSpec((D, tn), lambda j: (0, j)),
                  pl.BlockSpec((1, tn), lambda j: (0, j))],
        out_specs=pl.BlockSpec((B, tn), lambda j: (0, j)),
        out_shape=jax.ShapeDtypeStruct((B, E), F32),
        compiler_params=_params(("arbitrary",)),
        name="ada",
    )(c, ada_w, ada_b.reshape(1, E))


def _rms(x, g):
    ms = jnp.mean(x * x, axis=-1, keepdims=True)
    return x * lax.rsqrt(ms + EPS) * g


def _softplus(x):
    return jnp.maximum(x, 0.0) + jnp.log1p(jnp.exp(-jnp.abs(x)))


def _gelu_tanh(x):
    return 0.5 * x * (1.0 + jnp.tanh(0.7978845608028654 * (x + 0.044715 * x * x * x)))


LRU_PAD = 8


def _lru_tile(x, gate, cw_ref, cb_ref, wr_ref, br_ref, wi_ref, bi_ref, lam_ref, g_ref,
              xext, hc):
    T = x.shape[0]
    xext[LRU_PAD:LRU_PAD + T, :] = x
    xc = cb_ref[...] + cw_ref[CONV_WIDTH - 1:CONV_WIDTH, :] * x
    for j in range(CONV_WIDTH - 1):
        back = CONV_WIDTH - 1 - j
        xc = xc + cw_ref[j:j + 1, :] * xext[LRU_PAD - back:LRU_PAD - back + T, :]
    xext[0:LRU_PAD, :] = xext[T:T + LRU_PAD, :]

    xb = xc.astype(BF16)
    r = jax.nn.sigmoid(jnp.dot(xb, wr_ref[...], preferred_element_type=F32) + br_ref[...])
    ig = jax.nn.sigmoid(jnp.dot(xb, wi_ref[...], preferred_element_type=F32) + bi_ref[...])
    log_a = (-LRU_C) * r * _softplus(-lam_ref[...])
    a = jnp.exp(log_a)
    b = jnp.sqrt(-jnp.tanh(log_a) * (a * a + 1.0)) * (ig * xc)

    G = T // SUBLANES
    a = a.reshape(G, SUBLANES, -1)
    b = b.reshape(G, SUBLANES, -1)
    rows = lax.broadcasted_iota(jnp.int32, a.shape, 1)
    d = 1
    while d < SUBLANES:
        keep = rows >= d
        a_prev = jnp.where(keep, pltpu.roll(a, d, 1), 1.0)
        b_prev = jnp.where(keep, pltpu.roll(b, d, 1), 0.0)
        b = a * b_prev + b
        a = a * a_prev
        d *= 2
    prev = hc[...]
    groups = []
    for g in range(G):
        hg = a[g] * prev + b[g]
        prev = hg[SUBLANES - 1:SUBLANES, :]
        groups.append(hg)
    hc[...] = prev
    h = jnp.concatenate(groups, axis=0)
    return _rms(h * _gelu_tanh(gate), g_ref[...])


def _inproj_kernel(x_ref, mod_ref, g_ref, w_ref, cw_ref, cb_ref, wr_ref, br_ref, wi_ref,
                   bi_ref, lam_ref, gl_ref, q_ref, k_ref, v_ref, yl_ref, xext, hc):
    @pl.when(pl.program_id(1) == 0)
    def _():
        xext[0:LRU_PAD, :] = jnp.zeros((LRU_PAD, xext.shape[1]), F32)
        hc[...] = jnp.zeros_like(hc)

    x = x_ref[0]
    h = _rms(x, g_ref[...]) * (1.0 + mod_ref[0, 1:2, :]) + mod_ref[0, 0:1, :]
    hb = h.astype(BF16)
    W = SB_WIDTH

    def proj(c):
        return jnp.dot(hb, w_ref[:, c * W:(c + 1) * W], preferred_element_type=F32)

    xr, gr = proj(3), proj(4)
    T = LRU_SCAN_ROWS
    for t in range(x.shape[0] // T):
        rows = slice(t * T, (t + 1) * T)
        yl_ref[0, rows, :] = _lru_tile(xr[rows], gr[rows], cw_ref, cb_ref, wr_ref, br_ref,
                                       wi_ref, bi_ref, lam_ref, gl_ref, xext, hc)
    q_ref[0] = (proj(0) * (HEAD_DIM ** -0.5)).astype(BF16)
    k_ref[0] = proj(1).astype(BF16)
    v_ref[0] = proj(2).astype(BF16)


def _block_diag(w):
    H, I, J = w.shape
    eye = jnp.eye(H, dtype=w.dtype)
    return (w[:, :, None, :] * eye[:, None, :, None]).reshape(H * I, H * J)


def _inproj(x, mod3, g, w_in_b, conv_w, conv_b, rg_w, rg_b, ig_w, ig_b, lam, g_lru, ts=512):
    B, S, D = x.shape
    E = w_in_b.shape[1]
    W = SB_WIDTH
    row = pl.BlockSpec((1, ts, W), lambda b, s: (b, s, 0))
    vec = pl.BlockSpec((1, W), lambda b, s: (0, 0))
    mat = pl.BlockSpec((W, W), lambda b, s: (0, 0))
    return pl.pallas_call(
        _inproj_kernel,
        grid=(B, S // ts),
        in_specs=[pl.BlockSpec((1, ts, D), lambda b, s: (b, s, 0)),
                  pl.BlockSpec((1, N_MOD, D), lambda b, s: (b, 0, 0)),
                  pl.BlockSpec((1, D), lambda b, s: (0, 0)),
                  pl.BlockSpec((D, E), lambda b, s: (0, 0)),
                  pl.BlockSpec((CONV_WIDTH, W), lambda b, s: (0, 0)), vec,
                  mat, vec, mat, vec, vec, vec],
        out_specs=[row, row, row, row],
        out_shape=[jax.ShapeDtypeStruct((B, S, W), BF16)] * 3
        + [jax.ShapeDtypeStruct((B, S, W), F32)],
        scratch_shapes=[pltpu.VMEM((LRU_SCAN_ROWS + LRU_PAD, W), F32), pltpu.VMEM((1, W), F32)],
        compiler_params=_params(("arbitrary", "arbitrary")),
        name="inproj",
    )(x, mod3, g.reshape(1, D), w_in_b, conv_w, conv_b.reshape(1, W),
      _block_diag(rg_w).astype(BF16), rg_b.reshape(1, W), _block_diag(ig_w).astype(BF16),
      ig_b.reshape(1, W), lam.reshape(1, W), g_lru.reshape(1, W))


def _attn_kernel(q_ref, k_ref, v_ref, o_ref, acc_ref, carry_ref):
    i = pl.program_id(1)
    QB, KB = Q_BLOCK, K_BLOCK
    n_pairs = q_ref.shape[2] // LANES
    lane = lax.broadcasted_iota(jnp.int32, (QB, LANES), 1)

    def stacked_q(p):
        q = q_ref[0, :, p * LANES:(p + 1) * LANES]
        zero = jnp.zeros_like(q)
        return jnp.concatenate([jnp.where(lane < HEAD_DIM, q, zero),
                                jnp.where(lane >= HEAD_DIM, q, zero)], axis=0)

    qs = [stacked_q(p) for p in range(n_pairs)]

    uj = lax.broadcasted_iota(jnp.int32, (2 * KB, KB + LANES), 0) & (KB - 1)
    us = lax.broadcasted_iota(jnp.int32, (2 * KB, KB + LANES), 1)
    u2 = jnp.where((us >= KB) | (uj > us), -1.0, 0.0).astype(BF16)

    row = lax.broadcasted_iota(jnp.int32, (2 * QB, KB), 0) & (QB - 1)
    col = lax.broadcasted_iota(jnp.int32, (2 * QB, KB), 1)
    causal = col < row

    def tiles(p, j_hi, n, diagonal, acc, carry):
        start = pl.multiple_of((j_hi - (n - 1)) * KB, KB)
        kw = k_ref[0, pl.ds(start, n * KB), p * LANES:(p + 1) * LANES]
        vw = v_ref[0, pl.ds(start, n * KB), p * LANES:(p + 1) * LANES]
        z = lax.dot_general(qs[p], kw, (((1,), (1,)), ((), ())), preferred_element_type=F32)
        softplus = jnp.maximum(z, 0.0) + jnp.log(1.0 + jnp.exp2(jnp.abs(z) * (-LOG2E)))
        log_beta = z - softplus
        ws = [None] * n
        for c in reversed(range(n)):
            sp = softplus[:, c * KB:(c + 1) * KB]
            masked = diagonal and c == n - 1
            if masked:
                sp = jnp.where(causal, sp, 0.0)
            hi_f = lax.bitcast_convert_type(
                lax.bitcast_convert_type(sp, jnp.uint32) & jnp.uint32(0xFFFF0000), F32)
            hi_lo = jnp.concatenate([hi_f.astype(BF16), (sp - hi_f).astype(BF16)], axis=1)
            sums = jnp.dot(hi_lo, u2, preferred_element_type=F32)
            w = jnp.exp2((log_beta[:, c * KB:(c + 1) * KB] + sums[:, :KB] + carry) * LOG2E)
            if masked:
                w = jnp.where(causal, w, 0.0)
            ws[c] = w.astype(BF16)
            carry = carry + sums[:, KB:]
        w_all = ws[0] if n == 1 else jnp.concatenate(ws, axis=1)
        acc = acc + jnp.dot(w_all, vw, preferred_element_type=F32)
        return acc, carry

    def fold(j_hi, n, diagonal):
        cmax = None
        for p in range(n_pairs):
            if diagonal:
                acc = carry = jnp.zeros((2 * QB, LANES), F32)
            else:
                acc, carry = acc_ref[p], carry_ref[p]
            acc, carry = tiles(p, j_hi, n, diagonal, acc, carry)
            acc_ref[p] = acc
            carry_ref[p] = carry
            cmax = carry if cmax is None else jnp.maximum(cmax, carry)
        return jnp.max(cmax)

    n_first, n_loop = ATTN_WINDOW_BLOCKS, ATTN_LOOP_BLOCKS
    n0 = jnp.minimum(i + 1, n_first)
    cmax0 = lax.switch(n0 - 1, [functools.partial(fold, i, n, True)
                                for n in range(1, n_first + 1)])
    j0 = i - n0

    def loop(n, j, cmax):
        def more(st):
            return (st[0] >= n - 1) & (st[1] >= ATTN_UNDERFLOW_LOG)

        def step(st):
            return st[0] - n, fold(st[0], n, False)

        return lax.while_loop(more, step, (j, cmax))

    j, cmax = j0, cmax0
    while n_loop >= 1:
        j, cmax = loop(n_loop, j, cmax)
        n_loop //= 2
    for p in range(n_pairs):
        acc = acc_ref[p]
        o_ref[0, :, p * LANES:(p + 1) * LANES] = jnp.where(lane < HEAD_DIM, acc[:QB], acc[QB:])


def _attention(q, k, v):
    B, S, W = q.shape
    return pl.pallas_call(
        _attn_kernel,
        grid=(B, S // Q_BLOCK),
        in_specs=[pl.BlockSpec((1, Q_BLOCK, W), lambda b, i: (b, i, 0)),
                  pl.BlockSpec((1, S, W), lambda b, i: (b, 0, 0)),
                  pl.BlockSpec((1, S, W), lambda b, i: (b, 0, 0))],
        out_specs=pl.BlockSpec((1, Q_BLOCK, W), lambda b, i: (b, i, 0)),
        out_shape=jax.ShapeDtypeStruct((B, S, W), F32),
        scratch_shapes=[pltpu.VMEM((W // LANES, 2 * Q_BLOCK, LANES), F32),
                        pltpu.VMEM((W // LANES, 2 * Q_BLOCK, LANES), F32)],
        compiler_params=_params(("arbitrary", "arbitrary")),
        name="attn",
    )(q, k, v)


def _load_row_tiles(ref, n_rows, lead=()):
    chunks = [ref[lead + (pl.ds(c, n_rows, stride=SUBLANES), slice(None))]
              for c in range(SUBLANES)]
    return jnp.concatenate(chunks, axis=1)


def _store_row_tiles(ref, value):
    n_rows = value.shape[0]
    for c in range(SUBLANES):
        ref[pl.ds(c, n_rows, stride=SUBLANES), :] = value[:, c * LANES:(c + 1) * LANES]


PACK_ROWS = SUBLANES // 2


def _store_packed_rows(ref, value):
    n_rows, d = value.shape
    bits = lax.bitcast_convert_type(value.astype(BF16).astype(F32), jnp.uint32)
    words = (bits[:, :d // 2] >> 16) | bits[:, d // 2:]
    for c in range(PACK_ROWS):
        ref[pl.ds(c, n_rows, stride=PACK_ROWS), :] = words[:, c * LANES:(c + 1) * LANES]


def _load_packed_rows(ref, n_rows, lead=()):
    lo, hi = [], []
    for c in range(PACK_ROWS):
        w = ref[lead + (pl.ds(c, n_rows, stride=PACK_ROWS), slice(None))]
        lo.append(lax.bitcast_convert_type(w << 16, F32).astype(BF16))
        hi.append(lax.bitcast_convert_type(w & jnp.uint32(0xFFFF0000), F32).astype(BF16))
    return jnp.concatenate(lo + hi, axis=1)


def _outproj_kernel(x_ref, ya_ref, yl_ref, mod_ref, ga_ref, wo_ref, gf_ref, rw_ref, rb_ref,
                    x1_ref, h2_ref, idx_ref, prob_ref, rank_ref, cnt_ref, tri, run):
    W = SB_WIDTH
    ya = _rms(ya_ref[0], ga_ref[...]).astype(BF16)
    yl = yl_ref[0].astype(BF16)
    mix = (jnp.dot(ya, wo_ref[0:W, :], preferred_element_type=F32)
           + jnp.dot(yl, wo_ref[W:2 * W, :], preferred_element_type=F32))
    x1 = x_ref[0] + mod_ref[0, 2:3, :] * mix
    x1_ref[0] = x1
    h2 = _rms(x1, gf_ref[...]) * (1.0 + mod_ref[0, 4:5, :]) + mod_ref[0, 3:4, :]
    _store_packed_rows(h2_ref, h2)

    logits = lax.dot_general(rw_ref[...], h2, (((1,), (1,)), ((), ())), precision=HIGHEST,
                             preferred_element_type=F32) + rb_ref[...]
    n_exp = logits.shape[0]
    eid = lax.broadcasted_iota(jnp.int32, logits.shape, 0)
    vals, idxs = [], []
    for _ in range(TOP_K):
        m = jnp.max(logits, axis=0, keepdims=True)
        sel = jnp.min(jnp.where(logits == m, eid, n_exp), axis=0, keepdims=True)
        vals.append(m)
        idxs.append(sel)
        logits = jnp.where(eid == sel, -jnp.inf, logits)
    es = [jnp.exp(vv - vals[0]) for vv in vals]
    inv = 1.0 / (es[0] + es[1] + es[2] + es[3])

    first = (pl.program_id(0) == 0) & (pl.program_id(1) == 0)

    @pl.when(first)
    def _():
        ti = lax.broadcasted_iota(jnp.int32, tri.shape, 0)
        tj = lax.broadcasted_iota(jnp.int32, tri.shape, 1)
        tri[...] = jnp.where(ti < tj, 1.0, 0.0).astype(BF16)
        run[...] = jnp.zeros_like(run)

    hits = [eid == idxs[r] for r in range(TOP_K)]
    chosen = hits[0] | hits[1] | hits[2] | hits[3]
    cnt = jnp.where(chosen, 1.0, 0.0)
    before = jnp.dot(cnt.astype(BF16), tri[...], preferred_element_type=F32) + run[...]
    for r in range(TOP_K):
        idx_ref[0, r:r + 1, :] = idxs[r]
        prob_ref[0, r:r + 1, :] = es[r] * inv
        rank_ref[0, r:r + 1, :] = jnp.sum(jnp.where(hits[r], before, 0.0), axis=0,
                                          keepdims=True).astype(jnp.int32)
    run[...] = run[...] + jnp.sum(cnt, axis=1, keepdims=True)
    cnt_ref[...] = jnp.broadcast_to(run[...], cnt_ref.shape).astype(jnp.int32)


def _outproj(x, ya, yl, mod3, ga, w_out_b, gf, router_wt, router_b, ts=512):
    B, S, D = x.shape
    W = ya.shape[2]
    NE = router_wt.shape[0]
    rowd = pl.BlockSpec((1, ts, D), lambda b, s: (b, s, 0))
    roww = pl.BlockSpec((1, ts, W), lambda b, s: (b, s, 0))
    sel = pl.BlockSpec((1, TOP_K, ts), lambda b, s: (b, 0, s))
    return pl.pallas_call(
        _outproj_kernel,
        grid=(B, S // ts),
        in_specs=[rowd, roww, roww,
                  pl.BlockSpec((1, N_MOD, D), lambda b, s: (b, 0, 0)),
                  pl.BlockSpec((1, W), lambda b, s: (0, 0)),
                  pl.BlockSpec((2 * W, D), lambda b, s: (0, 0)),
                  pl.BlockSpec((1, D), lambda b, s: (0, 0)),
                  pl.BlockSpec((NE, D), lambda b, s: (0, 0)),
                  pl.BlockSpec((NE, 1), lambda b, s: (0, 0))],
        out_specs=[rowd,
                   pl.BlockSpec((ts * PACK_ROWS, LANES), lambda b, s: (b * (S // ts) + s, 0)),
                   sel, sel, sel, pl.BlockSpec((NE, LANES), lambda b, s: (0, 0))],
        out_shape=[jax.ShapeDtypeStruct((B, S, D), F32),
                   jax.ShapeDtypeStruct((B * S * PACK_ROWS, LANES), jnp.uint32),
                   jax.ShapeDtypeStruct((B, TOP_K, S), jnp.int32),
                   jax.ShapeDtypeStruct((B, TOP_K, S), F32),
                   jax.ShapeDtypeStruct((B, TOP_K, S), jnp.int32),
                   jax.ShapeDtypeStruct((NE, LANES), jnp.int32)],
        scratch_shapes=[pltpu.VMEM((ts, ts), BF16), pltpu.VMEM((NE, 1), F32)],
        compiler_params=_params(("arbitrary", "arbitrary")),
        name="outproj",
    )(x, ya, yl, mod3, ga.reshape(1, W), w_out_b, gf.reshape(1, D), router_wt,
      router_b.reshape(NE, 1))


def _dispatch_kernel(p0_ref, p1_ref, p2_ref, p3_ref, fill_ref, h_ref, xs_ref, zbuf, sem, zsem):
    TT = h_ref.shape[0] // PACK_ROWS
    n_fill = fill_ref.shape[0]

    @pl.when(pl.program_id(0) == 0)
    def _():
        zbuf[...] = jnp.zeros_like(zbuf)

        def fill_copy(e):
            start = pl.multiple_of(jnp.maximum(fill_ref[e], 0), PACK_ROWS)
            return pltpu.make_async_copy(zbuf, xs_ref.at[pl.ds(start, zbuf.shape[0]), :], zsem)

        def start(e, c):
            @pl.when(fill_ref[e] >= 0)
            def _():
                fill_copy(e).start()
            return c

        def wait(e, c):
            @pl.when(fill_ref[e] >= 0)
            def _():
                fill_copy(e).wait()
            return c

        lax.fori_loop(0, n_fill, start, 0)
        lax.fori_loop(0, n_fill, wait, 0)

    pos_refs = (p0_ref, p1_ref, p2_ref, p3_ref)

    def row_copy(t, r):
        src = pl.multiple_of(t * PACK_ROWS, PACK_ROWS)
        dst = pl.multiple_of(pos_refs[r][t], PACK_ROWS)
        return pltpu.make_async_copy(h_ref.at[pl.ds(src, PACK_ROWS), :],
                                     xs_ref.at[pl.ds(dst, PACK_ROWS), :], sem)

    def issue(g, c):
        for tt in range(ROW_DMA_UNROLL):
            for r in range(TOP_K):
                row_copy(g * ROW_DMA_UNROLL + tt, r).start(priority=(tt * TOP_K + r) % 2)
        return c

    lax.fori_loop(0, TT // ROW_DMA_UNROLL, issue, 0)
    for r in range(TOP_K):
        pltpu.make_async_copy(h_ref, xs_ref.at[pl.ds(0, TT * PACK_ROWS), :], sem).wait()


def _slot_specs(tiles_per_batch):
    def spec(r):
        return pl.BlockSpec(
            (TOKEN_TILE,),
            lambda i: ((i // tiles_per_batch * TOP_K + r) * tiles_per_batch
                       + i % tiles_per_batch,),
            memory_space=pltpu.SMEM)
    return [spec(r) for r in range(TOP_K)]


def _dispatch(h2t, pos_flat, fill_start, cap, tiles_per_batch):
    TT = TOKEN_TILE
    n_tiles = h2t.shape[0] // (TT * PACK_ROWS)
    return pl.pallas_call(
        _dispatch_kernel,
        grid=(n_tiles,),
        in_specs=_slot_specs(tiles_per_batch)
        + [pl.BlockSpec(memory_space=pltpu.SMEM),
           pl.BlockSpec((TT * PACK_ROWS, LANES), lambda i: (i, 0))],
        out_specs=pl.BlockSpec(memory_space=pl.ANY),
        out_shape=jax.ShapeDtypeStruct((cap * PACK_ROWS, LANES), jnp.uint32),
        scratch_shapes=[pltpu.VMEM((EXPERT_BLOCK * PACK_ROWS, LANES), jnp.uint32),
                        pltpu.SemaphoreType.DMA, pltpu.SemaphoreType.DMA],
        compiler_params=_params(("arbitrary",)),
        name="dispatch",
    )(pos_flat, pos_flat, pos_flat, pos_flat, fill_start, h2t)


def _expert_kernel(first_ref, count_ref, xs_ref, wg_ref, bg_ref, wu_ref, bu_ref, wd_ref,
                   bd_ref, ys_ref, wgb, wub, wdb, xbuf, ybuf, xsem, ysem):
    e = pl.program_id(0)
    n_exp = pl.num_programs(0)
    rows = ybuf.shape[1]
    first, count = first_ref[e], count_ref[e]

    wgb[...] = wg_ref[0].astype(BF16)
    wub[...] = wu_ref[0].astype(BF16)
    wdb[...] = wd_ref[0].astype(BF16)

    def block(ref, buf, j):
        n = buf.shape[1]
        return ref.at[pl.ds(pl.multiple_of((first + j) * n, n), n), :]

    def x_copy(j, slot):
        return pltpu.make_async_copy(block(xs_ref, xbuf, j), xbuf.at[slot], xsem.at[slot])

    def y_copy(j, slot):
        return pltpu.make_async_copy(ybuf.at[slot], block(ys_ref, ybuf, j), ysem.at[slot])

    n_x = xbuf.shape[0]
    for a in range(n_x - 1):
        @pl.when(count > a)
        def _():
            x_copy(a, a).start()

    def step(j, c):
        slot = j % 2
        xslot = j % n_x
        x_copy(j, xslot).wait()

        @pl.when(j + (n_x - 1) < count)
        def _():
            x_copy(j + (n_x - 1), (j + (n_x - 1)) % n_x).start()

        @pl.when(j >= 2)
        def _():
            y_copy(j - 2, slot).wait()

        xb = _load_packed_rows(xbuf, EXPERT_BLOCK, (xslot,))
        g = jnp.minimum(jnp.dot(xb, wgb[...], preferred_element_type=F32) + bg_ref[0],
                        SWIGLU_LIMIT)
        u = jnp.clip(jnp.dot(xb, wub[...], preferred_element_type=F32) + bu_ref[0],
                     -SWIGLU_LIMIT, SWIGLU_LIMIT)
        act = (u + 1.0) * (g * jax.nn.sigmoid(SWIGLU_ALPHA * g))
        _store_row_tiles(ybuf.at[slot], jnp.dot(act.astype(BF16), wdb[...],
                                                preferred_element_type=F32) + bd_ref[0])
        y_copy(j, slot).start()
        return c

    lax.fori_loop(0, count, step, 0)

    @pl.when(count >= 2)
    def _():
        y_copy(count - 2, count % 2).wait()

    @pl.when(count >= 1)
    def _():
        y_copy(count - 1, (count - 1) % 2).wait()

    @pl.when(e == n_exp - 1)
    def _():
        spare_first, spare_count = first_ref[n_exp], count_ref[n_exp]
        ybuf[0] = jnp.zeros(ybuf.shape[1:], F32)

        def spare_copy(j):
            dst = pl.ds(pl.multiple_of((spare_first + j) * rows, rows), rows)
            return pltpu.make_async_copy(ybuf.at[0], ys_ref.at[dst, :], ysem.at[0])

        def start(j, c):
            spare_copy(j).start()
            return c

        def wait(j, c):
            spare_copy(j).wait()
            return c

        lax.fori_loop(0, spare_count, start, 0)
        lax.fori_loop(0, spare_count, wait, 0)


def _experts(xs, first_block, block_count, wg, bg, wu, bu, wd, bd):
    NE, D, DE = wg.shape
    cap = xs.shape[0] // PACK_ROWS

    def wsel(e, first, count):
        return (e, 0, 0)

    grid_spec = pltpu.PrefetchScalarGridSpec(
        num_scalar_prefetch=2,
        grid=(NE,),
        in_specs=[pl.BlockSpec(memory_space=pl.ANY),
                  pl.BlockSpec((1, D, DE), wsel), pl.BlockSpec((1, 1, DE), wsel),
                  pl.BlockSpec((1, D, DE), wsel), pl.BlockSpec((1, 1, DE), wsel),
                  pl.BlockSpec((1, DE, D), wsel), pl.BlockSpec((1, 1, D), wsel)],
        out_specs=pl.BlockSpec(memory_space=pl.ANY),
        scratch_shapes=[pltpu.VMEM((D, DE), BF16), pltpu.VMEM((D, DE), BF16),
                        pltpu.VMEM((DE, D), BF16),
                        pltpu.VMEM((3, EXPERT_BLOCK * PACK_ROWS, LANES), jnp.uint32),
                        pltpu.VMEM((2, EXPERT_BLOCK * SUBLANES, LANES), F32),
                        pltpu.SemaphoreType.DMA((3,)), pltpu.SemaphoreType.DMA((2,))],
    )
    return pl.pallas_call(
        _expert_kernel,
        grid_spec=grid_spec,
        out_shape=jax.ShapeDtypeStruct((cap * SUBLANES, LANES), F32),
        compiler_params=_params(("arbitrary",)),
        name="experts",
    )(first_block, block_count, xs, wg, bg.reshape(NE, 1, DE), wu, bu.reshape(NE, 1, DE), wd,
      bd.reshape(NE, 1, D))


def _combine_kernel(c0_ref, c1_ref, c2_ref, c3_ref, n0_ref, n1_ref, n2_ref, n3_ref,
                    x1_ref, p_ref, mod_ref, g_ref, ys_ref, o_ref, buf, sem):
    TT = x1_ref.shape[0]
    i = pl.program_id(0)

    def gather(pos_refs, s):
        def issue(g, c):
            for tt in range(ROW_DMA_UNROLL):
                t = g * ROW_DMA_UNROLL + tt
                for r in range(TOP_K):
                    pltpu.make_async_copy(
                        ys_ref.at[pl.ds(pl.multiple_of(pos_refs[r][t], SUBLANES), SUBLANES), :],
                        buf.at[s, r, pl.ds(pl.multiple_of(t * SUBLANES, SUBLANES), SUBLANES), :],
                        sem.at[s]).start(priority=(tt * TOP_K + r) % 2)
            return c
        lax.fori_loop(0, TT // ROW_DMA_UNROLL, issue, 0)

    @pl.when(i == 0)
    def _():
        gather((c0_ref, c1_ref, c2_ref, c3_ref), 0)

    def reduce_tile(s):
        @pl.when(i + 1 < pl.num_programs(0))
        def _():
            gather((n0_ref, n1_ref, n2_ref, n3_ref), 1 - s)

        for r in range(TOP_K):
            pltpu.make_async_copy(ys_ref.at[pl.ds(0, TT * SUBLANES), :], buf.at[s, r],
                                  sem.at[s]).wait()
        p = p_ref[...]
        moe = p[:, 0:1] * _load_row_tiles(buf, TT, (s, 0))
        for r in range(1, TOP_K):
            moe = moe + p[:, r:r + 1] * _load_row_tiles(buf, TT, (s, r))
        x2 = x1_ref[...] + mod_ref[0, 5:6, :] * moe
        o_ref[...] = _rms(x2, g_ref[...])

    for s in range(2):
        pl.when(i % 2 == s)(functools.partial(reduce_tile, s))


def _combine(x1f, probs, pos_flat, mod3, g, ys, tiles_per_batch):
    N, D = x1f.shape
    TT = TOKEN_TILE
    n_tiles = N // TT

    def next_spec(r):
        def index(i):
            j = jnp.minimum(i + 1, n_tiles - 1)
            return ((j // tiles_per_batch * TOP_K + r) * tiles_per_batch + j % tiles_per_batch,)
        return pl.BlockSpec((TT,), index, memory_space=pltpu.SMEM)

    return pl.pallas_call(
        _combine_kernel,
        grid=(n_tiles,),
        in_specs=_slot_specs(tiles_per_batch) + [next_spec(r) for r in range(TOP_K)]
        + [pl.BlockSpec((TT, D), lambda i: (i, 0)),
           pl.BlockSpec((TT, TOP_K), lambda i: (i, 0)),
           pl.BlockSpec((1, N_MOD, D), lambda i: (i // tiles_per_batch, 0, 0)),
           pl.BlockSpec((1, D), lambda i: (0, 0)),
           pl.BlockSpec(memory_space=pl.ANY)],
        out_specs=pl.BlockSpec((TT, D), lambda i: (i, 0)),
        out_shape=jax.ShapeDtypeStruct((N, D), F32),
        scratch_shapes=[pltpu.VMEM((2, TOP_K, TT * SUBLANES, LANES), F32),
                        pltpu.SemaphoreType.DMA((2,))],
        compiler_params=_params(("arbitrary",)),
        name="combine",
    )(*([pos_flat] * (2 * TOP_K)), x1f, probs, mod3, g.reshape(1, D), ys)


def _routing(idx, rank, counts):
    B, K, S = idx.shape
    n_exp = counts.shape[0]
    TM = EXPERT_BLOCK
    n_assign = B * S * K
    padded = ((counts + TM - 1) // TM) * TM
    pad_ends = jnp.cumsum(padded)
    pad_starts = pad_ends - padded
    experts = jnp.arange(n_exp, dtype=jnp.int32).reshape(n_exp, 1, 1, 1)
    base = jnp.sum(jnp.where(idx[None] == experts, pad_starts.reshape(n_exp, 1, 1, 1), 0),
                   axis=0)
    pos = (base + rank).astype(jnp.int32).reshape(n_assign)
    n_blocks = n_assign // TM + n_exp
    first_block = jnp.concatenate([pad_starts, pad_ends[-1:]]) // TM
    block_count = jnp.concatenate([padded // TM, n_blocks - pad_ends[-1:] // TM])
    tail_start = jnp.where(padded > 0, pad_ends - TM, -1)
    spare = pad_ends[-1] + jnp.arange(n_exp, dtype=jnp.int32) * TM
    fill_start = jnp.concatenate(
        [tail_start, jnp.where(spare < n_blocks * TM, spare, -1)]).astype(jnp.int32)
    return (pos, first_block.astype(jnp.int32), block_count.astype(jnp.int32),
            fill_start, n_blocks * TM)


def kernel(x, c, ada_w, ada_b, mix_norm_g, w_in, conv_w, conv_b, rg_w, rg_b, ig_w, ig_b,
           lru_lambda, attn_out_g, lru_out_g, w_out, ffn_norm_g, router_w, router_b,
           exp_w_gate, exp_b_gate, exp_w_up, exp_b_up, exp_w_down, exp_b_down, final_norm_g):
    B, S, D = x.shape
    depth = ada_w.shape[0]
    assert S % 512 == 0 and S % TOKEN_TILE == 0 and D == SUBLANES * LANES
    for l in range(depth):
        mod3 = _ada(c, ada_w[l], ada_b[l]).reshape(B, N_MOD, D)
        q, k, v, yl = _inproj(x, mod3, mix_norm_g[l], w_in[l].astype(BF16), conv_w[l],
                              conv_b[l], rg_w[l], rg_b[l], ig_w[l], ig_b[l], lru_lambda[l],
                              lru_out_g[l])
        ya = _attention(q, k, v)
        x1, h2, idx, prob, rank, cnt = _outproj(
            x, ya, yl, mod3, attn_out_g[l], w_out[l].astype(BF16), ffn_norm_g[l],
            router_w[l].T, router_b[l])
        pos, first_block, block_count, fill_start, cap = _routing(idx, rank, cnt[:, 0])
        xs = _dispatch(h2, pos * PACK_ROWS, fill_start * PACK_ROWS, cap, S // TOKEN_TILE)
        ys = _experts(xs, first_block, block_count, exp_w_gate[l], exp_b_gate[l],
                      exp_w_up[l], exp_b_up[l], exp_w_down[l], exp_b_down[l])
        probs = jnp.transpose(prob, (0, 2, 1)).reshape(B * S, TOP_K)
        assert depth == 1
        x = _combine(x1.reshape(B * S, D), probs, pos * SUBLANES, mod3, final_norm_g, ys,
                     S // TOKEN_TILE).reshape(B, S, D)
    return x
```

```python
import functools

import jax
import jax.numpy as jnp
from jax import lax
from jax.experimental import pallas as pl
from jax.experimental.pallas import tpu as pltpu

F32 = jnp.float32
BF16 = jnp.bfloat16
HIGHEST = lax.Precision.HIGHEST

EPS = 1e-6
N_MOD = 6
SB_HEADS = 8
HEAD_DIM = 64
SB_WIDTH = SB_HEADS * HEAD_DIM
LRU_BLOCKS = 8
CONV_WIDTH = 4
LRU_C = 8.0
TOP_K = 4
SWIGLU_LIMIT = 7.0
SWIGLU_ALPHA = 1.702
LOG2E = 1.4426950408889634

LANES = 128
SUBLANES = 8
VMEM_LIMIT = 56 * 1024 * 1024

Q_BLOCK = 128
K_BLOCK = 128
ATTN_WINDOW_BLOCKS = 5
ATTN_LOOP_BLOCKS = 2
ATTN_UNDERFLOW_LOG = -110.0
EXPERT_BLOCK = 512
TOKEN_TILE = 512
ROW_DMA_UNROLL = 8
LRU_SCAN_ROWS = 256


def _params(sem):
    return pltpu.CompilerParams(dimension_semantics=sem, vmem_limit_bytes=VMEM_LIMIT)


def _ada_kernel(c_ref, w_ref, b_ref, o_ref):
    c = c_ref[...]
    ca = c * jax.nn.sigmoid(c)
    o_ref[...] = jnp.dot(ca, w_ref[...], precision=HIGHEST,
                         preferred_element_type=F32) + b_ref[...]


def _ada(c, ada_w, ada_b):
    B, D = c.shape
    E = ada_w.shape[1]
    tn = 1024
    return pl.pallas_call(
        _ada_kernel,
        grid=(E // tn,),
        in_specs=[pl.BlockSpec((B, D), lambda j: (0, 0)),
                  pl.BlockSpec((D, tn), lambda j: (0, j)),
                  pl.BlockSpec((1, tn), lambda j: (0, j))],
        out_specs=pl.BlockSpec((B, tn), lambda j: (0, j)),
        out_shape=jax.ShapeDtypeStruct((B, E), F32),
        compiler_params=_params(("arbitrary",)),
        name="ada",
    )(c, ada_w, ada_b.reshape(1, E))


def _rms(x, g):
    ms = jnp.mean(x * x, axis=-1, keepdims=True)
    return x * lax.rsqrt(ms + EPS) * g


def _softplus(x):
    return jnp.maximum(x, 0.0) + jnp.log1p(jnp.exp(-jnp.abs(x)))


def _gelu_tanh(x):
    return 0.5 * x * (1.0 + jnp.tanh(0.7978845608028654 * (x + 0.044715 * x * x * x)))


LRU_PAD = 8


def _lru_tile(x, gate, cw_ref, cb_ref, wr_ref, br_ref, wi_ref, bi_ref, lam_ref, g_ref,
              xext, hc):
    T = x.shape[0]
    xext[LRU_PAD:LRU_PAD + T, :] = x
    xc = cb_ref[...] + cw_ref[CONV_WIDTH - 1:CONV_WIDTH, :] * x
    for j in range(CONV_WIDTH - 1):
        back = CONV_WIDTH - 1 - j
        xc = xc + cw_ref[j:j + 1, :] * xext[LRU_PAD - back:LRU_PAD - back + T, :]
    xext[0:LRU_PAD, :] = xext[T:T + LRU_PAD, :]

    xb = xc.astype(BF16)
    r = jax.nn.sigmoid(jnp.dot(xb, wr_ref[...], preferred_element_type=F32) + br_ref[...])
    ig = jax.nn.sigmoid(jnp.dot(xb, wi_ref[...], preferred_element_type=F32) + bi_ref[...])
    log_a = (-LRU_C) * r * _softplus(-lam_ref[...])
    a = jnp.exp(log_a)
    b = jnp.sqrt(-jnp.tanh(log_a) * (a * a + 1.0)) * (ig * xc)

    G = T // SUBLANES
    a = a.reshape(G, SUBLANES, -1)
    b = b.reshape(G, SUBLANES, -1)
    rows = lax.broadcasted_iota(jnp.int32, a.shape, 1)
    d = 1
    while d < SUBLANES:
        keep = rows >= d
        a_prev = jnp.where(keep, pltpu.roll(a, d, 1), 1.0)
        b_prev = jnp.where(keep, pltpu.roll(b, d, 1), 0.0)
        b = a * b_prev + b
        a = a * a_prev
        d *= 2
    prev = hc[...]
    groups = []
    for g in range(G):
        hg = a[g] * prev + b[g]
        prev = hg[SUBLANES - 1:SUBLANES, :]
        groups.append(hg)
    hc[...] = prev
    h = jnp.concatenate(groups, axis=0)
    return _rms(h * _gelu_tanh(gate), g_ref[...])


def _inproj_kernel(x_ref, mod_ref, g_ref, w_ref, cw_ref, cb_ref, wr_ref, br_ref, wi_ref,
                   bi_ref, lam_ref, gl_ref, q_ref, k_ref, v_ref, yl_ref, xext, hc):
    @pl.when(pl.program_id(1) == 0)
    def _():
        xext[0:LRU_PAD, :] = jnp.zeros((LRU_PAD, xext.shape[1]), F32)
        hc[...] = jnp.zeros_like(hc)

    x = x_ref[0]
    h = _rms(x, g_ref[...]) * (1.0 + mod_ref[0, 1:2, :]) + mod_ref[0, 0:1, :]
    hb = h.astype(BF16)
    W = SB_WIDTH

    def proj(c):
        return jnp.dot(hb, w_ref[:, c * W:(c + 1) * W], preferred_element_type=F32)

    xr, gr = proj(3), proj(4)
    T = LRU_SCAN_ROWS
    for t in range(x.shape[0] // T):
        rows = slice(t * T, (t + 1) * T)
        yl_ref[0, rows, :] = _lru_tile(xr[rows], gr[rows], cw_ref, cb_ref, wr_ref, br_ref,
                                       wi_ref, bi_ref, lam_ref, gl_ref, xext, hc)
    q_ref[0] = (proj(0) * (HEAD_DIM ** -0.5)).astype(BF16)
    k_ref[0] = proj(1).astype(BF16)
    v_ref[0] = proj(2).astype(BF16)


def _block_diag(w):
    H, I, J = w.shape
    eye = jnp.eye(H, dtype=w.dtype)
    return (w[:, :, None, :] * eye[:, None, :, None]).reshape(H * I, H * J)


def _inproj(x, mod3, g, w_in_b, conv_w, conv_b, rg_w, rg_b, ig_w, ig_b, lam, g_lru, ts=512):
    B, S, D = x.shape
    E = w_in_b.shape[1]
    W = SB_WIDTH
    row = pl.BlockSpec((1, ts, W), lambda b, s: (b, s, 0))
    vec = pl.BlockSpec((1, W), lambda b, s: (0, 0))
    mat = pl.BlockSpec((W, W), lambda b, s: (0, 0))
    return pl.pallas_call(
        _inproj_kernel,
        grid=(B, S // ts),
        in_specs=[pl.BlockSpec((1, ts, D), lambda b, s: (b, s, 0)),
                  pl.BlockSpec((1, N_MOD, D), lambda b, s: (b, 0, 0)),
                  pl.BlockSpec((1, D), lambda b, s: (0, 0)),
                  pl.BlockSpec((D, E), lambda b, s: (0, 0)),
                  pl.BlockSpec((CONV_WIDTH, W), lambda b, s: (0, 0)), vec,
                  mat, vec, mat, vec, vec, vec],
        out_specs=[row, row, row, row],
        out_shape=[jax.ShapeDtypeStruct((B, S, W), BF16)] * 3
        + [jax.ShapeDtypeStruct((B, S, W), F32)],
        scratch_shapes=[pltpu.VMEM((LRU_SCAN_ROWS + LRU_PAD, W), F32), pltpu.VMEM((1, W), F32)],
        compiler_params=_params(("arbitrary", "arbitrary")),
        name="inproj",
    )(x, mod3, g.reshape(1, D), w_in_b, conv_w, conv_b.reshape(1, W),
      _block_diag(rg_w).astype(BF16), rg_b.reshape(1, W), _block_diag(ig_w).astype(BF16),
      ig_b.reshape(1, W), lam.reshape(1, W), g_lru.reshape(1, W))


def _attn_kernel(q_ref, k_ref, v_ref, o_ref, acc_ref, carry_ref):
    i = pl.program_id(1)
    QB, KB = Q_BLOCK, K_BLOCK
    n_pairs = q_ref.shape[2] // LANES
    lane = lax.broadcasted_iota(jnp.int32, (QB, LANES), 1)

    def stacked_q(p):
        q = q_ref[0, :, p * LANES:(p + 1) * LANES]
        zero = jnp.zeros_like(q)
        return jnp.concatenate([jnp.where(lane < HEAD_DIM, q, zero),
                                jnp.where(lane >= HEAD_DIM, q, zero)], axis=0)

    qs = [stacked_q(p) for p in range(n_pairs)]

    uj = lax.broadcasted_iota(jnp.int32, (2 * KB, KB + LANES), 0) & (KB - 1)
    us = lax.broadcasted_iota(jnp.int32, (2 * KB, KB + LANES), 1)
    u2 = jnp.where((us >= KB) | (uj > us), -1.0, 0.0).astype(BF16)

    row = lax.broadcasted_iota(jnp.int32, (2 * QB, KB), 0) & (QB - 1)
    col = lax.broadcasted_iota(jnp.int32, (2 * QB, KB), 1)
    causal = col < row

    def tiles(p, j_hi, n, diagonal, acc, carry):
        start = pl.multiple_of((j_hi - (n - 1)) * KB, KB)
        kw = k_ref[0, pl.ds(start, n * KB), p * LANES:(p + 1) * LANES]
        vw = v_ref[0, pl.ds(start, n * KB), p * LANES:(p + 1) * LANES]
        z = lax.dot_general(qs[p], kw, (((1,), (1,)), ((), ())), preferred_element_type=F32)
        softplus = jnp.maximum(z, 0.0) + jnp.log(1.0 + jnp.exp2(jnp.abs(z) * (-LOG2E)))
        log_beta = z - softplus
        ws = [None] * n
        for c in reversed(range(n)):
            sp = softplus[:, c * KB:(c + 1) * KB]
            masked = diagonal and c == n - 1
            if masked:
                sp = jnp.where(causal, sp, 0.0)
            hi_f = lax.bitcast_convert_type(
                lax.bitcast_convert_type(sp, jnp.uint32) & jnp.uint32(0xFFFF0000), F32)
            hi_lo = jnp.concatenate([hi_f.astype(BF16), (sp - hi_f).astype(BF16)], axis=1)
            sums = jnp.dot(hi_lo, u2, preferred_element_type=F32)
            w = jnp.exp2((log_beta[:, c * KB:(c + 1) * KB] + sums[:, :KB] + carry) * LOG2E)
            if masked:
                w = jnp.where(causal, w, 0.0)
            ws[c] = w.astype(BF16)
            carry = carry + sums[:, KB:]
        w_all = ws[0] if n == 1 else jnp.concatenate(ws, axis=1)
        acc = acc + jnp.dot(w_all, vw, preferred_element_type=F32)
        return acc, carry

    def fold(j_hi, n, diagonal):
        cmax = None
        for p in range(n_pairs):
            if diagonal:
                acc = carry = jnp.zeros((2 * QB, LANES), F32)
            else:
                acc, carry = acc_ref[p], carry_ref[p]
            acc, carry = tiles(p, j_hi, n, diagonal, acc, carry)
            acc_ref[p] = acc
            carry_ref[p] = carry
            cmax = carry if cmax is None else jnp.maximum(cmax, carry)
        return jnp.max(cmax)

    n_first, n_loop = ATTN_WINDOW_BLOCKS, ATTN_LOOP_BLOCKS
    n0 = jnp.minimum(i + 1, n_first)
    cmax0 = lax.switch(n0 - 1, [functools.partial(fold, i, n, True)
                                for n in range(1, n_first + 1)])
    j0 = i - n0

    def loop(n, j, cmax):
        def more(st):
            return (st[0] >= n - 1) & (st[1] >= ATTN_UNDERFLOW_LOG)

        def step(st):
            return st[0] - n, fold(st[0], n, False)

        return lax.while_loop(more, step, (j, cmax))

    j, cmax = j0, cmax0
    while n_loop >= 1:
        j, cmax = loop(n_loop, j, cmax)
        n_loop //= 2
    for p in range(n_pairs):
        acc = acc_ref[p]
        o_ref[0, :, p * LANES:(p + 1) * LANES] = jnp.where(lane < HEAD_DIM, acc[:QB], acc[QB:])


def _attention(q, k, v):
    B, S, W = q.shape
    return pl.pallas_call(
        _attn_kernel,
        grid=(B, S // Q_BLOCK),
        in_specs=[pl.BlockSpec((1, Q_BLOCK, W), lambda b, i: (b, i, 0)),
                  pl.BlockSpec((1, S, W), lambda b, i: (b, 0, 0)),
                  pl.BlockSpec((1, S, W), lambda b, i: (b, 0, 0))],
        out_specs=pl.BlockSpec((1, Q_BLOCK, W), lambda b, i: (b, i, 0)),
        out_shape=jax.ShapeDtypeStruct((B, S, W), F32),
        scratch_shapes=[pltpu.VMEM((W // LANES, 2 * Q_BLOCK, LANES), F32),
                        pltpu.VMEM((W // LANES, 2 * Q_BLOCK, LANES), F32)],
        compiler_params=_params(("arbitrary", "arbitrary")),
        name="attn",
    )(q, k, v)


def _load_row_tiles(ref, n_rows, lead=()):
    chunks = [ref[lead + (pl.ds(c, n_rows, stride=SUBLANES), slice(None))]
              for c in range(SUBLANES)]
    return jnp.concatenate(chunks, axis=1)


def _store_row_tiles(ref, value):
    n_rows = value.shape[0]
    for c in range(SUBLANES):
        ref[pl.ds(c, n_rows, stride=SUBLANES), :] = value[:, c * LANES:(c + 1) * LANES]


PACK_ROWS = SUBLANES // 2


def _store_packed_rows(ref, value):
    n_rows, d = value.shape
    bits = lax.bitcast_convert_type(value.astype(BF16).astype(F32), jnp.uint32)
    words = (bits[:, :d // 2] >> 16) | bits[:, d // 2:]
    for c in range(PACK_ROWS):
        ref[pl.ds(c, n_rows, stride=PACK_ROWS), :] = words[:, c * LANES:(c + 1) * LANES]


def _load_packed_rows(ref, n_rows, lead=()):
    lo, hi = [], []
    for c in range(PACK_ROWS):
        w = ref[lead + (pl.ds(c, n_rows, stride=PACK_ROWS), slice(None))]
        lo.append(lax.bitcast_convert_type(w << 16, F32).astype(BF16))
        hi.append(lax.bitcast_convert_type(w & jnp.uint32(0xFFFF0000), F32).astype(BF16))
    return jnp.concatenate(lo + hi, axis=1)


def _outproj_kernel(x_ref, ya_ref, yl_ref, mod_ref, ga_ref, wo_ref, gf_ref, rw_ref, rb_ref,
                    x1_ref, h2_ref, idx_ref, prob_ref, rank_ref, cnt_ref, tri, run):
    W = SB_WIDTH
    ya = _rms(ya_ref[0], ga_ref[...]).astype(BF16)
    yl = yl_ref[0].astype(BF16)
    mix = (jnp.dot(ya, wo_ref[0:W, :], preferred_element_type=F32)
           + jnp.dot(yl, wo_ref[W:2 * W, :], preferred_element_type=F32))
    x1 = x_ref[0] + mod_ref[0, 2:3, :] * mix
    x1_ref[0] = x1
    h2 = _rms(x1, gf_ref[...]) * (1.0 + mod_ref[0, 4:5, :]) + mod_ref[0, 3:4, :]
    _store_packed_rows(h2_ref, h2)

    logits = lax.dot_general(rw_ref[...], h2, (((1,), (1,)), ((), ())), precision=HIGHEST,
                             preferred_element_type=F32) + rb_ref[...]
    n_exp = logits.shape[0]
    eid = lax.broadcasted_iota(jnp.int32, logits.shape, 0)
    vals, idxs = [], []
    for _ in range(TOP_K):
        m = jnp.max(logits, axis=0, keepdims=True)
        sel = jnp.min(jnp.where(logits == m, eid, n_exp), axis=0, keepdims=True)
        vals.append(m)
        idxs.append(sel)
        logits = jnp.where(eid == sel, -jnp.inf, logits)
    es = [jnp.exp(vv - vals[0]) for vv in vals]
    inv = 1.0 / (es[0] + es[1] + es[2] + es[3])

    first = (pl.program_id(0) == 0) & (pl.program_id(1) == 0)

    @pl.when(first)
    def _():
        ti = lax.broadcasted_iota(jnp.int32, tri.shape, 0)
        tj = lax.broadcasted_iota(jnp.int32, tri.shape, 1)
        tri[...] = jnp.where(ti < tj, 1.0, 0.0).astype(BF16)
        run[...] = jnp.zeros_like(run)

    hits = [eid == idxs[r] for r in range(TOP_K)]
    chosen = hits[0] | hits[1] | hits[2] | hits[3]
    cnt = jnp.where(chosen, 1.0, 0.0)
    before = jnp.dot(cnt.astype(BF16), tri[...], preferred_element_type=F32) + run[...]
    for r in range(TOP_K):
        idx_ref[0, r:r + 1, :] = idxs[r]
        prob_ref[0, r:r + 1, :] = es[r] * inv
        rank_ref[0, r:r + 1, :] = jnp.sum(jnp.where(hits[r], before, 0.0), axis=0,
                                          keepdims=True).astype(jnp.int32)
    run[...] = run[...] + jnp.sum(cnt, axis=1, keepdims=True)
    cnt_ref[...] = jnp.broadcast_to(run[...], cnt_ref.shape).astype(jnp.int32)


def _outproj(x, ya, yl, mod3, ga, w_out_b, gf, router_wt, router_b, ts=512):
    B, S, D = x.shape
    W = ya.shape[2]
    NE = router_wt.shape[0]
    rowd = pl.BlockSpec((1, ts, D), lambda b, s: (b, s, 0))
    roww = pl.BlockSpec((1, ts, W), lambda b, s: (b, s, 0))
    sel = pl.BlockSpec((1, TOP_K, ts), lambda b, s: (b, 0, s))
    return pl.pallas_call(
        _outproj_kernel,
        grid=(B, S // ts),
        in_specs=[rowd, roww, roww,
                  pl.BlockSpec((1, N_MOD, D), lambda b, s: (b, 0, 0)),
                  pl.BlockSpec((1, W), lambda b, s: (0, 0)),
                  pl.BlockSpec((2 * W, D), lambda b, s: (0, 0)),
                  pl.BlockSpec((1, D), lambda b, s: (0, 0)),
                  pl.BlockSpec((NE, D), lambda b, s: (0, 0)),
                  pl.BlockSpec((NE, 1), lambda b, s: (0, 0))],
        out_specs=[rowd,
                   pl.BlockSpec((ts * PACK_ROWS, LANES), lambda b, s: (b * (S // ts) + s, 0)),
                   sel, sel, sel, pl.BlockSpec((NE, LANES), lambda b, s: (0, 0))],
        out_shape=[jax.ShapeDtypeStruct((B, S, D), F32),
                   jax.ShapeDtypeStruct((B * S * PACK_ROWS, LANES), jnp.uint32),
                   jax.ShapeDtypeStruct((B, TOP_K, S), jnp.int32),
                   jax.ShapeDtypeStruct((B, TOP_K, S), F32),
                   jax.ShapeDtypeStruct((B, TOP_K, S), jnp.int32),
                   jax.ShapeDtypeStruct((NE, LANES), jnp.int32)],
        scratch_shapes=[pltpu.VMEM((ts, ts), BF16), pltpu.VMEM((NE, 1), F32)],
        compiler_params=_params(("arbitrary", "arbitrary")),
        name="outproj",
    )(x, ya, yl, mod3, ga.reshape(1, W), w_out_b, gf.reshape(1, D), router_wt,
      router_b.reshape(NE, 1))


def _dispatch_kernel(p0_ref, p1_ref, p2_ref, p3_ref, fill_ref, h_ref, xs_ref, zbuf, sem, zsem):
    TT = h_ref.shape[0] // PACK_ROWS
    n_fill = fill_ref.shape[0]

    @pl.when(pl.program_id(0) == 0)
    def _():
        zbuf[...] = jnp.zeros_like(zbuf)

        def fill_copy(e):
            start = pl.multiple_of(jnp.maximum(fill_ref[e], 0), PACK_ROWS)
            return pltpu.make_async_copy(zbuf, xs_ref.at[pl.ds(start, zbuf.shape[0]), :], zsem)

        def start(e, c):
            @pl.when(fill_ref[e] >= 0)
            def _():
                fill_copy(e).start()
            return c

        def wait(e, c):
            @pl.when(fill_ref[e] >= 0)
            def _():
                fill_copy(e).wait()
            return c

        lax.fori_loop(0, n_fill, start, 0)
        lax.fori_loop(0, n_fill, wait, 0)

    pos_refs = (p0_ref, p1_ref, p2_ref, p3_ref)

    def row_copy(t, r):
        src = pl.multiple_of(t * PACK_ROWS, PACK_ROWS)
        dst = pl.multiple_of(pos_refs[r][t], PACK_ROWS)
        return pltpu.make_async_copy(h_ref.at[pl.ds(src, PACK_ROWS), :],
                                     xs_ref.at[pl.ds(dst, PACK_ROWS), :], sem)

    def issue(g, c):
        for tt in range(ROW_DMA_UNROLL):
            for r in range(TOP_K):
                row_copy(g * ROW_DMA_UNROLL + tt, r).start(priority=(tt * TOP_K + r) % 2)
        return c

    lax.fori_loop(0, TT // ROW_DMA_UNROLL, issue, 0)
    for r in range(TOP_K):
        pltpu.make_async_copy(h_ref, xs_ref.at[pl.ds(0, TT * PACK_ROWS), :], sem).wait()


def _slot_specs(tiles_per_batch):
    def spec(r):
        return pl.BlockSpec(
            (TOKEN_TILE,),
            lambda i: ((i // tiles_per_batch * TOP_K + r) * tiles_per_batch
                       + i % tiles_per_batch,),
            memory_space=pltpu.SMEM)
    return [spec(r) for r in range(TOP_K)]


def _dispatch(h2t, pos_flat, fill_start, cap, tiles_per_batch):
    TT = TOKEN_TILE
    n_tiles = h2t.shape[0] // (TT * PACK_ROWS)
    return pl.pallas_call(
        _dispatch_kernel,
        grid=(n_tiles,),
        in_specs=_slot_specs(tiles_per_batch)
        + [pl.BlockSpec(memory_space=pltpu.SMEM),
           pl.BlockSpec((TT * PACK_ROWS, LANES), lambda i: (i, 0))],
        out_specs=pl.BlockSpec(memory_space=pl.ANY),
        out_shape=jax.ShapeDtypeStruct((cap * PACK_ROWS, LANES), jnp.uint32),
        scratch_shapes=[pltpu.VMEM((EXPERT_BLOCK * PACK_ROWS, LANES), jnp.uint32),
                        pltpu.SemaphoreType.DMA, pltpu.SemaphoreType.DMA],
        compiler_params=_params(("arbitrary",)),
        name="dispatch",
    )(pos_flat, pos_flat, pos_flat, pos_flat, fill_start, h2t)


def _expert_kernel(first_ref, count_ref, xs_ref, wg_ref, bg_ref, wu_ref, bu_ref, wd_ref,
                   bd_ref, ys_ref, wgb, wub, wdb, xbuf, ybuf, xsem, ysem):
    e = pl.program_id(0)
    n_exp = pl.num_programs(0)
    rows = ybuf.shape[1]
    first, count = first_ref[e], count_ref[e]

    wgb[...] = wg_ref[0].astype(BF16)
    wub[...] = wu_ref[0].astype(BF16)
    wdb[...] = wd_ref[0].astype(BF16)

    def block(ref, buf, j):
        n = buf.shape[1]
        return ref.at[pl.ds(pl.multiple_of((first + j) * n, n), n), :]

    def x_copy(j, slot):
        return pltpu.make_async_copy(block(xs_ref, xbuf, j), xbuf.at[slot], xsem.at[slot])

    def y_copy(j, slot):
        return pltpu.make_async_copy(ybuf.at[slot], block(ys_ref, ybuf, j), ysem.at[slot])

    @pl.when(count > 0)
    def _():
        x_copy(0, 0).start()

    def step(j, c):
        slot = j % 2
        x_copy(j, slot).wait()

        @pl.when(j + 1 < count)
        def _():
            x_copy(j + 1, 1 - slot).start()

        @pl.when(j >= 2)
        def _():
            y_copy(j - 2, slot).wait()

        xb = _load_packed_rows(xbuf, EXPERT_BLOCK, (slot,))
        g = jnp.minimum(jnp.dot(xb, wgb[...], preferred_element_type=F32) + bg_ref[0],
                        SWIGLU_LIMIT)
        u = jnp.clip(jnp.dot(xb, wub[...], preferred_element_type=F32) + bu_ref[0],
                     -SWIGLU_LIMIT, SWIGLU_LIMIT)
        act = (u + 1.0) * (g * jax.nn.sigmoid(SWIGLU_ALPHA * g))
        _store_row_tiles(ybuf.at[slot], jnp.dot(act.astype(BF16), wdb[...],
                                                preferred_element_type=F32) + bd_ref[0])
        y_copy(j, slot).start()
        return c

    lax.fori_loop(0, count, step, 0)

    @pl.when(count >= 2)
    def _():
        y_copy(count - 2, count % 2).wait()

    @pl.when(count >= 1)
    def _():
        y_copy(count - 1, (count - 1) % 2).wait()

    @pl.when(e == n_exp - 1)
    def _():
        spare_first, spare_count = first_ref[n_exp], count_ref[n_exp]
        ybuf[0] = jnp.zeros(ybuf.shape[1:], F32)

        def spare_copy(j):
            dst = pl.ds(pl.multiple_of((spare_first + j) * rows, rows), rows)
            return pltpu.make_async_copy(ybuf.at[0], ys_ref.at[dst, :], ysem.at[0])

        def start(j, c):
            spare_copy(j).start()
            return c

        def wait(j, c):
            spare_copy(j).wait()
            return c

        lax.fori_loop(0, spare_count, start, 0)
        lax.fori_loop(0, spare_count, wait, 0)


def _experts(xs, first_block, block_count, wg, bg, wu, bu, wd, bd):
    NE, D, DE = wg.shape
    cap = xs.shape[0] // PACK_ROWS

    def wsel(e, first, count):
        return (e, 0, 0)

    grid_spec = pltpu.PrefetchScalarGridSpec(
        num_scalar_prefetch=2,
        grid=(NE,),
        in_specs=[pl.BlockSpec(memory_space=pl.ANY),
                  pl.BlockSpec((1, D, DE), wsel), pl.BlockSpec((1, 1, DE), wsel),
                  pl.BlockSpec((1, D, DE), wsel), pl.BlockSpec((1, 1, DE), wsel),
                  pl.BlockSpec((1, DE, D), wsel), pl.BlockSpec((1, 1, D), wsel)],
        out_specs=pl.BlockSpec(memory_space=pl.ANY),
        scratch_shapes=[pltpu.VMEM((D, DE), BF16), pltpu.VMEM((D, DE), BF16),
                        pltpu.VMEM((DE, D), BF16),
                        pltpu.VMEM((2, EXPERT_BLOCK * PACK_ROWS, LANES), jnp.uint32),
                        pltpu.VMEM((2, EXPERT_BLOCK * SUBLANES, LANES), F32),
                        pltpu.SemaphoreType.DMA((2,)), pltpu.SemaphoreType.DMA((2,))],
    )
    return pl.pallas_call(
        _expert_kernel,
        grid_spec=grid_spec,
        out_shape=jax.ShapeDtypeStruct((cap * SUBLANES, LANES), F32),
        compiler_params=_params(("arbitrary",)),
        name="experts",
    )(first_block, block_count, xs, wg, bg.reshape(NE, 1, DE), wu, bu.reshape(NE, 1, DE), wd,
      bd.reshape(NE, 1, D))


def _combine_kernel(c0_ref, c1_ref, c2_ref, c3_ref, n0_ref, n1_ref, n2_ref, n3_ref,
                    x1_ref, p_ref, mod_ref, g_ref, ys_ref, o_ref, buf, sem):
    TT = x1_ref.shape[0]
    i = pl.program_id(0)

    def gather(pos_refs, s):
        def issue(g, c):
            for tt in range(ROW_DMA_UNROLL):
                t = g * ROW_DMA_UNROLL + tt
                for r in range(TOP_K):
                    pltpu.make_async_copy(
                        ys_ref.at[pl.ds(pl.multiple_of(pos_refs[r][t], SUBLANES), SUBLANES), :],
                        buf.at[s, r, pl.ds(pl.multiple_of(t * SUBLANES, SUBLANES), SUBLANES), :],
                        sem.at[s]).start(priority=(tt * TOP_K + r) % 2)
            return c
        lax.fori_loop(0, TT // ROW_DMA_UNROLL, issue, 0)

    @pl.when(i == 0)
    def _():
        gather((c0_ref, c1_ref, c2_ref, c3_ref), 0)

    def reduce_tile(s):
        @pl.when(i + 1 < pl.num_programs(0))
        def _():
            gather((n0_ref, n1_ref, n2_ref, n3_ref), 1 - s)

        for r in range(TOP_K):
            pltpu.make_async_copy(ys_ref.at[pl.ds(0, TT * SUBLANES), :], buf.at[s, r],
                                  sem.at[s]).wait()
        p = p_ref[...]
        moe = p[:, 0:1] * _load_row_tiles(buf, TT, (s, 0))
        for r in range(1, TOP_K):
            moe = moe + p[:, r:r + 1] * _load_row_tiles(buf, TT, (s, r))
        x2 = x1_ref[...] + mod_ref[0, 5:6, :] * moe
        o_ref[...] = _rms(x2, g_ref[...])

    for s in range(2):
        pl.when(i % 2 == s)(functools.partial(reduce_tile, s))


def _combine(x1f, probs, pos_flat, mod3, g, ys, tiles_per_batch):
    N, D = x1f.shape
    TT = TOKEN_TILE
    n_tiles = N // TT

    def next_spec(r):
        def index(i):
            j = jnp.minimum(i + 1, n_tiles - 1)
            return ((j // tiles_per_batch * TOP_K + r) * tiles_per_batch + j % tiles_per_batch,)
        return pl.BlockSpec((TT,), index, memory_space=pltpu.SMEM)

    return pl.pallas_call(
        _combine_kernel,
        grid=(n_tiles,),
        in_specs=_slot_specs(tiles_per_batch) + [next_spec(r) for r in range(TOP_K)]
        + [pl.BlockSpec((TT, D), lambda i: (i, 0)),
           pl.BlockSpec((TT, TOP_K), lambda i: (i, 0)),
           pl.BlockSpec((1, N_MOD, D), lambda i: (i // tiles_per_batch, 0, 0)),
           pl.BlockSpec((1, D), lambda i: (0, 0)),
           pl.BlockSpec(memory_space=pl.ANY)],
        out_specs=pl.BlockSpec((TT, D), lambda i: (i, 0)),
        out_shape=jax.ShapeDtypeStruct((N, D), F32),
        scratch_shapes=[pltpu.VMEM((2, TOP_K, TT * SUBLANES, LANES), F32),
                        pltpu.SemaphoreType.DMA((2,))],
        compiler_params=_params(("arbitrary",)),
        name="combine",
    )(*([pos_flat] * (2 * TOP_K)), x1f, probs, mod3, g.reshape(1, D), ys)


def _routing(idx, rank, counts):
    B, K, S = idx.shape
    n_exp = counts.shape[0]
    TM = EXPERT_BLOCK
    n_assign = B * S * K
    padded = ((counts + TM - 1) // TM) * TM
    pad_ends = jnp.cumsum(padded)
    pad_starts = pad_ends - padded
    experts = jnp.arange(n_exp, dtype=jnp.int32).reshape(n_exp, 1, 1, 1)
    base = jnp.sum(jnp.where(idx[None] == experts, pad_starts.reshape(n_exp, 1, 1, 1), 0),
                   axis=0)
    pos = (base + rank).astype(jnp.int32).reshape(n_assign)
    n_blocks = n_assign // TM + n_exp
    first_block = jnp.concatenate([pad_starts, pad_ends[-1:]]) // TM
    block_count = jnp.concatenate([padded // TM, n_blocks - pad_ends[-1:] // TM])
    tail_start = jnp.where(padded > 0, pad_ends - TM, -1)
    spare = pad_ends[-1] + jnp.arange(n_exp, dtype=jnp.int32) * TM
    fill_start = jnp.concatenate(
        [tail_start, jnp.where(spare < n_blocks * TM, spare, -1)]).astype(jnp.int32)
    return (pos, first_block.astype(jnp.int32), block_count.astype(jnp.int32),
            fill_start, n_blocks * TM)


def kernel(x, c, ada_w, ada_b, mix_norm_g, w_in, conv_w, conv_b, rg_w, rg_b, ig_w, ig_b,
           lru_lambda, attn_out_g, lru_out_g, w_out, ffn_norm_g, router_w, router_b,
           exp_w_gate, exp_b_gate, exp_w_up, exp_b_up, exp_w_down, exp_b_down, final_norm_g):
    B, S, D = x.shape
    depth = ada_w.shape[0]
    assert S % 512 == 0 and S % TOKEN_TILE == 0 and D == SUBLANES * LANES
    for l in range(depth):
        mod3 = _ada(c, ada_w[l], ada_b[l]).reshape(B, N_MOD, D)
        q, k, v, yl = _inproj(x, mod3, mix_norm_g[l], w_in[l].astype(BF16), conv_w[l],
                              conv_b[l], rg_w[l], rg_b[l], ig_w[l], ig_b[l], lru_lambda[l],
                              lru_out_g[l])
        ya = _attention(q, k, v)
        x1, h2, idx, prob, rank, cnt = _outproj(
            x, ya, yl, mod3, attn_out_g[l], w_out[l].astype(BF16), ffn_norm_g[l],
            router_w[l].T, router_b[l])
        pos, first_block, block_count, fill_start, cap = _routing(idx, rank, cnt[:, 0])
        xs = _dispatch(h2, pos * PACK_ROWS, fill_start * PACK_ROWS, cap, S // TOKEN_TILE)
        ys = _experts(xs, first_block, block_count, exp_w_gate[l], exp_b_gate[l],
                      exp_w_up[l], exp_b_up[l], exp_w_down[l], exp_b_down[l])
        probs = jnp.transpose(prob, (0, 2, 1)).reshape(B * S, TOP_K)
        assert depth == 1
        x = _combine(x1.reshape(B * S, D), probs, pos * SUBLANES, mod3, final_norm_g, ys,
                     S // TOKEN_TILE).reshape(B, S, D)
    return x
```

```python
import functools

import jax
import jax.numpy as jnp
from jax import lax
from jax.experimental import pallas as pl
from jax.experimental.pallas import tpu as pltpu

F32 = jnp.float32
BF16 = jnp.bfloat16
HIGHEST = lax.Precision.HIGHEST

EPS = 1e-6
N_MOD = 6
SB_HEADS = 8
HEAD_DIM = 64
SB_WIDTH = SB_HEADS * HEAD_DIM
LRU_BLOCKS = 8
CONV_WIDTH = 4
LRU_C = 8.0
TOP_K = 4
SWIGLU_LIMIT = 7.0
SWIGLU_ALPHA = 1.702
LOG2E = 1.4426950408889634

LANES = 128
SUBLANES = 8
VMEM_LIMIT = 56 * 1024 * 1024

Q_BLOCK = 128
K_BLOCK = 128
ATTN_WINDOW_BLOCKS = 5
ATTN_LOOP_BLOCKS = 2
ATTN_UNDERFLOW_LOG = -110.0
EXPERT_BLOCK = 512
DISPATCH_TILE = 1024
COMBINE_TILE = 256
ROW_DMA_UNROLL = 8
LRU_SCAN_ROWS = 256


def _params(sem):
    return pltpu.CompilerParams(dimension_semantics=sem, vmem_limit_bytes=VMEM_LIMIT)


def _ada_kernel(c_ref, w_ref, b_ref, o_ref):
    c = c_ref[...]
    ca = c * jax.nn.sigmoid(c)
    o_ref[...] = jnp.dot(ca, w_ref[...], precision=HIGHEST,
                         preferred_element_type=F32) + b_ref[...]


def _ada(c, ada_w, ada_b):
    B, D = c.shape
    E = ada_w.shape[1]
    tn = 1024
    return pl.pallas_call(
        _ada_kernel,
        grid=(E // tn,),
        in_specs=[pl.BlockSpec((B, D), lambda j: (0, 0)),
                  pl.BlockSpec((D, tn), lambda j: (0, j)),
                  pl.BlockSpec((1, tn), lambda j: (0, j))],
        out_specs=pl.BlockSpec((B, tn), lambda j: (0, j)),
        out_shape=jax.ShapeDtypeStruct((B, E), F32),
        compiler_params=_params(("arbitrary",)),
        name="ada",
    )(c, ada_w, ada_b.reshape(1, E))


def _rms(x, g):
    ms = jnp.mean(x * x, axis=-1, keepdims=True)
    return x * lax.rsqrt(ms + EPS) * g


def _softplus(x):
    return jnp.maximum(x, 0.0) + jnp.log1p(jnp.exp(-jnp.abs(x)))


def _gelu_tanh(x):
    return 0.5 * x * (1.0 + jnp.tanh(0.7978845608028654 * (x + 0.044715 * x * x * x)))


LRU_PAD = 8


def _lru_tile(x, gate, cw_ref, cb_ref, wr_ref, br_ref, wi_ref, bi_ref, lam_ref, g_ref,
              xext, hc):
    T = x.shape[0]
    xext[LRU_PAD:LRU_PAD + T, :] = x
    xc = cb_ref[...] + cw_ref[CONV_WIDTH - 1:CONV_WIDTH, :] * x
    for j in range(CONV_WIDTH - 1):
        back = CONV_WIDTH - 1 - j
        xc = xc + cw_ref[j:j + 1, :] * xext[LRU_PAD - back:LRU_PAD - back + T, :]
    xext[0:LRU_PAD, :] = xext[T:T + LRU_PAD, :]

    xb = xc.astype(BF16)
    r = jax.nn.sigmoid(jnp.dot(xb, wr_ref[...], preferred_element_type=F32) + br_ref[...])
    ig = jax.nn.sigmoid(jnp.dot(xb, wi_ref[...], preferred_element_type=F32) + bi_ref[...])
    log_a = (-LRU_C) * r * _softplus(-lam_ref[...])
    a = jnp.exp(log_a)
    b = jnp.sqrt(-jnp.tanh(log_a) * (a * a + 1.0)) * (ig * xc)

    G = T // SUBLANES
    a = a.reshape(G, SUBLANES, -1)
    b = b.reshape(G, SUBLANES, -1)
    rows = lax.broadcasted_iota(jnp.int32, a.shape, 1)
    d = 1
    while d < SUBLANES:
        keep = rows >= d
        a_prev = jnp.where(keep, pltpu.roll(a, d, 1), 1.0)
        b_prev = jnp.where(keep, pltpu.roll(b, d, 1), 0.0)
        b = a * b_prev + b
        a = a * a_prev
        d *= 2
    prev = hc[...]
    groups = []
    for g in range(G):
        hg = a[g] * prev + b[g]
        prev = hg[SUBLANES - 1:SUBLANES, :]
        groups.append(hg)
    hc[...] = prev
    h = jnp.concatenate(groups, axis=0)
    return _rms(h * _gelu_tanh(gate), g_ref[...])


def _inproj_kernel(x_ref, mod_ref, g_ref, w_ref, cw_ref, cb_ref, wr_ref, br_ref, wi_ref,
                   bi_ref, lam_ref, gl_ref, q_ref, k_ref, v_ref, yl_ref, xext, hc):
    @pl.when(pl.program_id(1) == 0)
    def _():
        xext[0:LRU_PAD, :] = jnp.zeros((LRU_PAD, xext.shape[1]), F32)
        hc[...] = jnp.zeros_like(hc)

    x = x_ref[0]
    h = _rms(x, g_ref[...]) * (1.0 + mod_ref[0, 1:2, :]) + mod_ref[0, 0:1, :]
    hb = h.astype(BF16)
    W = SB_WIDTH

    def proj(c):
        return jnp.dot(hb, w_ref[:, c * W:(c + 1) * W], preferred_element_type=F32)

    xr, gr = proj(3), proj(4)
    T = LRU_SCAN_ROWS
    for t in range(x.shape[0] // T):
        rows = slice(t * T, (t + 1) * T)
        yl_ref[0, rows, :] = _lru_tile(xr[rows], gr[rows], cw_ref, cb_ref, wr_ref, br_ref,
                                       wi_ref, bi_ref, lam_ref, gl_ref, xext, hc)
    q_ref[0] = (proj(0) * (HEAD_DIM ** -0.5)).astype(BF16)
    k_ref[0] = proj(1).astype(BF16)
    v_ref[0] = proj(2).astype(BF16)


def _block_diag(w):
    H, I, J = w.shape
    eye = jnp.eye(H, dtype=w.dtype)
    return (w[:, :, None, :] * eye[:, None, :, None]).reshape(H * I, H * J)


def _inproj(x, mod3, g, w_in_b, conv_w, conv_b, rg_w, rg_b, ig_w, ig_b, lam, g_lru, ts=512):
    B, S, D = x.shape
    E = w_in_b.shape[1]
    W = SB_WIDTH
    row = pl.BlockSpec((1, ts, W), lambda b, s: (b, s, 0))
    vec = pl.BlockSpec((1, W), lambda b, s: (0, 0))
    mat = pl.BlockSpec((W, W), lambda b, s: (0, 0))
    return pl.pallas_call(
        _inproj_kernel,
        grid=(B, S // ts),
        in_specs=[pl.BlockSpec((1, ts, D), lambda b, s: (b, s, 0)),
                  pl.BlockSpec((1, N_MOD, D), lambda b, s: (b, 0, 0)),
                  pl.BlockSpec((1, D), lambda b, s: (0, 0)),
                  pl.BlockSpec((D, E), lambda b, s: (0, 0)),
                  pl.BlockSpec((CONV_WIDTH, W), lambda b, s: (0, 0)), vec,
                  mat, vec, mat, vec, vec, vec],
        out_specs=[row, row, row, row],
        out_shape=[jax.ShapeDtypeStruct((B, S, W), BF16)] * 3
        + [jax.ShapeDtypeStruct((B, S, W), F32)],
        scratch_shapes=[pltpu.VMEM((LRU_SCAN_ROWS + LRU_PAD, W), F32), pltpu.VMEM((1, W), F32)],
        compiler_params=_params(("arbitrary", "arbitrary")),
        name="inproj",
    )(x, mod3, g.reshape(1, D), w_in_b, conv_w, conv_b.reshape(1, W),
      _block_diag(rg_w).astype(BF16), rg_b.reshape(1, W), _block_diag(ig_w).astype(BF16),
      ig_b.reshape(1, W), lam.reshape(1, W), g_lru.reshape(1, W))


def _attn_kernel(q_ref, k_ref, v_ref, o_ref, acc_ref, carry_ref):
    i = pl.program_id(1)
    QB, KB = Q_BLOCK, K_BLOCK
    n_pairs = q_ref.shape[2] // LANES
    lane = lax.broadcasted_iota(jnp.int32, (QB, LANES), 1)

    def stacked_q(p):
        q = q_ref[0, :, p * LANES:(p + 1) * LANES]
        zero = jnp.zeros_like(q)
        return jnp.concatenate([jnp.where(lane < HEAD_DIM, q, zero),
                                jnp.where(lane >= HEAD_DIM, q, zero)], axis=0)

    qs = [stacked_q(p) for p in range(n_pairs)]

    uj = lax.broadcasted_iota(jnp.int32, (2 * KB, KB + LANES), 0) & (KB - 1)
    us = lax.broadcasted_iota(jnp.int32, (2 * KB, KB + LANES), 1)
    u2 = jnp.where((us >= KB) | (uj > us), -1.0, 0.0).astype(BF16)

    row = lax.broadcasted_iota(jnp.int32, (2 * QB, KB), 0) & (QB - 1)
    col = lax.broadcasted_iota(jnp.int32, (2 * QB, KB), 1)
    causal = col < row

    def tiles(p, j_hi, n, diagonal, acc, carry):
        start = pl.multiple_of((j_hi - (n - 1)) * KB, KB)
        kw = k_ref[0, pl.ds(start, n * KB), p * LANES:(p + 1) * LANES]
        vw = v_ref[0, pl.ds(start, n * KB), p * LANES:(p + 1) * LANES]
        z = lax.dot_general(qs[p], kw, (((1,), (1,)), ((), ())), preferred_element_type=F32)
        softplus = jnp.maximum(z, 0.0) + jnp.log(1.0 + jnp.exp2(jnp.abs(z) * (-LOG2E)))
        log_beta = z - softplus
        ws = [None] * n
        for c in reversed(range(n)):
            sp = softplus[:, c * KB:(c + 1) * KB]
            masked = diagonal and c == n - 1
            if masked:
                sp = jnp.where(causal, sp, 0.0)
            hi_f = lax.bitcast_convert_type(
                lax.bitcast_convert_type(sp, jnp.uint32) & jnp.uint32(0xFFFF0000), F32)
            hi_lo = jnp.concatenate([hi_f.astype(BF16), (sp - hi_f).astype(BF16)], axis=1)
            sums = jnp.dot(hi_lo, u2, preferred_element_type=F32)
            w = jnp.exp2((log_beta[:, c * KB:(c + 1) * KB] + sums[:, :KB] + carry) * LOG2E)
            if masked:
                w = jnp.where(causal, w, 0.0)
            ws[c] = w.astype(BF16)
            carry = carry + sums[:, KB:]
        w_all = ws[0] if n == 1 else jnp.concatenate(ws, axis=1)
        acc = acc + jnp.dot(w_all, vw, preferred_element_type=F32)
        return acc, carry

    def fold(j_hi, n, diagonal):
        cmax = None
        for p in range(n_pairs):
            if diagonal:
                acc = carry = jnp.zeros((2 * QB, LANES), F32)
            else:
                acc, carry = acc_ref[p], carry_ref[p]
            acc, carry = tiles(p, j_hi, n, diagonal, acc, carry)
            acc_ref[p] = acc
            carry_ref[p] = carry
            cmax = carry if cmax is None else jnp.maximum(cmax, carry)
        return jnp.max(cmax)

    n_first, n_loop = ATTN_WINDOW_BLOCKS, ATTN_LOOP_BLOCKS
    n0 = jnp.minimum(i + 1, n_first)
    cmax0 = lax.switch(n0 - 1, [functools.partial(fold, i, n, True)
                                for n in range(1, n_first + 1)])
    j0 = i - n0

    def loop(n, j, cmax):
        def more(st):
            return (st[0] >= n - 1) & (st[1] >= ATTN_UNDERFLOW_LOG)

        def step(st):
            return st[0] - n, fold(st[0], n, False)

        return lax.while_loop(more, step, (j, cmax))

    j, cmax = j0, cmax0
    while n_loop >= 1:
        j, cmax = loop(n_loop, j, cmax)
        n_loop //= 2
    for p in range(n_pairs):
        acc = acc_ref[p]
        o_ref[0, :, p * LANES:(p + 1) * LANES] = jnp.where(lane < HEAD_DIM, acc[:QB], acc[QB:])


def _attention(q, k, v):
    B, S, W = q.shape
    return pl.pallas_call(
        _attn_kernel,
        grid=(B, S // Q_BLOCK),
        in_specs=[pl.BlockSpec((1, Q_BLOCK, W), lambda b, i: (b, i, 0)),
                  pl.BlockSpec((1, S, W), lambda b, i: (b, 0, 0)),
                  pl.BlockSpec((1, S, W), lambda b, i: (b, 0, 0))],
        out_specs=pl.BlockSpec((1, Q_BLOCK, W), lambda b, i: (b, i, 0)),
        out_shape=jax.ShapeDtypeStruct((B, S, W), F32),
        scratch_shapes=[pltpu.VMEM((W // LANES, 2 * Q_BLOCK, LANES), F32),
                        pltpu.VMEM((W // LANES, 2 * Q_BLOCK, LANES), F32)],
        compiler_params=_params(("arbitrary", "arbitrary")),
        name="attn",
    )(q, k, v)


def _load_row_tiles(ref, n_rows, lead=()):
    chunks = [ref[lead + (pl.ds(c, n_rows, stride=SUBLANES), slice(None))]
              for c in range(SUBLANES)]
    return jnp.concatenate(chunks, axis=1)


def _store_row_tiles(ref, value):
    n_rows = value.shape[0]
    for c in range(SUBLANES):
        ref[pl.ds(c, n_rows, stride=SUBLANES), :] = value[:, c * LANES:(c + 1) * LANES]


PACK_ROWS = SUBLANES // 2


def _store_packed_rows(ref, value):
    n_rows, d = value.shape
    bits = lax.bitcast_convert_type(value.astype(BF16).astype(F32), jnp.uint32)
    words = (bits[:, :d // 2] >> 16) | bits[:, d // 2:]
    for c in range(PACK_ROWS):
        ref[pl.ds(c, n_rows, stride=PACK_ROWS), :] = words[:, c * LANES:(c + 1) * LANES]


def _load_packed_rows(ref, n_rows, lead=()):
    lo, hi = [], []
    for c in range(PACK_ROWS):
        w = ref[lead + (pl.ds(c, n_rows, stride=PACK_ROWS), slice(None))]
        lo.append(lax.bitcast_convert_type(w << 16, F32).astype(BF16))
        hi.append(lax.bitcast_convert_type(w & jnp.uint32(0xFFFF0000), F32).astype(BF16))
    return jnp.concatenate(lo + hi, axis=1)


def _outproj_kernel(x_ref, ya_ref, yl_ref, mod_ref, ga_ref, wo_ref, gf_ref, rw_ref, rb_ref,
                    x1_ref, h2_ref, idx_ref, prob_ref, rank_ref, cnt_ref, tri, run):
    W = SB_WIDTH
    ya = _rms(ya_ref[0], ga_ref[...]).astype(BF16)
    yl = yl_ref[0].astype(BF16)
    mix = (jnp.dot(ya, wo_ref[0:W, :], preferred_element_type=F32)
           + jnp.dot(yl, wo_ref[W:2 * W, :], preferred_element_type=F32))
    x1 = x_ref[0] + mod_ref[0, 2:3, :] * mix
    x1_ref[0] = x1
    h2 = _rms(x1, gf_ref[...]) * (1.0 + mod_ref[0, 4:5, :]) + mod_ref[0, 3:4, :]
    _store_packed_rows(h2_ref, h2)

    logits = lax.dot_general(rw_ref[...], h2, (((1,), (1,)), ((), ())), precision=HIGHEST,
                             preferred_element_type=F32) + rb_ref[...]
    n_exp = logits.shape[0]
    eid = lax.broadcasted_iota(jnp.int32, logits.shape, 0)
    vals, idxs = [], []
    for _ in range(TOP_K):
        m = jnp.max(logits, axis=0, keepdims=True)
        sel = jnp.min(jnp.where(logits == m, eid, n_exp), axis=0, keepdims=True)
        vals.append(m)
        idxs.append(sel)
        logits = jnp.where(eid == sel, -jnp.inf, logits)
    es = [jnp.exp(vv - vals[0]) for vv in vals]
    inv = 1.0 / (es[0] + es[1] + es[2] + es[3])

    first = (pl.program_id(0) == 0) & (pl.program_id(1) == 0)

    @pl.when(first)
    def _():
        ti = lax.broadcasted_iota(jnp.int32, tri.shape, 0)
        tj = lax.broadcasted_iota(jnp.int32, tri.shape, 1)
        tri[...] = jnp.where(ti < tj, 1.0, 0.0).astype(BF16)
        run[...] = jnp.zeros_like(run)

    hits = [eid == idxs[r] for r in range(TOP_K)]
    chosen = hits[0] | hits[1] | hits[2] | hits[3]
    cnt = jnp.where(chosen, 1.0, 0.0)
    before = jnp.dot(cnt.astype(BF16), tri[...], preferred_element_type=F32) + run[...]
    for r in range(TOP_K):
        idx_ref[0, r:r + 1, :] = idxs[r]
        prob_ref[0, r:r + 1, :] = es[r] * inv
        rank_ref[0, r:r + 1, :] = jnp.sum(jnp.where(hits[r], before, 0.0), axis=0,
                                          keepdims=True).astype(jnp.int32)
    run[...] = run[...] + jnp.sum(cnt, axis=1, keepdims=True)
    cnt_ref[...] = jnp.broadcast_to(run[...], cnt_ref.shape).astype(jnp.int32)


def _outproj(x, ya, yl, mod3, ga, w_out_b, gf, router_wt, router_b, ts=512):
    B, S, D = x.shape
    W = ya.shape[2]
    NE = router_wt.shape[0]
    rowd = pl.BlockSpec((1, ts, D), lambda b, s: (b, s, 0))
    roww = pl.BlockSpec((1, ts, W), lambda b, s: (b, s, 0))
    sel = pl.BlockSpec((1, TOP_K, ts), lambda b, s: (b, 0, s))
    return pl.pallas_call(
        _outproj_kernel,
        grid=(B, S // ts),
        in_specs=[rowd, roww, roww,
                  pl.BlockSpec((1, N_MOD, D), lambda b, s: (b, 0, 0)),
                  pl.BlockSpec((1, W), lambda b, s: (0, 0)),
                  pl.BlockSpec((2 * W, D), lambda b, s: (0, 0)),
                  pl.BlockSpec((1, D), lambda b, s: (0, 0)),
                  pl.BlockSpec((NE, D), lambda b, s: (0, 0)),
                  pl.BlockSpec((NE, 1), lambda b, s: (0, 0))],
        out_specs=[rowd,
                   pl.BlockSpec((ts * PACK_ROWS, LANES), lambda b, s: (b * (S // ts) + s, 0)),
                   sel, sel, sel, pl.BlockSpec((NE, LANES), lambda b, s: (0, 0))],
        out_shape=[jax.ShapeDtypeStruct((B, S, D), F32),
                   jax.ShapeDtypeStruct((B * S * PACK_ROWS, LANES), jnp.uint32),
                   jax.ShapeDtypeStruct((B, TOP_K, S), jnp.int32),
                   jax.ShapeDtypeStruct((B, TOP_K, S), F32),
                   jax.ShapeDtypeStruct((B, TOP_K, S), jnp.int32),
                   jax.ShapeDtypeStruct((NE, LANES), jnp.int32)],
        scratch_shapes=[pltpu.VMEM((ts, ts), BF16), pltpu.VMEM((NE, 1), F32)],
        compiler_params=_params(("arbitrary", "arbitrary")),
        name="outproj",
    )(x, ya, yl, mod3, ga.reshape(1, W), w_out_b, gf.reshape(1, D), router_wt,
      router_b.reshape(NE, 1))


def _dispatch_kernel(p0_ref, p1_ref, p2_ref, p3_ref, fill_ref, h_ref, xs_ref, zbuf, sem, zsem):
    TT = h_ref.shape[0] // PACK_ROWS
    n_fill = fill_ref.shape[0]

    @pl.when(pl.program_id(0) == 0)
    def _():
        zbuf[...] = jnp.zeros_like(zbuf)

        def fill_copy(e):
            start = pl.multiple_of(jnp.maximum(fill_ref[e], 0), PACK_ROWS)
            return pltpu.make_async_copy(zbuf, xs_ref.at[pl.ds(start, zbuf.shape[0]), :], zsem)

        def start(e, c):
            @pl.when(fill_ref[e] >= 0)
            def _():
                fill_copy(e).start()
            return c

        def wait(e, c):
            @pl.when(fill_ref[e] >= 0)
            def _():
                fill_copy(e).wait()
            return c

        lax.fori_loop(0, n_fill, start, 0)
        lax.fori_loop(0, n_fill, wait, 0)

    pos_refs = (p0_ref, p1_ref, p2_ref, p3_ref)

    def row_copy(t, r):
        src = pl.multiple_of(t * PACK_ROWS, PACK_ROWS)
        dst = pl.multiple_of(pos_refs[r][t], PACK_ROWS)
        return pltpu.make_async_copy(h_ref.at[pl.ds(src, PACK_ROWS), :],
                                     xs_ref.at[pl.ds(dst, PACK_ROWS), :], sem)

    def issue(g, c):
        for tt in range(ROW_DMA_UNROLL):
            for r in range(TOP_K):
                row_copy(g * ROW_DMA_UNROLL + tt, r).start(priority=(tt * TOP_K + r) % 2)
        return c

    lax.fori_loop(0, TT // ROW_DMA_UNROLL, issue, 0)
    for r in range(TOP_K):
        pltpu.make_async_copy(h_ref, xs_ref.at[pl.ds(0, TT * PACK_ROWS), :], sem).wait()


def _slot_specs(tile, tiles_per_batch):
    def spec(r):
        return pl.BlockSpec(
            (tile,),
            lambda i: ((i // tiles_per_batch * TOP_K + r) * tiles_per_batch
                       + i % tiles_per_batch,),
            memory_space=pltpu.SMEM)
    return [spec(r) for r in range(TOP_K)]


def _dispatch(h2t, pos_flat, fill_start, cap, tiles_per_batch):
    TT = DISPATCH_TILE
    n_tiles = h2t.shape[0] // (TT * PACK_ROWS)
    return pl.pallas_call(
        _dispatch_kernel,
        grid=(n_tiles,),
        in_specs=_slot_specs(TT, tiles_per_batch)
        + [pl.BlockSpec(memory_space=pltpu.SMEM),
           pl.BlockSpec((TT * PACK_ROWS, LANES), lambda i: (i, 0))],
        out_specs=pl.BlockSpec(memory_space=pl.ANY),
        out_shape=jax.ShapeDtypeStruct((cap * PACK_ROWS, LANES), jnp.uint32),
        scratch_shapes=[pltpu.VMEM((EXPERT_BLOCK * PACK_ROWS, LANES), jnp.uint32),
                        pltpu.SemaphoreType.DMA, pltpu.SemaphoreType.DMA],
        compiler_params=_params(("arbitrary",)),
        name="dispatch",
    )(pos_flat, pos_flat, pos_flat, pos_flat, fill_start, h2t)


def _expert_kernel(first_ref, count_ref, xs_ref, wg_ref, bg_ref, wu_ref, bu_ref, wd_ref,
                   bd_ref, ys_ref, wgb, wub, wdb, xbuf, ybuf, xsem, ysem):
    e = pl.program_id(0)
    n_exp = pl.num_programs(0)
    rows = ybuf.shape[1]
    first, count = first_ref[e], count_ref[e]

    wgb[...] = wg_ref[0].astype(BF16)
    wub[...] = wu_ref[0].astype(BF16)
    wdb[...] = wd_ref[0].astype(BF16)

    def block(ref, buf, j):
        n = buf.shape[1]
        return ref.at[pl.ds(pl.multiple_of((first + j) * n, n), n), :]

    def x_copy(j, slot):
        return pltpu.make_async_copy(block(xs_ref, xbuf, j), xbuf.at[slot], xsem.at[slot])

    def y_copy(j, slot):
        return pltpu.make_async_copy(ybuf.at[slot], block(ys_ref, ybuf, j), ysem.at[slot])

    @pl.when(count > 0)
    def _():
        x_copy(0, 0).start()

    def step(j, c):
        slot = j % 2
        x_copy(j, slot).wait()

        @pl.when(j + 1 < count)
        def _():
            x_copy(j + 1, 1 - slot).start()

        @pl.when(j >= 2)
        def _():
            y_copy(j - 2, slot).wait()

        xb = _load_packed_rows(xbuf, EXPERT_BLOCK, (slot,))
        g = jnp.minimum(jnp.dot(xb, wgb[...], preferred_element_type=F32) + bg_ref[0],
                        SWIGLU_LIMIT)
        u = jnp.clip(jnp.dot(xb, wub[...], preferred_element_type=F32) + bu_ref[0],
                     -SWIGLU_LIMIT, SWIGLU_LIMIT)
        act = (u + 1.0) * (g * jax.nn.sigmoid(SWIGLU_ALPHA * g))
        _store_row_tiles(ybuf.at[slot], jnp.dot(act.astype(BF16), wdb[...],
                                                preferred_element_type=F32) + bd_ref[0])
        y_copy(j, slot).start()
        return c

    lax.fori_loop(0, count, step, 0)

    @pl.when(count >= 2)
    def _():
        y_copy(count - 2, count % 2).wait()

    @pl.when(count >= 1)
    def _():
        y_copy(count - 1, (count - 1) % 2).wait()

    @pl.when(e == n_exp - 1)
    def _():
        spare_first, spare_count = first_ref[n_exp], count_ref[n_exp]
        ybuf[0] = jnp.zeros(ybuf.shape[1:], F32)

        def spare_copy(j):
            dst = pl.ds(pl.multiple_of((spare_first + j) * rows, rows), rows)
            return pltpu.make_async_copy(ybuf.at[0], ys_ref.at[dst, :], ysem.at[0])

        def start(j, c):
            spare_copy(j).start()
            return c

        def wait(j, c):
            spare_copy(j).wait()
            return c

        lax.fori_loop(0, spare_count, start, 0)
        lax.fori_loop(0, spare_count, wait, 0)


def _experts(xs, first_block, block_count, wg, bg, wu, bu, wd, bd):
    NE, D, DE = wg.shape
    cap = xs.shape[0] // PACK_ROWS

    def wsel(e, first, count):
        return (e, 0, 0)

    grid_spec = pltpu.PrefetchScalarGridSpec(
        num_scalar_prefetch=2,
        grid=(NE,),
        in_specs=[pl.BlockSpec(memory_space=pl.ANY),
                  pl.BlockSpec((1, D, DE), wsel), pl.BlockSpec((1, 1, DE), wsel),
                  pl.BlockSpec((1, D, DE), wsel), pl.BlockSpec((1, 1, DE), wsel),
                  pl.BlockSpec((1, DE, D), wsel), pl.BlockSpec((1, 1, D), wsel)],
        out_specs=pl.BlockSpec(memory_space=pl.ANY),
        scratch_shapes=[pltpu.VMEM((D, DE), BF16), pltpu.VMEM((D, DE), BF16),
                        pltpu.VMEM((DE, D), BF16),
                        pltpu.VMEM((2, EXPERT_BLOCK * PACK_ROWS, LANES), jnp.uint32),
                        pltpu.VMEM((2, EXPERT_BLOCK * SUBLANES, LANES), F32),
                        pltpu.SemaphoreType.DMA((2,)), pltpu.SemaphoreType.DMA((2,))],
    )
    return pl.pallas_call(
        _expert_kernel,
        grid_spec=grid_spec,
        out_shape=jax.ShapeDtypeStruct((cap * SUBLANES, LANES), F32),
        compiler_params=_params(("arbitrary",)),
        name="experts",
    )(first_block, block_count, xs, wg, bg.reshape(NE, 1, DE), wu, bu.reshape(NE, 1, DE), wd,
      bd.reshape(NE, 1, D))


def _combine_kernel(c0_ref, c1_ref, c2_ref, c3_ref, n0_ref, n1_ref, n2_ref, n3_ref,
                    x1_ref, p_ref, mod_ref, g_ref, ys_ref, o_ref, buf, sem):
    TT = x1_ref.shape[0]
    i = pl.program_id(0)

    def gather(pos_refs, s):
        def issue(g, c):
            for tt in range(ROW_DMA_UNROLL):
                t = g * ROW_DMA_UNROLL + tt
                for r in range(TOP_K):
                    pltpu.make_async_copy(
                        ys_ref.at[pl.ds(pl.multiple_of(pos_refs[r][t], SUBLANES), SUBLANES), :],
                        buf.at[s, r, pl.ds(pl.multiple_of(t * SUBLANES, SUBLANES), SUBLANES), :],
                        sem.at[s]).start(priority=(tt * TOP_K + r) % 2)
            return c
        lax.fori_loop(0, TT // ROW_DMA_UNROLL, issue, 0)

    @pl.when(i == 0)
    def _():
        gather((c0_ref, c1_ref, c2_ref, c3_ref), 0)

    def reduce_tile(s):
        @pl.when(i + 1 < pl.num_programs(0))
        def _():
            gather((n0_ref, n1_ref, n2_ref, n3_ref), 1 - s)

        for r in range(TOP_K):
            pltpu.make_async_copy(ys_ref.at[pl.ds(0, TT * SUBLANES), :], buf.at[s, r],
                                  sem.at[s]).wait()
        p = p_ref[...]
        moe = p[:, 0:1] * _load_row_tiles(buf, TT, (s, 0))
        for r in range(1, TOP_K):
            moe = moe + p[:, r:r + 1] * _load_row_tiles(buf, TT, (s, r))
        x2 = x1_ref[...] + mod_ref[0, 5:6, :] * moe
        o_ref[...] = _rms(x2, g_ref[...])

    for s in range(2):
        pl.when(i % 2 == s)(functools.partial(reduce_tile, s))


def _combine(x1f, probs, pos_flat, mod3, g, ys, tiles_per_batch):
    N, D = x1f.shape
    TT = COMBINE_TILE
    n_tiles = N // TT

    def next_spec(r):
        def index(i):
            j = jnp.minimum(i + 1, n_tiles - 1)
            return ((j // tiles_per_batch * TOP_K + r) * tiles_per_batch + j % tiles_per_batch,)
        return pl.BlockSpec((TT,), index, memory_space=pltpu.SMEM)

    return pl.pallas_call(
        _combine_kernel,
        grid=(n_tiles,),
        in_specs=_slot_specs(TT, tiles_per_batch) + [next_spec(r) for r in range(TOP_K)]
        + [pl.BlockSpec((TT, D), lambda i: (i, 0)),
           pl.BlockSpec((TT, TOP_K), lambda i: (i, 0)),
           pl.BlockSpec((1, N_MOD, D), lambda i: (i // tiles_per_batch, 0, 0)),
           pl.BlockSpec((1, D), lambda i: (0, 0)),
           pl.BlockSpec(memory_space=pl.ANY)],
        out_specs=pl.BlockSpec((TT, D), lambda i: (i, 0)),
        out_shape=jax.ShapeDtypeStruct((N, D), F32),
        scratch_shapes=[pltpu.VMEM((2, TOP_K, TT * SUBLANES, LANES), F32),
                        pltpu.SemaphoreType.DMA((2,))],
        compiler_params=_params(("arbitrary",)),
        name="combine",
    )(*([pos_flat] * (2 * TOP_K)), x1f, probs, mod3, g.reshape(1, D), ys)


def _routing(idx, rank, counts):
    B, K, S = idx.shape
    n_exp = counts.shape[0]
    TM = EXPERT_BLOCK
    n_assign = B * S * K
    padded = ((counts + TM - 1) // TM) * TM
    pad_ends = jnp.cumsum(padded)
    pad_starts = pad_ends - padded
    experts = jnp.arange(n_exp, dtype=jnp.int32).reshape(n_exp, 1, 1, 1)
    base = jnp.sum(jnp.where(idx[None] == experts, pad_starts.reshape(n_exp, 1, 1, 1), 0),
                   axis=0)
    pos = (base + rank).astype(jnp.int32).reshape(n_assign)
    n_blocks = n_assign // TM + n_exp
    first_block = jnp.concatenate([pad_starts, pad_ends[-1:]]) // TM
    block_count = jnp.concatenate([padded // TM, n_blocks - pad_ends[-1:] // TM])
    tail_start = jnp.where(padded > 0, pad_ends - TM, -1)
    spare = pad_ends[-1] + jnp.arange(n_exp, dtype=jnp.int32) * TM
    fill_start = jnp.concatenate(
        [tail_start, jnp.where(spare < n_blocks * TM, spare, -1)]).astype(jnp.int32)
    return (pos, first_block.astype(jnp.int32), block_count.astype(jnp.int32),
            fill_start, n_blocks * TM)


def kernel(x, c, ada_w, ada_b, mix_norm_g, w_in, conv_w, conv_b, rg_w, rg_b, ig_w, ig_b,
           lru_lambda, attn_out_g, lru_out_g, w_out, ffn_norm_g, router_w, router_b,
           exp_w_gate, exp_b_gate, exp_w_up, exp_b_up, exp_w_down, exp_b_down, final_norm_g):
    B, S, D = x.shape
    depth = ada_w.shape[0]
    assert S % DISPATCH_TILE == 0 and S % COMBINE_TILE == 0 and S % 512 == 0
    assert D == SUBLANES * LANES
    for l in range(depth):
        mod3 = _ada(c, ada_w[l], ada_b[l]).reshape(B, N_MOD, D)
        q, k, v, yl = _inproj(x, mod3, mix_norm_g[l], w_in[l].astype(BF16), conv_w[l],
                              conv_b[l], rg_w[l], rg_b[l], ig_w[l], ig_b[l], lru_lambda[l],
                              lru_out_g[l])
        ya = _attention(q, k, v)
        x1, h2, idx, prob, rank, cnt = _outproj(
            x, ya, yl, mod3, attn_out_g[l], w_out[l].astype(BF16), ffn_norm_g[l],
            router_w[l].T, router_b[l])
        pos, first_block, block_count, fill_start, cap = _routing(idx, rank, cnt[:, 0])
        xs = _dispatch(h2, pos * PACK_ROWS, fill_start * PACK_ROWS, cap, S // DISPATCH_TILE)
        ys = _experts(xs, first_block, block_count, exp_w_gate[l], exp_b_gate[l],
                      exp_w_up[l], exp_b_up[l], exp_w_down[l], exp_b_down[l])
        probs = jnp.transpose(prob, (0, 2, 1)).reshape(B * S, TOP_K)
        assert depth == 1
        x = _combine(x1.reshape(B * S, D), probs, pos * SUBLANES, mod3, final_norm_g, ys,
                     S // COMBINE_TILE).reshape(B, S, D)
    return x
```

```python
import functools

import jax
import jax.numpy as jnp
from jax import lax
from jax.experimental import pallas as pl
from jax.experimental.pallas import tpu as pltpu

F32 = jnp.float32
BF16 = jnp.bfloat16
HIGHEST = lax.Precision.HIGHEST

EPS = 1e-6
N_MOD = 6
SB_HEADS = 8
HEAD_DIM = 64
SB_WIDTH = SB_HEADS * HEAD_DIM
LRU_BLOCKS = 8
CONV_WIDTH = 4
LRU_C = 8.0
TOP_K = 4
SWIGLU_LIMIT = 7.0
SWIGLU_ALPHA = 1.702
LOG2E = 1.4426950408889634

LANES = 128
SUBLANES = 8
VMEM_LIMIT = 56 * 1024 * 1024

Q_BLOCK = 128
K_BLOCK = 128
ATTN_WINDOW_BLOCKS = 5
ATTN_LOOP_BLOCKS = 2
ATTN_UNDERFLOW_LOG = -110.0
EXPERT_BLOCK = 512
DISPATCH_TILE = 4096
COMBINE_TILE = 256
ROW_DMA_UNROLL = 8
LRU_SCAN_ROWS = 256


def _params(sem):
    return pltpu.CompilerParams(dimension_semantics=sem, vmem_limit_bytes=VMEM_LIMIT)


def _ada_kernel(c_ref, w_ref, b_ref, o_ref):
    c = c_ref[...]
    ca = c * jax.nn.sigmoid(c)
    o_ref[...] = jnp.dot(ca, w_ref[...], precision=HIGHEST,
                         preferred_element_type=F32) + b_ref[...]


def _ada(c, ada_w, ada_b):
    B, D = c.shape
    E = ada_w.shape[1]
    tn = 1024
    return pl.pallas_call(
        _ada_kernel,
        grid=(E // tn,),
        in_specs=[pl.BlockSpec((B, D), lambda j: (0, 0)),
                  pl.BlockSpec((D, tn), lambda j: (0, j)),
                  pl.BlockSpec((1, tn), lambda j: (0, j))],
        out_specs=pl.BlockSpec((B, tn), lambda j: (0, j)),
        out_shape=jax.ShapeDtypeStruct((B, E), F32),
        compiler_params=_params(("arbitrary",)),
        name="ada",
    )(c, ada_w, ada_b.reshape(1, E))


def _rms(x, g):
    ms = jnp.mean(x * x, axis=-1, keepdims=True)
    return x * lax.rsqrt(ms + EPS) * g


def _softplus(x):
    return jnp.maximum(x, 0.0) + jnp.log1p(jnp.exp(-jnp.abs(x)))


def _gelu_tanh(x):
    return 0.5 * x * (1.0 + jnp.tanh(0.7978845608028654 * (x + 0.044715 * x * x * x)))


LRU_PAD = 8


def _lru_tile(x, gate, cw_ref, cb_ref, wr_ref, br_ref, wi_ref, bi_ref, lam_ref, g_ref,
              xext, hc):
    T = x.shape[0]
    xext[LRU_PAD:LRU_PAD + T, :] = x
    xc = cb_ref[...] + cw_ref[CONV_WIDTH - 1:CONV_WIDTH, :] * x
    for j in range(CONV_WIDTH - 1):
        back = CONV_WIDTH - 1 - j
        xc = xc + cw_ref[j:j + 1, :] * xext[LRU_PAD - back:LRU_PAD - back + T, :]
    xext[0:LRU_PAD, :] = xext[T:T + LRU_PAD, :]

    xb = xc.astype(BF16)
    r = jax.nn.sigmoid(jnp.dot(xb, wr_ref[...], preferred_element_type=F32) + br_ref[...])
    ig = jax.nn.sigmoid(jnp.dot(xb, wi_ref[...], preferred_element_type=F32) + bi_ref[...])
    log_a = (-LRU_C) * r * _softplus(-lam_ref[...])
    a = jnp.exp(log_a)
    b = jnp.sqrt(-jnp.tanh(log_a) * (a * a + 1.0)) * (ig * xc)

    G = T // SUBLANES
    a = a.reshape(G, SUBLANES, -1)
    b = b.reshape(G, SUBLANES, -1)
    rows = lax.broadcasted_iota(jnp.int32, a.shape, 1)
    d = 1
    while d < SUBLANES:
        keep = rows >= d
        a_prev = jnp.where(keep, pltpu.roll(a, d, 1), 1.0)
        b_prev = jnp.where(keep, pltpu.roll(b, d, 1), 0.0)
        b = a * b_prev + b
        a = a * a_prev
        d *= 2
    prev = hc[...]
    groups = []
    for g in range(G):
        hg = a[g] * prev + b[g]
        prev = hg[SUBLANES - 1:SUBLANES, :]
        groups.append(hg)
    hc[...] = prev
    h = jnp.concatenate(groups, axis=0)
    return _rms(h * _gelu_tanh(gate), g_ref[...])


def _inproj_kernel(x_ref, mod_ref, g_ref, w_ref, cw_ref, cb_ref, wr_ref, br_ref, wi_ref,
                   bi_ref, lam_ref, gl_ref, q_ref, k_ref, v_ref, yl_ref, xext, hc):
    @pl.when(pl.program_id(1) == 0)
    def _():
        xext[0:LRU_PAD, :] = jnp.zeros((LRU_PAD, xext.shape[1]), F32)
        hc[...] = jnp.zeros_like(hc)

    x = x_ref[0]
    h = _rms(x, g_ref[...]) * (1.0 + mod_ref[0, 1:2, :]) + mod_ref[0, 0:1, :]
    hb = h.astype(BF16)
    W = SB_WIDTH

    def proj(c):
        return jnp.dot(hb, w_ref[:, c * W:(c + 1) * W], preferred_element_type=F32)

    xr, gr = proj(3), proj(4)
    T = LRU_SCAN_ROWS
    for t in range(x.shape[0] // T):
        rows = slice(t * T, (t + 1) * T)
        yl_ref[0, rows, :] = _lru_tile(xr[rows], gr[rows], cw_ref, cb_ref, wr_ref, br_ref,
                                       wi_ref, bi_ref, lam_ref, gl_ref, xext, hc)
    q_ref[0] = (proj(0) * (HEAD_DIM ** -0.5)).astype(BF16)
    k_ref[0] = proj(1).astype(BF16)
    v_ref[0] = proj(2).astype(BF16)


def _block_diag(w):
    H, I, J = w.shape
    eye = jnp.eye(H, dtype=w.dtype)
    return (w[:, :, None, :] * eye[:, None, :, None]).reshape(H * I, H * J)


def _inproj(x, mod3, g, w_in_b, conv_w, conv_b, rg_w, rg_b, ig_w, ig_b, lam, g_lru, ts=512):
    B, S, D = x.shape
    E = w_in_b.shape[1]
    W = SB_WIDTH
    row = pl.BlockSpec((1, ts, W), lambda b, s: (b, s, 0))
    vec = pl.BlockSpec((1, W), lambda b, s: (0, 0))
    mat = pl.BlockSpec((W, W), lambda b, s: (0, 0))
    return pl.pallas_call(
        _inproj_kernel,
        grid=(B, S // ts),
        in_specs=[pl.BlockSpec((1, ts, D), lambda b, s: (b, s, 0)),
                  pl.BlockSpec((1, N_MOD, D), lambda b, s: (b, 0, 0)),
                  pl.BlockSpec((1, D), lambda b, s: (0, 0)),
                  pl.BlockSpec((D, E), lambda b, s: (0, 0)),
                  pl.BlockSpec((CONV_WIDTH, W), lambda b, s: (0, 0)), vec,
                  mat, vec, mat, vec, vec, vec],
        out_specs=[row, row, row, row],
        out_shape=[jax.ShapeDtypeStruct((B, S, W), BF16)] * 3
        + [jax.ShapeDtypeStruct((B, S, W), F32)],
        scratch_shapes=[pltpu.VMEM((LRU_SCAN_ROWS + LRU_PAD, W), F32), pltpu.VMEM((1, W), F32)],
        compiler_params=_params(("arbitrary", "arbitrary")),
        name="inproj",
    )(x, mod3, g.reshape(1, D), w_in_b, conv_w, conv_b.reshape(1, W),
      _block_diag(rg_w).astype(BF16), rg_b.reshape(1, W), _block_diag(ig_w).astype(BF16),
      ig_b.reshape(1, W), lam.reshape(1, W), g_lru.reshape(1, W))


def _attn_kernel(q_ref, k_ref, v_ref, o_ref, acc_ref, carry_ref):
    i = pl.program_id(1)
    QB, KB = Q_BLOCK, K_BLOCK
    n_pairs = q_ref.shape[2] // LANES
    lane = lax.broadcasted_iota(jnp.int32, (QB, LANES), 1)

    def stacked_q(p):
        q = q_ref[0, :, p * LANES:(p + 1) * LANES]
        zero = jnp.zeros_like(q)
        return jnp.concatenate([jnp.where(lane < HEAD_DIM, q, zero),
                                jnp.where(lane >= HEAD_DIM, q, zero)], axis=0)

    qs = [stacked_q(p) for p in range(n_pairs)]

    uj = lax.broadcasted_iota(jnp.int32, (2 * KB, KB + LANES), 0) & (KB - 1)
    us = lax.broadcasted_iota(jnp.int32, (2 * KB, KB + LANES), 1)
    u2 = jnp.where((us >= KB) | (uj > us), -1.0, 0.0).astype(BF16)

    row = lax.broadcasted_iota(jnp.int32, (2 * QB, KB), 0) & (QB - 1)
    col = lax.broadcasted_iota(jnp.int32, (2 * QB, KB), 1)
    causal = col < row

    def tiles(p, j_hi, n, diagonal, acc, carry):
        start = pl.multiple_of((j_hi - (n - 1)) * KB, KB)
        kw = k_ref[0, pl.ds(start, n * KB), p * LANES:(p + 1) * LANES]
        vw = v_ref[0, pl.ds(start, n * KB), p * LANES:(p + 1) * LANES]
        z = lax.dot_general(qs[p], kw, (((1,), (1,)), ((), ())), preferred_element_type=F32)
        softplus = jnp.maximum(z, 0.0) + jnp.log(1.0 + jnp.exp2(jnp.abs(z) * (-LOG2E)))
        log_beta = z - softplus
        ws = [None] * n
        for c in reversed(range(n)):
            sp = softplus[:, c * KB:(c + 1) * KB]
            masked = diagonal and c == n - 1
            if masked:
                sp = jnp.where(causal, sp, 0.0)
            hi_f = lax.bitcast_convert_type(
                lax.bitcast_convert_type(sp, jnp.uint32) & jnp.uint32(0xFFFF0000), F32)
            hi_lo = jnp.concatenate([hi_f.astype(BF16), (sp - hi_f).astype(BF16)], axis=1)
            sums = jnp.dot(hi_lo, u2, preferred_element_type=F32)
            w = jnp.exp2((log_beta[:, c * KB:(c + 1) * KB] + sums[:, :KB] + carry) * LOG2E)
            if masked:
                w = jnp.where(causal, w, 0.0)
            ws[c] = w.astype(BF16)
            carry = carry + sums[:, KB:]
        w_all = ws[0] if n == 1 else jnp.concatenate(ws, axis=1)
        acc = acc + jnp.dot(w_all, vw, preferred_element_type=F32)
        return acc, carry

    def fold(j_hi, n, diagonal):
        cmax = None
        for p in range(n_pairs):
            if diagonal:
                acc = carry = jnp.zeros((2 * QB, LANES), F32)
            else:
                acc, carry = acc_ref[p], carry_ref[p]
            acc, carry = tiles(p, j_hi, n, diagonal, acc, carry)
            acc_ref[p] = acc
            carry_ref[p] = carry
            cmax = carry if cmax is None else jnp.maximum(cmax, carry)
        return jnp.max(cmax)

    n_first, n_loop = ATTN_WINDOW_BLOCKS, ATTN_LOOP_BLOCKS
    n0 = jnp.minimum(i + 1, n_first)
    cmax0 = lax.switch(n0 - 1, [functools.partial(fold, i, n, True)
                                for n in range(1, n_first + 1)])
    j0 = i - n0

    def loop(n, j, cmax):
        def more(st):
            return (st[0] >= n - 1) & (st[1] >= ATTN_UNDERFLOW_LOG)

        def step(st):
            return st[0] - n, fold(st[0], n, False)

        return lax.while_loop(more, step, (j, cmax))

    j, cmax = j0, cmax0
    while n_loop >= 1:
        j, cmax = loop(n_loop, j, cmax)
        n_loop //= 2
    for p in range(n_pairs):
        acc = acc_ref[p]
        o_ref[0, :, p * LANES:(p + 1) * LANES] = jnp.where(lane < HEAD_DIM, acc[:QB], acc[QB:])


def _attention(q, k, v):
    B, S, W = q.shape
    return pl.pallas_call(
        _attn_kernel,
        grid=(B, S // Q_BLOCK),
        in_specs=[pl.BlockSpec((1, Q_BLOCK, W), lambda b, i: (b, i, 0)),
                  pl.BlockSpec((1, S, W), lambda b, i: (b, 0, 0)),
                  pl.BlockSpec((1, S, W), lambda b, i: (b, 0, 0))],
        out_specs=pl.BlockSpec((1, Q_BLOCK, W), lambda b, i: (b, i, 0)),
        out_shape=jax.ShapeDtypeStruct((B, S, W), F32),
        scratch_shapes=[pltpu.VMEM((W // LANES, 2 * Q_BLOCK, LANES), F32),
                        pltpu.VMEM((W // LANES, 2 * Q_BLOCK, LANES), F32)],
        compiler_params=_params(("arbitrary", "arbitrary")),
        name="attn",
    )(q, k, v)


def _load_row_tiles(ref, n_rows, lead=()):
    chunks = [ref[lead + (pl.ds(c, n_rows, stride=SUBLANES), slice(None))]
              for c in range(SUBLANES)]
    return jnp.concatenate(chunks, axis=1)


def _store_row_tiles(ref, value):
    n_rows = value.shape[0]
    for c in range(SUBLANES):
        ref[pl.ds(c, n_rows, stride=SUBLANES), :] = value[:, c * LANES:(c + 1) * LANES]


PACK_ROWS = SUBLANES // 2


def _store_packed_rows(ref, value):
    n_rows, d = value.shape
    bits = lax.bitcast_convert_type(value.astype(BF16).astype(F32), jnp.uint32)
    words = (bits[:, :d // 2] >> 16) | bits[:, d // 2:]
    for c in range(PACK_ROWS):
        ref[pl.ds(c, n_rows, stride=PACK_ROWS), :] = words[:, c * LANES:(c + 1) * LANES]


def _load_packed_rows(ref, n_rows, lead=()):
    lo, hi = [], []
    for c in range(PACK_ROWS):
        w = ref[lead + (pl.ds(c, n_rows, stride=PACK_ROWS), slice(None))]
        lo.append(lax.bitcast_convert_type(w << 16, F32).astype(BF16))
        hi.append(lax.bitcast_convert_type(w & jnp.uint32(0xFFFF0000), F32).astype(BF16))
    return jnp.concatenate(lo + hi, axis=1)


def _outproj_kernel(x_ref, ya_ref, yl_ref, mod_ref, ga_ref, wo_ref, gf_ref, rw_ref, rb_ref,
                    x1_ref, h2_ref, idx_ref, prob_ref, rank_ref, cnt_ref, tri, run):
    W = SB_WIDTH
    ya = _rms(ya_ref[0], ga_ref[...]).astype(BF16)
    yl = yl_ref[0].astype(BF16)
    mix = (jnp.dot(ya, wo_ref[0:W, :], preferred_element_type=F32)
           + jnp.dot(yl, wo_ref[W:2 * W, :], preferred_element_type=F32))
    x1 = x_ref[0] + mod_ref[0, 2:3, :] * mix
    x1_ref[0] = x1
    h2 = _rms(x1, gf_ref[...]) * (1.0 + mod_ref[0, 4:5, :]) + mod_ref[0, 3:4, :]
    _store_packed_rows(h2_ref, h2)

    logits = lax.dot_general(rw_ref[...], h2, (((1,), (1,)), ((), ())), precision=HIGHEST,
                             preferred_element_type=F32) + rb_ref[...]
    n_exp = logits.shape[0]
    eid = lax.broadcasted_iota(jnp.int32, logits.shape, 0)
    vals, idxs = [], []
    for _ in range(TOP_K):
        m = jnp.max(logits, axis=0, keepdims=True)
        sel = jnp.min(jnp.where(logits == m, eid, n_exp), axis=0, keepdims=True)
        vals.append(m)
        idxs.append(sel)
        logits = jnp.where(eid == sel, -jnp.inf, logits)
    es = [jnp.exp(vv - vals[0]) for vv in vals]
    inv = 1.0 / (es[0] + es[1] + es[2] + es[3])

    first = (pl.program_id(0) == 0) & (pl.program_id(1) == 0)

    @pl.when(first)
    def _():
        ti = lax.broadcasted_iota(jnp.int32, tri.shape, 0)
        tj = lax.broadcasted_iota(jnp.int32, tri.shape, 1)
        tri[...] = jnp.where(ti < tj, 1.0, 0.0).astype(BF16)
        run[...] = jnp.zeros_like(run)

    hits = [eid == idxs[r] for r in range(TOP_K)]
    chosen = hits[0] | hits[1] | hits[2] | hits[3]
    cnt = jnp.where(chosen, 1.0, 0.0)
    before = jnp.dot(cnt.astype(BF16), tri[...], preferred_element_type=F32) + run[...]
    for r in range(TOP_K):
        idx_ref[0, r:r + 1, :] = idxs[r]
        prob_ref[0, r:r + 1, :] = es[r] * inv
        rank_ref[0, r:r + 1, :] = jnp.sum(jnp.where(hits[r], before, 0.0), axis=0,
                                          keepdims=True).astype(jnp.int32)
    run[...] = run[...] + jnp.sum(cnt, axis=1, keepdims=True)
    cnt_ref[...] = jnp.broadcast_to(run[...], cnt_ref.shape).astype(jnp.int32)


def _outproj(x, ya, yl, mod3, ga, w_out_b, gf, router_wt, router_b, ts=512):
    B, S, D = x.shape
    W = ya.shape[2]
    NE = router_wt.shape[0]
    rowd = pl.BlockSpec((1, ts, D), lambda b, s: (b, s, 0))
    roww = pl.BlockSpec((1, ts, W), lambda b, s: (b, s, 0))
    sel = pl.BlockSpec((1, TOP_K, ts), lambda b, s: (b, 0, s))
    return pl.pallas_call(
        _outproj_kernel,
        grid=(B, S // ts),
        in_specs=[rowd, roww, roww,
                  pl.BlockSpec((1, N_MOD, D), lambda b, s: (b, 0, 0)),
                  pl.BlockSpec((1, W), lambda b, s: (0, 0)),
                  pl.BlockSpec((2 * W, D), lambda b, s: (0, 0)),
                  pl.BlockSpec((1, D), lambda b, s: (0, 0)),
                  pl.BlockSpec((NE, D), lambda b, s: (0, 0)),
                  pl.BlockSpec((NE, 1), lambda b, s: (0, 0))],
        out_specs=[rowd,
                   pl.BlockSpec((ts * PACK_ROWS, LANES), lambda b, s: (b * (S // ts) + s, 0)),
                   sel, sel, sel, pl.BlockSpec((NE, LANES), lambda b, s: (0, 0))],
        out_shape=[jax.ShapeDtypeStruct((B, S, D), F32),
                   jax.ShapeDtypeStruct((B * S * PACK_ROWS, LANES), jnp.uint32),
                   jax.ShapeDtypeStruct((B, TOP_K, S), jnp.int32),
                   jax.ShapeDtypeStruct((B, TOP_K, S), F32),
                   jax.ShapeDtypeStruct((B, TOP_K, S), jnp.int32),
                   jax.ShapeDtypeStruct((NE, LANES), jnp.int32)],
        scratch_shapes=[pltpu.VMEM((ts, ts), BF16), pltpu.VMEM((NE, 1), F32)],
        compiler_params=_params(("arbitrary", "arbitrary")),
        name="outproj",
    )(x, ya, yl, mod3, ga.reshape(1, W), w_out_b, gf.reshape(1, D), router_wt,
      router_b.reshape(NE, 1))


def _dispatch_kernel(p0_ref, p1_ref, p2_ref, p3_ref, fill_ref, h_ref, xs_ref, zbuf, sem, zsem):
    TT = h_ref.shape[0] // PACK_ROWS
    n_fill = fill_ref.shape[0]

    @pl.when(pl.program_id(0) == 0)
    def _():
        zbuf[...] = jnp.zeros_like(zbuf)

        def fill_copy(e):
            start = pl.multiple_of(jnp.maximum(fill_ref[e], 0), PACK_ROWS)
            return pltpu.make_async_copy(zbuf, xs_ref.at[pl.ds(start, zbuf.shape[0]), :], zsem)

        def start(e, c):
            @pl.when(fill_ref[e] >= 0)
            def _():
                fill_copy(e).start()
            return c

        def wait(e, c):
            @pl.when(fill_ref[e] >= 0)
            def _():
                fill_copy(e).wait()
            return c

        lax.fori_loop(0, n_fill, start, 0)
        lax.fori_loop(0, n_fill, wait, 0)

    pos_refs = (p0_ref, p1_ref, p2_ref, p3_ref)

    def row_copy(t, r):
        src = pl.multiple_of(t * PACK_ROWS, PACK_ROWS)
        dst = pl.multiple_of(pos_refs[r][t], PACK_ROWS)
        return pltpu.make_async_copy(h_ref.at[pl.ds(src, PACK_ROWS), :],
                                     xs_ref.at[pl.ds(dst, PACK_ROWS), :], sem)

    def issue(g, c):
        for tt in range(ROW_DMA_UNROLL):
            for r in range(TOP_K):
                row_copy(g * ROW_DMA_UNROLL + tt, r).start(priority=(tt * TOP_K + r) % 2)
        return c

    lax.fori_loop(0, TT // ROW_DMA_UNROLL, issue, 0)
    for r in range(TOP_K):
        pltpu.make_async_copy(h_ref, xs_ref.at[pl.ds(0, TT * PACK_ROWS), :], sem).wait()


def _slot_specs(tile, tiles_per_batch):
    def spec(r):
        return pl.BlockSpec(
            (tile,),
            lambda i: ((i // tiles_per_batch * TOP_K + r) * tiles_per_batch
                       + i % tiles_per_batch,),
            memory_space=pltpu.SMEM)
    return [spec(r) for r in range(TOP_K)]


def _dispatch(h2t, pos_flat, fill_start, cap, seq_len):
    TT = min(DISPATCH_TILE, seq_len)
    assert seq_len % TT == 0
    tiles_per_batch = seq_len // TT
    n_tiles = h2t.shape[0] // (TT * PACK_ROWS)
    return pl.pallas_call(
        _dispatch_kernel,
        grid=(n_tiles,),
        in_specs=_slot_specs(TT, tiles_per_batch)
        + [pl.BlockSpec(memory_space=pltpu.SMEM),
           pl.BlockSpec((TT * PACK_ROWS, LANES), lambda i: (i, 0))],
        out_specs=pl.BlockSpec(memory_space=pl.ANY),
        out_shape=jax.ShapeDtypeStruct((cap * PACK_ROWS, LANES), jnp.uint32),
        scratch_shapes=[pltpu.VMEM((EXPERT_BLOCK * PACK_ROWS, LANES), jnp.uint32),
                        pltpu.SemaphoreType.DMA, pltpu.SemaphoreType.DMA],
        compiler_params=_params(("arbitrary",)),
        name="dispatch",
    )(pos_flat, pos_flat, pos_flat, pos_flat, fill_start, h2t)


def _expert_kernel(first_ref, count_ref, xs_ref, wg_ref, bg_ref, wu_ref, bu_ref, wd_ref,
                   bd_ref, ys_ref, wgb, wub, wdb, xbuf, ybuf, xsem, ysem):
    e = pl.program_id(0)
    n_exp = pl.num_programs(0)
    rows = ybuf.shape[1]
    first, count = first_ref[e], count_ref[e]

    wgb[...] = wg_ref[0].astype(BF16)
    wub[...] = wu_ref[0].astype(BF16)
    wdb[...] = wd_ref[0].astype(BF16)

    def block(ref, buf, j):
        n = buf.shape[1]
        return ref.at[pl.ds(pl.multiple_of((first + j) * n, n), n), :]

    def x_copy(j, slot):
        return pltpu.make_async_copy(block(xs_ref, xbuf, j), xbuf.at[slot], xsem.at[slot])

    def y_copy(j, slot):
        return pltpu.make_async_copy(ybuf.at[slot], block(ys_ref, ybuf, j), ysem.at[slot])

    @pl.when(count > 0)
    def _():
        x_copy(0, 0).start()

    def step(j, c):
        slot = j % 2
        x_copy(j, slot).wait()

        @pl.when(j + 1 < count)
        def _():
            x_copy(j + 1, 1 - slot).start()

        @pl.when(j >= 2)
        def _():
            y_copy(j - 2, slot).wait()

        xb = _load_packed_rows(xbuf, EXPERT_BLOCK, (slot,))
        g = jnp.minimum(jnp.dot(xb, wgb[...], preferred_element_type=F32) + bg_ref[0],
                        SWIGLU_LIMIT)
        u = jnp.clip(jnp.dot(xb, wub[...], preferred_element_type=F32) + bu_ref[0],
                     -SWIGLU_LIMIT, SWIGLU_LIMIT)
        act = (u + 1.0) * (g * jax.nn.sigmoid(SWIGLU_ALPHA * g))
        _store_row_tiles(ybuf.at[slot], jnp.dot(act.astype(BF16), wdb[...],
                                                preferred_element_type=F32) + bd_ref[0])
        y_copy(j, slot).start()
        return c

    lax.fori_loop(0, count, step, 0)

    @pl.when(count >= 2)
    def _():
        y_copy(count - 2, count % 2).wait()

    @pl.when(count >= 1)
    def _():
        y_copy(count - 1, (count - 1) % 2).wait()

    @pl.when(e == n_exp - 1)
    def _():
        spare_first, spare_count = first_ref[n_exp], count_ref[n_exp]
        ybuf[0] = jnp.zeros(ybuf.shape[1:], F32)

        def spare_copy(j):
            dst = pl.ds(pl.multiple_of((spare_first + j) * rows, rows), rows)
            return pltpu.make_async_copy(ybuf.at[0], ys_ref.at[dst, :], ysem.at[0])

        def start(j, c):
            spare_copy(j).start()
            return c

        def wait(j, c):
            spare_copy(j).wait()
            return c

        lax.fori_loop(0, spare_count, start, 0)
        lax.fori_loop(0, spare_count, wait, 0)


def _experts(xs, first_block, block_count, wg, bg, wu, bu, wd, bd):
    NE, D, DE = wg.shape
    cap = xs.shape[0] // PACK_ROWS

    def wsel(e, first, count):
        return (e, 0, 0)

    grid_spec = pltpu.PrefetchScalarGridSpec(
        num_scalar_prefetch=2,
        grid=(NE,),
        in_specs=[pl.BlockSpec(memory_space=pl.ANY),
                  pl.BlockSpec((1, D, DE), wsel), pl.BlockSpec((1, 1, DE), wsel),
                  pl.BlockSpec((1, D, DE), wsel), pl.BlockSpec((1, 1, DE), wsel),
                  pl.BlockSpec((1, DE, D), wsel), pl.BlockSpec((1, 1, D), wsel)],
        out_specs=pl.BlockSpec(memory_space=pl.ANY),
        scratch_shapes=[pltpu.VMEM((D, DE), BF16), pltpu.VMEM((D, DE), BF16),
                        pltpu.VMEM((DE, D), BF16),
                        pltpu.VMEM((2, EXPERT_BLOCK * PACK_ROWS, LANES), jnp.uint32),
                        pltpu.VMEM((2, EXPERT_BLOCK * SUBLANES, LANES), F32),
                        pltpu.SemaphoreType.DMA((2,)), pltpu.SemaphoreType.DMA((2,))],
    )
    return pl.pallas_call(
        _expert_kernel,
        grid_spec=grid_spec,
        out_shape=jax.ShapeDtypeStruct((cap * SUBLANES, LANES), F32),
        compiler_params=_params(("arbitrary",)),
        name="experts",
    )(first_block, block_count, xs, wg, bg.reshape(NE, 1, DE), wu, bu.reshape(NE, 1, DE), wd,
      bd.reshape(NE, 1, D))


def _combine_kernel(c0_ref, c1_ref, c2_ref, c3_ref, n0_ref, n1_ref, n2_ref, n3_ref,
                    x1_ref, p_ref, mod_ref, g_ref, ys_ref, o_ref, buf, sem):
    TT = x1_ref.shape[0]
    i = pl.program_id(0)

    def gather(pos_refs, s):
        def issue(g, c):
            for tt in range(ROW_DMA_UNROLL):
                t = g * ROW_DMA_UNROLL + tt
                for r in range(TOP_K):
                    pltpu.make_async_copy(
                        ys_ref.at[pl.ds(pl.multiple_of(pos_refs[r][t], SUBLANES), SUBLANES), :],
                        buf.at[s, r, pl.ds(pl.multiple_of(t * SUBLANES, SUBLANES), SUBLANES), :],
                        sem.at[s]).start(priority=(tt * TOP_K + r) % 2)
            return c
        lax.fori_loop(0, TT // ROW_DMA_UNROLL, issue, 0)

    @pl.when(i == 0)
    def _():
        gather((c0_ref, c1_ref, c2_ref, c3_ref), 0)

    def reduce_tile(s):
        @pl.when(i + 1 < pl.num_programs(0))
        def _():
            gather((n0_ref, n1_ref, n2_ref, n3_ref), 1 - s)

        for r in range(TOP_K):
            pltpu.make_async_copy(ys_ref.at[pl.ds(0, TT * SUBLANES), :], buf.at[s, r],
                                  sem.at[s]).wait()
        p = p_ref[...]
        moe = p[:, 0:1] * _load_row_tiles(buf, TT, (s, 0))
        for r in range(1, TOP_K):
            moe = moe + p[:, r:r + 1] * _load_row_tiles(buf, TT, (s, r))
        x2 = x1_ref[...] + mod_ref[0, 5:6, :] * moe
        o_ref[...] = _rms(x2, g_ref[...])

    for s in range(2):
        pl.when(i % 2 == s)(functools.partial(reduce_tile, s))


def _combine(x1f, probs, pos_flat, mod3, g, ys, tiles_per_batch):
    N, D = x1f.shape
    TT = COMBINE_TILE
    n_tiles = N // TT

    def next_spec(r):
        def index(i):
            j = jnp.minimum(i + 1, n_tiles - 1)
            return ((j // tiles_per_batch * TOP_K + r) * tiles_per_batch + j % tiles_per_batch,)
        return pl.BlockSpec((TT,), index, memory_space=pltpu.SMEM)

    return pl.pallas_call(
        _combine_kernel,
        grid=(n_tiles,),
        in_specs=_slot_specs(TT, tiles_per_batch) + [next_spec(r) for r in range(TOP_K)]
        + [pl.BlockSpec((TT, D), lambda i: (i, 0)),
           pl.BlockSpec((TT, TOP_K), lambda i: (i, 0)),
           pl.BlockSpec((1, N_MOD, D), lambda i: (i // tiles_per_batch, 0, 0)),
           pl.BlockSpec((1, D), lambda i: (0, 0)),
           pl.BlockSpec(memory_space=pl.ANY)],
        out_specs=pl.BlockSpec((TT, D), lambda i: (i, 0)),
        out_shape=jax.ShapeDtypeStruct((N, D), F32),
        scratch_shapes=[pltpu.VMEM((2, TOP_K, TT * SUBLANES, LANES), F32),
                        pltpu.SemaphoreType.DMA((2,))],
        compiler_params=_params(("arbitrary",)),
        name="combine",
    )(*([pos_flat] * (2 * TOP_K)), x1f, probs, mod3, g.reshape(1, D), ys)


def _routing(idx, rank, counts):
    B, K, S = idx.shape
    n_exp = counts.shape[0]
    TM = EXPERT_BLOCK
    n_assign = B * S * K
    padded = ((counts + TM - 1) // TM) * TM
    pad_ends = jnp.cumsum(padded)
    pad_starts = pad_ends - padded
    experts = jnp.arange(n_exp, dtype=jnp.int32).reshape(n_exp, 1, 1, 1)
    base = jnp.sum(jnp.where(idx[None] == experts, pad_starts.reshape(n_exp, 1, 1, 1), 0),
                   axis=0)
    pos = (base + rank).astype(jnp.int32).reshape(n_assign)
    n_blocks = n_assign // TM + n_exp
    first_block = jnp.concatenate([pad_starts, pad_ends[-1:]]) // TM
    block_count = jnp.concatenate([padded // TM, n_blocks - pad_ends[-1:] // TM])
    tail_start = jnp.where(padded > 0, pad_ends - TM, -1)
    spare = pad_ends[-1] + jnp.arange(n_exp, dtype=jnp.int32) * TM
    fill_start = jnp.concatenate(
        [tail_start, jnp.where(spare < n_blocks * TM, spare, -1)]).astype(jnp.int32)
    return (pos, first_block.astype(jnp.int32), block_count.astype(jnp.int32),
            fill_start, n_blocks * TM)


def kernel(x, c, ada_w, ada_b, mix_norm_g, w_in, conv_w, conv_b, rg_w, rg_b, ig_w, ig_b,
           lru_lambda, attn_out_g, lru_out_g, w_out, ffn_norm_g, router_w, router_b,
           exp_w_gate, exp_b_gate, exp_w_up, exp_b_up, exp_w_down, exp_b_down, final_norm_g):
    B, S, D = x.shape
    depth = ada_w.shape[0]
    assert S % COMBINE_TILE == 0 and S % 512 == 0
    assert D == SUBLANES * LANES
    for l in range(depth):
        mod3 = _ada(c, ada_w[l], ada_b[l]).reshape(B, N_MOD, D)
        q, k, v, yl = _inproj(x, mod3, mix_norm_g[l], w_in[l].astype(BF16), conv_w[l],
                              conv_b[l], rg_w[l], rg_b[l], ig_w[l], ig_b[l], lru_lambda[l],
                              lru_out_g[l])
        ya = _attention(q, k, v)
        x1, h2, idx, prob, rank, cnt = _outproj(
            x, ya, yl, mod3, attn_out_g[l], w_out[l].astype(BF16), ffn_norm_g[l],
            router_w[l].T, router_b[l])
        pos, first_block, block_count, fill_start, cap = _routing(idx, rank, cnt[:, 0])
        xs = _dispatch(h2, pos * PACK_ROWS, fill_start * PACK_ROWS, cap, S)
        ys = _experts(xs, first_block, block_count, exp_w_gate[l], exp_b_gate[l],
                      exp_w_up[l], exp_b_up[l], exp_w_down[l], exp_b_down[l])
        probs = jnp.transpose(prob, (0, 2, 1)).reshape(B * S, TOP_K)
        assert depth == 1
        x = _combine(x1.reshape(B * S, D), probs, pos * SUBLANES, mod3, final_norm_g, ys,
                     S // COMBINE_TILE).reshape(B, S, D)
    return x
```

```python
import functools

import jax
import jax.numpy as jnp
from jax import lax
from jax.experimental import pallas as pl
from jax.experimental.pallas import tpu as pltpu

F32 = jnp.float32
BF16 = jnp.bfloat16
HIGHEST = lax.Precision.HIGHEST

EPS = 1e-6
N_MOD = 6
SB_HEADS = 8
HEAD_DIM = 64
SB_WIDTH = SB_HEADS * HEAD_DIM
LRU_BLOCKS = 8
CONV_WIDTH = 4
LRU_C = 8.0
TOP_K = 4
SWIGLU_LIMIT = 7.0
SWIGLU_ALPHA = 1.702
LOG2E = 1.4426950408889634

LANES = 128
SUBLANES = 8
VMEM_LIMIT = 56 * 1024 * 1024

Q_BLOCK = 128
K_BLOCK = 128
ATTN_WINDOW_BLOCKS = 5
ATTN_LOOP_BLOCKS = 2
ATTN_UNDERFLOW_LOG = -110.0
EXPERT_BLOCK = 512
DISPATCH_TILE = 4096
COMBINE_TILE = 256
ROW_DMA_UNROLL = 8
LRU_SCAN_ROWS = 256


def _params(sem):
    return pltpu.CompilerParams(dimension_semantics=sem, vmem_limit_bytes=VMEM_LIMIT)


def _ada_kernel(c_ref, w_ref, b_ref, o_ref):
    c = c_ref[...]
    ca = c * jax.nn.sigmoid(c)
    o_ref[...] = jnp.dot(ca, w_ref[...], precision=HIGHEST,
                         preferred_element_type=F32) + b_ref[...]


def _ada(c, ada_w, ada_b):
    B, D = c.shape
    E = ada_w.shape[1]
    tn = 1024
    return pl.pallas_call(
        _ada_kernel,
        grid=(E // tn,),
        in_specs=[pl.BlockSpec((B, D), lambda j: (0, 0)),
                  pl.BlockSpec((D, tn), lambda j: (0, j)),
                  pl.BlockSpec((1, tn), lambda j: (0, j))],
        out_specs=pl.BlockSpec((B, tn), lambda j: (0, j)),
        out_shape=jax.ShapeDtypeStruct((B, E), F32),
        compiler_params=_params(("arbitrary",)),
        name="ada",
    )(c, ada_w, ada_b.reshape(1, E))


def _rms(x, g):
    ms = jnp.mean(x * x, axis=-1, keepdims=True)
    return x * lax.rsqrt(ms + EPS) * g


def _softplus(x):
    return jnp.maximum(x, 0.0) + jnp.log1p(jnp.exp(-jnp.abs(x)))


def _gelu_tanh(x):
    return 0.5 * x * (1.0 + jnp.tanh(0.7978845608028654 * (x + 0.044715 * x * x * x)))


LRU_PAD = 8


def _lru_tile(x, gate, cw_ref, cb_ref, wr_ref, br_ref, wi_ref, bi_ref, lam_ref, g_ref,
              xext, hc):
    T = x.shape[0]
    xext[LRU_PAD:LRU_PAD + T, :] = x
    xc = cb_ref[...] + cw_ref[CONV_WIDTH - 1:CONV_WIDTH, :] * x
    for j in range(CONV_WIDTH - 1):
        back = CONV_WIDTH - 1 - j
        xc = xc + cw_ref[j:j + 1, :] * xext[LRU_PAD - back:LRU_PAD - back + T, :]
    xext[0:LRU_PAD, :] = xext[T:T + LRU_PAD, :]

    xb = xc.astype(BF16)
    r = jax.nn.sigmoid(jnp.dot(xb, wr_ref[...], preferred_element_type=F32) + br_ref[...])
    ig = jax.nn.sigmoid(jnp.dot(xb, wi_ref[...], preferred_element_type=F32) + bi_ref[...])
    log_a = (-LRU_C) * r * _softplus(-lam_ref[...])
    a = jnp.exp(log_a)
    b = jnp.sqrt(-jnp.tanh(log_a) * (a * a + 1.0)) * (ig * xc)

    G = T // SUBLANES
    a = a.reshape(G, SUBLANES, -1)
    b = b.reshape(G, SUBLANES, -1)
    rows = lax.broadcasted_iota(jnp.int32, a.shape, 1)
    d = 1
    while d < SUBLANES:
        keep = rows >= d
        a_prev = jnp.where(keep, pltpu.roll(a, d, 1), 1.0)
        b_prev = jnp.where(keep, pltpu.roll(b, d, 1), 0.0)
        b = a * b_prev + b
        a = a * a_prev
        d *= 2
    prev = hc[...]
    groups = []
    for g in range(G):
        hg = a[g] * prev + b[g]
        prev = hg[SUBLANES - 1:SUBLANES, :]
        groups.append(hg)
    hc[...] = prev
    h = jnp.concatenate(groups, axis=0)
    return _rms(h * _gelu_tanh(gate), g_ref[...])


def _inproj_kernel(x_ref, mod_ref, g_ref, w_ref, cw_ref, cb_ref, wr_ref, br_ref, wi_ref,
                   bi_ref, lam_ref, gl_ref, q_ref, k_ref, v_ref, yl_ref, xext, hc):
    @pl.when(pl.program_id(1) == 0)
    def _():
        xext[0:LRU_PAD, :] = jnp.zeros((LRU_PAD, xext.shape[1]), F32)
        hc[...] = jnp.zeros_like(hc)

    x = x_ref[0]
    h = _rms(x, g_ref[...]) * (1.0 + mod_ref[0, 1:2, :]) + mod_ref[0, 0:1, :]
    hb = h.astype(BF16)
    W = SB_WIDTH

    def proj(c):
        return jnp.dot(hb, w_ref[:, c * W:(c + 1) * W], preferred_element_type=F32)

    xr, gr = proj(3), proj(4)
    T = LRU_SCAN_ROWS
    for t in range(x.shape[0] // T):
        rows = slice(t * T, (t + 1) * T)
        yl_ref[0, rows, :] = _lru_tile(xr[rows], gr[rows], cw_ref, cb_ref, wr_ref, br_ref,
                                       wi_ref, bi_ref, lam_ref, gl_ref, xext, hc)
    q_ref[0] = (proj(0) * (HEAD_DIM ** -0.5)).astype(BF16)
    k_ref[0] = proj(1).astype(BF16)
    v_ref[0] = proj(2).astype(BF16)


def _block_diag(w):
    H, I, J = w.shape
    eye = jnp.eye(H, dtype=w.dtype)
    return (w[:, :, None, :] * eye[:, None, :, None]).reshape(H * I, H * J)


def _inproj(x, mod3, g, w_in_b, conv_w, conv_b, rg_w, rg_b, ig_w, ig_b, lam, g_lru, ts=512):
    B, S, D = x.shape
    E = w_in_b.shape[1]
    W = SB_WIDTH
    row = pl.BlockSpec((1, ts, W), lambda b, s: (b, s, 0))
    vec = pl.BlockSpec((1, W), lambda b, s: (0, 0))
    mat = pl.BlockSpec((W, W), lambda b, s: (0, 0))
    return pl.pallas_call(
        _inproj_kernel,
        grid=(B, S // ts),
        in_specs=[pl.BlockSpec((1, ts, D), lambda b, s: (b, s, 0)),
                  pl.BlockSpec((1, N_MOD, D), lambda b, s: (b, 0, 0)),
                  pl.BlockSpec((1, D), lambda b, s: (0, 0)),
                  pl.BlockSpec((D, E), lambda b, s: (0, 0)),
                  pl.BlockSpec((CONV_WIDTH, W), lambda b, s: (0, 0)), vec,
                  mat, vec, mat, vec, vec, vec],
        out_specs=[row, row, row, row],
        out_shape=[jax.ShapeDtypeStruct((B, S, W), BF16)] * 3
        + [jax.ShapeDtypeStruct((B, S, W), F32)],
        scratch_shapes=[pltpu.VMEM((LRU_SCAN_ROWS + LRU_PAD, W), F32), pltpu.VMEM((1, W), F32)],
        compiler_params=_params(("arbitrary", "arbitrary")),
        name="inproj",
    )(x, mod3, g.reshape(1, D), w_in_b, conv_w, conv_b.reshape(1, W),
      _block_diag(rg_w).astype(BF16), rg_b.reshape(1, W), _block_diag(ig_w).astype(BF16),
      ig_b.reshape(1, W), lam.reshape(1, W), g_lru.reshape(1, W))


def _attn_kernel(q_ref, k_ref, v_ref, o_ref, acc_ref, carry_ref):
    i = pl.program_id(1)
    QB, KB = Q_BLOCK, K_BLOCK
    n_pairs = q_ref.shape[2] // LANES
    lane = lax.broadcasted_iota(jnp.int32, (QB, LANES), 1)

    def stacked_q(p):
        q = q_ref[0, :, p * LANES:(p + 1) * LANES]
        zero = jnp.zeros_like(q)
        return jnp.concatenate([jnp.where(lane < HEAD_DIM, q, zero),
                                jnp.where(lane >= HEAD_DIM, q, zero)], axis=0)

    qs = [stacked_q(p) for p in range(n_pairs)]

    uj = lax.broadcasted_iota(jnp.int32, (2 * KB, KB + LANES), 0) & (KB - 1)
    us = lax.broadcasted_iota(jnp.int32, (2 * KB, KB + LANES), 1)
    u2 = jnp.where((us >= KB) | (uj > us), -1.0, 0.0).astype(BF16)

    row = lax.broadcasted_iota(jnp.int32, (2 * QB, KB), 0) & (QB - 1)
    col = lax.broadcasted_iota(jnp.int32, (2 * QB, KB), 1)
    causal = col < row

    def tiles(p, j_hi, n, diagonal, acc, carry):
        start = pl.multiple_of((j_hi - (n - 1)) * KB, KB)
        kw = k_ref[0, pl.ds(start, n * KB), p * LANES:(p + 1) * LANES]
        vw = v_ref[0, pl.ds(start, n * KB), p * LANES:(p + 1) * LANES]
        z = lax.dot_general(qs[p], kw, (((1,), (1,)), ((), ())), preferred_element_type=F32)
        softplus = jnp.maximum(z, 0.0) + jnp.log(1.0 + jnp.exp2(jnp.abs(z) * (-LOG2E)))
        log_beta = z - softplus
        ws = [None] * n
        for c in reversed(range(n)):
            sp = softplus[:, c * KB:(c + 1) * KB]
            masked = diagonal and c == n - 1
            if masked:
                sp = jnp.where(causal, sp, 0.0)
            hi_f = lax.bitcast_convert_type(
                lax.bitcast_convert_type(sp, jnp.uint32) & jnp.uint32(0xFFFF0000), F32)
            hi_lo = jnp.concatenate([hi_f.astype(BF16), (sp - hi_f).astype(BF16)], axis=1)
            sums = jnp.dot(hi_lo, u2, preferred_element_type=F32)
            w = jnp.exp2((log_beta[:, c * KB:(c + 1) * KB] + sums[:, :KB] + carry) * LOG2E)
            if masked:
                w = jnp.where(causal, w, 0.0)
            ws[c] = w.astype(BF16)
            carry = carry + sums[:, KB:]
        w_all = ws[0] if n == 1 else jnp.concatenate(ws, axis=1)
        acc = acc + jnp.dot(w_all, vw, preferred_element_type=F32)
        return acc, carry

    def fold(j_hi, n, diagonal):
        cmax = None
        for p in range(n_pairs):
            if diagonal:
                acc = carry = jnp.zeros((2 * QB, LANES), F32)
            else:
                acc, carry = acc_ref[p], carry_ref[p]
            acc, carry = tiles(p, j_hi, n, diagonal, acc, carry)
            acc_ref[p] = acc
            carry_ref[p] = carry
            cmax = carry if cmax is None else jnp.maximum(cmax, carry)
        return jnp.max(cmax)

    n_first, n_loop = ATTN_WINDOW_BLOCKS, ATTN_LOOP_BLOCKS
    n0 = jnp.minimum(i + 1, n_first)
    cmax0 = lax.switch(n0 - 1, [functools.partial(fold, i, n, True)
                                for n in range(1, n_first + 1)])
    j0 = i - n0

    def loop(n, j, cmax):
        def more(st):
            return (st[0] >= n - 1) & (st[1] >= ATTN_UNDERFLOW_LOG)

        def step(st):
            return st[0] - n, fold(st[0], n, False)

        return lax.while_loop(more, step, (j, cmax))

    j, cmax = j0, cmax0
    while n_loop >= 1:
        j, cmax = loop(n_loop, j, cmax)
        n_loop //= 2
    for p in range(n_pairs):
        acc = acc_ref[p]
        o_ref[0, :, p * LANES:(p + 1) * LANES] = jnp.where(lane < HEAD_DIM, acc[:QB], acc[QB:])


def _attention(q, k, v):
    B, S, W = q.shape
    return pl.pallas_call(
        _attn_kernel,
        grid=(B, S // Q_BLOCK),
        in_specs=[pl.BlockSpec((1, Q_BLOCK, W), lambda b, i: (b, i, 0)),
                  pl.BlockSpec((1, S, W), lambda b, i: (b, 0, 0)),
                  pl.BlockSpec((1, S, W), lambda b, i: (b, 0, 0))],
        out_specs=pl.BlockSpec((1, Q_BLOCK, W), lambda b, i: (b, i, 0)),
        out_shape=jax.ShapeDtypeStruct((B, S, W), F32),
        scratch_shapes=[pltpu.VMEM((W // LANES, 2 * Q_BLOCK, LANES), F32),
                        pltpu.VMEM((W // LANES, 2 * Q_BLOCK, LANES), F32)],
        compiler_params=_params(("arbitrary", "arbitrary")),
        name="attn",
    )(q, k, v)


PACK_ROWS = SUBLANES // 2


def _store_packed_rows(ref, value):
    n_rows, d = value.shape
    bits = lax.bitcast_convert_type(value.astype(BF16).astype(F32), jnp.uint32)
    words = (bits[:, :d // 2] >> 16) | bits[:, d // 2:]
    for c in range(PACK_ROWS):
        ref[pl.ds(c, n_rows, stride=PACK_ROWS), :] = words[:, c * LANES:(c + 1) * LANES]


def _load_packed_rows(ref, n_rows, lead=(), dtype=BF16):
    lo, hi = [], []
    for c in range(PACK_ROWS):
        w = ref[lead + (pl.ds(c, n_rows, stride=PACK_ROWS), slice(None))]
        lo.append(lax.bitcast_convert_type(w << 16, F32).astype(dtype))
        hi.append(lax.bitcast_convert_type(w & jnp.uint32(0xFFFF0000), F32).astype(dtype))
    return jnp.concatenate(lo + hi, axis=1)


def _outproj_kernel(x_ref, ya_ref, yl_ref, mod_ref, ga_ref, wo_ref, gf_ref, rw_ref, rb_ref,
                    x1_ref, h2_ref, idx_ref, prob_ref, rank_ref, cnt_ref, tri, run):
    W = SB_WIDTH
    ya = _rms(ya_ref[0], ga_ref[...]).astype(BF16)
    yl = yl_ref[0].astype(BF16)
    mix = (jnp.dot(ya, wo_ref[0:W, :], preferred_element_type=F32)
           + jnp.dot(yl, wo_ref[W:2 * W, :], preferred_element_type=F32))
    x1 = x_ref[0] + mod_ref[0, 2:3, :] * mix
    x1_ref[0] = x1
    h2 = _rms(x1, gf_ref[...]) * (1.0 + mod_ref[0, 4:5, :]) + mod_ref[0, 3:4, :]
    _store_packed_rows(h2_ref, h2)

    logits = lax.dot_general(rw_ref[...], h2, (((1,), (1,)), ((), ())), precision=HIGHEST,
                             preferred_element_type=F32) + rb_ref[...]
    n_exp = logits.shape[0]
    eid = lax.broadcasted_iota(jnp.int32, logits.shape, 0)
    vals, idxs = [], []
    for _ in range(TOP_K):
        m = jnp.max(logits, axis=0, keepdims=True)
        sel = jnp.min(jnp.where(logits == m, eid, n_exp), axis=0, keepdims=True)
        vals.append(m)
        idxs.append(sel)
        logits = jnp.where(eid == sel, -jnp.inf, logits)
    es = [jnp.exp(vv - vals[0]) for vv in vals]
    inv = 1.0 / (es[0] + es[1] + es[2] + es[3])

    first = (pl.program_id(0) == 0) & (pl.program_id(1) == 0)

    @pl.when(first)
    def _():
        ti = lax.broadcasted_iota(jnp.int32, tri.shape, 0)
        tj = lax.broadcasted_iota(jnp.int32, tri.shape, 1)
        tri[...] = jnp.where(ti < tj, 1.0, 0.0).astype(BF16)
        run[...] = jnp.zeros_like(run)

    hits = [eid == idxs[r] for r in range(TOP_K)]
    chosen = hits[0] | hits[1] | hits[2] | hits[3]
    cnt = jnp.where(chosen, 1.0, 0.0)
    before = jnp.dot(cnt.astype(BF16), tri[...], preferred_element_type=F32) + run[...]
    for r in range(TOP_K):
        idx_ref[0, r:r + 1, :] = idxs[r]
        prob_ref[0, r:r + 1, :] = es[r] * inv
        rank_ref[0, r:r + 1, :] = jnp.sum(jnp.where(hits[r], before, 0.0), axis=0,
                                          keepdims=True).astype(jnp.int32)
    run[...] = run[...] + jnp.sum(cnt, axis=1, keepdims=True)
    cnt_ref[...] = jnp.broadcast_to(run[...], cnt_ref.shape).astype(jnp.int32)


def _outproj(x, ya, yl, mod3, ga, w_out_b, gf, router_wt, router_b, ts=512):
    B, S, D = x.shape
    W = ya.shape[2]
    NE = router_wt.shape[0]
    rowd = pl.BlockSpec((1, ts, D), lambda b, s: (b, s, 0))
    roww = pl.BlockSpec((1, ts, W), lambda b, s: (b, s, 0))
    sel = pl.BlockSpec((1, TOP_K, ts), lambda b, s: (b, 0, s))
    return pl.pallas_call(
        _outproj_kernel,
        grid=(B, S // ts),
        in_specs=[rowd, roww, roww,
                  pl.BlockSpec((1, N_MOD, D), lambda b, s: (b, 0, 0)),
                  pl.BlockSpec((1, W), lambda b, s: (0, 0)),
                  pl.BlockSpec((2 * W, D), lambda b, s: (0, 0)),
                  pl.BlockSpec((1, D), lambda b, s: (0, 0)),
                  pl.BlockSpec((NE, D), lambda b, s: (0, 0)),
                  pl.BlockSpec((NE, 1), lambda b, s: (0, 0))],
        out_specs=[rowd,
                   pl.BlockSpec((ts * PACK_ROWS, LANES), lambda b, s: (b * (S // ts) + s, 0)),
                   sel, sel, sel, pl.BlockSpec((NE, LANES), lambda b, s: (0, 0))],
        out_shape=[jax.ShapeDtypeStruct((B, S, D), F32),
                   jax.ShapeDtypeStruct((B * S * PACK_ROWS, LANES), jnp.uint32),
                   jax.ShapeDtypeStruct((B, TOP_K, S), jnp.int32),
                   jax.ShapeDtypeStruct((B, TOP_K, S), F32),
                   jax.ShapeDtypeStruct((B, TOP_K, S), jnp.int32),
                   jax.ShapeDtypeStruct((NE, LANES), jnp.int32)],
        scratch_shapes=[pltpu.VMEM((ts, ts), BF16), pltpu.VMEM((NE, 1), F32)],
        compiler_params=_params(("arbitrary", "arbitrary")),
        name="outproj",
    )(x, ya, yl, mod3, ga.reshape(1, W), w_out_b, gf.reshape(1, D), router_wt,
      router_b.reshape(NE, 1))


def _dispatch_kernel(p0_ref, p1_ref, p2_ref, p3_ref, fill_ref, h_ref, xs_ref, zbuf, sem, zsem):
    TT = h_ref.shape[0] // PACK_ROWS
    n_fill = fill_ref.shape[0]

    @pl.when(pl.program_id(0) == 0)
    def _():
        zbuf[...] = jnp.zeros_like(zbuf)

        def fill_copy(e):
            start = pl.multiple_of(jnp.maximum(fill_ref[e], 0), PACK_ROWS)
            return pltpu.make_async_copy(zbuf, xs_ref.at[pl.ds(start, zbuf.shape[0]), :], zsem)

        def start(e, c):
            @pl.when(fill_ref[e] >= 0)
            def _():
                fill_copy(e).start()
            return c

        def wait(e, c):
            @pl.when(fill_ref[e] >= 0)
            def _():
                fill_copy(e).wait()
            return c

        lax.fori_loop(0, n_fill, start, 0)
        lax.fori_loop(0, n_fill, wait, 0)

    pos_refs = (p0_ref, p1_ref, p2_ref, p3_ref)

    def row_copy(t, r):
        src = pl.multiple_of(t * PACK_ROWS, PACK_ROWS)
        dst = pl.multiple_of(pos_refs[r][t], PACK_ROWS)
        return pltpu.make_async_copy(h_ref.at[pl.ds(src, PACK_ROWS), :],
                                     xs_ref.at[pl.ds(dst, PACK_ROWS), :], sem)

    def issue(g, c):
        for tt in range(ROW_DMA_UNROLL):
            for r in range(TOP_K):
                row_copy(g * ROW_DMA_UNROLL + tt, r).start(priority=(tt * TOP_K + r) % 2)
        return c

    lax.fori_loop(0, TT // ROW_DMA_UNROLL, issue, 0)
    for r in range(TOP_K):
        pltpu.make_async_copy(h_ref, xs_ref.at[pl.ds(0, TT * PACK_ROWS), :], sem).wait()


def _slot_specs(tile, tiles_per_batch):
    def spec(r):
        return pl.BlockSpec(
            (tile,),
            lambda i: ((i // tiles_per_batch * TOP_K + r) * tiles_per_batch
                       + i % tiles_per_batch,),
            memory_space=pltpu.SMEM)
    return [spec(r) for r in range(TOP_K)]


def _dispatch(h2t, pos_flat, fill_start, cap, seq_len):
    TT = min(DISPATCH_TILE, seq_len)
    assert seq_len % TT == 0
    tiles_per_batch = seq_len // TT
    n_tiles = h2t.shape[0] // (TT * PACK_ROWS)
    return pl.pallas_call(
        _dispatch_kernel,
        grid=(n_tiles,),
        in_specs=_slot_specs(TT, tiles_per_batch)
        + [pl.BlockSpec(memory_space=pltpu.SMEM),
           pl.BlockSpec((TT * PACK_ROWS, LANES), lambda i: (i, 0))],
        out_specs=pl.BlockSpec(memory_space=pl.ANY),
        out_shape=jax.ShapeDtypeStruct((cap * PACK_ROWS, LANES), jnp.uint32),
        scratch_shapes=[pltpu.VMEM((EXPERT_BLOCK * PACK_ROWS, LANES), jnp.uint32),
                        pltpu.SemaphoreType.DMA, pltpu.SemaphoreType.DMA],
        compiler_params=_params(("arbitrary",)),
        name="dispatch",
    )(pos_flat, pos_flat, pos_flat, pos_flat, fill_start, h2t)


def _expert_kernel(first_ref, count_ref, xs_ref, wg_ref, bg_ref, wu_ref, bu_ref, wd_ref,
                   bd_ref, ys_ref, wgb, wub, wdb, xbuf, ybuf, xsem, ysem):
    e = pl.program_id(0)
    n_exp = pl.num_programs(0)
    rows = ybuf.shape[1]
    first, count = first_ref[e], count_ref[e]

    wgb[...] = wg_ref[0].astype(BF16)
    wub[...] = wu_ref[0].astype(BF16)
    wdb[...] = wd_ref[0].astype(BF16)

    def block(ref, buf, j):
        n = buf.shape[1]
        return ref.at[pl.ds(pl.multiple_of((first + j) * n, n), n), :]

    def x_copy(j, slot):
        return pltpu.make_async_copy(block(xs_ref, xbuf, j), xbuf.at[slot], xsem.at[slot])

    def y_copy(j, slot):
        return pltpu.make_async_copy(ybuf.at[slot], block(ys_ref, ybuf, j), ysem.at[slot])

    @pl.when(count > 0)
    def _():
        x_copy(0, 0).start()

    def step(j, c):
        slot = j % 2
        x_copy(j, slot).wait()

        @pl.when(j + 1 < count)
        def _():
            x_copy(j + 1, 1 - slot).start()

        @pl.when(j >= 2)
        def _():
            y_copy(j - 2, slot).wait()

        xb = _load_packed_rows(xbuf, EXPERT_BLOCK, (slot,))
        g = jnp.minimum(jnp.dot(xb, wgb[...], preferred_element_type=F32) + bg_ref[0],
                        SWIGLU_LIMIT)
        u = jnp.clip(jnp.dot(xb, wub[...], preferred_element_type=F32) + bu_ref[0],
                     -SWIGLU_LIMIT, SWIGLU_LIMIT)
        act = (u + 1.0) * (g * jax.nn.sigmoid(SWIGLU_ALPHA * g))
        _store_packed_rows(ybuf.at[slot], jnp.dot(act.astype(BF16), wdb[...],
                                                  preferred_element_type=F32) + bd_ref[0])
        y_copy(j, slot).start()
        return c

    lax.fori_loop(0, count, step, 0)

    @pl.when(count >= 2)
    def _():
        y_copy(count - 2, count % 2).wait()

    @pl.when(count >= 1)
    def _():
        y_copy(count - 1, (count - 1) % 2).wait()

    @pl.when(e == n_exp - 1)
    def _():
        spare_first, spare_count = first_ref[n_exp], count_ref[n_exp]
        ybuf[0] = jnp.zeros(ybuf.shape[1:], ybuf.dtype)

        def spare_copy(j):
            dst = pl.ds(pl.multiple_of((spare_first + j) * rows, rows), rows)
            return pltpu.make_async_copy(ybuf.at[0], ys_ref.at[dst, :], ysem.at[0])

        def start(j, c):
            spare_copy(j).start()
            return c

        def wait(j, c):
            spare_copy(j).wait()
            return c

        lax.fori_loop(0, spare_count, start, 0)
        lax.fori_loop(0, spare_count, wait, 0)


def _experts(xs, first_block, block_count, wg, bg, wu, bu, wd, bd):
    NE, D, DE = wg.shape
    cap = xs.shape[0] // PACK_ROWS

    def wsel(e, first, count):
        return (e, 0, 0)

    grid_spec = pltpu.PrefetchScalarGridSpec(
        num_scalar_prefetch=2,
        grid=(NE,),
        in_specs=[pl.BlockSpec(memory_space=pl.ANY),
                  pl.BlockSpec((1, D, DE), wsel), pl.BlockSpec((1, 1, DE), wsel),
                  pl.BlockSpec((1, D, DE), wsel), pl.BlockSpec((1, 1, DE), wsel),
                  pl.BlockSpec((1, DE, D), wsel), pl.BlockSpec((1, 1, D), wsel)],
        out_specs=pl.BlockSpec(memory_space=pl.ANY),
        scratch_shapes=[pltpu.VMEM((D, DE), BF16), pltpu.VMEM((D, DE), BF16),
                        pltpu.VMEM((DE, D), BF16),
                        pltpu.VMEM((2, EXPERT_BLOCK * PACK_ROWS, LANES), jnp.uint32),
                        pltpu.VMEM((2, EXPERT_BLOCK * PACK_ROWS, LANES), jnp.uint32),
                        pltpu.SemaphoreType.DMA((2,)), pltpu.SemaphoreType.DMA((2,))],
    )
    return pl.pallas_call(
        _expert_kernel,
        grid_spec=grid_spec,
        out_shape=jax.ShapeDtypeStruct((cap * PACK_ROWS, LANES), jnp.uint32),
        compiler_params=_params(("arbitrary",)),
        name="experts",
    )(first_block, block_count, xs, wg, bg.reshape(NE, 1, DE), wu, bu.reshape(NE, 1, DE), wd,
      bd.reshape(NE, 1, D))


def _combine_kernel(c0_ref, c1_ref, c2_ref, c3_ref, n0_ref, n1_ref, n2_ref, n3_ref,
                    x1_ref, p_ref, mod_ref, g_ref, ys_ref, o_ref, buf, sem):
    TT = x1_ref.shape[0]
    i = pl.program_id(0)
    R = PACK_ROWS

    def gather(pos_refs, s):
        def issue(g, c):
            for tt in range(ROW_DMA_UNROLL):
                t = g * ROW_DMA_UNROLL + tt
                for r in range(TOP_K):
                    pltpu.make_async_copy(
                        ys_ref.at[pl.ds(pl.multiple_of(pos_refs[r][t], R), R), :],
                        buf.at[s, r, pl.ds(pl.multiple_of(t * R, R), R), :],
                        sem.at[s]).start(priority=(tt * TOP_K + r) % 2)
            return c
        lax.fori_loop(0, TT // ROW_DMA_UNROLL, issue, 0)

    @pl.when(i == 0)
    def _():
        gather((c0_ref, c1_ref, c2_ref, c3_ref), 0)

    def reduce_tile(s):
        @pl.when(i + 1 < pl.num_programs(0))
        def _():
            gather((n0_ref, n1_ref, n2_ref, n3_ref), 1 - s)

        for r in range(TOP_K):
            pltpu.make_async_copy(ys_ref.at[pl.ds(0, TT * R), :], buf.at[s, r],
                                  sem.at[s]).wait()
        p = p_ref[...]
        moe = p[:, 0:1] * _load_packed_rows(buf, TT, (s, 0), F32)
        for r in range(1, TOP_K):
            moe = moe + p[:, r:r + 1] * _load_packed_rows(buf, TT, (s, r), F32)
        x2 = x1_ref[...] + mod_ref[0, 5:6, :] * moe
        o_ref[...] = _rms(x2, g_ref[...])

    for s in range(2):
        pl.when(i % 2 == s)(functools.partial(reduce_tile, s))


def _combine(x1f, probs, pos_flat, mod3, g, ys, tiles_per_batch):
    N, D = x1f.shape
    TT = COMBINE_TILE
    n_tiles = N // TT

    def next_spec(r):
        def index(i):
            j = jnp.minimum(i + 1, n_tiles - 1)
            return ((j // tiles_per_batch * TOP_K + r) * tiles_per_batch + j % tiles_per_batch,)
        return pl.BlockSpec((TT,), index, memory_space=pltpu.SMEM)

    return pl.pallas_call(
        _combine_kernel,
        grid=(n_tiles,),
        in_specs=_slot_specs(TT, tiles_per_batch) + [next_spec(r) for r in range(TOP_K)]
        + [pl.BlockSpec((TT, D), lambda i: (i, 0)),
           pl.BlockSpec((TT, TOP_K), lambda i: (i, 0)),
           pl.BlockSpec((1, N_MOD, D), lambda i: (i // tiles_per_batch, 0, 0)),
           pl.BlockSpec((1, D), lambda i: (0, 0)),
           pl.BlockSpec(memory_space=pl.ANY)],
        out_specs=pl.BlockSpec((TT, D), lambda i: (i, 0)),
        out_shape=jax.ShapeDtypeStruct((N, D), F32),
        scratch_shapes=[pltpu.VMEM((2, TOP_K, TT * PACK_ROWS, LANES), jnp.uint32),
                        pltpu.SemaphoreType.DMA((2,))],
        compiler_params=_params(("arbitrary",)),
        name="combine",
    )(*([pos_flat] * (2 * TOP_K)), x1f, probs, mod3, g.reshape(1, D), ys)


def _routing(idx, rank, counts):
    B, K, S = idx.shape
    n_exp = counts.shape[0]
    TM = EXPERT_BLOCK
    n_assign = B * S * K
    padded = ((counts + TM - 1) // TM) * TM
    pad_ends = jnp.cumsum(padded)
    pad_starts = pad_ends - padded
    experts = jnp.arange(n_exp, dtype=jnp.int32).reshape(n_exp, 1, 1, 1)
    base = jnp.sum(jnp.where(idx[None] == experts, pad_starts.reshape(n_exp, 1, 1, 1), 0),
                   axis=0)
    pos = (base + rank).astype(jnp.int32).reshape(n_assign)
    n_blocks = n_assign // TM + n_exp
    first_block = jnp.concatenate([pad_starts, pad_ends[-1:]]) // TM
    block_count = jnp.concatenate([padded // TM, n_blocks - pad_ends[-1:] // TM])
    tail_start = jnp.where(padded > 0, pad_ends - TM, -1)
    spare = pad_ends[-1] + jnp.arange(n_exp, dtype=jnp.int32) * TM
    fill_start = jnp.concatenate(
        [tail_start, jnp.where(spare < n_blocks * TM, spare, -1)]).astype(jnp.int32)
    return (pos, first_block.astype(jnp.int32), block_count.astype(jnp.int32),
            fill_start, n_blocks * TM)


def kernel(x, c, ada_w, ada_b, mix_norm_g, w_in, conv_w, conv_b, rg_w, rg_b, ig_w, ig_b,
           lru_lambda, attn_out_g, lru_out_g, w_out, ffn_norm_g, router_w, router_b,
           exp_w_gate, exp_b_gate, exp_w_up, exp_b_up, exp_w_down, exp_b_down, final_norm_g):
    B, S, D = x.shape
    depth = ada_w.shape[0]
    assert S % COMBINE_TILE == 0 and S % 512 == 0
    assert D == SUBLANES * LANES
    for l in range(depth):
        mod3 = _ada(c, ada_w[l], ada_b[l]).reshape(B, N_MOD, D)
        q, k, v, yl = _inproj(x, mod3, mix_norm_g[l], w_in[l].astype(BF16), conv_w[l],
                              conv_b[l], rg_w[l], rg_b[l], ig_w[l], ig_b[l], lru_lambda[l],
                              lru_out_g[l])
        ya = _attention(q, k, v)
        x1, h2, idx, prob, rank, cnt = _outproj(
            x, ya, yl, mod3, attn_out_g[l], w_out[l].astype(BF16), ffn_norm_g[l],
            router_w[l].T, router_b[l])
        pos, first_block, block_count, fill_start, cap = _routing(idx, rank, cnt[:, 0])
        xs = _dispatch(h2, pos * PACK_ROWS, fill_start * PACK_ROWS, cap, S)
        ys = _experts(xs, first_block, block_count, exp_w_gate[l], exp_b_gate[l],
                      exp_w_up[l], exp_b_up[l], exp_w_down[l], exp_b_down[l])
        probs = jnp.transpose(prob, (0, 2, 1)).reshape(B * S, TOP_K)
        assert depth == 1
        x = _combine(x1.reshape(B * S, D), probs, pos * PACK_ROWS, mod3, final_norm_g, ys,
                     S // COMBINE_TILE).reshape(B, S, D)
    return x
```

```python
import functools

import jax
import jax.numpy as jnp
from jax import lax
from jax.experimental import pallas as pl
from jax.experimental.pallas import tpu as pltpu

F32 = jnp.float32
BF16 = jnp.bfloat16
HIGHEST = lax.Precision.HIGHEST

EPS = 1e-6
N_MOD = 6
SB_HEADS = 8
HEAD_DIM = 64
SB_WIDTH = SB_HEADS * HEAD_DIM
LRU_BLOCKS = 8
CONV_WIDTH = 4
LRU_C = 8.0
TOP_K = 4
SWIGLU_LIMIT = 7.0
SWIGLU_ALPHA = 1.702
LOG2E = 1.4426950408889634

LANES = 128
SUBLANES = 8
VMEM_LIMIT = 56 * 1024 * 1024

Q_BLOCK = 128
K_BLOCK = 128
ATTN_WINDOW_BLOCKS = 5
ATTN_LOOP_BLOCKS = 2
ATTN_UNDERFLOW_LOG = -110.0
EXPERT_BLOCK = 512
DISPATCH_TILE = 4096
COMBINE_TILE = 512
ROW_DMA_UNROLL = 8
LRU_SCAN_ROWS = 256


def _params(sem):
    return pltpu.CompilerParams(dimension_semantics=sem, vmem_limit_bytes=VMEM_LIMIT)


def _ada_kernel(c_ref, w_ref, b_ref, o_ref):
    c = c_ref[...]
    ca = c * jax.nn.sigmoid(c)
    o_ref[...] = jnp.dot(ca, w_ref[...], precision=HIGHEST,
                         preferred_element_type=F32) + b_ref[...]


def _ada(c, ada_w, ada_b):
    B, D = c.shape
    E = ada_w.shape[1]
    tn = 1024
    return pl.pallas_call(
        _ada_kernel,
        grid=(E // tn,),
        in_specs=[pl.BlockSpec((B, D), lambda j: (0, 0)),
                  pl.BlockSpec((D, tn), lambda j: (0, j)),
                  pl.BlockSpec((1, tn), lambda j: (0, j))],
        out_specs=pl.BlockSpec((B, tn), lambda j: (0, j)),
        out_shape=jax.ShapeDtypeStruct((B, E), F32),
        compiler_params=_params(("arbitrary",)),
        name="ada",
    )(c, ada_w, ada_b.reshape(1, E))


def _rms(x, g):
    ms = jnp.mean(x * x, axis=-1, keepdims=True)
    return x * lax.rsqrt(ms + EPS) * g


def _softplus(x):
    return jnp.maximum(x, 0.0) + jnp.log1p(jnp.exp(-jnp.abs(x)))


def _gelu_tanh(x):
    return 0.5 * x * (1.0 + jnp.tanh(0.7978845608028654 * (x + 0.044715 * x * x * x)))


LRU_PAD = 8


def _lru_tile(x, gate, cw_ref, cb_ref, wr_ref, br_ref, wi_ref, bi_ref, lam_ref, g_ref,
              xext, hc):
    T = x.shape[0]
    xext[LRU_PAD:LRU_PAD + T, :] = x
    xc = cb_ref[...] + cw_ref[CONV_WIDTH - 1:CONV_WIDTH, :] * x
    for j in range(CONV_WIDTH - 1):
        back = CONV_WIDTH - 1 - j
        xc = xc + cw_ref[j:j + 1, :] * xext[LRU_PAD - back:LRU_PAD - back + T, :]
    xext[0:LRU_PAD, :] = xext[T:T + LRU_PAD, :]

    xb = xc.astype(BF16)
    r = jax.nn.sigmoid(jnp.dot(xb, wr_ref[...], preferred_element_type=F32) + br_ref[...])
    ig = jax.nn.sigmoid(jnp.dot(xb, wi_ref[...], preferred_element_type=F32) + bi_ref[...])
    log_a = (-LRU_C) * r * _softplus(-lam_ref[...])
    a = jnp.exp(log_a)
    b = jnp.sqrt(-jnp.tanh(log_a) * (a * a + 1.0)) * (ig * xc)

    G = T // SUBLANES
    a = a.reshape(G, SUBLANES, -1)
    b = b.reshape(G, SUBLANES, -1)
    rows = lax.broadcasted_iota(jnp.int32, a.shape, 1)
    d = 1
    while d < SUBLANES:
        keep = rows >= d
        a_prev = jnp.where(keep, pltpu.roll(a, d, 1), 1.0)
        b_prev = jnp.where(keep, pltpu.roll(b, d, 1), 0.0)
        b = a * b_prev + b
        a = a * a_prev
        d *= 2
    prev = hc[...]
    groups = []
    for g in range(G):
        hg = a[g] * prev + b[g]
        prev = hg[SUBLANES - 1:SUBLANES, :]
        groups.append(hg)
    hc[...] = prev
    h = jnp.concatenate(groups, axis=0)
    return _rms(h * _gelu_tanh(gate), g_ref[...])


def _inproj_kernel(x_ref, mod_ref, g_ref, w_ref, cw_ref, cb_ref, wr_ref, br_ref, wi_ref,
                   bi_ref, lam_ref, gl_ref, q_ref, k_ref, v_ref, yl_ref, xext, hc):
    @pl.when(pl.program_id(1) == 0)
    def _():
        xext[0:LRU_PAD, :] = jnp.zeros((LRU_PAD, xext.shape[1]), F32)
        hc[...] = jnp.zeros_like(hc)

    x = x_ref[0]
    h = _rms(x, g_ref[...]) * (1.0 + mod_ref[0, 1:2, :]) + mod_ref[0, 0:1, :]
    hb = h.astype(BF16)
    W = SB_WIDTH

    def proj(c):
        return jnp.dot(hb, w_ref[:, c * W:(c + 1) * W], preferred_element_type=F32)

    xr, gr = proj(3), proj(4)
    T = LRU_SCAN_ROWS
    for t in range(x.shape[0] // T):
        rows = slice(t * T, (t + 1) * T)
        yl_ref[0, rows, :] = _lru_tile(xr[rows], gr[rows], cw_ref, cb_ref, wr_ref, br_ref,
                                       wi_ref, bi_ref, lam_ref, gl_ref, xext, hc)
    q_ref[0] = (proj(0) * (HEAD_DIM ** -0.5)).astype(BF16)
    k_ref[0] = proj(1).astype(BF16)
    v_ref[0] = proj(2).astype(BF16)


def _block_diag(w):
    H, I, J = w.shape
    eye = jnp.eye(H, dtype=w.dtype)
    return (w[:, :, None, :] * eye[:, None, :, None]).reshape(H * I, H * J)


def _inproj(x, mod3, g, w_in_b, conv_w, conv_b, rg_w, rg_b, ig_w, ig_b, lam, g_lru, ts=512):
    B, S, D = x.shape
    E = w_in_b.shape[1]
    W = SB_WIDTH
    row = pl.BlockSpec((1, ts, W), lambda b, s: (b, s, 0))
    vec = pl.BlockSpec((1, W), lambda b, s: (0, 0))
    mat = pl.BlockSpec((W, W), lambda b, s: (0, 0))
    return pl.pallas_call(
        _inproj_kernel,
        grid=(B, S // ts),
        in_specs=[pl.BlockSpec((1, ts, D), lambda b, s: (b, s, 0)),
                  pl.BlockSpec((1, N_MOD, D), lambda b, s: (b, 0, 0)),
                  pl.BlockSpec((1, D), lambda b, s: (0, 0)),
                  pl.BlockSpec((D, E), lambda b, s: (0, 0)),
                  pl.BlockSpec((CONV_WIDTH, W), lambda b, s: (0, 0)), vec,
                  mat, vec, mat, vec, vec, vec],
        out_specs=[row, row, row, row],
        out_shape=[jax.ShapeDtypeStruct((B, S, W), BF16)] * 3
        + [jax.ShapeDtypeStruct((B, S, W), F32)],
        scratch_shapes=[pltpu.VMEM((LRU_SCAN_ROWS + LRU_PAD, W), F32), pltpu.VMEM((1, W), F32)],
        compiler_params=_params(("arbitrary", "arbitrary")),
        name="inproj",
    )(x, mod3, g.reshape(1, D), w_in_b, conv_w, conv_b.reshape(1, W),
      _block_diag(rg_w).astype(BF16), rg_b.reshape(1, W), _block_diag(ig_w).astype(BF16),
      ig_b.reshape(1, W), lam.reshape(1, W), g_lru.reshape(1, W))


def _attn_kernel(q_ref, k_ref, v_ref, o_ref, acc_ref, carry_ref):
    i = pl.program_id(1)
    QB, KB = Q_BLOCK, K_BLOCK
    n_pairs = q_ref.shape[2] // LANES
    lane = lax.broadcasted_iota(jnp.int32, (QB, LANES), 1)

    def stacked_q(p):
        q = q_ref[0, :, p * LANES:(p + 1) * LANES]
        zero = jnp.zeros_like(q)
        return jnp.concatenate([jnp.where(lane < HEAD_DIM, q, zero),
                                jnp.where(lane >= HEAD_DIM, q, zero)], axis=0)

    qs = [stacked_q(p) for p in range(n_pairs)]

    uj = lax.broadcasted_iota(jnp.int32, (2 * KB, KB + LANES), 0) & (KB - 1)
    us = lax.broadcasted_iota(jnp.int32, (2 * KB, KB + LANES), 1)
    u2 = jnp.where((us >= KB) | (uj > us), -1.0, 0.0).astype(BF16)

    row = lax.broadcasted_iota(jnp.int32, (2 * QB, KB), 0) & (QB - 1)
    col = lax.broadcasted_iota(jnp.int32, (2 * QB, KB), 1)
    causal = col < row

    def tiles(p, j_hi, n, diagonal, acc, carry):
        start = pl.multiple_of((j_hi - (n - 1)) * KB, KB)
        kw = k_ref[0, pl.ds(start, n * KB), p * LANES:(p + 1) * LANES]
        vw = v_ref[0, pl.ds(start, n * KB), p * LANES:(p + 1) * LANES]
        z = lax.dot_general(qs[p], kw, (((1,), (1,)), ((), ())), preferred_element_type=F32)
        softplus = jnp.maximum(z, 0.0) + jnp.log(1.0 + jnp.exp2(jnp.abs(z) * (-LOG2E)))
        log_beta = z - softplus
        ws = [None] * n
        for c in reversed(range(n)):
            sp = softplus[:, c * KB:(c + 1) * KB]
            masked = diagonal and c == n - 1
            if masked:
                sp = jnp.where(causal, sp, 0.0)
            hi_f = lax.bitcast_convert_type(
                lax.bitcast_convert_type(sp, jnp.uint32) & jnp.uint32(0xFFFF0000), F32)
            hi_lo = jnp.concatenate([hi_f.astype(BF16), (sp - hi_f).astype(BF16)], axis=1)
            sums = jnp.dot(hi_lo, u2, preferred_element_type=F32)
            w = jnp.exp2((log_beta[:, c * KB:(c + 1) * KB] + sums[:, :KB] + carry) * LOG2E)
            if masked:
                w = jnp.where(causal, w, 0.0)
            ws[c] = w.astype(BF16)
            carry = carry + sums[:, KB:]
        w_all = ws[0] if n == 1 else jnp.concatenate(ws, axis=1)
        acc = acc + jnp.dot(w_all, vw, preferred_element_type=F32)
        return acc, carry

    def fold(j_hi, n, diagonal):
        cmax = None
        for p in range(n_pairs):
            if diagonal:
                acc = carry = jnp.zeros((2 * QB, LANES), F32)
            else:
                acc, carry = acc_ref[p], carry_ref[p]
            acc, carry = tiles(p, j_hi, n, diagonal, acc, carry)
            acc_ref[p] = acc
            carry_ref[p] = carry
            cmax = carry if cmax is None else jnp.maximum(cmax, carry)
        return jnp.max(cmax)

    n_first, n_loop = ATTN_WINDOW_BLOCKS, ATTN_LOOP_BLOCKS
    n0 = jnp.minimum(i + 1, n_first)
    cmax0 = lax.switch(n0 - 1, [functools.partial(fold, i, n, True)
                                for n in range(1, n_first + 1)])
    j0 = i - n0

    def loop(n, j, cmax):
        def more(st):
            return (st[0] >= n - 1) & (st[1] >= ATTN_UNDERFLOW_LOG)

        def step(st):
            return st[0] - n, fold(st[0], n, False)

        return lax.while_loop(more, step, (j, cmax))

    j, cmax = j0, cmax0
    while n_loop >= 1:
        j, cmax = loop(n_loop, j, cmax)
        n_loop //= 2
    for p in range(n_pairs):
        acc = acc_ref[p]
        o_ref[0, :, p * LANES:(p + 1) * LANES] = jnp.where(lane < HEAD_DIM, acc[:QB], acc[QB:])


def _attention(q, k, v):
    B, S, W = q.shape
    return pl.pallas_call(
        _attn_kernel,
        grid=(B, S // Q_BLOCK),
        in_specs=[pl.BlockSpec((1, Q_BLOCK, W), lambda b, i: (b, i, 0)),
                  pl.BlockSpec((1, S, W), lambda b, i: (b, 0, 0)),
                  pl.BlockSpec((1, S, W), lambda b, i: (b, 0, 0))],
        out_specs=pl.BlockSpec((1, Q_BLOCK, W), lambda b, i: (b, i, 0)),
        out_shape=jax.ShapeDtypeStruct((B, S, W), F32),
        scratch_shapes=[pltpu.VMEM((W // LANES, 2 * Q_BLOCK, LANES), F32),
                        pltpu.VMEM((W // LANES, 2 * Q_BLOCK, LANES), F32)],
        compiler_params=_params(("arbitrary", "arbitrary")),
        name="attn",
    )(q, k, v)


PACK_ROWS = SUBLANES // 2


def _store_packed_rows(ref, value):
    n_rows, d = value.shape
    bits = lax.bitcast_convert_type(value.astype(BF16).astype(F32), jnp.uint32)
    words = (bits[:, :d // 2] >> 16) | bits[:, d // 2:]
    for c in range(PACK_ROWS):
        ref[pl.ds(c, n_rows, stride=PACK_ROWS), :] = words[:, c * LANES:(c + 1) * LANES]


def _load_packed_rows(ref, n_rows, lead=(), dtype=BF16):
    lo, hi = [], []
    for c in range(PACK_ROWS):
        w = ref[lead + (pl.ds(c, n_rows, stride=PACK_ROWS), slice(None))]
        lo.append(lax.bitcast_convert_type(w << 16, F32).astype(dtype))
        hi.append(lax.bitcast_convert_type(w & jnp.uint32(0xFFFF0000), F32).astype(dtype))
    return jnp.concatenate(lo + hi, axis=1)


def _outproj_kernel(x_ref, ya_ref, yl_ref, mod_ref, ga_ref, wo_ref, gf_ref, rw_ref, rb_ref,
                    x1_ref, h2_ref, idx_ref, prob_ref, rank_ref, cnt_ref, tri, run):
    W = SB_WIDTH
    ya = _rms(ya_ref[0], ga_ref[...]).astype(BF16)
    yl = yl_ref[0].astype(BF16)
    mix = (jnp.dot(ya, wo_ref[0:W, :], preferred_element_type=F32)
           + jnp.dot(yl, wo_ref[W:2 * W, :], preferred_element_type=F32))
    x1 = x_ref[0] + mod_ref[0, 2:3, :] * mix
    x1_ref[0] = x1
    h2 = _rms(x1, gf_ref[...]) * (1.0 + mod_ref[0, 4:5, :]) + mod_ref[0, 3:4, :]
    _store_packed_rows(h2_ref, h2)

    logits = lax.dot_general(rw_ref[...], h2, (((1,), (1,)), ((), ())), precision=HIGHEST,
                             preferred_element_type=F32) + rb_ref[...]
    n_exp = logits.shape[0]
    eid = lax.broadcasted_iota(jnp.int32, logits.shape, 0)
    vals, idxs = [], []
    for _ in range(TOP_K):
        m = jnp.max(logits, axis=0, keepdims=True)
        sel = jnp.min(jnp.where(logits == m, eid, n_exp), axis=0, keepdims=True)
        vals.append(m)
        idxs.append(sel)
        logits = jnp.where(eid == sel, -jnp.inf, logits)
    es = [jnp.exp(vv - vals[0]) for vv in vals]
    inv = 1.0 / (es[0] + es[1] + es[2] + es[3])

    first = (pl.program_id(0) == 0) & (pl.program_id(1) == 0)

    @pl.when(first)
    def _():
        ti = lax.broadcasted_iota(jnp.int32, tri.shape, 0)
        tj = lax.broadcasted_iota(jnp.int32, tri.shape, 1)
        tri[...] = jnp.where(ti < tj, 1.0, 0.0).astype(BF16)
        run[...] = jnp.zeros_like(run)

    hits = [eid == idxs[r] for r in range(TOP_K)]
    chosen = hits[0] | hits[1] | hits[2] | hits[3]
    cnt = jnp.where(chosen, 1.0, 0.0)
    before = jnp.dot(cnt.astype(BF16), tri[...], preferred_element_type=F32) + run[...]
    for r in range(TOP_K):
        idx_ref[0, r:r + 1, :] = idxs[r]
        prob_ref[0, r:r + 1, :] = es[r] * inv
        rank_ref[0, r:r + 1, :] = jnp.sum(jnp.where(hits[r], before, 0.0), axis=0,
                                          keepdims=True).astype(jnp.int32)
    run[...] = run[...] + jnp.sum(cnt, axis=1, keepdims=True)
    cnt_ref[...] = jnp.broadcast_to(run[...], cnt_ref.shape).astype(jnp.int32)


def _outproj(x, ya, yl, mod3, ga, w_out_b, gf, router_wt, router_b, ts=512):
    B, S, D = x.shape
    W = ya.shape[2]
    NE = router_wt.shape[0]
    rowd = pl.BlockSpec((1, ts, D), lambda b, s: (b, s, 0))
    roww = pl.BlockSpec((1, ts, W), lambda b, s: (b, s, 0))
    sel = pl.BlockSpec((1, TOP_K, ts), lambda b, s: (b, 0, s))
    return pl.pallas_call(
        _outproj_kernel,
        grid=(B, S // ts),
        in_specs=[rowd, roww, roww,
                  pl.BlockSpec((1, N_MOD, D), lambda b, s: (b, 0, 0)),
                  pl.BlockSpec((1, W), lambda b, s: (0, 0)),
                  pl.BlockSpec((2 * W, D), lambda b, s: (0, 0)),
                  pl.BlockSpec((1, D), lambda b, s: (0, 0)),
                  pl.BlockSpec((NE, D), lambda b, s: (0, 0)),
                  pl.BlockSpec((NE, 1), lambda b, s: (0, 0))],
        out_specs=[rowd,
                   pl.BlockSpec((ts * PACK_ROWS, LANES), lambda b, s: (b * (S // ts) + s, 0)),
                   sel, sel, sel, pl.BlockSpec((NE, LANES), lambda b, s: (0, 0))],
        out_shape=[jax.ShapeDtypeStruct((B, S, D), F32),
                   jax.ShapeDtypeStruct((B * S * PACK_ROWS, LANES), jnp.uint32),
                   jax.ShapeDtypeStruct((B, TOP_K, S), jnp.int32),
                   jax.ShapeDtypeStruct((B, TOP_K, S), F32),
                   jax.ShapeDtypeStruct((B, TOP_K, S), jnp.int32),
                   jax.ShapeDtypeStruct((NE, LANES), jnp.int32)],
        scratch_shapes=[pltpu.VMEM((ts, ts), BF16), pltpu.VMEM((NE, 1), F32)],
        compiler_params=_params(("arbitrary", "arbitrary")),
        name="outproj",
    )(x, ya, yl, mod3, ga.reshape(1, W), w_out_b, gf.reshape(1, D), router_wt,
      router_b.reshape(NE, 1))


def _dispatch_kernel(p0_ref, p1_ref, p2_ref, p3_ref, fill_ref, h_ref, xs_ref, zbuf, sem, zsem):
    TT = h_ref.shape[0] // PACK_ROWS
    n_fill = fill_ref.shape[0]

    @pl.when(pl.program_id(0) == 0)
    def _():
        zbuf[...] = jnp.zeros_like(zbuf)

        def fill_copy(e):
            start = pl.multiple_of(jnp.maximum(fill_ref[e], 0), PACK_ROWS)
            return pltpu.make_async_copy(zbuf, xs_ref.at[pl.ds(start, zbuf.shape[0]), :], zsem)

        def start(e, c):
            @pl.when(fill_ref[e] >= 0)
            def _():
                fill_copy(e).start()
            return c

        def wait(e, c):
            @pl.when(fill_ref[e] >= 0)
            def _():
                fill_copy(e).wait()
            return c

        lax.fori_loop(0, n_fill, start, 0)
        lax.fori_loop(0, n_fill, wait, 0)

    pos_refs = (p0_ref, p1_ref, p2_ref, p3_ref)

    def row_copy(t, r):
        src = pl.multiple_of(t * PACK_ROWS, PACK_ROWS)
        dst = pl.multiple_of(pos_refs[r][t], PACK_ROWS)
        return pltpu.make_async_copy(h_ref.at[pl.ds(src, PACK_ROWS), :],
                                     xs_ref.at[pl.ds(dst, PACK_ROWS), :], sem)

    def issue(g, c):
        for tt in range(ROW_DMA_UNROLL):
            for r in range(TOP_K):
                row_copy(g * ROW_DMA_UNROLL + tt, r).start(priority=(tt * TOP_K + r) % 2)
        return c

    lax.fori_loop(0, TT // ROW_DMA_UNROLL, issue, 0)
    for r in range(TOP_K):
        pltpu.make_async_copy(h_ref, xs_ref.at[pl.ds(0, TT * PACK_ROWS), :], sem).wait()


def _slot_specs(tile, tiles_per_batch):
    def spec(r):
        return pl.BlockSpec(
            (tile,),
            lambda i: ((i // tiles_per_batch * TOP_K + r) * tiles_per_batch
                       + i % tiles_per_batch,),
            memory_space=pltpu.SMEM)
    return [spec(r) for r in range(TOP_K)]


def _dispatch(h2t, pos_flat, fill_start, cap, seq_len):
    TT = min(DISPATCH_TILE, seq_len)
    assert seq_len % TT == 0
    tiles_per_batch = seq_len // TT
    n_tiles = h2t.shape[0] // (TT * PACK_ROWS)
    return pl.pallas_call(
        _dispatch_kernel,
        grid=(n_tiles,),
        in_specs=_slot_specs(TT, tiles_per_batch)
        + [pl.BlockSpec(memory_space=pltpu.SMEM),
           pl.BlockSpec((TT * PACK_ROWS, LANES), lambda i: (i, 0))],
        out_specs=pl.BlockSpec(memory_space=pl.ANY),
        out_shape=jax.ShapeDtypeStruct((cap * PACK_ROWS, LANES), jnp.uint32),
        scratch_shapes=[pltpu.VMEM((EXPERT_BLOCK * PACK_ROWS, LANES), jnp.uint32),
                        pltpu.SemaphoreType.DMA, pltpu.SemaphoreType.DMA],
        compiler_params=_params(("arbitrary",)),
        name="dispatch",
    )(pos_flat, pos_flat, pos_flat, pos_flat, fill_start, h2t)


def _expert_kernel(first_ref, count_ref, xs_ref, wg_ref, bg_ref, wu_ref, bu_ref, wd_ref,
                   bd_ref, ys_ref, wgb, wub, wdb, xbuf, ybuf, xsem, ysem):
    e = pl.program_id(0)
    n_exp = pl.num_programs(0)
    rows = ybuf.shape[1]
    first, count = first_ref[e], count_ref[e]

    wgb[...] = wg_ref[0].astype(BF16)
    wub[...] = wu_ref[0].astype(BF16)
    wdb[...] = wd_ref[0].astype(BF16)

    def block(ref, buf, j):
        n = buf.shape[1]
        return ref.at[pl.ds(pl.multiple_of((first + j) * n, n), n), :]

    def x_copy(j, slot):
        return pltpu.make_async_copy(block(xs_ref, xbuf, j), xbuf.at[slot], xsem.at[slot])

    def y_copy(j, slot):
        return pltpu.make_async_copy(ybuf.at[slot], block(ys_ref, ybuf, j), ysem.at[slot])

    @pl.when(count > 0)
    def _():
        x_copy(0, 0).start()

    def step(j, c):
        slot = j % 2
        x_copy(j, slot).wait()

        @pl.when(j + 1 < count)
        def _():
            x_copy(j + 1, 1 - slot).start()

        @pl.when(j >= 2)
        def _():
            y_copy(j - 2, slot).wait()

        xb = _load_packed_rows(xbuf, EXPERT_BLOCK, (slot,))
        g = jnp.minimum(jnp.dot(xb, wgb[...], preferred_element_type=F32) + bg_ref[0],
                        SWIGLU_LIMIT)
        u = jnp.clip(jnp.dot(xb, wub[...], preferred_element_type=F32) + bu_ref[0],
                     -SWIGLU_LIMIT, SWIGLU_LIMIT)
        act = (u + 1.0) * (g * jax.nn.sigmoid(SWIGLU_ALPHA * g))
        _store_packed_rows(ybuf.at[slot], jnp.dot(act.astype(BF16), wdb[...],
                                                  preferred_element_type=F32) + bd_ref[0])
        y_copy(j, slot).start()
        return c

    lax.fori_loop(0, count, step, 0)

    @pl.when(count >= 2)
    def _():
        y_copy(count - 2, count % 2).wait()

    @pl.when(count >= 1)
    def _():
        y_copy(count - 1, (count - 1) % 2).wait()

    @pl.when(e == n_exp - 1)
    def _():
        spare_first, spare_count = first_ref[n_exp], count_ref[n_exp]
        ybuf[0] = jnp.zeros(ybuf.shape[1:], ybuf.dtype)

        def spare_copy(j):
            dst = pl.ds(pl.multiple_of((spare_first + j) * rows, rows), rows)
            return pltpu.make_async_copy(ybuf.at[0], ys_ref.at[dst, :], ysem.at[0])

        def start(j, c):
            spare_copy(j).start()
            return c

        def wait(j, c):
            spare_copy(j).wait()
            return c

        lax.fori_loop(0, spare_count, start, 0)
        lax.fori_loop(0, spare_count, wait, 0)


def _experts(xs, first_block, block_count, wg, bg, wu, bu, wd, bd):
    NE, D, DE = wg.shape
    cap = xs.shape[0] // PACK_ROWS

    def wsel(e, first, count):
        return (e, 0, 0)

    grid_spec = pltpu.PrefetchScalarGridSpec(
        num_scalar_prefetch=2,
        grid=(NE,),
        in_specs=[pl.BlockSpec(memory_space=pl.ANY),
                  pl.BlockSpec((1, D, DE), wsel), pl.BlockSpec((1, 1, DE), wsel),
                  pl.BlockSpec((1, D, DE), wsel), pl.BlockSpec((1, 1, DE), wsel),
                  pl.BlockSpec((1, DE, D), wsel), pl.BlockSpec((1, 1, D), wsel)],
        out_specs=pl.BlockSpec(memory_space=pl.ANY),
        scratch_shapes=[pltpu.VMEM((D, DE), BF16), pltpu.VMEM((D, DE), BF16),
                        pltpu.VMEM((DE, D), BF16),
                        pltpu.VMEM((2, EXPERT_BLOCK * PACK_ROWS, LANES), jnp.uint32),
                        pltpu.VMEM((2, EXPERT_BLOCK * PACK_ROWS, LANES), jnp.uint32),
                        pltpu.SemaphoreType.DMA((2,)), pltpu.SemaphoreType.DMA((2,))],
    )
    return pl.pallas_call(
        _expert_kernel,
        grid_spec=grid_spec,
        out_shape=jax.ShapeDtypeStruct((cap * PACK_ROWS, LANES), jnp.uint32),
        compiler_params=_params(("arbitrary",)),
        name="experts",
    )(first_block, block_count, xs, wg, bg.reshape(NE, 1, DE), wu, bu.reshape(NE, 1, DE), wd,
      bd.reshape(NE, 1, D))


def _combine_kernel(c0_ref, c1_ref, c2_ref, c3_ref, n0_ref, n1_ref, n2_ref, n3_ref,
                    x1_ref, p_ref, mod_ref, g_ref, ys_ref, o_ref, buf, sem):
    TT = x1_ref.shape[0]
    i = pl.program_id(0)
    R = PACK_ROWS

    def gather(pos_refs, s):
        def issue(g, c):
            for tt in range(ROW_DMA_UNROLL):
                t = g * ROW_DMA_UNROLL + tt
                for r in range(TOP_K):
                    pltpu.make_async_copy(
                        ys_ref.at[pl.ds(pl.multiple_of(pos_refs[r][t], R), R), :],
                        buf.at[s, r, pl.ds(pl.multiple_of(t * R, R), R), :],
                        sem.at[s]).start(priority=(tt * TOP_K + r) % 2)
            return c
        lax.fori_loop(0, TT // ROW_DMA_UNROLL, issue, 0)

    @pl.when(i == 0)
    def _():
        gather((c0_ref, c1_ref, c2_ref, c3_ref), 0)

    def reduce_tile(s):
        @pl.when(i + 1 < pl.num_programs(0))
        def _():
            gather((n0_ref, n1_ref, n2_ref, n3_ref), 1 - s)

        for r in range(TOP_K):
            pltpu.make_async_copy(ys_ref.at[pl.ds(0, TT * R), :], buf.at[s, r],
                                  sem.at[s]).wait()
        p = p_ref[...]
        moe = p[:, 0:1] * _load_packed_rows(buf, TT, (s, 0), F32)
        for r in range(1, TOP_K):
            moe = moe + p[:, r:r + 1] * _load_packed_rows(buf, TT, (s, r), F32)
        x2 = x1_ref[...] + mod_ref[0, 5:6, :] * moe
        o_ref[...] = _rms(x2, g_ref[...])

    for s in range(2):
        pl.when(i % 2 == s)(functools.partial(reduce_tile, s))


def _combine(x1f, probs, pos_flat, mod3, g, ys, tiles_per_batch):
    N, D = x1f.shape
    TT = COMBINE_TILE
    n_tiles = N // TT

    def next_spec(r):
        def index(i):
            j = jnp.minimum(i + 1, n_tiles - 1)
            return ((j // tiles_per_batch * TOP_K + r) * tiles_per_batch + j % tiles_per_batch,)
        return pl.BlockSpec((TT,), index, memory_space=pltpu.SMEM)

    return pl.pallas_call(
        _combine_kernel,
        grid=(n_tiles,),
        in_specs=_slot_specs(TT, tiles_per_batch) + [next_spec(r) for r in range(TOP_K)]
        + [pl.BlockSpec((TT, D), lambda i: (i, 0)),
           pl.BlockSpec((TT, TOP_K), lambda i: (i, 0)),
           pl.BlockSpec((1, N_MOD, D), lambda i: (i // tiles_per_batch, 0, 0)),
           pl.BlockSpec((1, D), lambda i: (0, 0)),
           pl.BlockSpec(memory_space=pl.ANY)],
        out_specs=pl.BlockSpec((TT, D), lambda i: (i, 0)),
        out_shape=jax.ShapeDtypeStruct((N, D), F32),
        scratch_shapes=[pltpu.VMEM((2, TOP_K, TT * PACK_ROWS, LANES), jnp.uint32),
                        pltpu.SemaphoreType.DMA((2,))],
        compiler_params=_params(("arbitrary",)),
        name="combine",
    )(*([pos_flat] * (2 * TOP_K)), x1f, probs, mod3, g.reshape(1, D), ys)


def _routing(idx, rank, counts):
    B, K, S = idx.shape
    n_exp = counts.shape[0]
    TM = EXPERT_BLOCK
    n_assign = B * S * K
    padded = ((counts + TM - 1) // TM) * TM
    pad_ends = jnp.cumsum(padded)
    pad_starts = pad_ends - padded
    experts = jnp.arange(n_exp, dtype=jnp.int32).reshape(n_exp, 1, 1, 1)
    base = jnp.sum(jnp.where(idx[None] == experts, pad_starts.reshape(n_exp, 1, 1, 1), 0),
                   axis=0)
    pos = (base + rank).astype(jnp.int32).reshape(n_assign)
    n_blocks = n_assign // TM + n_exp
    first_block = jnp.concatenate([pad_starts, pad_ends[-1:]]) // TM
    block_count = jnp.concatenate([padded // TM, n_blocks - pad_ends[-1:] // TM])
    tail_start = jnp.where(padded > 0, pad_ends - TM, -1)
    spare = pad_ends[-1] + jnp.arange(n_exp, dtype=jnp.int32) * TM
    fill_start = jnp.concatenate(
        [tail_start, jnp.where(spare < n_blocks * TM, spare, -1)]).astype(jnp.int32)
    return (pos, first_block.astype(jnp.int32), block_count.astype(jnp.int32),
            fill_start, n_blocks * TM)


def kernel(x, c, ada_w, ada_b, mix_norm_g, w_in, conv_w, conv_b, rg_w, rg_b, ig_w, ig_b,
           lru_lambda, attn_out_g, lru_out_g, w_out, ffn_norm_g, router_w, router_b,
           exp_w_gate, exp_b_gate, exp_w_up, exp_b_up, exp_w_down, exp_b_down, final_norm_g):
    B, S, D = x.shape
    depth = ada_w.shape[0]
    assert S % COMBINE_TILE == 0 and S % 512 == 0
    assert D == SUBLANES * LANES
    for l in range(depth):
        mod3 = _ada(c, ada_w[l], ada_b[l]).reshape(B, N_MOD, D)
        q, k, v, yl = _inproj(x, mod3, mix_norm_g[l], w_in[l].astype(BF16), conv_w[l],
                              conv_b[l], rg_w[l], rg_b[l], ig_w[l], ig_b[l], lru_lambda[l],
                              lru_out_g[l])
        ya = _attention(q, k, v)
        x1, h2, idx, prob, rank, cnt = _outproj(
            x, ya, yl, mod3, attn_out_g[l], w_out[l].astype(BF16), ffn_norm_g[l],
            router_w[l].T, router_b[l])
        pos, first_block, block_count, fill_start, cap = _routing(idx, rank, cnt[:, 0])
        xs = _dispatch(h2, pos * PACK_ROWS, fill_start * PACK_ROWS, cap, S)
        ys = _experts(xs, first_block, block_count, exp_w_gate[l], exp_b_gate[l],
                      exp_w_up[l], exp_b_up[l], exp_w_down[l], exp_b_down[l])
        probs = jnp.transpose(prob, (0, 2, 1)).reshape(B * S, TOP_K)
        assert depth == 1
        x = _combine(x1.reshape(B * S, D), probs, pos * PACK_ROWS, mod3, final_norm_g, ys,
                     S // COMBINE_TILE).reshape(B, S, D)
    return x
```

```python
import functools

import jax
import jax.numpy as jnp
from jax import lax
from jax.experimental import pallas as pl
from jax.experimental.pallas import tpu as pltpu

F32 = jnp.float32
BF16 = jnp.bfloat16
HIGHEST = lax.Precision.HIGHEST

EPS = 1e-6
N_MOD = 6
SB_HEADS = 8
HEAD_DIM = 64
SB_WIDTH = SB_HEADS * HEAD_DIM
LRU_BLOCKS = 8
CONV_WIDTH = 4
LRU_C = 8.0
TOP_K = 4
SWIGLU_LIMIT = 7.0
SWIGLU_ALPHA = 1.702
LOG2E = 1.4426950408889634

LANES = 128
SUBLANES = 8
VMEM_LIMIT = 56 * 1024 * 1024

Q_BLOCK = 128
K_BLOCK = 128
ATTN_WINDOW_BLOCKS = 5
ATTN_LOOP_BLOCKS = 2
ATTN_UNDERFLOW_LOG = -110.0
EXPERT_BLOCK = 512
DISPATCH_TILE = 4096
COMBINE_TILE = 256
ROW_DMA_UNROLL = 8
LRU_SCAN_ROWS = 256


def _params(sem):
    return pltpu.CompilerParams(dimension_semantics=sem, vmem_limit_bytes=VMEM_LIMIT)


def _ada_kernel(c_ref, w_ref, b_ref, o_ref):
    c = c_ref[...]
    ca = c * jax.nn.sigmoid(c)
    o_ref[...] = jnp.dot(ca, w_ref[...], precision=HIGHEST,
                         preferred_element_type=F32) + b_ref[...]


def _ada(c, ada_w, ada_b):
    B, D = c.shape
    E = ada_w.shape[1]
    tn = 1024
    return pl.pallas_call(
        _ada_kernel,
        grid=(E // tn,),
        in_specs=[pl.BlockSpec((B, D), lambda j: (0, 0)),
                  pl.BlockSpec((D, tn), lambda j: (0, j)),
                  pl.BlockSpec((1, tn), lambda j: (0, j))],
        out_specs=pl.BlockSpec((B, tn), lambda j: (0, j)),
        out_shape=jax.ShapeDtypeStruct((B, E), F32),
        compiler_params=_params(("arbitrary",)),
        name="ada",
    )(c, ada_w, ada_b.reshape(1, E))


def _rms(x, g):
    ms = jnp.mean(x * x, axis=-1, keepdims=True)
    return x * lax.rsqrt(ms + EPS) * g


def _softplus(x):
    return jnp.maximum(x, 0.0) + jnp.log1p(jnp.exp(-jnp.abs(x)))


def _gelu_tanh(x):
    return 0.5 * x * (1.0 + jnp.tanh(0.7978845608028654 * (x + 0.044715 * x * x * x)))


LRU_PAD = 8


def _lru_tile(x, gate, cw_ref, cb_ref, wr_ref, br_ref, wi_ref, bi_ref, lam_ref, g_ref,
              xext, hc):
    T = x.shape[0]
    xext[LRU_PAD:LRU_PAD + T, :] = x
    xc = cb_ref[...] + cw_ref[CONV_WIDTH - 1:CONV_WIDTH, :] * x
    for j in range(CONV_WIDTH - 1):
        back = CONV_WIDTH - 1 - j
        xc = xc + cw_ref[j:j + 1, :] * xext[LRU_PAD - back:LRU_PAD - back + T, :]
    xext[0:LRU_PAD, :] = xext[T:T + LRU_PAD, :]

    xb = xc.astype(BF16)
    r = jax.nn.sigmoid(jnp.dot(xb, wr_ref[...], preferred_element_type=F32) + br_ref[...])
    ig = jax.nn.sigmoid(jnp.dot(xb, wi_ref[...], preferred_element_type=F32) + bi_ref[...])
    log_a = (-LRU_C) * r * _softplus(-lam_ref[...])
    a = jnp.exp(log_a)
    b = jnp.sqrt(-jnp.tanh(log_a) * (a * a + 1.0)) * (ig * xc)

    G = T // SUBLANES
    a = a.reshape(G, SUBLANES, -1)
    b = b.reshape(G, SUBLANES, -1)
    rows = lax.broadcasted_iota(jnp.int32, a.shape, 1)
    d = 1
    while d < SUBLANES:
        keep = rows >= d
        a_prev = jnp.where(keep, pltpu.roll(a, d, 1), 1.0)
        b_prev = jnp.where(keep, pltpu.roll(b, d, 1), 0.0)
        b = a * b_prev + b
        a = a * a_prev
        d *= 2
    prev = hc[...]
    groups = []
    for g in range(G):
        hg = a[g] * prev + b[g]
        prev = hg[SUBLANES - 1:SUBLANES, :]
        groups.append(hg)
    hc[...] = prev
    h = jnp.concatenate(groups, axis=0)
    return _rms(h * _gelu_tanh(gate), g_ref[...])


def _inproj_kernel(x_ref, mod_ref, g_ref, w_ref, cw_ref, cb_ref, wr_ref, br_ref, wi_ref,
                   bi_ref, lam_ref, gl_ref, q_ref, k_ref, v_ref, yl_ref, xext, hc):
    @pl.when(pl.program_id(1) == 0)
    def _():
        xext[0:LRU_PAD, :] = jnp.zeros((LRU_PAD, xext.shape[1]), F32)
        hc[...] = jnp.zeros_like(hc)

    x = x_ref[0]
    h = _rms(x, g_ref[...]) * (1.0 + mod_ref[0, 1:2, :]) + mod_ref[0, 0:1, :]
    hb = h.astype(BF16)
    W = SB_WIDTH

    def proj(c):
        return jnp.dot(hb, w_ref[:, c * W:(c + 1) * W], preferred_element_type=F32)

    xr, gr = proj(3), proj(4)
    T = LRU_SCAN_ROWS
    for t in range(x.shape[0] // T):
        rows = slice(t * T, (t + 1) * T)
        yl_ref[0, rows, :] = _lru_tile(xr[rows], gr[rows], cw_ref, cb_ref, wr_ref, br_ref,
                                       wi_ref, bi_ref, lam_ref, gl_ref, xext, hc)
    q_ref[0] = (proj(0) * (HEAD_DIM ** -0.5)).astype(BF16)
    k_ref[0] = proj(1).astype(BF16)
    v_ref[0] = proj(2).astype(BF16)


def _block_diag(w):
    H, I, J = w.shape
    eye = jnp.eye(H, dtype=w.dtype)
    return (w[:, :, None, :] * eye[:, None, :, None]).reshape(H * I, H * J)


def _inproj(x, mod3, g, w_in_b, conv_w, conv_b, rg_w, rg_b, ig_w, ig_b, lam, g_lru, ts=1024):
    B, S, D = x.shape
    E = w_in_b.shape[1]
    W = SB_WIDTH
    row = pl.BlockSpec((1, ts, W), lambda b, s: (b, s, 0))
    vec = pl.BlockSpec((1, W), lambda b, s: (0, 0))
    mat = pl.BlockSpec((W, W), lambda b, s: (0, 0))
    return pl.pallas_call(
        _inproj_kernel,
        grid=(B, S // ts),
        in_specs=[pl.BlockSpec((1, ts, D), lambda b, s: (b, s, 0)),
                  pl.BlockSpec((1, N_MOD, D), lambda b, s: (b, 0, 0)),
                  pl.BlockSpec((1, D), lambda b, s: (0, 0)),
                  pl.BlockSpec((D, E), lambda b, s: (0, 0)),
                  pl.BlockSpec((CONV_WIDTH, W), lambda b, s: (0, 0)), vec,
                  mat, vec, mat, vec, vec, vec],
        out_specs=[row, row, row, row],
        out_shape=[jax.ShapeDtypeStruct((B, S, W), BF16)] * 3
        + [jax.ShapeDtypeStruct((B, S, W), F32)],
        scratch_shapes=[pltpu.VMEM((LRU_SCAN_ROWS + LRU_PAD, W), F32), pltpu.VMEM((1, W), F32)],
        compiler_params=_params(("arbitrary", "arbitrary")),
        name="inproj",
    )(x, mod3, g.reshape(1, D), w_in_b, conv_w, conv_b.reshape(1, W),
      _block_diag(rg_w).astype(BF16), rg_b.reshape(1, W), _block_diag(ig_w).astype(BF16),
      ig_b.reshape(1, W), lam.reshape(1, W), g_lru.reshape(1, W))


def _attn_kernel(q_ref, k_ref, v_ref, o_ref, acc_ref, carry_ref):
    i = pl.program_id(1)
    QB, KB = Q_BLOCK, K_BLOCK
    n_pairs = q_ref.shape[2] // LANES
    lane = lax.broadcasted_iota(jnp.int32, (QB, LANES), 1)

    def stacked_q(p):
        q = q_ref[0, :, p * LANES:(p + 1) * LANES]
        zero = jnp.zeros_like(q)
        return jnp.concatenate([jnp.where(lane < HEAD_DIM, q, zero),
                                jnp.where(lane >= HEAD_DIM, q, zero)], axis=0)

    qs = [stacked_q(p) for p in range(n_pairs)]

    uj = lax.broadcasted_iota(jnp.int32, (2 * KB, KB + LANES), 0) & (KB - 1)
    us = lax.broadcasted_iota(jnp.int32, (2 * KB, KB + LANES), 1)
    u2 = jnp.where((us >= KB) | (uj > us), -1.0, 0.0).astype(BF16)

    row = lax.broadcasted_iota(jnp.int32, (2 * QB, KB), 0) & (QB - 1)
    col = lax.broadcasted_iota(jnp.int32, (2 * QB, KB), 1)
    causal = col < row

    def tiles(p, j_hi, n, diagonal, acc, carry):
        start = pl.multiple_of((j_hi - (n - 1)) * KB, KB)
        kw = k_ref[0, pl.ds(start, n * KB), p * LANES:(p + 1) * LANES]
        vw = v_ref[0, pl.ds(start, n * KB), p * LANES:(p + 1) * LANES]
        z = lax.dot_general(qs[p], kw, (((1,), (1,)), ((), ())), preferred_element_type=F32)
        softplus = jnp.maximum(z, 0.0) + jnp.log(1.0 + jnp.exp2(jnp.abs(z) * (-LOG2E)))
        log_beta = z - softplus
        ws = [None] * n
        for c in reversed(range(n)):
            sp = softplus[:, c * KB:(c + 1) * KB]
            masked = diagonal and c == n - 1
            if masked:
                sp = jnp.where(causal, sp, 0.0)
            hi_f = lax.bitcast_convert_type(
                lax.bitcast_convert_type(sp, jnp.uint32) & jnp.uint32(0xFFFF0000), F32)
            hi_lo = jnp.concatenate([hi_f.astype(BF16), (sp - hi_f).astype(BF16)], axis=1)
            sums = jnp.dot(hi_lo, u2, preferred_element_type=F32)
            w = jnp.exp2((log_beta[:, c * KB:(c + 1) * KB] + sums[:, :KB] + carry) * LOG2E)
            if masked:
                w = jnp.where(causal, w, 0.0)
            ws[c] = w.astype(BF16)
            carry = carry + sums[:, KB:]
        w_all = ws[0] if n == 1 else jnp.concatenate(ws, axis=1)
        acc = acc + jnp.dot(w_all, vw, preferred_element_type=F32)
        return acc, carry

    def fold(j_hi, n, diagonal):
        cmax = None
        for p in range(n_pairs):
            if diagonal:
                acc = carry = jnp.zeros((2 * QB, LANES), F32)
            else:
                acc, carry = acc_ref[p], carry_ref[p]
            acc, carry = tiles(p, j_hi, n, diagonal, acc, carry)
            acc_ref[p] = acc
            carry_ref[p] = carry
            cmax = carry if cmax is None else jnp.maximum(cmax, carry)
        return jnp.max(cmax)

    n_first, n_loop = ATTN_WINDOW_BLOCKS, ATTN_LOOP_BLOCKS
    n0 = jnp.minimum(i + 1, n_first)
    cmax0 = lax.switch(n0 - 1, [functools.partial(fold, i, n, True)
                                for n in range(1, n_first + 1)])
    j0 = i - n0

    def loop(n, j, cmax):
        def more(st):
            return (st[0] >= n - 1) & (st[1] >= ATTN_UNDERFLOW_LOG)

        def step(st):
            return st[0] - n, fold(st[0], n, False)

        return lax.while_loop(more, step, (j, cmax))

    j, cmax = j0, cmax0
    while n_loop >= 1:
        j, cmax = loop(n_loop, j, cmax)
        n_loop //= 2
    for p in range(n_pairs):
        acc = acc_ref[p]
        o_ref[0, :, p * LANES:(p + 1) * LANES] = jnp.where(lane < HEAD_DIM, acc[:QB], acc[QB:])


def _attention(q, k, v):
    B, S, W = q.shape
    return pl.pallas_call(
        _attn_kernel,
        grid=(B, S // Q_BLOCK),
        in_specs=[pl.BlockSpec((1, Q_BLOCK, W), lambda b, i: (b, i, 0)),
                  pl.BlockSpec((1, S, W), lambda b, i: (b, 0, 0)),
                  pl.BlockSpec((1, S, W), lambda b, i: (b, 0, 0))],
        out_specs=pl.BlockSpec((1, Q_BLOCK, W), lambda b, i: (b, i, 0)),
        out_shape=jax.ShapeDtypeStruct((B, S, W), F32),
        scratch_shapes=[pltpu.VMEM((W // LANES, 2 * Q_BLOCK, LANES), F32),
                        pltpu.VMEM((W // LANES, 2 * Q_BLOCK, LANES), F32)],
        compiler_params=_params(("arbitrary", "arbitrary")),
        name="attn",
    )(q, k, v)


PACK_ROWS = SUBLANES // 2


def _store_packed_rows(ref, value):
    n_rows, d = value.shape
    bits = lax.bitcast_convert_type(value.astype(BF16).astype(F32), jnp.uint32)
    words = (bits[:, :d // 2] >> 16) | bits[:, d // 2:]
    for c in range(PACK_ROWS):
        ref[pl.ds(c, n_rows, stride=PACK_ROWS), :] = words[:, c * LANES:(c + 1) * LANES]


def _load_packed_rows(ref, n_rows, lead=(), dtype=BF16):
    lo, hi = [], []
    for c in range(PACK_ROWS):
        w = ref[lead + (pl.ds(c, n_rows, stride=PACK_ROWS), slice(None))]
        lo.append(lax.bitcast_convert_type(w << 16, F32).astype(dtype))
        hi.append(lax.bitcast_convert_type(w & jnp.uint32(0xFFFF0000), F32).astype(dtype))
    return jnp.concatenate(lo + hi, axis=1)


def _outproj_kernel(x_ref, ya_ref, yl_ref, mod_ref, ga_ref, wo_ref, gf_ref, rw_ref, rb_ref,
                    x1_ref, h2_ref, idx_ref, prob_ref, rank_ref, cnt_ref, tri, run):
    W = SB_WIDTH
    ya = _rms(ya_ref[0], ga_ref[...]).astype(BF16)
    yl = yl_ref[0].astype(BF16)
    mix = (jnp.dot(ya, wo_ref[0:W, :], preferred_element_type=F32)
           + jnp.dot(yl, wo_ref[W:2 * W, :], preferred_element_type=F32))
    x1 = x_ref[0] + mod_ref[0, 2:3, :] * mix
    x1_ref[0] = x1
    h2 = _rms(x1, gf_ref[...]) * (1.0 + mod_ref[0, 4:5, :]) + mod_ref[0, 3:4, :]
    _store_packed_rows(h2_ref, h2)

    logits = lax.dot_general(rw_ref[...], h2, (((1,), (1,)), ((), ())), precision=HIGHEST,
                             preferred_element_type=F32) + rb_ref[...]
    n_exp = logits.shape[0]
    eid = lax.broadcasted_iota(jnp.int32, logits.shape, 0)
    vals, idxs = [], []
    for _ in range(TOP_K):
        m = jnp.max(logits, axis=0, keepdims=True)
        sel = jnp.min(jnp.where(logits == m, eid, n_exp), axis=0, keepdims=True)
        vals.append(m)
        idxs.append(sel)
        logits = jnp.where(eid == sel, -jnp.inf, logits)
    es = [jnp.exp(vv - vals[0]) for vv in vals]
    inv = 1.0 / (es[0] + es[1] + es[2] + es[3])

    first = (pl.program_id(0) == 0) & (pl.program_id(1) == 0)

    @pl.when(first)
    def _():
        ti = lax.broadcasted_iota(jnp.int32, tri.shape, 0)
        tj = lax.broadcasted_iota(jnp.int32, tri.shape, 1)
        tri[...] = jnp.where(ti < tj, 1.0, 0.0).astype(BF16)
        run[...] = jnp.zeros_like(run)

    hits = [eid == idxs[r] for r in range(TOP_K)]
    chosen = hits[0] | hits[1] | hits[2] | hits[3]
    cnt = jnp.where(chosen, 1.0, 0.0)
    before = jnp.dot(cnt.astype(BF16), tri[...], preferred_element_type=F32) + run[...]
    for r in range(TOP_K):
        idx_ref[0, r:r + 1, :] = idxs[r]
        prob_ref[0, r:r + 1, :] = es[r] * inv
        rank_ref[0, r:r + 1, :] = jnp.sum(jnp.where(hits[r], before, 0.0), axis=0,
                                          keepdims=True).astype(jnp.int32)
    run[...] = run[...] + jnp.sum(cnt, axis=1, keepdims=True)
    cnt_ref[...] = jnp.broadcast_to(run[...], cnt_ref.shape).astype(jnp.int32)


def _outproj(x, ya, yl, mod3, ga, w_out_b, gf, router_wt, router_b, ts=1024):
    B, S, D = x.shape
    W = ya.shape[2]
    NE = router_wt.shape[0]
    rowd = pl.BlockSpec((1, ts, D), lambda b, s: (b, s, 0))
    roww = pl.BlockSpec((1, ts, W), lambda b, s: (b, s, 0))
    sel = pl.BlockSpec((1, TOP_K, ts), lambda b, s: (b, 0, s))
    return pl.pallas_call(
        _outproj_kernel,
        grid=(B, S // ts),
        in_specs=[rowd, roww, roww,
                  pl.BlockSpec((1, N_MOD, D), lambda b, s: (b, 0, 0)),
                  pl.BlockSpec((1, W), lambda b, s: (0, 0)),
                  pl.BlockSpec((2 * W, D), lambda b, s: (0, 0)),
                  pl.BlockSpec((1, D), lambda b, s: (0, 0)),
                  pl.BlockSpec((NE, D), lambda b, s: (0, 0)),
                  pl.BlockSpec((NE, 1), lambda b, s: (0, 0))],
        out_specs=[rowd,
                   pl.BlockSpec((ts * PACK_ROWS, LANES), lambda b, s: (b * (S // ts) + s, 0)),
                   sel, sel, sel, pl.BlockSpec((NE, LANES), lambda b, s: (0, 0))],
        out_shape=[jax.ShapeDtypeStruct((B, S, D), F32),
                   jax.ShapeDtypeStruct((B * S * PACK_ROWS, LANES), jnp.uint32),
                   jax.ShapeDtypeStruct((B, TOP_K, S), jnp.int32),
                   jax.ShapeDtypeStruct((B, TOP_K, S), F32),
                   jax.ShapeDtypeStruct((B, TOP_K, S), jnp.int32),
                   jax.ShapeDtypeStruct((NE, LANES), jnp.int32)],
        scratch_shapes=[pltpu.VMEM((ts, ts), BF16), pltpu.VMEM((NE, 1), F32)],
        compiler_params=_params(("arbitrary", "arbitrary")),
        name="outproj",
    )(x, ya, yl, mod3, ga.reshape(1, W), w_out_b, gf.reshape(1, D), router_wt,
      router_b.reshape(NE, 1))


def _dispatch_kernel(p0_ref, p1_ref, p2_ref, p3_ref, fill_ref, h_ref, xs_ref, zbuf, sem, zsem):
    TT = h_ref.shape[0] // PACK_ROWS
    n_fill = fill_ref.shape[0]

    @pl.when(pl.program_id(0) == 0)
    def _():
        zbuf[...] = jnp.zeros_like(zbuf)

        def fill_copy(e):
            start = pl.multiple_of(jnp.maximum(fill_ref[e], 0), PACK_ROWS)
            return pltpu.make_async_copy(zbuf, xs_ref.at[pl.ds(start, zbuf.shape[0]), :], zsem)

        def start(e, c):
            @pl.when(fill_ref[e] >= 0)
            def _():
                fill_copy(e).start()
            return c

        def wait(e, c):
            @pl.when(fill_ref[e] >= 0)
            def _():
                fill_copy(e).wait()
            return c

        lax.fori_loop(0, n_fill, start, 0)
        lax.fori_loop(0, n_fill, wait, 0)

    pos_refs = (p0_ref, p1_ref, p2_ref, p3_ref)

    def row_copy(t, r):
        src = pl.multiple_of(t * PACK_ROWS, PACK_ROWS)
        dst = pl.multiple_of(pos_refs[r][t], PACK_ROWS)
        return pltpu.make_async_copy(h_ref.at[pl.ds(src, PACK_ROWS), :],
                                     xs_ref.at[pl.ds(dst, PACK_ROWS), :], sem)

    def issue(g, c):
        for tt in range(ROW_DMA_UNROLL):
            for r in range(TOP_K):
                row_copy(g * ROW_DMA_UNROLL + tt, r).start(priority=(tt * TOP_K + r) % 2)
        return c

    lax.fori_loop(0, TT // ROW_DMA_UNROLL, issue, 0)
    for r in range(TOP_K):
        pltpu.make_async_copy(h_ref, xs_ref.at[pl.ds(0, TT * PACK_ROWS), :], sem).wait()


def _slot_specs(tile, tiles_per_batch):
    def spec(r):
        return pl.BlockSpec(
            (tile,),
            lambda i: ((i // tiles_per_batch * TOP_K + r) * tiles_per_batch
                       + i % tiles_per_batch,),
            memory_space=pltpu.SMEM)
    return [spec(r) for r in range(TOP_K)]


def _dispatch(h2t, pos_flat, fill_start, cap, seq_len):
    TT = min(DISPATCH_TILE, seq_len)
    assert seq_len % TT == 0
    tiles_per_batch = seq_len // TT
    n_tiles = h2t.shape[0] // (TT * PACK_ROWS)
    return pl.pallas_call(
        _dispatch_kernel,
        grid=(n_tiles,),
        in_specs=_slot_specs(TT, tiles_per_batch)
        + [pl.BlockSpec(memory_space=pltpu.SMEM),
           pl.BlockSpec((TT * PACK_ROWS, LANES), lambda i: (i, 0))],
        out_specs=pl.BlockSpec(memory_space=pl.ANY),
        out_shape=jax.ShapeDtypeStruct((cap * PACK_ROWS, LANES), jnp.uint32),
        scratch_shapes=[pltpu.VMEM((EXPERT_BLOCK * PACK_ROWS, LANES), jnp.uint32),
                        pltpu.SemaphoreType.DMA, pltpu.SemaphoreType.DMA],
        compiler_params=_params(("arbitrary",)),
        name="dispatch",
    )(pos_flat, pos_flat, pos_flat, pos_flat, fill_start, h2t)


def _expert_kernel(first_ref, count_ref, xs_ref, wg_ref, bg_ref, wu_ref, bu_ref, wd_ref,
                   bd_ref, ys_ref, wgb, wub, wdb, xbuf, ybuf, xsem, ysem):
    e = pl.program_id(0)
    n_exp = pl.num_programs(0)
    rows = ybuf.shape[1]
    first, count = first_ref[e], count_ref[e]

    wgb[...] = wg_ref[0].astype(BF16)
    wub[...] = wu_ref[0].astype(BF16)
    wdb[...] = wd_ref[0].astype(BF16)

    def block(ref, buf, j):
        n = buf.shape[1]
        return ref.at[pl.ds(pl.multiple_of((first + j) * n, n), n), :]

    def x_copy(j, slot):
        return pltpu.make_async_copy(block(xs_ref, xbuf, j), xbuf.at[slot], xsem.at[slot])

    def y_copy(j, slot):
        return pltpu.make_async_copy(ybuf.at[slot], block(ys_ref, ybuf, j), ysem.at[slot])

    @pl.when(count > 0)
    def _():
        x_copy(0, 0).start()

    def step(j, c):
        slot = j % 2
        x_copy(j, slot).wait()

        @pl.when(j + 1 < count)
        def _():
            x_copy(j + 1, 1 - slot).start()

        @pl.when(j >= 2)
        def _():
            y_copy(j - 2, slot).wait()

        xb = _load_packed_rows(xbuf, EXPERT_BLOCK, (slot,))
        g = jnp.minimum(jnp.dot(xb, wgb[...], preferred_element_type=F32) + bg_ref[0],
                        SWIGLU_LIMIT)
        u = jnp.clip(jnp.dot(xb, wub[...], preferred_element_type=F32) + bu_ref[0],
                     -SWIGLU_LIMIT, SWIGLU_LIMIT)
        act = (u + 1.0) * (g * jax.nn.sigmoid(SWIGLU_ALPHA * g))
        _store_packed_rows(ybuf.at[slot], jnp.dot(act.astype(BF16), wdb[...],
                                                  preferred_element_type=F32) + bd_ref[0])
        y_copy(j, slot).start()
        return c

    lax.fori_loop(0, count, step, 0)

    @pl.when(count >= 2)
    def _():
        y_copy(count - 2, count % 2).wait()

    @pl.when(count >= 1)
    def _():
        y_copy(count - 1, (count - 1) % 2).wait()

    @pl.when(e == n_exp - 1)
    def _():
        spare_first, spare_count = first_ref[n_exp], count_ref[n_exp]
        ybuf[0] = jnp.zeros(ybuf.shape[1:], ybuf.dtype)

        def spare_copy(j):
            dst = pl.ds(pl.multiple_of((spare_first + j) * rows, rows), rows)
            return pltpu.make_async_copy(ybuf.at[0], ys_ref.at[dst, :], ysem.at[0])

        def start(j, c):
            spare_copy(j).start()
            return c

        def wait(j, c):
            spare_copy(j).wait()
            return c

        lax.fori_loop(0, spare_count, start, 0)
        lax.fori_loop(0, spare_count, wait, 0)


def _experts(xs, first_block, block_count, wg, bg, wu, bu, wd, bd):
    NE, D, DE = wg.shape
    cap = xs.shape[0] // PACK_ROWS

    def wsel(e, first, count):
        return (e, 0, 0)

    grid_spec = pltpu.PrefetchScalarGridSpec(
        num_scalar_prefetch=2,
        grid=(NE,),
        in_specs=[pl.BlockSpec(memory_space=pl.ANY),
                  pl.BlockSpec((1, D, DE), wsel), pl.BlockSpec((1, 1, DE), wsel),
                  pl.BlockSpec((1, D, DE), wsel), pl.BlockSpec((1, 1, DE), wsel),
                  pl.BlockSpec((1, DE, D), wsel), pl.BlockSpec((1, 1, D), wsel)],
        out_specs=pl.BlockSpec(memory_space=pl.ANY),
        scratch_shapes=[pltpu.VMEM((D, DE), BF16), pltpu.VMEM((D, DE), BF16),
                        pltpu.VMEM((DE, D), BF16),
                        pltpu.VMEM((2, EXPERT_BLOCK * PACK_ROWS, LANES), jnp.uint32),
                        pltpu.VMEM((2, EXPERT_BLOCK * PACK_ROWS, LANES), jnp.uint32),
                        pltpu.SemaphoreType.DMA((2,)), pltpu.SemaphoreType.DMA((2,))],
    )
    return pl.pallas_call(
        _expert_kernel,
        grid_spec=grid_spec,
        out_shape=jax.ShapeDtypeStruct((cap * PACK_ROWS, LANES), jnp.uint32),
        compiler_params=_params(("arbitrary",)),
        name="experts",
    )(first_block, block_count, xs, wg, bg.reshape(NE, 1, DE), wu, bu.reshape(NE, 1, DE), wd,
      bd.reshape(NE, 1, D))


def _combine_kernel(c0_ref, c1_ref, c2_ref, c3_ref, n0_ref, n1_ref, n2_ref, n3_ref,
                    x1_ref, p_ref, mod_ref, g_ref, ys_ref, o_ref, buf, sem):
    TT = x1_ref.shape[0]
    i = pl.program_id(0)
    R = PACK_ROWS

    def gather(pos_refs, s):
        def issue(g, c):
            for tt in range(ROW_DMA_UNROLL):
                t = g * ROW_DMA_UNROLL + tt
                for r in range(TOP_K):
                    pltpu.make_async_copy(
                        ys_ref.at[pl.ds(pl.multiple_of(pos_refs[r][t], R), R), :],
                        buf.at[s, r, pl.ds(pl.multiple_of(t * R, R), R), :],
                        sem.at[s]).start(priority=(tt * TOP_K + r) % 2)
            return c
        lax.fori_loop(0, TT // ROW_DMA_UNROLL, issue, 0)

    @pl.when(i == 0)
    def _():
        gather((c0_ref, c1_ref, c2_ref, c3_ref), 0)

    def reduce_tile(s):
        @pl.when(i + 1 < pl.num_programs(0))
        def _():
            gather((n0_ref, n1_ref, n2_ref, n3_ref), 1 - s)

        for r in range(TOP_K):
            pltpu.make_async_copy(ys_ref.at[pl.ds(0, TT * R), :], buf.at[s, r],
                                  sem.at[s]).wait()
        p = p_ref[...]
        moe = p[:, 0:1] * _load_packed_rows(buf, TT, (s, 0), F32)
        for r in range(1, TOP_K):
            moe = moe + p[:, r:r + 1] * _load_packed_rows(buf, TT, (s, r), F32)
        x2 = x1_ref[...] + mod_ref[0, 5:6, :] * moe
        o_ref[...] = _rms(x2, g_ref[...])

    for s in range(2):
        pl.when(i % 2 == s)(functools.partial(reduce_tile, s))


def _combine(x1f, probs, pos_flat, mod3, g, ys, tiles_per_batch):
    N, D = x1f.shape
    TT = COMBINE_TILE
    n_tiles = N // TT

    def next_spec(r):
        def index(i):
            j = jnp.minimum(i + 1, n_tiles - 1)
            return ((j // tiles_per_batch * TOP_K + r) * tiles_per_batch + j % tiles_per_batch,)
        return pl.BlockSpec((TT,), index, memory_space=pltpu.SMEM)

    return pl.pallas_call(
        _combine_kernel,
        grid=(n_tiles,),
        in_specs=_slot_specs(TT, tiles_per_batch) + [next_spec(r) for r in range(TOP_K)]
        + [pl.BlockSpec((TT, D), lambda i: (i, 0)),
           pl.BlockSpec((TT, TOP_K), lambda i: (i, 0)),
           pl.BlockSpec((1, N_MOD, D), lambda i: (i // tiles_per_batch, 0, 0)),
           pl.BlockSpec((1, D), lambda i: (0, 0)),
           pl.BlockSpec(memory_space=pl.ANY)],
        out_specs=pl.BlockSpec((TT, D), lambda i: (i, 0)),
        out_shape=jax.ShapeDtypeStruct((N, D), F32),
        scratch_shapes=[pltpu.VMEM((2, TOP_K, TT * PACK_ROWS, LANES), jnp.uint32),
                        pltpu.SemaphoreType.DMA((2,))],
        compiler_params=_params(("arbitrary",)),
        name="combine",
    )(*([pos_flat] * (2 * TOP_K)), x1f, probs, mod3, g.reshape(1, D), ys)


def _routing(idx, rank, counts):
    B, K, S = idx.shape
    n_exp = counts.shape[0]
    TM = EXPERT_BLOCK
    n_assign = B * S * K
    padded = ((counts + TM - 1) // TM) * TM
    pad_ends = jnp.cumsum(padded)
    pad_starts = pad_ends - padded
    experts = jnp.arange(n_exp, dtype=jnp.int32).reshape(n_exp, 1, 1, 1)
    base = jnp.sum(jnp.where(idx[None] == experts, pad_starts.reshape(n_exp, 1, 1, 1), 0),
                   axis=0)
    pos = (base + rank).astype(jnp.int32).reshape(n_assign)
    n_blocks = n_assign // TM + n_exp
    first_block = jnp.concatenate([pad_starts, pad_ends[-1:]]) // TM
    block_count = jnp.concatenate([padded // TM, n_blocks - pad_ends[-1:] // TM])
    tail_start = jnp.where(padded > 0, pad_ends - TM, -1)
    spare = pad_ends[-1] + jnp.arange(n_exp, dtype=jnp.int32) * TM
    fill_start = jnp.concatenate(
        [tail_start, jnp.where(spare < n_blocks * TM, spare, -1)]).astype(jnp.int32)
    return (pos, first_block.astype(jnp.int32), block_count.astype(jnp.int32),
            fill_start, n_blocks * TM)


def kernel(x, c, ada_w, ada_b, mix_norm_g, w_in, conv_w, conv_b, rg_w, rg_b, ig_w, ig_b,
           lru_lambda, attn_out_g, lru_out_g, w_out, ffn_norm_g, router_w, router_b,
           exp_w_gate, exp_b_gate, exp_w_up, exp_b_up, exp_w_down, exp_b_down, final_norm_g):
    B, S, D = x.shape
    depth = ada_w.shape[0]
    assert S % COMBINE_TILE == 0 and S % 512 == 0
    assert D == SUBLANES * LANES
    for l in range(depth):
        mod3 = _ada(c, ada_w[l], ada_b[l]).reshape(B, N_MOD, D)
        q, k, v, yl = _inproj(x, mod3, mix_norm_g[l], w_in[l].astype(BF16), conv_w[l],
                              conv_b[l], rg_w[l], rg_b[l], ig_w[l], ig_b[l], lru_lambda[l],
                              lru_out_g[l])
        ya = _attention(q, k, v)
        x1, h2, idx, prob, rank, cnt = _outproj(
            x, ya, yl, mod3, attn_out_g[l], w_out[l].astype(BF16), ffn_norm_g[l],
            router_w[l].T, router_b[l])
        pos, first_block, block_count, fill_start, cap = _routing(idx, rank, cnt[:, 0])
        xs = _dispatch(h2, pos * PACK_ROWS, fill_start * PACK_ROWS, cap, S)
        ys = _experts(xs, first_block, block_count, exp_w_gate[l], exp_b_gate[l],
                      exp_w_up[l], exp_b_up[l], exp_w_down[l], exp_b_down[l])
        probs = jnp.transpose(prob, (0, 2, 1)).reshape(B * S, TOP_K)
        assert depth == 1
        x = _combine(x1.reshape(B * S, D), probs, pos * PACK_ROWS, mod3, final_norm_g, ys,
                     S // COMBINE_TILE).reshape(B, S, D)
    return x
```

```python
import functools

import jax
import jax.numpy as jnp
from jax import lax
from jax.experimental import pallas as pl
from jax.experimental.pallas import tpu as pltpu

F32 = jnp.float32
BF16 = jnp.bfloat16
HIGHEST = lax.Precision.HIGHEST

EPS = 1e-6
N_MOD = 6
SB_HEADS = 8
HEAD_DIM = 64
SB_WIDTH = SB_HEADS * HEAD_DIM
LRU_BLOCKS = 8
CONV_WIDTH = 4
LRU_C = 8.0
TOP_K = 4
SWIGLU_LIMIT = 7.0
SWIGLU_ALPHA = 1.702
LOG2E = 1.4426950408889634

LANES = 128
SUBLANES = 8
VMEM_LIMIT = 56 * 1024 * 1024

Q_BLOCK = 128
K_BLOCK = 128
ATTN_WINDOW_BLOCKS = 5
ATTN_LOOP_BLOCKS = 2
ATTN_UNDERFLOW_LOG = -110.0
EXPERT_BLOCK = 256
DISPATCH_TILE = 4096
COMBINE_TILE = 256
ROW_DMA_UNROLL = 8
LRU_SCAN_ROWS = 256


def _params(sem):
    return pltpu.CompilerParams(dimension_semantics=sem, vmem_limit_bytes=VMEM_LIMIT)


def _ada_kernel(c_ref, w_ref, b_ref, o_ref):
    c = c_ref[...]
    ca = c * jax.nn.sigmoid(c)
    o_ref[...] = jnp.dot(ca, w_ref[...], precision=HIGHEST,
                         preferred_element_type=F32) + b_ref[...]


def _ada(c, ada_w, ada_b):
    B, D = c.shape
    E = ada_w.shape[1]
    tn = 1024
    return pl.pallas_call(
        _ada_kernel,
        grid=(E // tn,),
        in_specs=[pl.BlockSpec((B, D), lambda j: (0, 0)),
                  pl.BlockSpec((D, tn), lambda j: (0, j)),
                  pl.BlockSpec((1, tn), lambda j: (0, j))],
        out_specs=pl.BlockSpec((B, tn), lambda j: (0, j)),
        out_shape=jax.ShapeDtypeStruct((B, E), F32),
        compiler_params=_params(("arbitrary",)),
        name="ada",
    )(c, ada_w, ada_b.reshape(1, E))


def _rms(x, g):
    ms = jnp.mean(x * x, axis=-1, keepdims=True)
    return x * lax.rsqrt(ms + EPS) * g


def _softplus(x):
    return jnp.maximum(x, 0.0) + jnp.log1p(jnp.exp(-jnp.abs(x)))


def _gelu_tanh(x):
    return 0.5 * x * (1.0 + jnp.tanh(0.7978845608028654 * (x + 0.044715 * x * x * x)))


LRU_PAD = 8


def _lru_tile(x, gate, cw_ref, cb_ref, wr_ref, br_ref, wi_ref, bi_ref, lam_ref, g_ref,
              xext, hc):
    T = x.shape[0]
    xext[LRU_PAD:LRU_PAD + T, :] = x
    xc = cb_ref[...] + cw_ref[CONV_WIDTH - 1:CONV_WIDTH, :] * x
    for j in range(CONV_WIDTH - 1):
        back = CONV_WIDTH - 1 - j
        xc = xc + cw_ref[j:j + 1, :] * xext[LRU_PAD - back:LRU_PAD - back + T, :]
    xext[0:LRU_PAD, :] = xext[T:T + LRU_PAD, :]

    xb = xc.astype(BF16)
    r = jax.nn.sigmoid(jnp.dot(xb, wr_ref[...], preferred_element_type=F32) + br_ref[...])
    ig = jax.nn.sigmoid(jnp.dot(xb, wi_ref[...], preferred_element_type=F32) + bi_ref[...])
    log_a = (-LRU_C) * r * _softplus(-lam_ref[...])
    a = jnp.exp(log_a)
    b = jnp.sqrt(-jnp.tanh(log_a) * (a * a + 1.0)) * (ig * xc)

    G = T // SUBLANES
    a = a.reshape(G, SUBLANES, -1)
    b = b.reshape(G, SUBLANES, -1)
    rows = lax.broadcasted_iota(jnp.int32, a.shape, 1)
    d = 1
    while d < SUBLANES:
        keep = rows >= d
        a_prev = jnp.where(keep, pltpu.roll(a, d, 1), 1.0)
        b_prev = jnp.where(keep, pltpu.roll(b, d, 1), 0.0)
        b = a * b_prev + b
        a = a * a_prev
        d *= 2
    prev = hc[...]
    groups = []
    for g in range(G):
        hg = a[g] * prev + b[g]
        prev = hg[SUBLANES - 1:SUBLANES, :]
        groups.append(hg)
    hc[...] = prev
    h = jnp.concatenate(groups, axis=0)
    return _rms(h * _gelu_tanh(gate), g_ref[...])


def _inproj_kernel(x_ref, mod_ref, g_ref, w_ref, cw_ref, cb_ref, wr_ref, br_ref, wi_ref,
                   bi_ref, lam_ref, gl_ref, q_ref, k_ref, v_ref, yl_ref, xext, hc):
    @pl.when(pl.program_id(1) == 0)
    def _():
        xext[0:LRU_PAD, :] = jnp.zeros((LRU_PAD, xext.shape[1]), F32)
        hc[...] = jnp.zeros_like(hc)

    x = x_ref[0]
    h = _rms(x, g_ref[...]) * (1.0 + mod_ref[0, 1:2, :]) + mod_ref[0, 0:1, :]
    hb = h.astype(BF16)
    W = SB_WIDTH

    def proj(c):
        return jnp.dot(hb, w_ref[:, c * W:(c + 1) * W], preferred_element_type=F32)

    xr, gr = proj(3), proj(4)
    T = LRU_SCAN_ROWS
    for t in range(x.shape[0] // T):
        rows = slice(t * T, (t + 1) * T)
        yl_ref[0, rows, :] = _lru_tile(xr[rows], gr[rows], cw_ref, cb_ref, wr_ref, br_ref,
                                       wi_ref, bi_ref, lam_ref, gl_ref, xext, hc)
    q_ref[0] = (proj(0) * (HEAD_DIM ** -0.5)).astype(BF16)
    k_ref[0] = proj(1).astype(BF16)
    v_ref[0] = proj(2).astype(BF16)


def _block_diag(w):
    H, I, J = w.shape
    eye = jnp.eye(H, dtype=w.dtype)
    return (w[:, :, None, :] * eye[:, None, :, None]).reshape(H * I, H * J)


def _inproj(x, mod3, g, w_in_b, conv_w, conv_b, rg_w, rg_b, ig_w, ig_b, lam, g_lru, ts=1024):
    B, S, D = x.shape
    E = w_in_b.shape[1]
    W = SB_WIDTH
    row = pl.BlockSpec((1, ts, W), lambda b, s: (b, s, 0))
    vec = pl.BlockSpec((1, W), lambda b, s: (0, 0))
    mat = pl.BlockSpec((W, W), lambda b, s: (0, 0))
    return pl.pallas_call(
        _inproj_kernel,
        grid=(B, S // ts),
        in_specs=[pl.BlockSpec((1, ts, D), lambda b, s: (b, s, 0)),
                  pl.BlockSpec((1, N_MOD, D), lambda b, s: (b, 0, 0)),
                  pl.BlockSpec((1, D), lambda b, s: (0, 0)),
                  pl.BlockSpec((D, E), lambda b, s: (0, 0)),
                  pl.BlockSpec((CONV_WIDTH, W), lambda b, s: (0, 0)), vec,
                  mat, vec, mat, vec, vec, vec],
        out_specs=[row, row, row, row],
        out_shape=[jax.ShapeDtypeStruct((B, S, W), BF16)] * 3
        + [jax.ShapeDtypeStruct((B, S, W), F32)],
        scratch_shapes=[pltpu.VMEM((LRU_SCAN_ROWS + LRU_PAD, W), F32), pltpu.VMEM((1, W), F32)],
        compiler_params=_params(("arbitrary", "arbitrary")),
        name="inproj",
    )(x, mod3, g.reshape(1, D), w_in_b, conv_w, conv_b.reshape(1, W),
      _block_diag(rg_w).astype(BF16), rg_b.reshape(1, W), _block_diag(ig_w).astype(BF16),
      ig_b.reshape(1, W), lam.reshape(1, W), g_lru.reshape(1, W))


def _attn_kernel(q_ref, k_ref, v_ref, o_ref, acc_ref, carry_ref):
    i = pl.program_id(1)
    QB, KB = Q_BLOCK, K_BLOCK
    n_pairs = q_ref.shape[2] // LANES
    lane = lax.broadcasted_iota(jnp.int32, (QB, LANES), 1)

    def stacked_q(p):
        q = q_ref[0, :, p * LANES:(p + 1) * LANES]
        zero = jnp.zeros_like(q)
        return jnp.concatenate([jnp.where(lane < HEAD_DIM, q, zero),
                                jnp.where(lane >= HEAD_DIM, q, zero)], axis=0)

    qs = [stacked_q(p) for p in range(n_pairs)]

    uj = lax.broadcasted_iota(jnp.int32, (2 * KB, KB + LANES), 0) & (KB - 1)
    us = lax.broadcasted_iota(jnp.int32, (2 * KB, KB + LANES), 1)
    u2 = jnp.where((us >= KB) | (uj > us), -1.0, 0.0).astype(BF16)

    row = lax.broadcasted_iota(jnp.int32, (2 * QB, KB), 0) & (QB - 1)
    col = lax.broadcasted_iota(jnp.int32, (2 * QB, KB), 1)
    causal = col < row

    def tiles(p, j_hi, n, diagonal, acc, carry):
        start = pl.multiple_of((j_hi - (n - 1)) * KB, KB)
        kw = k_ref[0, pl.ds(start, n * KB), p * LANES:(p + 1) * LANES]
        vw = v_ref[0, pl.ds(start, n * KB), p * LANES:(p + 1) * LANES]
        z = lax.dot_general(qs[p], kw, (((1,), (1,)), ((), ())), preferred_element_type=F32)
        softplus = jnp.maximum(z, 0.0) + jnp.log(1.0 + jnp.exp2(jnp.abs(z) * (-LOG2E)))
        log_beta = z - softplus
        ws = [None] * n
        for c in reversed(range(n)):
            sp = softplus[:, c * KB:(c + 1) * KB]
            masked = diagonal and c == n - 1
            if masked:
                sp = jnp.where(causal, sp, 0.0)
            hi_f = lax.bitcast_convert_type(
                lax.bitcast_convert_type(sp, jnp.uint32) & jnp.uint32(0xFFFF0000), F32)
            hi_lo = jnp.concatenate([hi_f.astype(BF16), (sp - hi_f).astype(BF16)], axis=1)
            sums = jnp.dot(hi_lo, u2, preferred_element_type=F32)
            w = jnp.exp2((log_beta[:, c * KB:(c + 1) * KB] + sums[:, :KB] + carry) * LOG2E)
            if masked:
                w = jnp.where(causal, w, 0.0)
            ws[c] = w.astype(BF16)
            carry = carry + sums[:, KB:]
        w_all = ws[0] if n == 1 else jnp.concatenate(ws, axis=1)
        acc = acc + jnp.dot(w_all, vw, preferred_element_type=F32)
        return acc, carry

    def fold(j_hi, n, diagonal):
        cmax = None
        for p in range(n_pairs):
            if diagonal:
                acc = carry = jnp.zeros((2 * QB, LANES), F32)
            else:
                acc, carry = acc_ref[p], carry_ref[p]
            acc, carry = tiles(p, j_hi, n, diagonal, acc, carry)
            acc_ref[p] = acc
            carry_ref[p] = carry
            cmax = carry if cmax is None else jnp.maximum(cmax, carry)
        return jnp.max(cmax)

    n_first, n_loop = ATTN_WINDOW_BLOCKS, ATTN_LOOP_BLOCKS
    n0 = jnp.minimum(i + 1, n_first)
    cmax0 = lax.switch(n0 - 1, [functools.partial(fold, i, n, True)
                                for n in range(1, n_first + 1)])
    j0 = i - n0

    def loop(n, j, cmax):
        def more(st):
            return (st[0] >= n - 1) & (st[1] >= ATTN_UNDERFLOW_LOG)

        def step(st):
            return st[0] - n, fold(st[0], n, False)

        return lax.while_loop(more, step, (j, cmax))

    j, cmax = j0, cmax0
    while n_loop >= 1:
        j, cmax = loop(n_loop, j, cmax)
        n_loop //= 2
    for p in range(n_pairs):
        acc = acc_ref[p]
        o_ref[0, :, p * LANES:(p + 1) * LANES] = jnp.where(lane < HEAD_DIM, acc[:QB], acc[QB:])


def _attention(q, k, v):
    B, S, W = q.shape
    return pl.pallas_call(
        _attn_kernel,
        grid=(B, S // Q_BLOCK),
        in_specs=[pl.BlockSpec((1, Q_BLOCK, W), lambda b, i: (b, i, 0)),
                  pl.BlockSpec((1, S, W), lambda b, i: (b, 0, 0)),
                  pl.BlockSpec((1, S, W), lambda b, i: (b, 0, 0))],
        out_specs=pl.BlockSpec((1, Q_BLOCK, W), lambda b, i: (b, i, 0)),
        out_shape=jax.ShapeDtypeStruct((B, S, W), F32),
        scratch_shapes=[pltpu.VMEM((W // LANES, 2 * Q_BLOCK, LANES), F32),
                        pltpu.VMEM((W // LANES, 2 * Q_BLOCK, LANES), F32)],
        compiler_params=_params(("arbitrary", "arbitrary")),
        name="attn",
    )(q, k, v)


PACK_ROWS = SUBLANES // 2


def _store_packed_rows(ref, value):
    n_rows, d = value.shape
    bits = lax.bitcast_convert_type(value.astype(BF16).astype(F32), jnp.uint32)
    words = (bits[:, :d // 2] >> 16) | bits[:, d // 2:]
    for c in range(PACK_ROWS):
        ref[pl.ds(c, n_rows, stride=PACK_ROWS), :] = words[:, c * LANES:(c + 1) * LANES]


def _load_packed_rows(ref, n_rows, lead=(), dtype=BF16):
    lo, hi = [], []
    for c in range(PACK_ROWS):
        w = ref[lead + (pl.ds(c, n_rows, stride=PACK_ROWS), slice(None))]
        lo.append(lax.bitcast_convert_type(w << 16, F32).astype(dtype))
        hi.append(lax.bitcast_convert_type(w & jnp.uint32(0xFFFF0000), F32).astype(dtype))
    return jnp.concatenate(lo + hi, axis=1)


def _outproj_kernel(x_ref, ya_ref, yl_ref, mod_ref, ga_ref, wo_ref, gf_ref, rw_ref, rb_ref,
                    x1_ref, h2_ref, idx_ref, prob_ref, rank_ref, cnt_ref, tri, run):
    W = SB_WIDTH
    ya = _rms(ya_ref[0], ga_ref[...]).astype(BF16)
    yl = yl_ref[0].astype(BF16)
    mix = (jnp.dot(ya, wo_ref[0:W, :], preferred_element_type=F32)
           + jnp.dot(yl, wo_ref[W:2 * W, :], preferred_element_type=F32))
    x1 = x_ref[0] + mod_ref[0, 2:3, :] * mix
    x1_ref[0] = x1
    h2 = _rms(x1, gf_ref[...]) * (1.0 + mod_ref[0, 4:5, :]) + mod_ref[0, 3:4, :]
    _store_packed_rows(h2_ref, h2)

    logits = lax.dot_general(rw_ref[...], h2, (((1,), (1,)), ((), ())), precision=HIGHEST,
                             preferred_element_type=F32) + rb_ref[...]
    n_exp = logits.shape[0]
    eid = lax.broadcasted_iota(jnp.int32, logits.shape, 0)
    vals, idxs = [], []
    for _ in range(TOP_K):
        m = jnp.max(logits, axis=0, keepdims=True)
        sel = jnp.min(jnp.where(logits == m, eid, n_exp), axis=0, keepdims=True)
        vals.append(m)
        idxs.append(sel)
        logits = jnp.where(eid == sel, -jnp.inf, logits)
    es = [jnp.exp(vv - vals[0]) for vv in vals]
    inv = 1.0 / (es[0] + es[1] + es[2] + es[3])

    first = (pl.program_id(0) == 0) & (pl.program_id(1) == 0)

    @pl.when(first)
    def _():
        ti = lax.broadcasted_iota(jnp.int32, tri.shape, 0)
        tj = lax.broadcasted_iota(jnp.int32, tri.shape, 1)
        tri[...] = jnp.where(ti < tj, 1.0, 0.0).astype(BF16)
        run[...] = jnp.zeros_like(run)

    hits = [eid == idxs[r] for r in range(TOP_K)]
    chosen = hits[0] | hits[1] | hits[2] | hits[3]
    cnt = jnp.where(chosen, 1.0, 0.0)
    before = jnp.dot(cnt.astype(BF16), tri[...], preferred_element_type=F32) + run[...]
    for r in range(TOP_K):
        idx_ref[0, r:r + 1, :] = idxs[r]
        prob_ref[0, r:r + 1, :] = es[r] * inv
        rank_ref[0, r:r + 1, :] = jnp.sum(jnp.where(hits[r], before, 0.0), axis=0,
                                          keepdims=True).astype(jnp.int32)
    run[...] = run[...] + jnp.sum(cnt, axis=1, keepdims=True)
    cnt_ref[...] = jnp.broadcast_to(run[...], cnt_ref.shape).astype(jnp.int32)


def _outproj(x, ya, yl, mod3, ga, w_out_b, gf, router_wt, router_b, ts=1024):
    B, S, D = x.shape
    W = ya.shape[2]
    NE = router_wt.shape[0]
    rowd = pl.BlockSpec((1, ts, D), lambda b, s: (b, s, 0))
    roww = pl.BlockSpec((1, ts, W), lambda b, s: (b, s, 0))
    sel = pl.BlockSpec((1, TOP_K, ts), lambda b, s: (b, 0, s))
    return pl.pallas_call(
        _outproj_kernel,
        grid=(B, S // ts),
        in_specs=[rowd, roww, roww,
                  pl.BlockSpec((1, N_MOD, D), lambda b, s: (b, 0, 0)),
                  pl.BlockSpec((1, W), lambda b, s: (0, 0)),
                  pl.BlockSpec((2 * W, D), lambda b, s: (0, 0)),
                  pl.BlockSpec((1, D), lambda b, s: (0, 0)),
                  pl.BlockSpec((NE, D), lambda b, s: (0, 0)),
                  pl.BlockSpec((NE, 1), lambda b, s: (0, 0))],
        out_specs=[rowd,
                   pl.BlockSpec((ts * PACK_ROWS, LANES), lambda b, s: (b * (S // ts) + s, 0)),
                   sel, sel, sel, pl.BlockSpec((NE, LANES), lambda b, s: (0, 0))],
        out_shape=[jax.ShapeDtypeStruct((B, S, D), F32),
                   jax.ShapeDtypeStruct((B * S * PACK_ROWS, LANES), jnp.uint32),
                   jax.ShapeDtypeStruct((B, TOP_K, S), jnp.int32),
                   jax.ShapeDtypeStruct((B, TOP_K, S), F32),
                   jax.ShapeDtypeStruct((B, TOP_K, S), jnp.int32),
                   jax.ShapeDtypeStruct((NE, LANES), jnp.int32)],
        scratch_shapes=[pltpu.VMEM((ts, ts), BF16), pltpu.VMEM((NE, 1), F32)],
        compiler_params=_params(("arbitrary", "arbitrary")),
        name="outproj",
    )(x, ya, yl, mod3, ga.reshape(1, W), w_out_b, gf.reshape(1, D), router_wt,
      router_b.reshape(NE, 1))


def _dispatch_kernel(p0_ref, p1_ref, p2_ref, p3_ref, fill_ref, h_ref, xs_ref, zbuf, sem, zsem):
    TT = h_ref.shape[0] // PACK_ROWS
    n_fill = fill_ref.shape[0]

    @pl.when(pl.program_id(0) == 0)
    def _():
        zbuf[...] = jnp.zeros_like(zbuf)

        def fill_copy(e):
            start = pl.multiple_of(jnp.maximum(fill_ref[e], 0), PACK_ROWS)
            return pltpu.make_async_copy(zbuf, xs_ref.at[pl.ds(start, zbuf.shape[0]), :], zsem)

        def start(e, c):
            @pl.when(fill_ref[e] >= 0)
            def _():
                fill_copy(e).start()
            return c

        def wait(e, c):
            @pl.when(fill_ref[e] >= 0)
            def _():
                fill_copy(e).wait()
            return c

        lax.fori_loop(0, n_fill, start, 0)
        lax.fori_loop(0, n_fill, wait, 0)

    pos_refs = (p0_ref, p1_ref, p2_ref, p3_ref)

    def row_copy(t, r):
        src = pl.multiple_of(t * PACK_ROWS, PACK_ROWS)
        dst = pl.multiple_of(pos_refs[r][t], PACK_ROWS)
        return pltpu.make_async_copy(h_ref.at[pl.ds(src, PACK_ROWS), :],
                                     xs_ref.at[pl.ds(dst, PACK_ROWS), :], sem)

    def issue(g, c):
        for tt in range(ROW_DMA_UNROLL):
            for r in range(TOP_K):
                row_copy(g * ROW_DMA_UNROLL + tt, r).start(priority=(tt * TOP_K + r) % 2)
        return c

    lax.fori_loop(0, TT // ROW_DMA_UNROLL, issue, 0)
    for r in range(TOP_K):
        pltpu.make_async_copy(h_ref, xs_ref.at[pl.ds(0, TT * PACK_ROWS), :], sem).wait()


def _slot_specs(tile, tiles_per_batch):
    def spec(r):
        return pl.BlockSpec(
            (tile,),
            lambda i: ((i // tiles_per_batch * TOP_K + r) * tiles_per_batch
                       + i % tiles_per_batch,),
            memory_space=pltpu.SMEM)
    return [spec(r) for r in range(TOP_K)]


def _dispatch(h2t, pos_flat, fill_start, cap, seq_len):
    TT = min(DISPATCH_TILE, seq_len)
    assert seq_len % TT == 0
    tiles_per_batch = seq_len // TT
    n_tiles = h2t.shape[0] // (TT * PACK_ROWS)
    return pl.pallas_call(
        _dispatch_kernel,
        grid=(n_tiles,),
        in_specs=_slot_specs(TT, tiles_per_batch)
        + [pl.BlockSpec(memory_space=pltpu.SMEM),
           pl.BlockSpec((TT * PACK_ROWS, LANES), lambda i: (i, 0))],
        out_specs=pl.BlockSpec(memory_space=pl.ANY),
        out_shape=jax.ShapeDtypeStruct((cap * PACK_ROWS, LANES), jnp.uint32),
        scratch_shapes=[pltpu.VMEM((EXPERT_BLOCK * PACK_ROWS, LANES), jnp.uint32),
                        pltpu.SemaphoreType.DMA, pltpu.SemaphoreType.DMA],
        compiler_params=_params(("arbitrary",)),
        name="dispatch",
    )(pos_flat, pos_flat, pos_flat, pos_flat, fill_start, h2t)


def _expert_kernel(first_ref, count_ref, xs_ref, wg_ref, bg_ref, wu_ref, bu_ref, wd_ref,
                   bd_ref, ys_ref, wgb, wub, wdb, xbuf, ybuf, xsem, ysem):
    e = pl.program_id(0)
    n_exp = pl.num_programs(0)
    rows = ybuf.shape[1]
    first, count = first_ref[e], count_ref[e]

    wgb[...] = wg_ref[0].astype(BF16)
    wub[...] = wu_ref[0].astype(BF16)
    wdb[...] = wd_ref[0].astype(BF16)

    def block(ref, buf, j):
        n = buf.shape[1]
        return ref.at[pl.ds(pl.multiple_of((first + j) * n, n), n), :]

    def x_copy(j, slot):
        return pltpu.make_async_copy(block(xs_ref, xbuf, j), xbuf.at[slot], xsem.at[slot])

    def y_copy(j, slot):
        return pltpu.make_async_copy(ybuf.at[slot], block(ys_ref, ybuf, j), ysem.at[slot])

    @pl.when(count > 0)
    def _():
        x_copy(0, 0).start()

    def step(j, c):
        slot = j % 2
        x_copy(j, slot).wait()

        @pl.when(j + 1 < count)
        def _():
            x_copy(j + 1, 1 - slot).start()

        @pl.when(j >= 2)
        def _():
            y_copy(j - 2, slot).wait()

        xb = _load_packed_rows(xbuf, EXPERT_BLOCK, (slot,))
        g = jnp.minimum(jnp.dot(xb, wgb[...], preferred_element_type=F32) + bg_ref[0],
                        SWIGLU_LIMIT)
        u = jnp.clip(jnp.dot(xb, wub[...], preferred_element_type=F32) + bu_ref[0],
                     -SWIGLU_LIMIT, SWIGLU_LIMIT)
        act = (u + 1.0) * (g * jax.nn.sigmoid(SWIGLU_ALPHA * g))
        _store_packed_rows(ybuf.at[slot], jnp.dot(act.astype(BF16), wdb[...],
                                                  preferred_element_type=F32) + bd_ref[0])
        y_copy(j, slot).start()
        return c

    lax.fori_loop(0, count, step, 0)

    @pl.when(count >= 2)
    def _():
        y_copy(count - 2, count % 2).wait()

    @pl.when(count >= 1)
    def _():
        y_copy(count - 1, (count - 1) % 2).wait()

    @pl.when(e == n_exp - 1)
    def _():
        spare_first, spare_count = first_ref[n_exp], count_ref[n_exp]
        ybuf[0] = jnp.zeros(ybuf.shape[1:], ybuf.dtype)

        def spare_copy(j):
            dst = pl.ds(pl.multiple_of((spare_first + j) * rows, rows), rows)
            return pltpu.make_async_copy(ybuf.at[0], ys_ref.at[dst, :], ysem.at[0])

        def start(j, c):
            spare_copy(j).start()
            return c

        def wait(j, c):
            spare_copy(j).wait()
            return c

        lax.fori_loop(0, spare_count, start, 0)
        lax.fori_loop(0, spare_count, wait, 0)


def _experts(xs, first_block, block_count, wg, bg, wu, bu, wd, bd):
    NE, D, DE = wg.shape
    cap = xs.shape[0] // PACK_ROWS

    def wsel(e, first, count):
        return (e, 0, 0)

    grid_spec = pltpu.PrefetchScalarGridSpec(
        num_scalar_prefetch=2,
        grid=(NE,),
        in_specs=[pl.BlockSpec(memory_space=pl.ANY),
                  pl.BlockSpec((1, D, DE), wsel), pl.BlockSpec((1, 1, DE), wsel),
                  pl.BlockSpec((1, D, DE), wsel), pl.BlockSpec((1, 1, DE), wsel),
                  pl.BlockSpec((1, DE, D), wsel), pl.BlockSpec((1, 1, D), wsel)],
        out_specs=pl.BlockSpec(memory_space=pl.ANY),
        scratch_shapes=[pltpu.VMEM((D, DE), BF16), pltpu.VMEM((D, DE), BF16),
                        pltpu.VMEM((DE, D), BF16),
                        pltpu.VMEM((2, EXPERT_BLOCK * PACK_ROWS, LANES), jnp.uint32),
                        pltpu.VMEM((2, EXPERT_BLOCK * PACK_ROWS, LANES), jnp.uint32),
                        pltpu.SemaphoreType.DMA((2,)), pltpu.SemaphoreType.DMA((2,))],
    )
    return pl.pallas_call(
        _expert_kernel,
        grid_spec=grid_spec,
        out_shape=jax.ShapeDtypeStruct((cap * PACK_ROWS, LANES), jnp.uint32),
        compiler_params=_params(("arbitrary",)),
        name="experts",
    )(first_block, block_count, xs, wg, bg.reshape(NE, 1, DE), wu, bu.reshape(NE, 1, DE), wd,
      bd.reshape(NE, 1, D))


def _combine_kernel(c0_ref, c1_ref, c2_ref, c3_ref, n0_ref, n1_ref, n2_ref, n3_ref,
                    x1_ref, p_ref, mod_ref, g_ref, ys_ref, o_ref, buf, sem):
    TT = x1_ref.shape[0]
    i = pl.program_id(0)
    R = PACK_ROWS

    def gather(pos_refs, s):
        def issue(g, c):
            for tt in range(ROW_DMA_UNROLL):
                t = g * ROW_DMA_UNROLL + tt
                for r in range(TOP_K):
                    pltpu.make_async_copy(
                        ys_ref.at[pl.ds(pl.multiple_of(pos_refs[r][t], R), R), :],
                        buf.at[s, r, pl.ds(pl.multiple_of(t * R, R), R), :],
                        sem.at[s]).start(priority=(tt * TOP_K + r) % 2)
            return c
        lax.fori_loop(0, TT // ROW_DMA_UNROLL, issue, 0)

    @pl.when(i == 0)
    def _():
        gather((c0_ref, c1_ref, c2_ref, c3_ref), 0)

    def reduce_tile(s):
        @pl.when(i + 1 < pl.num_programs(0))
        def _():
            gather((n0_ref, n1_ref, n2_ref, n3_ref), 1 - s)

        for r in range(TOP_K):
            pltpu.make_async_copy(ys_ref.at[pl.ds(0, TT * R), :], buf.at[s, r],
                                  sem.at[s]).wait()
        p = p_ref[...]
        moe = p[:, 0:1] * _load_packed_rows(buf, TT, (s, 0), F32)
        for r in range(1, TOP_K):
            moe = moe + p[:, r:r + 1] * _load_packed_rows(buf, TT, (s, r), F32)
        x2 = x1_ref[...] + mod_ref[0, 5:6, :] * moe
        o_ref[...] = _rms(x2, g_ref[...])

    for s in range(2):
        pl.when(i % 2 == s)(functools.partial(reduce_tile, s))


def _combine(x1f, probs, pos_flat, mod3, g, ys, tiles_per_batch):
    N, D = x1f.shape
    TT = COMBINE_TILE
    n_tiles = N // TT

    def next_spec(r):
        def index(i):
            j = jnp.minimum(i + 1, n_tiles - 1)
            return ((j // tiles_per_batch * TOP_K + r) * tiles_per_batch + j % tiles_per_batch,)
        return pl.BlockSpec((TT,), index, memory_space=pltpu.SMEM)

    return pl.pallas_call(
        _combine_kernel,
        grid=(n_tiles,),
        in_specs=_slot_specs(TT, tiles_per_batch) + [next_spec(r) for r in range(TOP_K)]
        + [pl.BlockSpec((TT, D), lambda i: (i, 0)),
           pl.BlockSpec((TT, TOP_K), lambda i: (i, 0)),
           pl.BlockSpec((1, N_MOD, D), lambda i: (i // tiles_per_batch, 0, 0)),
           pl.BlockSpec((1, D), lambda i: (0, 0)),
           pl.BlockSpec(memory_space=pl.ANY)],
        out_specs=pl.BlockSpec((TT, D), lambda i: (i, 0)),
        out_shape=jax.ShapeDtypeStruct((N, D), F32),
        scratch_shapes=[pltpu.VMEM((2, TOP_K, TT * PACK_ROWS, LANES), jnp.uint32),
                        pltpu.SemaphoreType.DMA((2,))],
        compiler_params=_params(("arbitrary",)),
        name="combine",
    )(*([pos_flat] * (2 * TOP_K)), x1f, probs, mod3, g.reshape(1, D), ys)


def _routing(idx, rank, counts):
    B, K, S = idx.shape
    n_exp = counts.shape[0]
    TM = EXPERT_BLOCK
    n_assign = B * S * K
    padded = ((counts + TM - 1) // TM) * TM
    pad_ends = jnp.cumsum(padded)
    pad_starts = pad_ends - padded
    experts = jnp.arange(n_exp, dtype=jnp.int32).reshape(n_exp, 1, 1, 1)
    base = jnp.sum(jnp.where(idx[None] == experts, pad_starts.reshape(n_exp, 1, 1, 1), 0),
                   axis=0)
    pos = (base + rank).astype(jnp.int32).reshape(n_assign)
    n_blocks = n_assign // TM + n_exp
    first_block = jnp.concatenate([pad_starts, pad_ends[-1:]]) // TM
    block_count = jnp.concatenate([padded // TM, n_blocks - pad_ends[-1:] // TM])
    tail_start = jnp.where(padded > 0, pad_ends - TM, -1)
    spare = pad_ends[-1] + jnp.arange(n_exp, dtype=jnp.int32) * TM
    fill_start = jnp.concatenate(
        [tail_start, jnp.where(spare < n_blocks * TM, spare, -1)]).astype(jnp.int32)
    return (pos, first_block.astype(jnp.int32), block_count.astype(jnp.int32),
            fill_start, n_blocks * TM)


def kernel(x, c, ada_w, ada_b, mix_norm_g, w_in, conv_w, conv_b, rg_w, rg_b, ig_w, ig_b,
           lru_lambda, attn_out_g, lru_out_g, w_out, ffn_norm_g, router_w, router_b,
           exp_w_gate, exp_b_gate, exp_w_up, exp_b_up, exp_w_down, exp_b_down, final_norm_g):
    B, S, D = x.shape
    depth = ada_w.shape[0]
    assert S % COMBINE_TILE == 0 and S % 512 == 0
    assert D == SUBLANES * LANES
    for l in range(depth):
        mod3 = _ada(c, ada_w[l], ada_b[l]).reshape(B, N_MOD, D)
        q, k, v, yl = _inproj(x, mod3, mix_norm_g[l], w_in[l].astype(BF16), conv_w[l],
                              conv_b[l], rg_w[l], rg_b[l], ig_w[l], ig_b[l], lru_lambda[l],
                              lru_out_g[l])
        ya = _attention(q, k, v)
        x1, h2, idx, prob, rank, cnt = _outproj(
            x, ya, yl, mod3, attn_out_g[l], w_out[l].astype(BF16), ffn_norm_g[l],
            router_w[l].T, router_b[l])
        pos, first_block, block_count, fill_start, cap = _routing(idx, rank, cnt[:, 0])
        xs = _dispatch(h2, pos * PACK_ROWS, fill_start * PACK_ROWS, cap, S)
        ys = _experts(xs, first_block, block_count, exp_w_gate[l], exp_b_gate[l],
                      exp_w_up[l], exp_b_up[l], exp_w_down[l], exp_b_down[l])
        probs = jnp.transpose(prob, (0, 2, 1)).reshape(B * S, TOP_K)
        assert depth == 1
        x = _combine(x1.reshape(B * S, D), probs, pos * PACK_ROWS, mod3, final_norm_g, ys,
                     S // COMBINE_TILE).reshape(B, S, D)
    return x
```

```python
import functools

import jax
import jax.numpy as jnp
from jax import lax
from jax.experimental import pallas as pl
from jax.experimental.pallas import tpu as pltpu

F32 = jnp.float32
BF16 = jnp.bfloat16
HIGHEST = lax.Precision.HIGHEST

EPS = 1e-6
N_MOD = 6
SB_HEADS = 8
HEAD_DIM = 64
SB_WIDTH = SB_HEADS * HEAD_DIM
LRU_BLOCKS = 8
CONV_WIDTH = 4
LRU_C = 8.0
TOP_K = 4
SWIGLU_LIMIT = 7.0
SWIGLU_ALPHA = 1.702
LOG2E = 1.4426950408889634

LANES = 128
SUBLANES = 8
VMEM_LIMIT = 56 * 1024 * 1024

Q_BLOCK = 128
K_BLOCK = 128
ATTN_WINDOW_BLOCKS = 6
ATTN_LOOP_BLOCKS = 2
ATTN_UNDERFLOW_LOG = -110.0
EXPERT_BLOCK = 512
DISPATCH_TILE = 4096
COMBINE_TILE = 256
ROW_DMA_UNROLL = 8
LRU_SCAN_ROWS = 256


def _params(sem):
    return pltpu.CompilerParams(dimension_semantics=sem, vmem_limit_bytes=VMEM_LIMIT)


def _ada_kernel(c_ref, w_ref, b_ref, o_ref):
    c = c_ref[...]
    ca = c * jax.nn.sigmoid(c)
    o_ref[...] = jnp.dot(ca, w_ref[...], precision=HIGHEST,
                         preferred_element_type=F32) + b_ref[...]


def _ada(c, ada_w, ada_b):
    B, D = c.shape
    E = ada_w.shape[1]
    tn = 1024
    return pl.pallas_call(
        _ada_kernel,
        grid=(E // tn,),
        in_specs=[pl.BlockSpec((B, D), lambda j: (0, 0)),
                  pl.BlockSpec((D, tn), lambda j: (0, j)),
                  pl.BlockSpec((1, tn), lambda j: (0, j))],
        out_specs=pl.BlockSpec((B, tn), lambda j: (0, j)),
        out_shape=jax.ShapeDtypeStruct((B, E), F32),
        compiler_params=_params(("arbitrary",)),
        name="ada",
    )(c, ada_w, ada_b.reshape(1, E))


def _rms(x, g):
    ms = jnp.mean(x * x, axis=-1, keepdims=True)
    return x * lax.rsqrt(ms + EPS) * g


def _softplus(x):
    return jnp.maximum(x, 0.0) + jnp.log1p(jnp.exp(-jnp.abs(x)))


def _gelu_tanh(x):
    return 0.5 * x * (1.0 + jnp.tanh(0.7978845608028654 * (x + 0.044715 * x * x * x)))


LRU_PAD = 8


def _lru_tile(x, gate, cw_ref, cb_ref, wr_ref, br_ref, wi_ref, bi_ref, lam_ref, g_ref,
              xext, hc):
    T = x.shape[0]
    xext[LRU_PAD:LRU_PAD + T, :] = x
    xc = cb_ref[...] + cw_ref[CONV_WIDTH - 1:CONV_WIDTH, :] * x
    for j in range(CONV_WIDTH - 1):
        back = CONV_WIDTH - 1 - j
        xc = xc + cw_ref[j:j + 1, :] * xext[LRU_PAD - back:LRU_PAD - back + T, :]
    xext[0:LRU_PAD, :] = xext[T:T + LRU_PAD, :]

    xb = xc.astype(BF16)
    r = jax.nn.sigmoid(jnp.dot(xb, wr_ref[...], preferred_element_type=F32) + br_ref[...])
    ig = jax.nn.sigmoid(jnp.dot(xb, wi_ref[...], preferred_element_type=F32) + bi_ref[...])
    log_a = (-LRU_C) * r * _softplus(-lam_ref[...])
    a = jnp.exp(log_a)
    b = jnp.sqrt(-jnp.tanh(log_a) * (a * a + 1.0)) * (ig * xc)

    G = T // SUBLANES
    a = a.reshape(G, SUBLANES, -1)
    b = b.reshape(G, SUBLANES, -1)
    rows = lax.broadcasted_iota(jnp.int32, a.shape, 1)
    d = 1
    while d < SUBLANES:
        keep = rows >= d
        a_prev = jnp.where(keep, pltpu.roll(a, d, 1), 1.0)
        b_prev = jnp.where(keep, pltpu.roll(b, d, 1), 0.0)
        b = a * b_prev + b
        a = a * a_prev
        d *= 2
    prev = hc[...]
    groups = []
    for g in range(G):
        hg = a[g] * prev + b[g]
        prev = hg[SUBLANES - 1:SUBLANES, :]
        groups.append(hg)
    hc[...] = prev
    h = jnp.concatenate(groups, axis=0)
    return _rms(h * _gelu_tanh(gate), g_ref[...])


def _inproj_kernel(x_ref, mod_ref, g_ref, w_ref, cw_ref, cb_ref, wr_ref, br_ref, wi_ref,
                   bi_ref, lam_ref, gl_ref, q_ref, k_ref, v_ref, yl_ref, xext, hc):
    @pl.when(pl.program_id(1) == 0)
    def _():
        xext[0:LRU_PAD, :] = jnp.zeros((LRU_PAD, xext.shape[1]), F32)
        hc[...] = jnp.zeros_like(hc)

    x = x_ref[0]
    h = _rms(x, g_ref[...]) * (1.0 + mod_ref[0, 1:2, :]) + mod_ref[0, 0:1, :]
    hb = h.astype(BF16)
    W = SB_WIDTH

    def proj(c):
        return jnp.dot(hb, w_ref[:, c * W:(c + 1) * W], preferred_element_type=F32)

    xr, gr = proj(3), proj(4)
    T = LRU_SCAN_ROWS
    for t in range(x.shape[0] // T):
        rows = slice(t * T, (t + 1) * T)
        yl_ref[0, rows, :] = _lru_tile(xr[rows], gr[rows], cw_ref, cb_ref, wr_ref, br_ref,
                                       wi_ref, bi_ref, lam_ref, gl_ref, xext, hc)
    q_ref[0] = (proj(0) * (HEAD_DIM ** -0.5)).astype(BF16)
    k_ref[0] = proj(1).astype(BF16)
    v_ref[0] = proj(2).astype(BF16)


def _block_diag(w):
    H, I, J = w.shape
    eye = jnp.eye(H, dtype=w.dtype)
    return (w[:, :, None, :] * eye[:, None, :, None]).reshape(H * I, H * J)


def _inproj(x, mod3, g, w_in_b, conv_w, conv_b, rg_w, rg_b, ig_w, ig_b, lam, g_lru, ts=1024):
    B, S, D = x.shape
    E = w_in_b.shape[1]
    W = SB_WIDTH
    row = pl.BlockSpec((1, ts, W), lambda b, s: (b, s, 0))
    vec = pl.BlockSpec((1, W), lambda b, s: (0, 0))
    mat = pl.BlockSpec((W, W), lambda b, s: (0, 0))
    return pl.pallas_call(
        _inproj_kernel,
        grid=(B, S // ts),
        in_specs=[pl.BlockSpec((1, ts, D), lambda b, s: (b, s, 0)),
                  pl.BlockSpec((1, N_MOD, D), lambda b, s: (b, 0, 0)),
                  pl.BlockSpec((1, D), lambda b, s: (0, 0)),
                  pl.BlockSpec((D, E), lambda b, s: (0, 0)),
                  pl.BlockSpec((CONV_WIDTH, W), lambda b, s: (0, 0)), vec,
                  mat, vec, mat, vec, vec, vec],
        out_specs=[row, row, row, row],
        out_shape=[jax.ShapeDtypeStruct((B, S, W), BF16)] * 3
        + [jax.ShapeDtypeStruct((B, S, W), F32)],
        scratch_shapes=[pltpu.VMEM((LRU_SCAN_ROWS + LRU_PAD, W), F32), pltpu.VMEM((1, W), F32)],
        compiler_params=_params(("arbitrary", "arbitrary")),
        name="inproj",
    )(x, mod3, g.reshape(1, D), w_in_b, conv_w, conv_b.reshape(1, W),
      _block_diag(rg_w).astype(BF16), rg_b.reshape(1, W), _block_diag(ig_w).astype(BF16),
      ig_b.reshape(1, W), lam.reshape(1, W), g_lru.reshape(1, W))


def _attn_kernel(q_ref, k_ref, v_ref, o_ref, acc_ref, carry_ref):
    i = pl.program_id(1)
    QB, KB = Q_BLOCK, K_BLOCK
    n_pairs = q_ref.shape[2] // LANES
    lane = lax.broadcasted_iota(jnp.int32, (QB, LANES), 1)

    def stacked_q(p):
        q = q_ref[0, :, p * LANES:(p + 1) * LANES]
        zero = jnp.zeros_like(q)
        return jnp.concatenate([jnp.where(lane < HEAD_DIM, q, zero),
                                jnp.where(lane >= HEAD_DIM, q, zero)], axis=0)

    qs = [stacked_q(p) for p in range(n_pairs)]

    uj = lax.broadcasted_iota(jnp.int32, (2 * KB, KB + LANES), 0) & (KB - 1)
    us = lax.broadcasted_iota(jnp.int32, (2 * KB, KB + LANES), 1)
    u2 = jnp.where((us >= KB) | (uj > us), -1.0, 0.0).astype(BF16)

    row = lax.broadcasted_iota(jnp.int32, (2 * QB, KB), 0) & (QB - 1)
    col = lax.broadcasted_iota(jnp.int32, (2 * QB, KB), 1)
    causal = col < row

    def tiles(p, j_hi, n, diagonal, acc, carry):
        start = pl.multiple_of((j_hi - (n - 1)) * KB, KB)
        kw = k_ref[0, pl.ds(start, n * KB), p * LANES:(p + 1) * LANES]
        vw = v_ref[0, pl.ds(start, n * KB), p * LANES:(p + 1) * LANES]
        z = lax.dot_general(qs[p], kw, (((1,), (1,)), ((), ())), preferred_element_type=F32)
        softplus = jnp.maximum(z, 0.0) + jnp.log(1.0 + jnp.exp2(jnp.abs(z) * (-LOG2E)))
        log_beta = z - softplus
        ws = [None] * n
        for c in reversed(range(n)):
            sp = softplus[:, c * KB:(c + 1) * KB]
            masked = diagonal and c == n - 1
            if masked:
                sp = jnp.where(causal, sp, 0.0)
            hi_f = lax.bitcast_convert_type(
                lax.bitcast_convert_type(sp, jnp.uint32) & jnp.uint32(0xFFFF0000), F32)
            hi_lo = jnp.concatenate([hi_f.astype(BF16), (sp - hi_f).astype(BF16)], axis=1)
            sums = jnp.dot(hi_lo, u2, preferred_element_type=F32)
            w = jnp.exp2((log_beta[:, c * KB:(c + 1) * KB] + sums[:, :KB] + carry) * LOG2E)
            if masked:
                w = jnp.where(causal, w, 0.0)
            ws[c] = w.astype(BF16)
            carry = carry + sums[:, KB:]
        w_all = ws[0] if n == 1 else jnp.concatenate(ws, axis=1)
        acc = acc + jnp.dot(w_all, vw, preferred_element_type=F32)
        return acc, carry

    def fold(j_hi, n, diagonal):
        cmax = None
        for p in range(n_pairs):
            if diagonal:
                acc = carry = jnp.zeros((2 * QB, LANES), F32)
            else:
                acc, carry = acc_ref[p], carry_ref[p]
            acc, carry = tiles(p, j_hi, n, diagonal, acc, carry)
            acc_ref[p] = acc
            carry_ref[p] = carry
            cmax = carry if cmax is None else jnp.maximum(cmax, carry)
        return jnp.max(cmax)

    n_first, n_loop = ATTN_WINDOW_BLOCKS, ATTN_LOOP_BLOCKS
    n0 = jnp.minimum(i + 1, n_first)
    cmax0 = lax.switch(n0 - 1, [functools.partial(fold, i, n, True)
                                for n in range(1, n_first + 1)])
    j0 = i - n0

    def loop(n, j, cmax):
        def more(st):
            return (st[0] >= n - 1) & (st[1] >= ATTN_UNDERFLOW_LOG)

        def step(st):
            return st[0] - n, fold(st[0], n, False)

        return lax.while_loop(more, step, (j, cmax))

    j, cmax = j0, cmax0
    while n_loop >= 1:
        j, cmax = loop(n_loop, j, cmax)
        n_loop //= 2
    for p in range(n_pairs):
        acc = acc_ref[p]
        o_ref[0, :, p * LANES:(p + 1) * LANES] = jnp.where(lane < HEAD_DIM, acc[:QB], acc[QB:])


def _attention(q, k, v):
    B, S, W = q.shape
    return pl.pallas_call(
        _attn_kernel,
        grid=(B, S // Q_BLOCK),
        in_specs=[pl.BlockSpec((1, Q_BLOCK, W), lambda b, i: (b, i, 0)),
                  pl.BlockSpec((1, S, W), lambda b, i: (b, 0, 0)),
                  pl.BlockSpec((1, S, W), lambda b, i: (b, 0, 0))],
        out_specs=pl.BlockSpec((1, Q_BLOCK, W), lambda b, i: (b, i, 0)),
        out_shape=jax.ShapeDtypeStruct((B, S, W), F32),
        scratch_shapes=[pltpu.VMEM((W // LANES, 2 * Q_BLOCK, LANES), F32),
                        pltpu.VMEM((W // LANES, 2 * Q_BLOCK, LANES), F32)],
        compiler_params=_params(("arbitrary", "arbitrary")),
        name="attn",
    )(q, k, v)


PACK_ROWS = SUBLANES // 2


def _store_packed_rows(ref, value):
    n_rows, d = value.shape
    bits = lax.bitcast_convert_type(value.astype(BF16).astype(F32), jnp.uint32)
    words = (bits[:, :d // 2] >> 16) | bits[:, d // 2:]
    for c in range(PACK_ROWS):
        ref[pl.ds(c, n_rows, stride=PACK_ROWS), :] = words[:, c * LANES:(c + 1) * LANES]


def _load_packed_rows(ref, n_rows, lead=(), dtype=BF16):
    lo, hi = [], []
    for c in range(PACK_ROWS):
        w = ref[lead + (pl.ds(c, n_rows, stride=PACK_ROWS), slice(None))]
        lo.append(lax.bitcast_convert_type(w << 16, F32).astype(dtype))
        hi.append(lax.bitcast_convert_type(w & jnp.uint32(0xFFFF0000), F32).astype(dtype))
    return jnp.concatenate(lo + hi, axis=1)


def _outproj_kernel(x_ref, ya_ref, yl_ref, mod_ref, ga_ref, wo_ref, gf_ref, rw_ref, rb_ref,
                    x1_ref, h2_ref, idx_ref, prob_ref, rank_ref, cnt_ref, tri, run):
    W = SB_WIDTH
    ya = _rms(ya_ref[0], ga_ref[...]).astype(BF16)
    yl = yl_ref[0].astype(BF16)
    mix = (jnp.dot(ya, wo_ref[0:W, :], preferred_element_type=F32)
           + jnp.dot(yl, wo_ref[W:2 * W, :], preferred_element_type=F32))
    x1 = x_ref[0] + mod_ref[0, 2:3, :] * mix
    x1_ref[0] = x1
    h2 = _rms(x1, gf_ref[...]) * (1.0 + mod_ref[0, 4:5, :]) + mod_ref[0, 3:4, :]
    _store_packed_rows(h2_ref, h2)

    logits = lax.dot_general(rw_ref[...], h2, (((1,), (1,)), ((), ())), precision=HIGHEST,
                             preferred_element_type=F32) + rb_ref[...]
    n_exp = logits.shape[0]
    eid = lax.broadcasted_iota(jnp.int32, logits.shape, 0)
    vals, idxs = [], []
    for _ in range(TOP_K):
        m = jnp.max(logits, axis=0, keepdims=True)
        sel = jnp.min(jnp.where(logits == m, eid, n_exp), axis=0, keepdims=True)
        vals.append(m)
        idxs.append(sel)
        logits = jnp.where(eid == sel, -jnp.inf, logits)
    es = [jnp.exp(vv - vals[0]) for vv in vals]
    inv = 1.0 / (es[0] + es[1] + es[2] + es[3])

    first = (pl.program_id(0) == 0) & (pl.program_id(1) == 0)

    @pl.when(first)
    def _():
        ti = lax.broadcasted_iota(jnp.int32, tri.shape, 0)
        tj = lax.broadcasted_iota(jnp.int32, tri.shape, 1)
        tri[...] = jnp.where(ti < tj, 1.0, 0.0).astype(BF16)
        run[...] = jnp.zeros_like(run)

    hits = [eid == idxs[r] for r in range(TOP_K)]
    chosen = hits[0] | hits[1] | hits[2] | hits[3]
    cnt = jnp.where(chosen, 1.0, 0.0)
    before = jnp.dot(cnt.astype(BF16), tri[...], preferred_element_type=F32) + run[...]
    for r in range(TOP_K):
        idx_ref[0, r:r + 1, :] = idxs[r]
        prob_ref[0, r:r + 1, :] = es[r] * inv
        rank_ref[0, r:r + 1, :] = jnp.sum(jnp.where(hits[r], before, 0.0), axis=0,
                                          keepdims=True).astype(jnp.int32)
    run[...] = run[...] + jnp.sum(cnt, axis=1, keepdims=True)
    cnt_ref[...] = jnp.broadcast_to(run[...], cnt_ref.shape).astype(jnp.int32)


def _outproj(x, ya, yl, mod3, ga, w_out_b, gf, router_wt, router_b, ts=1024):
    B, S, D = x.shape
    W = ya.shape[2]
    NE = router_wt.shape[0]
    rowd = pl.BlockSpec((1, ts, D), lambda b, s: (b, s, 0))
    roww = pl.BlockSpec((1, ts, W), lambda b, s: (b, s, 0))
    sel = pl.BlockSpec((1, TOP_K, ts), lambda b, s: (b, 0, s))
    return pl.pallas_call(
        _outproj_kernel,
        grid=(B, S // ts),
        in_specs=[rowd, roww, roww,
                  pl.BlockSpec((1, N_MOD, D), lambda b, s: (b, 0, 0)),
                  pl.BlockSpec((1, W), lambda b, s: (0, 0)),
                  pl.BlockSpec((2 * W, D), lambda b, s: (0, 0)),
                  pl.BlockSpec((1, D), lambda b, s: (0, 0)),
                  pl.BlockSpec((NE, D), lambda b, s: (0, 0)),
                  pl.BlockSpec((NE, 1), lambda b, s: (0, 0))],
        out_specs=[rowd,
                   pl.BlockSpec((ts * PACK_ROWS, LANES), lambda b, s: (b * (S // ts) + s, 0)),
                   sel, sel, sel, pl.BlockSpec((NE, LANES), lambda b, s: (0, 0))],
        out_shape=[jax.ShapeDtypeStruct((B, S, D), F32),
                   jax.ShapeDtypeStruct((B * S * PACK_ROWS, LANES), jnp.uint32),
                   jax.ShapeDtypeStruct((B, TOP_K, S), jnp.int32),
                   jax.ShapeDtypeStruct((B, TOP_K, S), F32),
                   jax.ShapeDtypeStruct((B, TOP_K, S), jnp.int32),
                   jax.ShapeDtypeStruct((NE, LANES), jnp.int32)],
        scratch_shapes=[pltpu.VMEM((ts, ts), BF16), pltpu.VMEM((NE, 1), F32)],
        compiler_params=_params(("arbitrary", "arbitrary")),
        name="outproj",
    )(x, ya, yl, mod3, ga.reshape(1, W), w_out_b, gf.reshape(1, D), router_wt,
      router_b.reshape(NE, 1))


def _dispatch_kernel(p0_ref, p1_ref, p2_ref, p3_ref, fill_ref, h_ref, xs_ref, zbuf, sem, zsem):
    TT = h_ref.shape[0] // PACK_ROWS
    n_fill = fill_ref.shape[0]

    @pl.when(pl.program_id(0) == 0)
    def _():
        zbuf[...] = jnp.zeros_like(zbuf)

        def fill_copy(e):
            start = pl.multiple_of(jnp.maximum(fill_ref[e], 0), PACK_ROWS)
            return pltpu.make_async_copy(zbuf, xs_ref.at[pl.ds(start, zbuf.shape[0]), :], zsem)

        def start(e, c):
            @pl.when(fill_ref[e] >= 0)
            def _():
                fill_copy(e).start()
            return c

        def wait(e, c):
            @pl.when(fill_ref[e] >= 0)
            def _():
                fill_copy(e).wait()
            return c

        lax.fori_loop(0, n_fill, start, 0)
        lax.fori_loop(0, n_fill, wait, 0)

    pos_refs = (p0_ref, p1_ref, p2_ref, p3_ref)

    def row_copy(t, r):
        src = pl.multiple_of(t * PACK_ROWS, PACK_ROWS)
        dst = pl.multiple_of(pos_refs[r][t], PACK_ROWS)
        return pltpu.make_async_copy(h_ref.at[pl.ds(src, PACK_ROWS), :],
                                     xs_ref.at[pl.ds(dst, PACK_ROWS), :], sem)

    def issue(g, c):
        for tt in range(ROW_DMA_UNROLL):
            for r in range(TOP_K):
                row_copy(g * ROW_DMA_UNROLL + tt, r).start(priority=(tt * TOP_K + r) % 2)
        return c

    lax.fori_loop(0, TT // ROW_DMA_UNROLL, issue, 0)
    for r in range(TOP_K):
        pltpu.make_async_copy(h_ref, xs_ref.at[pl.ds(0, TT * PACK_ROWS), :], sem).wait()


def _slot_specs(tile, tiles_per_batch):
    def spec(r):
        return pl.BlockSpec(
            (tile,),
            lambda i: ((i // tiles_per_batch * TOP_K + r) * tiles_per_batch
                       + i % tiles_per_batch,),
            memory_space=pltpu.SMEM)
    return [spec(r) for r in range(TOP_K)]


def _dispatch(h2t, pos_flat, fill_start, cap, seq_len):
    TT = min(DISPATCH_TILE, seq_len)
    assert seq_len % TT == 0
    tiles_per_batch = seq_len // TT
    n_tiles = h2t.shape[0] // (TT * PACK_ROWS)
    return pl.pallas_call(
        _dispatch_kernel,
        grid=(n_tiles,),
        in_specs=_slot_specs(TT, tiles_per_batch)
        + [pl.BlockSpec(memory_space=pltpu.SMEM),
           pl.BlockSpec((TT * PACK_ROWS, LANES), lambda i: (i, 0))],
        out_specs=pl.BlockSpec(memory_space=pl.ANY),
        out_shape=jax.ShapeDtypeStruct((cap * PACK_ROWS, LANES), jnp.uint32),
        scratch_shapes=[pltpu.VMEM((EXPERT_BLOCK * PACK_ROWS, LANES), jnp.uint32),
                        pltpu.SemaphoreType.DMA, pltpu.SemaphoreType.DMA],
        compiler_params=_params(("arbitrary",)),
        name="dispatch",
    )(pos_flat, pos_flat, pos_flat, pos_flat, fill_start, h2t)


def _expert_kernel(first_ref, count_ref, xs_ref, wg_ref, bg_ref, wu_ref, bu_ref, wd_ref,
                   bd_ref, ys_ref, wgb, wub, wdb, xbuf, ybuf, xsem, ysem):
    e = pl.program_id(0)
    n_exp = pl.num_programs(0)
    rows = ybuf.shape[1]
    first, count = first_ref[e], count_ref[e]

    wgb[...] = wg_ref[0].astype(BF16)
    wub[...] = wu_ref[0].astype(BF16)
    wdb[...] = wd_ref[0].astype(BF16)

    def block(ref, buf, j):
        n = buf.shape[1]
        return ref.at[pl.ds(pl.multiple_of((first + j) * n, n), n), :]

    def x_copy(j, slot):
        return pltpu.make_async_copy(block(xs_ref, xbuf, j), xbuf.at[slot], xsem.at[slot])

    def y_copy(j, slot):
        return pltpu.make_async_copy(ybuf.at[slot], block(ys_ref, ybuf, j), ysem.at[slot])

    @pl.when(count > 0)
    def _():
        x_copy(0, 0).start()

    def step(j, c):
        slot = j % 2
        x_copy(j, slot).wait()

        @pl.when(j + 1 < count)
        def _():
            x_copy(j + 1, 1 - slot).start()

        @pl.when(j >= 2)
        def _():
            y_copy(j - 2, slot).wait()

        xb = _load_packed_rows(xbuf, EXPERT_BLOCK, (slot,))
        g = jnp.minimum(jnp.dot(xb, wgb[...], preferred_element_type=F32) + bg_ref[0],
                        SWIGLU_LIMIT)
        u = jnp.clip(jnp.dot(xb, wub[...], preferred_element_type=F32) + bu_ref[0],
                     -SWIGLU_LIMIT, SWIGLU_LIMIT)
        act = (u + 1.0) * (g * jax.nn.sigmoid(SWIGLU_ALPHA * g))
        _store_packed_rows(ybuf.at[slot], jnp.dot(act.astype(BF16), wdb[...],
                                                  preferred_element_type=F32) + bd_ref[0])
        y_copy(j, slot).start()
        return c

    lax.fori_loop(0, count, step, 0)

    @pl.when(count >= 2)
    def _():
        y_copy(count - 2, count % 2).wait()

    @pl.when(count >= 1)
    def _():
        y_copy(count - 1, (count - 1) % 2).wait()

    @pl.when(e == n_exp - 1)
    def _():
        spare_first, spare_count = first_ref[n_exp], count_ref[n_exp]
        ybuf[0] = jnp.zeros(ybuf.shape[1:], ybuf.dtype)

        def spare_copy(j):
            dst = pl.ds(pl.multiple_of((spare_first + j) * rows, rows), rows)
            return pltpu.make_async_copy(ybuf.at[0], ys_ref.at[dst, :], ysem.at[0])

        def start(j, c):
            spare_copy(j).start()
            return c

        def wait(j, c):
            spare_copy(j).wait()
            return c

        lax.fori_loop(0, spare_count, start, 0)
        lax.fori_loop(0, spare_count, wait, 0)


def _experts(xs, first_block, block_count, wg, bg, wu, bu, wd, bd):
    NE, D, DE = wg.shape
    cap = xs.shape[0] // PACK_ROWS

    def wsel(e, first, count):
        return (e, 0, 0)

    grid_spec = pltpu.PrefetchScalarGridSpec(
        num_scalar_prefetch=2,
        grid=(NE,),
        in_specs=[pl.BlockSpec(memory_space=pl.ANY),
                  pl.BlockSpec((1, D, DE), wsel), pl.BlockSpec((1, 1, DE), wsel),
                  pl.BlockSpec((1, D, DE), wsel), pl.BlockSpec((1, 1, DE), wsel),
                  pl.BlockSpec((1, DE, D), wsel), pl.BlockSpec((1, 1, D), wsel)],
        out_specs=pl.BlockSpec(memory_space=pl.ANY),
        scratch_shapes=[pltpu.VMEM((D, DE), BF16), pltpu.VMEM((D, DE), BF16),
                        pltpu.VMEM((DE, D), BF16),
                        pltpu.VMEM((2, EXPERT_BLOCK * PACK_ROWS, LANES), jnp.uint32),
                        pltpu.VMEM((2, EXPERT_BLOCK * PACK_ROWS, LANES), jnp.uint32),
                        pltpu.SemaphoreType.DMA((2,)), pltpu.SemaphoreType.DMA((2,))],
    )
    return pl.pallas_call(
        _expert_kernel,
        grid_spec=grid_spec,
        out_shape=jax.ShapeDtypeStruct((cap * PACK_ROWS, LANES), jnp.uint32),
        compiler_params=_params(("arbitrary",)),
        name="experts",
    )(first_block, block_count, xs, wg, bg.reshape(NE, 1, DE), wu, bu.reshape(NE, 1, DE), wd,
      bd.reshape(NE, 1, D))


def _combine_kernel(c0_ref, c1_ref, c2_ref, c3_ref, n0_ref, n1_ref, n2_ref, n3_ref,
                    x1_ref, p_ref, mod_ref, g_ref, ys_ref, o_ref, buf, sem):
    TT = x1_ref.shape[0]
    i = pl.program_id(0)
    R = PACK_ROWS

    def gather(pos_refs, s):
        def issue(g, c):
            for tt in range(ROW_DMA_UNROLL):
                t = g * ROW_DMA_UNROLL + tt
                for r in range(TOP_K):
                    pltpu.make_async_copy(
                        ys_ref.at[pl.ds(pl.multiple_of(pos_refs[r][t], R), R), :],
                        buf.at[s, r, pl.ds(pl.multiple_of(t * R, R), R), :],
                        sem.at[s]).start(priority=(tt * TOP_K + r) % 2)
            return c
        lax.fori_loop(0, TT // ROW_DMA_UNROLL, issue, 0)

    @pl.when(i == 0)
    def _():
        gather((c0_ref, c1_ref, c2_ref, c3_ref), 0)

    def reduce_tile(s):
        @pl.when(i + 1 < pl.num_programs(0))
        def _():
            gather((n0_ref, n1_ref, n2_ref, n3_ref), 1 - s)

        for r in range(TOP_K):
            pltpu.make_async_copy(ys_ref.at[pl.ds(0, TT * R), :], buf.at[s, r],
                                  sem.at[s]).wait()
        p = p_ref[...]
        moe = p[:, 0:1] * _load_packed_rows(buf, TT, (s, 0), F32)
        for r in range(1, TOP_K):
            moe = moe + p[:, r:r + 1] * _load_packed_rows(buf, TT, (s, r), F32)
        x2 = x1_ref[...] + mod_ref[0, 5:6, :] * moe
        o_ref[...] = _rms(x2, g_ref[...])

    for s in range(2):
        pl.when(i % 2 == s)(functools.partial(reduce_tile, s))


def _combine(x1f, probs, pos_flat, mod3, g, ys, tiles_per_batch):
    N, D = x1f.shape
    TT = COMBINE_TILE
    n_tiles = N // TT

    def next_spec(r):
        def index(i):
            j = jnp.minimum(i + 1, n_tiles - 1)
            return ((j // tiles_per_batch * TOP_K + r) * tiles_per_batch + j % tiles_per_batch,)
        return pl.BlockSpec((TT,), index, memory_space=pltpu.SMEM)

    return pl.pallas_call(
        _combine_kernel,
        grid=(n_tiles,),
        in_specs=_slot_specs(TT, tiles_per_batch) + [next_spec(r) for r in range(TOP_K)]
        + [pl.BlockSpec((TT, D), lambda i: (i, 0)),
           pl.BlockSpec((TT, TOP_K), lambda i: (i, 0)),
           pl.BlockSpec((1, N_MOD, D), lambda i: (i // tiles_per_batch, 0, 0)),
           pl.BlockSpec((1, D), lambda i: (0, 0)),
           pl.BlockSpec(memory_space=pl.ANY)],
        out_specs=pl.BlockSpec((TT, D), lambda i: (i, 0)),
        out_shape=jax.ShapeDtypeStruct((N, D), F32),
        scratch_shapes=[pltpu.VMEM((2, TOP_K, TT * PACK_ROWS, LANES), jnp.uint32),
                        pltpu.SemaphoreType.DMA((2,))],
        compiler_params=_params(("arbitrary",)),
        name="combine",
    )(*([pos_flat] * (2 * TOP_K)), x1f, probs, mod3, g.reshape(1, D), ys)


def _routing(idx, rank, counts):
    B, K, S = idx.shape
    n_exp = counts.shape[0]
    TM = EXPERT_BLOCK
    n_assign = B * S * K
    padded = ((counts + TM - 1) // TM) * TM
    pad_ends = jnp.cumsum(padded)
    pad_starts = pad_ends - padded
    experts = jnp.arange(n_exp, dtype=jnp.int32).reshape(n_exp, 1, 1, 1)
    base = jnp.sum(jnp.where(idx[None] == experts, pad_starts.reshape(n_exp, 1, 1, 1), 0),
                   axis=0)
    pos = (base + rank).astype(jnp.int32).reshape(n_assign)
    n_blocks = n_assign // TM + n_exp
    first_block = jnp.concatenate([pad_starts, pad_ends[-1:]]) // TM
    block_count = jnp.concatenate([padded // TM, n_blocks - pad_ends[-1:] // TM])
    tail_start = jnp.where(padded > 0, pad_ends - TM, -1)
    spare = pad_ends[-1] + jnp.arange(n_exp, dtype=jnp.int32) * TM
    fill_start = jnp.concatenate(
        [tail_start, jnp.where(spare < n_blocks * TM, spare, -1)]).astype(jnp.int32)
    return (pos, first_block.astype(jnp.int32), block_count.astype(jnp.int32),
            fill_start, n_blocks * TM)


def kernel(x, c, ada_w, ada_b, mix_norm_g, w_in, conv_w, conv_b, rg_w, rg_b, ig_w, ig_b,
           lru_lambda, attn_out_g, lru_out_g, w_out, ffn_norm_g, router_w, router_b,
           exp_w_gate, exp_b_gate, exp_w_up, exp_b_up, exp_w_down, exp_b_down, final_norm_g):
    B, S, D = x.shape
    depth = ada_w.shape[0]
    assert S % COMBINE_TILE == 0 and S % 512 == 0
    assert D == SUBLANES * LANES
    for l in range(depth):
        mod3 = _ada(c, ada_w[l], ada_b[l]).reshape(B, N_MOD, D)
        q, k, v, yl = _inproj(x, mod3, mix_norm_g[l], w_in[l].astype(BF16), conv_w[l],
                              conv_b[l], rg_w[l], rg_b[l], ig_w[l], ig_b[l], lru_lambda[l],
                              lru_out_g[l])
        ya = _attention(q, k, v)
        x1, h2, idx, prob, rank, cnt = _outproj(
            x, ya, yl, mod3, attn_out_g[l], w_out[l].astype(BF16), ffn_norm_g[l],
            router_w[l].T, router_b[l])
        pos, first_block, block_count, fill_start, cap = _routing(idx, rank, cnt[:, 0])
        xs = _dispatch(h2, pos * PACK_ROWS, fill_start * PACK_ROWS, cap, S)
        ys = _experts(xs, first_block, block_count, exp_w_gate[l], exp_b_gate[l],
                      exp_w_up[l], exp_b_up[l], exp_w_down[l], exp_b_down[l])
        probs = jnp.transpose(prob, (0, 2, 1)).reshape(B * S, TOP_K)
        assert depth == 1
        x = _combine(x1.reshape(B * S, D), probs, pos * PACK_ROWS, mod3, final_norm_g, ys,
                     S // COMBINE_TILE).reshape(B, S, D)
    return x
```

```python
import functools

import jax
import jax.numpy as jnp
from jax import lax
from jax.experimental import pallas as pl
from jax.experimental.pallas import tpu as pltpu

F32 = jnp.float32
BF16 = jnp.bfloat16
HIGHEST = lax.Precision.HIGHEST

EPS = 1e-6
N_MOD = 6
SB_HEADS = 8
HEAD_DIM = 64
SB_WIDTH = SB_HEADS * HEAD_DIM
LRU_BLOCKS = 8
CONV_WIDTH = 4
LRU_C = 8.0
TOP_K = 4
SWIGLU_LIMIT = 7.0
SWIGLU_ALPHA = 1.702
LOG2E = 1.4426950408889634

LANES = 128
SUBLANES = 8
VMEM_LIMIT = 56 * 1024 * 1024

Q_BLOCK = 128
K_BLOCK = 128
ATTN_WINDOW_BLOCKS = 6
ATTN_LOOP_BLOCKS = 2
ATTN_UNDERFLOW_LOG = -105.0
EXPERT_BLOCK = 512
DISPATCH_TILE = 4096
COMBINE_TILE = 256
ROW_DMA_UNROLL = 8
LRU_SCAN_ROWS = 256


def _params(sem):
    return pltpu.CompilerParams(dimension_semantics=sem, vmem_limit_bytes=VMEM_LIMIT)


def _ada_kernel(c_ref, w_ref, b_ref, o_ref):
    c = c_ref[...]
    ca = c * jax.nn.sigmoid(c)
    o_ref[...] = jnp.dot(ca, w_ref[...], precision=HIGHEST,
                         preferred_element_type=F32) + b_ref[...]


def _ada(c, ada_w, ada_b):
    B, D = c.shape
    E = ada_w.shape[1]
    tn = 1024
    return pl.pallas_call(
        _ada_kernel,
        grid=(E // tn,),
        in_specs=[pl.BlockSpec((B, D), lambda j: (0, 0)),
                  pl.BlockSpec((D, tn), lambda j: (0, j)),
                  pl.BlockSpec((1, tn), lambda j: (0, j))],
        out_specs=pl.BlockSpec((B, tn), lambda j: (0, j)),
        out_shape=jax.ShapeDtypeStruct((B, E), F32),
        compiler_params=_params(("arbitrary",)),
        name="ada",
    )(c, ada_w, ada_b.reshape(1, E))


def _rms(x, g):
    ms = jnp.mean(x * x, axis=-1, keepdims=True)
    return x * lax.rsqrt(ms + EPS) * g


def _softplus(x):
    return jnp.maximum(x, 0.0) + jnp.log1p(jnp.exp(-jnp.abs(x)))


def _gelu_tanh(x):
    return 0.5 * x * (1.0 + jnp.tanh(0.7978845608028654 * (x + 0.044715 * x * x * x)))


LRU_PAD = 8


def _lru_tile(x, gate, cw_ref, cb_ref, wr_ref, br_ref, wi_ref, bi_ref, lam_ref, g_ref,
              xext, hc):
    T = x.shape[0]
    xext[LRU_PAD:LRU_PAD + T, :] = x
    xc = cb_ref[...] + cw_ref[CONV_WIDTH - 1:CONV_WIDTH, :] * x
    for j in range(CONV_WIDTH - 1):
        back = CONV_WIDTH - 1 - j
        xc = xc + cw_ref[j:j + 1, :] * xext[LRU_PAD - back:LRU_PAD - back + T, :]
    xext[0:LRU_PAD, :] = xext[T:T + LRU_PAD, :]

    xb = xc.astype(BF16)
    r = jax.nn.sigmoid(jnp.dot(xb, wr_ref[...], preferred_element_type=F32) + br_ref[...])
    ig = jax.nn.sigmoid(jnp.dot(xb, wi_ref[...], preferred_element_type=F32) + bi_ref[...])
    log_a = (-LRU_C) * r * _softplus(-lam_ref[...])
    a = jnp.exp(log_a)
    b = jnp.sqrt(-jnp.tanh(log_a) * (a * a + 1.0)) * (ig * xc)

    G = T // SUBLANES
    a = a.reshape(G, SUBLANES, -1)
    b = b.reshape(G, SUBLANES, -1)
    rows = lax.broadcasted_iota(jnp.int32, a.shape, 1)
    d = 1
    while d < SUBLANES:
        keep = rows >= d
        a_prev = jnp.where(keep, pltpu.roll(a, d, 1), 1.0)
        b_prev = jnp.where(keep, pltpu.roll(b, d, 1), 0.0)
        b = a * b_prev + b
        a = a * a_prev
        d *= 2
    prev = hc[...]
    groups = []
    for g in range(G):
        hg = a[g] * prev + b[g]
        prev = hg[SUBLANES - 1:SUBLANES, :]
        groups.append(hg)
    hc[...] = prev
    h = jnp.concatenate(groups, axis=0)
    return _rms(h * _gelu_tanh(gate), g_ref[...])


def _inproj_kernel(x_ref, mod_ref, g_ref, w_ref, cw_ref, cb_ref, wr_ref, br_ref, wi_ref,
                   bi_ref, lam_ref, gl_ref, q_ref, k_ref, v_ref, yl_ref, xext, hc):
    @pl.when(pl.program_id(1) == 0)
    def _():
        xext[0:LRU_PAD, :] = jnp.zeros((LRU_PAD, xext.shape[1]), F32)
        hc[...] = jnp.zeros_like(hc)

    x = x_ref[0]
    h = _rms(x, g_ref[...]) * (1.0 + mod_ref[0, 1:2, :]) + mod_ref[0, 0:1, :]
    hb = h.astype(BF16)
    W = SB_WIDTH

    def proj(c):
        return jnp.dot(hb, w_ref[:, c * W:(c + 1) * W], preferred_element_type=F32)

    xr, gr = proj(3), proj(4)
    T = LRU_SCAN_ROWS
    for t in range(x.shape[0] // T):
        rows = slice(t * T, (t + 1) * T)
        yl_ref[0, rows, :] = _lru_tile(xr[rows], gr[rows], cw_ref, cb_ref, wr_ref, br_ref,
                                       wi_ref, bi_ref, lam_ref, gl_ref, xext, hc)
    q_ref[0] = (proj(0) * (HEAD_DIM ** -0.5)).astype(BF16)
    k_ref[0] = proj(1).astype(BF16)
    v_ref[0] = proj(2).astype(BF16)


def _block_diag(w):
    H, I, J = w.shape
    eye = jnp.eye(H, dtype=w.dtype)
    return (w[:, :, None, :] * eye[:, None, :, None]).reshape(H * I, H * J)


def _inproj(x, mod3, g, w_in_b, conv_w, conv_b, rg_w, rg_b, ig_w, ig_b, lam, g_lru, ts=1024):
    B, S, D = x.shape
    E = w_in_b.shape[1]
    W = SB_WIDTH
    row = pl.BlockSpec((1, ts, W), lambda b, s: (b, s, 0))
    vec = pl.BlockSpec((1, W), lambda b, s: (0, 0))
    mat = pl.BlockSpec((W, W), lambda b, s: (0, 0))
    return pl.pallas_call(
        _inproj_kernel,
        grid=(B, S // ts),
        in_specs=[pl.BlockSpec((1, ts, D), lambda b, s: (b, s, 0)),
                  pl.BlockSpec((1, N_MOD, D), lambda b, s: (b, 0, 0)),
                  pl.BlockSpec((1, D), lambda b, s: (0, 0)),
                  pl.BlockSpec((D, E), lambda b, s: (0, 0)),
                  pl.BlockSpec((CONV_WIDTH, W), lambda b, s: (0, 0)), vec,
                  mat, vec, mat, vec, vec, vec],
        out_specs=[row, row, row, row],
        out_shape=[jax.ShapeDtypeStruct((B, S, W), BF16)] * 3
        + [jax.ShapeDtypeStruct((B, S, W), F32)],
        scratch_shapes=[pltpu.VMEM((LRU_SCAN_ROWS + LRU_PAD, W), F32), pltpu.VMEM((1, W), F32)],
        compiler_params=_params(("arbitrary", "arbitrary")),
        name="inproj",
    )(x, mod3, g.reshape(1, D), w_in_b, conv_w, conv_b.reshape(1, W),
      _block_diag(rg_w).astype(BF16), rg_b.reshape(1, W), _block_diag(ig_w).astype(BF16),
      ig_b.reshape(1, W), lam.reshape(1, W), g_lru.reshape(1, W))


def _attn_kernel(q_ref, k_ref, v_ref, o_ref, acc_ref, carry_ref):
    i = pl.program_id(1)
    QB, KB = Q_BLOCK, K_BLOCK
    n_pairs = q_ref.shape[2] // LANES
    lane = lax.broadcasted_iota(jnp.int32, (QB, LANES), 1)

    def stacked_q(p):
        q = q_ref[0, :, p * LANES:(p + 1) * LANES]
        zero = jnp.zeros_like(q)
        return jnp.concatenate([jnp.where(lane < HEAD_DIM, q, zero),
                                jnp.where(lane >= HEAD_DIM, q, zero)], axis=0)

    qs = [stacked_q(p) for p in range(n_pairs)]

    uj = lax.broadcasted_iota(jnp.int32, (2 * KB, KB + LANES), 0) & (KB - 1)
    us = lax.broadcasted_iota(jnp.int32, (2 * KB, KB + LANES), 1)
    u2 = jnp.where((us >= KB) | (uj > us), -1.0, 0.0).astype(BF16)

    row = lax.broadcasted_iota(jnp.int32, (2 * QB, KB), 0) & (QB - 1)
    col = lax.broadcasted_iota(jnp.int32, (2 * QB, KB), 1)
    causal = col < row

    def tiles(p, j_hi, n, diagonal, acc, carry):
        start = pl.multiple_of((j_hi - (n - 1)) * KB, KB)
        kw = k_ref[0, pl.ds(start, n * KB), p * LANES:(p + 1) * LANES]
        vw = v_ref[0, pl.ds(start, n * KB), p * LANES:(p + 1) * LANES]
        z = lax.dot_general(qs[p], kw, (((1,), (1,)), ((), ())), preferred_element_type=F32)
        softplus = jnp.maximum(z, 0.0) + jnp.log(1.0 + jnp.exp2(jnp.abs(z) * (-LOG2E)))
        log_beta = z - softplus
        ws = [None] * n
        for c in reversed(range(n)):
            sp = softplus[:, c * KB:(c + 1) * KB]
            masked = diagonal and c == n - 1
            if masked:
                sp = jnp.where(causal, sp, 0.0)
            hi_f = lax.bitcast_convert_type(
                lax.bitcast_convert_type(sp, jnp.uint32) & jnp.uint32(0xFFFF0000), F32)
            hi_lo = jnp.concatenate([hi_f.astype(BF16), (sp - hi_f).astype(BF16)], axis=1)
            sums = jnp.dot(hi_lo, u2, preferred_element_type=F32)
            w = jnp.exp2((log_beta[:, c * KB:(c + 1) * KB] + sums[:, :KB] + carry) * LOG2E)
            if masked:
                w = jnp.where(causal, w, 0.0)
            ws[c] = w.astype(BF16)
            carry = carry + sums[:, KB:]
        w_all = ws[0] if n == 1 else jnp.concatenate(ws, axis=1)
        acc = acc + jnp.dot(w_all, vw, preferred_element_type=F32)
        return acc, carry

    def fold(j_hi, n, diagonal):
        cmax = None
        for p in range(n_pairs):
            if diagonal:
                acc = carry = jnp.zeros((2 * QB, LANES), F32)
            else:
                acc, carry = acc_ref[p], carry_ref[p]
            acc, carry = tiles(p, j_hi, n, diagonal, acc, carry)
            acc_ref[p] = acc
            carry_ref[p] = carry
            cmax = carry if cmax is None else jnp.maximum(cmax, carry)
        return jnp.max(cmax)

    n_first, n_loop = ATTN_WINDOW_BLOCKS, ATTN_LOOP_BLOCKS
    n0 = jnp.minimum(i + 1, n_first)
    cmax0 = lax.switch(n0 - 1, [functools.partial(fold, i, n, True)
                                for n in range(1, n_first + 1)])
    j0 = i - n0

    def loop(n, j, cmax):
        def more(st):
            return (st[0] >= n - 1) & (st[1] >= ATTN_UNDERFLOW_LOG)

        def step(st):
            return st[0] - n, fold(st[0], n, False)

        return lax.while_loop(more, step, (j, cmax))

    j, cmax = j0, cmax0
    while n_loop >= 1:
        j, cmax = loop(n_loop, j, cmax)
        n_loop //= 2
    for p in range(n_pairs):
        acc = acc_ref[p]
        o_ref[0, :, p * LANES:(p + 1) * LANES] = jnp.where(lane < HEAD_DIM, acc[:QB], acc[QB:])


def _attention(q, k, v):
    B, S, W = q.shape
    return pl.pallas_call(
        _attn_kernel,
        grid=(B, S // Q_BLOCK),
        in_specs=[pl.BlockSpec((1, Q_BLOCK, W), lambda b, i: (b, i, 0)),
                  pl.BlockSpec((1, S, W), lambda b, i: (b, 0, 0)),
                  pl.BlockSpec((1, S, W), lambda b, i: (b, 0, 0))],
        out_specs=pl.BlockSpec((1, Q_BLOCK, W), lambda b, i: (b, i, 0)),
        out_shape=jax.ShapeDtypeStruct((B, S, W), F32),
        scratch_shapes=[pltpu.VMEM((W // LANES, 2 * Q_BLOCK, LANES), F32),
                        pltpu.VMEM((W // LANES, 2 * Q_BLOCK, LANES), F32)],
        compiler_params=_params(("arbitrary", "arbitrary")),
        name="attn",
    )(q, k, v)


PACK_ROWS = SUBLANES // 2


def _store_packed_rows(ref, value):
    n_rows, d = value.shape
    bits = lax.bitcast_convert_type(value.astype(BF16).astype(F32), jnp.uint32)
    words = (bits[:, :d // 2] >> 16) | bits[:, d // 2:]
    for c in range(PACK_ROWS):
        ref[pl.ds(c, n_rows, stride=PACK_ROWS), :] = words[:, c * LANES:(c + 1) * LANES]


def _load_packed_rows(ref, n_rows, lead=(), dtype=BF16):
    lo, hi = [], []
    for c in range(PACK_ROWS):
        w = ref[lead + (pl.ds(c, n_rows, stride=PACK_ROWS), slice(None))]
        lo.append(lax.bitcast_convert_type(w << 16, F32).astype(dtype))
        hi.append(lax.bitcast_convert_type(w & jnp.uint32(0xFFFF0000), F32).astype(dtype))
    return jnp.concatenate(lo + hi, axis=1)


def _outproj_kernel(x_ref, ya_ref, yl_ref, mod_ref, ga_ref, wo_ref, gf_ref, rw_ref, rb_ref,
                    x1_ref, h2_ref, idx_ref, prob_ref, rank_ref, cnt_ref, tri, run):
    W = SB_WIDTH
    ya = _rms(ya_ref[0], ga_ref[...]).astype(BF16)
    yl = yl_ref[0].astype(BF16)
    mix = (jnp.dot(ya, wo_ref[0:W, :], preferred_element_type=F32)
           + jnp.dot(yl, wo_ref[W:2 * W, :], preferred_element_type=F32))
    x1 = x_ref[0] + mod_ref[0, 2:3, :] * mix
    x1_ref[0] = x1
    h2 = _rms(x1, gf_ref[...]) * (1.0 + mod_ref[0, 4:5, :]) + mod_ref[0, 3:4, :]
    _store_packed_rows(h2_ref, h2)

    logits = lax.dot_general(rw_ref[...], h2, (((1,), (1,)), ((), ())), precision=HIGHEST,
                             preferred_element_type=F32) + rb_ref[...]
    n_exp = logits.shape[0]
    eid = lax.broadcasted_iota(jnp.int32, logits.shape, 0)
    vals, idxs = [], []
    for _ in range(TOP_K):
        m = jnp.max(logits, axis=0, keepdims=True)
        sel = jnp.min(jnp.where(logits == m, eid, n_exp), axis=0, keepdims=True)
        vals.append(m)
        idxs.append(sel)
        logits = jnp.where(eid == sel, -jnp.inf, logits)
    es = [jnp.exp(vv - vals[0]) for vv in vals]
    inv = 1.0 / (es[0] + es[1] + es[2] + es[3])

    first = (pl.program_id(0) == 0) & (pl.program_id(1) == 0)

    @pl.when(first)
    def _():
        ti = lax.broadcasted_iota(jnp.int32, tri.shape, 0)
        tj = lax.broadcasted_iota(jnp.int32, tri.shape, 1)
        tri[...] = jnp.where(ti < tj, 1.0, 0.0).astype(BF16)
        run[...] = jnp.zeros_like(run)

    hits = [eid == idxs[r] for r in range(TOP_K)]
    chosen = hits[0] | hits[1] | hits[2] | hits[3]
    cnt = jnp.where(chosen, 1.0, 0.0)
    before = jnp.dot(cnt.astype(BF16), tri[...], preferred_element_type=F32) + run[...]
    for r in range(TOP_K):
        idx_ref[0, r:r + 1, :] = idxs[r]
        prob_ref[0, r:r + 1, :] = es[r] * inv
        rank_ref[0, r:r + 1, :] = jnp.sum(jnp.where(hits[r], before, 0.0), axis=0,
                                          keepdims=True).astype(jnp.int32)
    run[...] = run[...] + jnp.sum(cnt, axis=1, keepdims=True)
    cnt_ref[...] = jnp.broadcast_to(run[...], cnt_ref.shape).astype(jnp.int32)


def _outproj(x, ya, yl, mod3, ga, w_out_b, gf, router_wt, router_b, ts=1024):
    B, S, D = x.shape
    W = ya.shape[2]
    NE = router_wt.shape[0]
    rowd = pl.BlockSpec((1, ts, D), lambda b, s: (b, s, 0))
    roww = pl.BlockSpec((1, ts, W), lambda b, s: (b, s, 0))
    sel = pl.BlockSpec((1, TOP_K, ts), lambda b, s: (b, 0, s))
    return pl.pallas_call(
        _outproj_kernel,
        grid=(B, S // ts),
        in_specs=[rowd, roww, roww,
                  pl.BlockSpec((1, N_MOD, D), lambda b, s: (b, 0, 0)),
                  pl.BlockSpec((1, W), lambda b, s: (0, 0)),
                  pl.BlockSpec((2 * W, D), lambda b, s: (0, 0)),
                  pl.BlockSpec((1, D), lambda b, s: (0, 0)),
                  pl.BlockSpec((NE, D), lambda b, s: (0, 0)),
                  pl.BlockSpec((NE, 1), lambda b, s: (0, 0))],
        out_specs=[rowd,
                   pl.BlockSpec((ts * PACK_ROWS, LANES), lambda b, s: (b * (S // ts) + s, 0)),
                   sel, sel, sel, pl.BlockSpec((NE, LANES), lambda b, s: (0, 0))],
        out_shape=[jax.ShapeDtypeStruct((B, S, D), F32),
                   jax.ShapeDtypeStruct((B * S * PACK_ROWS, LANES), jnp.uint32),
                   jax.ShapeDtypeStruct((B, TOP_K, S), jnp.int32),
                   jax.ShapeDtypeStruct((B, TOP_K, S), F32),
                   jax.ShapeDtypeStruct((B, TOP_K, S), jnp.int32),
                   jax.ShapeDtypeStruct((NE, LANES), jnp.int32)],
        scratch_shapes=[pltpu.VMEM((ts, ts), BF16), pltpu.VMEM((NE, 1), F32)],
        compiler_params=_params(("arbitrary", "arbitrary")),
        name="outproj",
    )(x, ya, yl, mod3, ga.reshape(1, W), w_out_b, gf.reshape(1, D), router_wt,
      router_b.reshape(NE, 1))


def _dispatch_kernel(p0_ref, p1_ref, p2_ref, p3_ref, fill_ref, h_ref, xs_ref, zbuf, sem, zsem):
    TT = h_ref.shape[0] // PACK_ROWS
    n_fill = fill_ref.shape[0]

    @pl.when(pl.program_id(0) == 0)
    def _():
        zbuf[...] = jnp.zeros_like(zbuf)

        def fill_copy(e):
            start = pl.multiple_of(jnp.maximum(fill_ref[e], 0), PACK_ROWS)
            return pltpu.make_async_copy(zbuf, xs_ref.at[pl.ds(start, zbuf.shape[0]), :], zsem)

        def start(e, c):
            @pl.when(fill_ref[e] >= 0)
            def _():
                fill_copy(e).start()
            return c

        def wait(e, c):
            @pl.when(fill_ref[e] >= 0)
            def _():
                fill_copy(e).wait()
            return c

        lax.fori_loop(0, n_fill, start, 0)
        lax.fori_loop(0, n_fill, wait, 0)

    pos_refs = (p0_ref, p1_ref, p2_ref, p3_ref)

    def row_copy(t, r):
        src = pl.multiple_of(t * PACK_ROWS, PACK_ROWS)
        dst = pl.multiple_of(pos_refs[r][t], PACK_ROWS)
        return pltpu.make_async_copy(h_ref.at[pl.ds(src, PACK_ROWS), :],
                                     xs_ref.at[pl.ds(dst, PACK_ROWS), :], sem)

    def issue(g, c):
        for tt in range(ROW_DMA_UNROLL):
            for r in range(TOP_K):
                row_copy(g * ROW_DMA_UNROLL + tt, r).start(priority=(tt * TOP_K + r) % 2)
        return c

    lax.fori_loop(0, TT // ROW_DMA_UNROLL, issue, 0)
    for r in range(TOP_K):
        pltpu.make_async_copy(h_ref, xs_ref.at[pl.ds(0, TT * PACK_ROWS), :], sem).wait()


def _slot_specs(tile, tiles_per_batch):
    def spec(r):
        return pl.BlockSpec(
            (tile,),
            lambda i: ((i // tiles_per_batch * TOP_K + r) * tiles_per_batch
                       + i % tiles_per_batch,),
            memory_space=pltpu.SMEM)
    return [spec(r) for r in range(TOP_K)]


def _dispatch(h2t, pos_flat, fill_start, cap, seq_len):
    TT = min(DISPATCH_TILE, seq_len)
    assert seq_len % TT == 0
    tiles_per_batch = seq_len // TT
    n_tiles = h2t.shape[0] // (TT * PACK_ROWS)
    return pl.pallas_call(
        _dispatch_kernel,
        grid=(n_tiles,),
        in_specs=_slot_specs(TT, tiles_per_batch)
        + [pl.BlockSpec(memory_space=pltpu.SMEM),
           pl.BlockSpec((TT * PACK_ROWS, LANES), lambda i: (i, 0))],
        out_specs=pl.BlockSpec(memory_space=pl.ANY),
        out_shape=jax.ShapeDtypeStruct((cap * PACK_ROWS, LANES), jnp.uint32),
        scratch_shapes=[pltpu.VMEM((EXPERT_BLOCK * PACK_ROWS, LANES), jnp.uint32),
                        pltpu.SemaphoreType.DMA, pltpu.SemaphoreType.DMA],
        compiler_params=_params(("arbitrary",)),
        name="dispatch",
    )(pos_flat, pos_flat, pos_flat, pos_flat, fill_start, h2t)


def _expert_kernel(first_ref, count_ref, xs_ref, wg_ref, bg_ref, wu_ref, bu_ref, wd_ref,
                   bd_ref, ys_ref, wgb, wub, wdb, xbuf, ybuf, xsem, ysem):
    e = pl.program_id(0)
    n_exp = pl.num_programs(0)
    rows = ybuf.shape[1]
    first, count = first_ref[e], count_ref[e]

    wgb[...] = wg_ref[0].astype(BF16)
    wub[...] = wu_ref[0].astype(BF16)
    wdb[...] = wd_ref[0].astype(BF16)

    def block(ref, buf, j):
        n = buf.shape[1]
        return ref.at[pl.ds(pl.multiple_of((first + j) * n, n), n), :]

    def x_copy(j, slot):
        return pltpu.make_async_copy(block(xs_ref, xbuf, j), xbuf.at[slot], xsem.at[slot])

    def y_copy(j, slot):
        return pltpu.make_async_copy(ybuf.at[slot], block(ys_ref, ybuf, j), ysem.at[slot])

    @pl.when(count > 0)
    def _():
        x_copy(0, 0).start()

    def step(j, c):
        slot = j % 2
        x_copy(j, slot).wait()

        @pl.when(j + 1 < count)
        def _():
            x_copy(j + 1, 1 - slot).start()

        @pl.when(j >= 2)
        def _():
            y_copy(j - 2, slot).wait()

        xb = _load_packed_rows(xbuf, EXPERT_BLOCK, (slot,))
        g = jnp.minimum(jnp.dot(xb, wgb[...], preferred_element_type=F32) + bg_ref[0],
                        SWIGLU_LIMIT)
        u = jnp.clip(jnp.dot(xb, wub[...], preferred_element_type=F32) + bu_ref[0],
                     -SWIGLU_LIMIT, SWIGLU_LIMIT)
        act = (u + 1.0) * (g * jax.nn.sigmoid(SWIGLU_ALPHA * g))
        _store_packed_rows(ybuf.at[slot], jnp.dot(act.astype(BF16), wdb[...],
                                                  preferred_element_type=F32) + bd_ref[0])
        y_copy(j, slot).start()
        return c

    lax.fori_loop(0, count, step, 0)

    @pl.when(count >= 2)
    def _():
        y_copy(count - 2, count % 2).wait()

    @pl.when(count >= 1)
    def _():
        y_copy(count - 1, (count - 1) % 2).wait()

    @pl.when(e == n_exp - 1)
    def _():
        spare_first, spare_count = first_ref[n_exp], count_ref[n_exp]
        ybuf[0] = jnp.zeros(ybuf.shape[1:], ybuf.dtype)

        def spare_copy(j):
            dst = pl.ds(pl.multiple_of((spare_first + j) * rows, rows), rows)
            return pltpu.make_async_copy(ybuf.at[0], ys_ref.at[dst, :], ysem.at[0])

        def start(j, c):
            spare_copy(j).start()
            return c

        def wait(j, c):
            spare_copy(j).wait()
            return c

        lax.fori_loop(0, spare_count, start, 0)
        lax.fori_loop(0, spare_count, wait, 0)


def _experts(xs, first_block, block_count, wg, bg, wu, bu, wd, bd):
    NE, D, DE = wg.shape
    cap = xs.shape[0] // PACK_ROWS

    def wsel(e, first, count):
        return (e, 0, 0)

    grid_spec = pltpu.PrefetchScalarGridSpec(
        num_scalar_prefetch=2,
        grid=(NE,),
        in_specs=[pl.BlockSpec(memory_space=pl.ANY),
                  pl.BlockSpec((1, D, DE), wsel), pl.BlockSpec((1, 1, DE), wsel),
                  pl.BlockSpec((1, D, DE), wsel), pl.BlockSpec((1, 1, DE), wsel),
                  pl.BlockSpec((1, DE, D), wsel), pl.BlockSpec((1, 1, D), wsel)],
        out_specs=pl.BlockSpec(memory_space=pl.ANY),
        scratch_shapes=[pltpu.VMEM((D, DE), BF16), pltpu.VMEM((D, DE), BF16),
                        pltpu.VMEM((DE, D), BF16),
                        pltpu.VMEM((2, EXPERT_BLOCK * PACK_ROWS, LANES), jnp.uint32),
                        pltpu.VMEM((2, EXPERT_BLOCK * PACK_ROWS, LANES), jnp.uint32),
                        pltpu.SemaphoreType.DMA((2,)), pltpu.SemaphoreType.DMA((2,))],
    )
    return pl.pallas_call(
        _expert_kernel,
        grid_spec=grid_spec,
        out_shape=jax.ShapeDtypeStruct((cap * PACK_ROWS, LANES), jnp.uint32),
        compiler_params=_params(("arbitrary",)),
        name="experts",
    )(first_block, block_count, xs, wg, bg.reshape(NE, 1, DE), wu, bu.reshape(NE, 1, DE), wd,
      bd.reshape(NE, 1, D))


def _combine_kernel(c0_ref, c1_ref, c2_ref, c3_ref, n0_ref, n1_ref, n2_ref, n3_ref,
                    x1_ref, p_ref, mod_ref, g_ref, ys_ref, o_ref, buf, sem):
    TT = x1_ref.shape[0]
    i = pl.program_id(0)
    R = PACK_ROWS

    def gather(pos_refs, s):
        def issue(g, c):
            for tt in range(ROW_DMA_UNROLL):
                t = g * ROW_DMA_UNROLL + tt
                for r in range(TOP_K):
                    pltpu.make_async_copy(
                        ys_ref.at[pl.ds(pl.multiple_of(pos_refs[r][t], R), R), :],
                        buf.at[s, r, pl.ds(pl.multiple_of(t * R, R), R), :],
                        sem.at[s]).start(priority=(tt * TOP_K + r) % 2)
            return c
        lax.fori_loop(0, TT // ROW_DMA_UNROLL, issue, 0)

    @pl.when(i == 0)
    def _():
        gather((c0_ref, c1_ref, c2_ref, c3_ref), 0)

    def reduce_tile(s):
        @pl.when(i + 1 < pl.num_programs(0))
        def _():
            gather((n0_ref, n1_ref, n2_ref, n3_ref), 1 - s)

        for r in range(TOP_K):
            pltpu.make_async_copy(ys_ref.at[pl.ds(0, TT * R), :], buf.at[s, r],
                                  sem.at[s]).wait()
        p = p_ref[...]
        moe = p[:, 0:1] * _load_packed_rows(buf, TT, (s, 0), F32)
        for r in range(1, TOP_K):
            moe = moe + p[:, r:r + 1] * _load_packed_rows(buf, TT, (s, r), F32)
        x2 = x1_ref[...] + mod_ref[0, 5:6, :] * moe
        o_ref[...] = _rms(x2, g_ref[...])

    for s in range(2):
        pl.when(i % 2 == s)(functools.partial(reduce_tile, s))


def _combine(x1f, probs, pos_flat, mod3, g, ys, tiles_per_batch):
    N, D = x1f.shape
    TT = COMBINE_TILE
    n_tiles = N // TT

    def next_spec(r):
        def index(i):
            j = jnp.minimum(i + 1, n_tiles - 1)
            return ((j // tiles_per_batch * TOP_K + r) * tiles_per_batch + j % tiles_per_batch,)
        return pl.BlockSpec((TT,), index, memory_space=pltpu.SMEM)

    return pl.pallas_call(
        _combine_kernel,
        grid=(n_tiles,),
        in_specs=_slot_specs(TT, tiles_per_batch) + [next_spec(r) for r in range(TOP_K)]
        + [pl.BlockSpec((TT, D), lambda i: (i, 0)),
           pl.BlockSpec((TT, TOP_K), lambda i: (i, 0)),
           pl.BlockSpec((1, N_MOD, D), lambda i: (i // tiles_per_batch, 0, 0)),
           pl.BlockSpec((1, D), lambda i: (0, 0)),
           pl.BlockSpec(memory_space=pl.ANY)],
        out_specs=pl.BlockSpec((TT, D), lambda i: (i, 0)),
        out_shape=jax.ShapeDtypeStruct((N, D), F32),
        scratch_shapes=[pltpu.VMEM((2, TOP_K, TT * PACK_ROWS, LANES), jnp.uint32),
                        pltpu.SemaphoreType.DMA((2,))],
        compiler_params=_params(("arbitrary",)),
        name="combine",
    )(*([pos_flat] * (2 * TOP_K)), x1f, probs, mod3, g.reshape(1, D), ys)


def _routing(idx, rank, counts):
    B, K, S = idx.shape
    n_exp = counts.shape[0]
    TM = EXPERT_BLOCK
    n_assign = B * S * K
    padded = ((counts + TM - 1) // TM) * TM
    pad_ends = jnp.cumsum(padded)
    pad_starts = pad_ends - padded
    experts = jnp.arange(n_exp, dtype=jnp.int32).reshape(n_exp, 1, 1, 1)
    base = jnp.sum(jnp.where(idx[None] == experts, pad_starts.reshape(n_exp, 1, 1, 1), 0),
                   axis=0)
    pos = (base + rank).astype(jnp.int32).reshape(n_assign)
    n_blocks = n_assign // TM + n_exp
    first_block = jnp.concatenate([pad_starts, pad_ends[-1:]]) // TM
    block_count = jnp.concatenate([padded // TM, n_blocks - pad_ends[-1:] // TM])
    tail_start = jnp.where(padded > 0, pad_ends - TM, -1)
    spare = pad_ends[-1] + jnp.arange(n_exp, dtype=jnp.int32) * TM
    fill_start = jnp.concatenate(
        [tail_start, jnp.where(spare < n_blocks * TM, spare, -1)]).astype(jnp.int32)
    return (pos, first_block.astype(jnp.int32), block_count.astype(jnp.int32),
            fill_start, n_blocks * TM)


def kernel(x, c, ada_w, ada_b, mix_norm_g, w_in, conv_w, conv_b, rg_w, rg_b, ig_w, ig_b,
           lru_lambda, attn_out_g, lru_out_g, w_out, ffn_norm_g, router_w, router_b,
           exp_w_gate, exp_b_gate, exp_w_up, exp_b_up, exp_w_down, exp_b_down, final_norm_g):
    B, S, D = x.shape
    depth = ada_w.shape[0]
    assert S % COMBINE_TILE == 0 and S % 512 == 0
    assert D == SUBLANES * LANES
    for l in range(depth):
        mod3 = _ada(c, ada_w[l], ada_b[l]).reshape(B, N_MOD, D)
        q, k, v, yl = _inproj(x, mod3, mix_norm_g[l], w_in[l].astype(BF16), conv_w[l],
                              conv_b[l], rg_w[l], rg_b[l], ig_w[l], ig_b[l], lru_lambda[l],
                              lru_out_g[l])
        ya = _attention(q, k, v)
        x1, h2, idx, prob, rank, cnt = _outproj(
            x, ya, yl, mod3, attn_out_g[l], w_out[l].astype(BF16), ffn_norm_g[l],
            router_w[l].T, router_b[l])
        pos, first_block, block_count, fill_start, cap = _routing(idx, rank, cnt[:, 0])
        xs = _dispatch(h2, pos * PACK_ROWS, fill_start * PACK_ROWS, cap, S)
        ys = _experts(xs, first_block, block_count, exp_w_gate[l], exp_b_gate[l],
                      exp_w_up[l], exp_b_up[l], exp_w_down[l], exp_b_down[l])
        probs = jnp.transpose(prob, (0, 2, 1)).reshape(B * S, TOP_K)
        assert depth == 1
        x = _combine(x1.reshape(B * S, D), probs, pos * PACK_ROWS, mod3, final_norm_g, ys,
                     S // COMBINE_TILE).reshape(B, S, D)
    return x
```

```python
import functools

import jax
import jax.numpy as jnp
from jax import lax
from jax.experimental import pallas as pl
from jax.experimental.pallas import tpu as pltpu

F32 = jnp.float32
BF16 = jnp.bfloat16
HIGHEST = lax.Precision.HIGHEST

EPS = 1e-6
N_MOD = 6
SB_HEADS = 8
HEAD_DIM = 64
SB_WIDTH = SB_HEADS * HEAD_DIM
LRU_BLOCKS = 8
CONV_WIDTH = 4
LRU_C = 8.0
TOP_K = 4
SWIGLU_LIMIT = 7.0
SWIGLU_ALPHA = 1.702
LOG2E = 1.4426950408889634

LANES = 128
SUBLANES = 8
VMEM_LIMIT = 56 * 1024 * 1024

Q_BLOCK = 128
K_BLOCK = 128
ATTN_WINDOW_BLOCKS = 6
ATTN_LOOP_BLOCKS = 2
ATTN_UNDERFLOW_LOG = -105.0
EXPERT_BLOCK = 512
DISPATCH_TILE = 4096
COMBINE_TILE = 256
ROW_DMA_UNROLL = 8
LRU_SCAN_ROWS = 256


def _params(sem):
    return pltpu.CompilerParams(dimension_semantics=sem, vmem_limit_bytes=VMEM_LIMIT)


def _ada_kernel(c_ref, w_ref, b_ref, o_ref):
    c = c_ref[...]
    ca = c * jax.nn.sigmoid(c)
    o_ref[...] = jnp.dot(ca, w_ref[...], precision=HIGHEST,
                         preferred_element_type=F32) + b_ref[...]


def _ada(c, ada_w, ada_b):
    B, D = c.shape
    E = ada_w.shape[1]
    tn = 1024
    return pl.pallas_call(
        _ada_kernel,
        grid=(E // tn,),
        in_specs=[pl.BlockSpec((B, D), lambda j: (0, 0)),
                  pl.BlockSpec((D, tn), lambda j: (0, j)),
                  pl.BlockSpec((1, tn), lambda j: (0, j))],
        out_specs=pl.BlockSpec((B, tn), lambda j: (0, j)),
        out_shape=jax.ShapeDtypeStruct((B, E), F32),
        compiler_params=_params(("arbitrary",)),
        name="ada",
    )(c, ada_w, ada_b.reshape(1, E))


def _rms(x, g):
    ms = jnp.mean(x * x, axis=-1, keepdims=True)
    return x * lax.rsqrt(ms + EPS) * g


def _softplus(x):
    return jnp.maximum(x, 0.0) + jnp.log1p(jnp.exp(-jnp.abs(x)))


def _gelu_tanh(x):
    return 0.5 * x * (1.0 + jnp.tanh(0.7978845608028654 * (x + 0.044715 * x * x * x)))


LRU_PAD = 8


def _lru_tile(x, gate, cw_ref, cb_ref, wr_ref, br_ref, wi_ref, bi_ref, lam_ref, g_ref,
              xext, hc):
    T = x.shape[0]
    xext[LRU_PAD:LRU_PAD + T, :] = x
    xc = cb_ref[...] + cw_ref[CONV_WIDTH - 1:CONV_WIDTH, :] * x
    for j in range(CONV_WIDTH - 1):
        back = CONV_WIDTH - 1 - j
        xc = xc + cw_ref[j:j + 1, :] * xext[LRU_PAD - back:LRU_PAD - back + T, :]
    xext[0:LRU_PAD, :] = xext[T:T + LRU_PAD, :]

    xb = xc.astype(BF16)
    r = jax.nn.sigmoid(jnp.dot(xb, wr_ref[...], preferred_element_type=F32) + br_ref[...])
    ig = jax.nn.sigmoid(jnp.dot(xb, wi_ref[...], preferred_element_type=F32) + bi_ref[...])
    log_a = (-LRU_C) * r * _softplus(-lam_ref[...])
    a = jnp.exp(log_a)
    b = jnp.sqrt(-jnp.tanh(log_a) * (a * a + 1.0)) * (ig * xc)

    G = T // SUBLANES
    a = a.reshape(G, SUBLANES, -1)
    b = b.reshape(G, SUBLANES, -1)
    rows = lax.broadcasted_iota(jnp.int32, a.shape, 1)
    d = 1
    while d < SUBLANES:
        keep = rows >= d
        a_prev = jnp.where(keep, pltpu.roll(a, d, 1), 1.0)
        b_prev = jnp.where(keep, pltpu.roll(b, d, 1), 0.0)
        b = a * b_prev + b
        a = a * a_prev
        d *= 2
    prev = hc[...]
    groups = []
    for g in range(G):
        hg = a[g] * prev + b[g]
        prev = hg[SUBLANES - 1:SUBLANES, :]
        groups.append(hg)
    hc[...] = prev
    h = jnp.concatenate(groups, axis=0)
    return _rms(h * _gelu_tanh(gate), g_ref[...])


def _inproj_kernel(x_ref, mod_ref, g_ref, w_ref, cw_ref, cb_ref, wr_ref, br_ref, wi_ref,
                   bi_ref, lam_ref, gl_ref, q_ref, k_ref, v_ref, yl_ref, xext, hc):
    @pl.when(pl.program_id(1) == 0)
    def _():
        xext[0:LRU_PAD, :] = jnp.zeros((LRU_PAD, xext.shape[1]), F32)
        hc[...] = jnp.zeros_like(hc)

    x = x_ref[0]
    h = _rms(x, g_ref[...]) * (1.0 + mod_ref[0, 1:2, :]) + mod_ref[0, 0:1, :]
    hb = h.astype(BF16)
    W = SB_WIDTH

    def proj(c):
        return jnp.dot(hb, w_ref[:, c * W:(c + 1) * W], preferred_element_type=F32)

    xr, gr = proj(3), proj(4)
    T = LRU_SCAN_ROWS
    for t in range(x.shape[0] // T):
        rows = slice(t * T, (t + 1) * T)
        yl_ref[0, rows, :] = _lru_tile(xr[rows], gr[rows], cw_ref, cb_ref, wr_ref, br_ref,
                                       wi_ref, bi_ref, lam_ref, gl_ref, xext, hc)
    q_ref[0] = (proj(0) * (HEAD_DIM ** -0.5)).astype(BF16)
    k_ref[0] = proj(1).astype(BF16)
    v_ref[0] = proj(2).astype(BF16)


def _block_diag(w):
    H, I, J = w.shape
    eye = jnp.eye(H, dtype=w.dtype)
    return (w[:, :, None, :] * eye[:, None, :, None]).reshape(H * I, H * J)


def _inproj(x, mod3, g, w_in_b, conv_w, conv_b, rg_w, rg_b, ig_w, ig_b, lam, g_lru, ts=1024):
    B, S, D = x.shape
    E = w_in_b.shape[1]
    W = SB_WIDTH
    row = pl.BlockSpec((1, ts, W), lambda b, s: (b, s, 0))
    vec = pl.BlockSpec((1, W), lambda b, s: (0, 0))
    mat = pl.BlockSpec((W, W), lambda b, s: (0, 0))
    return pl.pallas_call(
        _inproj_kernel,
        grid=(B, S // ts),
        in_specs=[pl.BlockSpec((1, ts, D), lambda b, s: (b, s, 0)),
                  pl.BlockSpec((1, N_MOD, D), lambda b, s: (b, 0, 0)),
                  pl.BlockSpec((1, D), lambda b, s: (0, 0)),
                  pl.BlockSpec((D, E), lambda b, s: (0, 0)),
                  pl.BlockSpec((CONV_WIDTH, W), lambda b, s: (0, 0)), vec,
                  mat, vec, mat, vec, vec, vec],
        out_specs=[row, row, row, row],
        out_shape=[jax.ShapeDtypeStruct((B, S, W), BF16)] * 3
        + [jax.ShapeDtypeStruct((B, S, W), F32)],
        scratch_shapes=[pltpu.VMEM((LRU_SCAN_ROWS + LRU_PAD, W), F32), pltpu.VMEM((1, W), F32)],
        compiler_params=_params(("arbitrary", "arbitrary")),
        name="inproj",
    )(x, mod3, g.reshape(1, D), w_in_b, conv_w, conv_b.reshape(1, W),
      _block_diag(rg_w).astype(BF16), rg_b.reshape(1, W), _block_diag(ig_w).astype(BF16),
      ig_b.reshape(1, W), lam.reshape(1, W), g_lru.reshape(1, W))


def _attn_kernel(q_ref, k_ref, v_ref, o_ref, acc_ref, carry_ref):
    i = pl.program_id(1)
    QB, KB = Q_BLOCK, K_BLOCK
    n_pairs = q_ref.shape[2] // LANES
    lane = lax.broadcasted_iota(jnp.int32, (QB, LANES), 1)

    def stacked_q(p):
        q = q_ref[0, :, p * LANES:(p + 1) * LANES]
        zero = jnp.zeros_like(q)
        return jnp.concatenate([jnp.where(lane < HEAD_DIM, q, zero),
                                jnp.where(lane >= HEAD_DIM, q, zero)], axis=0)

    qs = [stacked_q(p) for p in range(n_pairs)]

    uj = lax.broadcasted_iota(jnp.int32, (2 * KB, KB + LANES), 0) & (KB - 1)
    us = lax.broadcasted_iota(jnp.int32, (2 * KB, KB + LANES), 1)
    u2 = jnp.where((us >= KB) | (uj > us), -1.0, 0.0).astype(BF16)

    row = lax.broadcasted_iota(jnp.int32, (2 * QB, KB), 0) & (QB - 1)
    col = lax.broadcasted_iota(jnp.int32, (2 * QB, KB), 1)
    causal = col < row

    def tiles(p, j_hi, n, diagonal, acc, carry):
        start = pl.multiple_of((j_hi - (n - 1)) * KB, KB)
        kw = k_ref[0, pl.ds(start, n * KB), p * LANES:(p + 1) * LANES]
        vw = v_ref[0, pl.ds(start, n * KB), p * LANES:(p + 1) * LANES]
        z = lax.dot_general(qs[p], kw, (((1,), (1,)), ((), ())), preferred_element_type=F32)
        softplus = jnp.maximum(z, 0.0) + jnp.log(1.0 + jnp.exp2(jnp.abs(z) * (-LOG2E)))
        log_beta = z - softplus
        ws = [None] * n
        for c in reversed(range(n)):
            sp = softplus[:, c * KB:(c + 1) * KB]
            masked = diagonal and c == n - 1
            if masked:
                sp = jnp.where(causal, sp, 0.0)
            hi_f = lax.bitcast_convert_type(
                lax.bitcast_convert_type(sp, jnp.uint32) & jnp.uint32(0xFFFF0000), F32)
            hi_lo = jnp.concatenate([hi_f.astype(BF16), (sp - hi_f).astype(BF16)], axis=1)
            sums = jnp.dot(hi_lo, u2, preferred_element_type=F32)
            w = jnp.exp2((log_beta[:, c * KB:(c + 1) * KB] + sums[:, :KB] + carry) * LOG2E)
            if masked:
                w = jnp.where(causal, w, 0.0)
            ws[c] = w.astype(BF16)
            carry = carry + sums[:, KB:]
        w_all = ws[0] if n == 1 else jnp.concatenate(ws, axis=1)
        acc = acc + jnp.dot(w_all, vw, preferred_element_type=F32)
        return acc, carry

    def fold(j_hi, n, diagonal):
        cmax = None
        for p in range(n_pairs):
            if diagonal:
                acc = carry = jnp.zeros((2 * QB, LANES), F32)
            else:
                acc, carry = acc_ref[p], carry_ref[p]
            acc, carry = tiles(p, j_hi, n, diagonal, acc, carry)
            acc_ref[p] = acc
            carry_ref[p] = carry
            cmax = carry if cmax is None else jnp.maximum(cmax, carry)
        return jnp.max(cmax)

    n_first, n_loop = ATTN_WINDOW_BLOCKS, ATTN_LOOP_BLOCKS
    n0 = jnp.minimum(i + 1, n_first)
    cmax0 = lax.switch(n0 - 1, [functools.partial(fold, i, n, True)
                                for n in range(1, n_first + 1)])
    j0 = i - n0

    def loop(n, j, cmax):
        def more(st):
            return (st[0] >= n - 1) & (st[1] >= ATTN_UNDERFLOW_LOG)

        def step(st):
            return st[0] - n, fold(st[0], n, False)

        return lax.while_loop(more, step, (j, cmax))

    j, cmax = j0, cmax0
    while n_loop >= 1:
        j, cmax = loop(n_loop, j, cmax)
        n_loop //= 2
    for p in range(n_pairs):
        acc = acc_ref[p]
        o_ref[0, :, p * LANES:(p + 1) * LANES] = jnp.where(lane < HEAD_DIM, acc[:QB], acc[QB:])


def _attention(q, k, v):
    B, S, W = q.shape
    return pl.pallas_call(
        _attn_kernel,
        grid=(B, S // Q_BLOCK),
        in_specs=[pl.BlockSpec((1, Q_BLOCK, W), lambda b, i: (b, i, 0)),
                  pl.BlockSpec((1, S, W), lambda b, i: (b, 0, 0)),
                  pl.BlockSpec((1, S, W), lambda b, i: (b, 0, 0))],
        out_specs=pl.BlockSpec((1, Q_BLOCK, W), lambda b, i: (b, i, 0)),
        out_shape=jax.ShapeDtypeStruct((B, S, W), F32),
        scratch_shapes=[pltpu.VMEM((W // LANES, 2 * Q_BLOCK, LANES), F32),
                        pltpu.VMEM((W // LANES, 2 * Q_BLOCK, LANES), F32)],
        compiler_params=_params(("arbitrary", "arbitrary")),
        name="attn",
    )(q, k, v)


PACK_ROWS = SUBLANES // 2


def _store_packed_rows(ref, value):
    n_rows, d = value.shape
    bits = lax.bitcast_convert_type(value.astype(BF16).astype(F32), jnp.uint32)
    words = (bits[:, :d // 2] >> 16) | bits[:, d // 2:]
    for c in range(PACK_ROWS):
        ref[pl.ds(c, n_rows, stride=PACK_ROWS), :] = words[:, c * LANES:(c + 1) * LANES]


def _load_packed_rows(ref, n_rows, lead=(), dtype=BF16):
    lo, hi = [], []
    for c in range(PACK_ROWS):
        w = ref[lead + (pl.ds(c, n_rows, stride=PACK_ROWS), slice(None))]
        lo.append(lax.bitcast_convert_type(w << 16, F32).astype(dtype))
        hi.append(lax.bitcast_convert_type(w & jnp.uint32(0xFFFF0000), F32).astype(dtype))
    return jnp.concatenate(lo + hi, axis=1)


def _outproj_kernel(x_ref, ya_ref, yl_ref, mod_ref, ga_ref, wo_ref, gf_ref, rw_ref, rb_ref,
                    x1_ref, h2_ref, idx_ref, prob_ref, rank_ref, cnt_ref, tri, run):
    W = SB_WIDTH
    ya = _rms(ya_ref[0], ga_ref[...]).astype(BF16)
    yl = yl_ref[0].astype(BF16)
    mix = (jnp.dot(ya, wo_ref[0:W, :], preferred_element_type=F32)
           + jnp.dot(yl, wo_ref[W:2 * W, :], preferred_element_type=F32))
    x1 = x_ref[0] + mod_ref[0, 2:3, :] * mix
    x1_ref[0] = x1
    h2 = _rms(x1, gf_ref[...]) * (1.0 + mod_ref[0, 4:5, :]) + mod_ref[0, 3:4, :]
    _store_packed_rows(h2_ref, h2)

    logits = lax.dot_general(rw_ref[...], h2, (((1,), (1,)), ((), ())), precision=HIGHEST,
                             preferred_element_type=F32) + rb_ref[...]
    n_exp = logits.shape[0]
    eid = lax.broadcasted_iota(jnp.int32, logits.shape, 0)
    vals, idxs = [], []
    for _ in range(TOP_K):
        m = jnp.max(logits, axis=0, keepdims=True)
        sel = jnp.min(jnp.where(logits == m, eid, n_exp), axis=0, keepdims=True)
        vals.append(m)
        idxs.append(sel)
        logits = jnp.where(eid == sel, -jnp.inf, logits)
    es = [jnp.exp(vv - vals[0]) for vv in vals]
    inv = 1.0 / (es[0] + es[1] + es[2] + es[3])

    first = (pl.program_id(0) == 0) & (pl.program_id(1) == 0)

    @pl.when(first)
    def _():
        ti = lax.broadcasted_iota(jnp.int32, tri.shape, 0)
        tj = lax.broadcasted_iota(jnp.int32, tri.shape, 1)
        tri[...] = jnp.where(ti < tj, 1.0, 0.0).astype(BF16)
        run[...] = jnp.zeros_like(run)

    hits = [eid == idxs[r] for r in range(TOP_K)]
    chosen = hits[0] | hits[1] | hits[2] | hits[3]
    cnt = jnp.where(chosen, 1.0, 0.0)
    before = jnp.dot(cnt.astype(BF16), tri[...], preferred_element_type=F32) + run[...]
    for r in range(TOP_K):
        idx_ref[0, r:r + 1, :] = idxs[r]
        prob_ref[0, r:r + 1, :] = es[r] * inv
        rank_ref[0, r:r + 1, :] = jnp.sum(jnp.where(hits[r], before, 0.0), axis=0,
                                          keepdims=True).astype(jnp.int32)
    run[...] = run[...] + jnp.sum(cnt, axis=1, keepdims=True)
    cnt_ref[...] = jnp.broadcast_to(run[...], cnt_ref.shape).astype(jnp.int32)


def _outproj(x, ya, yl, mod3, ga, w_out_b, gf, router_wt, router_b, ts=1024):
    B, S, D = x.shape
    W = ya.shape[2]
    NE = router_wt.shape[0]
    rowd = pl.BlockSpec((1, ts, D), lambda b, s: (b, s, 0))
    roww = pl.BlockSpec((1, ts, W), lambda b, s: (b, s, 0))
    sel = pl.BlockSpec((1, TOP_K, ts), lambda b, s: (b, 0, s))
    return pl.pallas_call(
        _outproj_kernel,
        grid=(B, S // ts),
        in_specs=[rowd, roww, roww,
                  pl.BlockSpec((1, N_MOD, D), lambda b, s: (b, 0, 0)),
                  pl.BlockSpec((1, W), lambda b, s: (0, 0)),
                  pl.BlockSpec((2 * W, D), lambda b, s: (0, 0)),
                  pl.BlockSpec((1, D), lambda b, s: (0, 0)),
                  pl.BlockSpec((NE, D), lambda b, s: (0, 0)),
                  pl.BlockSpec((NE, 1), lambda b, s: (0, 0))],
        out_specs=[rowd,
                   pl.BlockSpec((ts * PACK_ROWS, LANES), lambda b, s: (b * (S // ts) + s, 0)),
                   sel, sel, sel, pl.BlockSpec((NE, LANES), lambda b, s: (0, 0))],
        out_shape=[jax.ShapeDtypeStruct((B, S, D), F32),
                   jax.ShapeDtypeStruct((B * S * PACK_ROWS, LANES), jnp.uint32),
                   jax.ShapeDtypeStruct((B, TOP_K, S), jnp.int32),
                   jax.ShapeDtypeStruct((B, TOP_K, S), F32),
                   jax.ShapeDtypeStruct((B, TOP_K, S), jnp.int32),
                   jax.ShapeDtypeStruct((NE, LANES), jnp.int32)],
        scratch_shapes=[pltpu.VMEM((ts, ts), BF16), pltpu.VMEM((NE, 1), F32)],
        compiler_params=_params(("arbitrary", "arbitrary")),
        name="outproj",
    )(x, ya, yl, mod3, ga.reshape(1, W), w_out_b, gf.reshape(1, D), router_wt,
      router_b.reshape(NE, 1))


def _dispatch_kernel(p0_ref, p1_ref, p2_ref, p3_ref, fill_ref, h_ref, xs_ref, zbuf, sem, zsem):
    TT = h_ref.shape[0] // PACK_ROWS
    n_fill = fill_ref.shape[0]

    @pl.when(pl.program_id(0) == 0)
    def _():
        zbuf[...] = jnp.zeros_like(zbuf)

        def fill_copy(e):
            start = pl.multiple_of(jnp.maximum(fill_ref[e], 0), PACK_ROWS)
            return pltpu.make_async_copy(zbuf, xs_ref.at[pl.ds(start, zbuf.shape[0]), :], zsem)

        def start(e, c):
            @pl.when(fill_ref[e] >= 0)
            def _():
                fill_copy(e).start()
            return c

        def wait(e, c):
            @pl.when(fill_ref[e] >= 0)
            def _():
                fill_copy(e).wait()
            return c

        lax.fori_loop(0, n_fill, start, 0)
        lax.fori_loop(0, n_fill, wait, 0)

    pos_refs = (p0_ref, p1_ref, p2_ref, p3_ref)

    def row_copy(t, r):
        src = pl.multiple_of(t * PACK_ROWS, PACK_ROWS)
        dst = pl.multiple_of(pos_refs[r][t], PACK_ROWS)
        return pltpu.make_async_copy(h_ref.at[pl.ds(src, PACK_ROWS), :],
                                     xs_ref.at[pl.ds(dst, PACK_ROWS), :], sem)

    def issue(g, c):
        for tt in range(ROW_DMA_UNROLL):
            for r in range(TOP_K):
                row_copy(g * ROW_DMA_UNROLL + tt, r).start(priority=(tt * TOP_K + r) % 2)
        return c

    lax.fori_loop(0, TT // ROW_DMA_UNROLL, issue, 0)
    for r in range(TOP_K):
        pltpu.make_async_copy(h_ref, xs_ref.at[pl.ds(0, TT * PACK_ROWS), :], sem).wait()


def _slot_specs(tile, tiles_per_batch):
    def spec(r):
        return pl.BlockSpec(
            (tile,),
            lambda i: ((i // tiles_per_batch * TOP_K + r) * tiles_per_batch
                       + i % tiles_per_batch,),
            memory_space=pltpu.SMEM)
    return [spec(r) for r in range(TOP_K)]


def _dispatch(h2t, pos_flat, fill_start, cap, seq_len):
    TT = min(DISPATCH_TILE, seq_len)
    assert seq_len % TT == 0
    tiles_per_batch = seq_len // TT
    n_tiles = h2t.shape[0] // (TT * PACK_ROWS)
    return pl.pallas_call(
        _dispatch_kernel,
        grid=(n_tiles,),
        in_specs=_slot_specs(TT, tiles_per_batch)
        + [pl.BlockSpec(memory_space=pltpu.SMEM),
           pl.BlockSpec((TT * PACK_ROWS, LANES), lambda i: (i, 0))],
        out_specs=pl.BlockSpec(memory_space=pl.ANY),
        out_shape=jax.ShapeDtypeStruct((cap * PACK_ROWS, LANES), jnp.uint32),
        scratch_shapes=[pltpu.VMEM((EXPERT_BLOCK * PACK_ROWS, LANES), jnp.uint32),
                        pltpu.SemaphoreType.DMA, pltpu.SemaphoreType.DMA],
        compiler_params=_params(("arbitrary",)),
        name="dispatch",
    )(pos_flat, pos_flat, pos_flat, pos_flat, fill_start, h2t)


def _expert_kernel(first_ref, count_ref, xs_ref, wg_ref, bg_ref, wu_ref, bu_ref, wd_ref,
                   bd_ref, ys_ref, wgb, wub, wdb, xbuf, ybuf, xsem, ysem):
    e = pl.program_id(0)
    n_exp = pl.num_programs(0)
    rows = ybuf.shape[1]
    first, count = first_ref[e], count_ref[e]

    def block(ref, buf, j):
        n = buf.shape[1]
        return ref.at[pl.ds(pl.multiple_of((first + j) * n, n), n), :]

    def x_copy(j, slot):
        return pltpu.make_async_copy(block(xs_ref, xbuf, j), xbuf.at[slot], xsem.at[slot])

    def y_copy(j, slot):
        return pltpu.make_async_copy(ybuf.at[slot], block(ys_ref, ybuf, j), ysem.at[slot])

    @pl.when(count > 0)
    def _():
        x_copy(0, 0).start()

    wgb[...] = wg_ref[0].astype(BF16)
    wub[...] = wu_ref[0].astype(BF16)
    wdb[...] = wd_ref[0].astype(BF16)

    def step(j, c):
        slot = j % 2
        x_copy(j, slot).wait()

        @pl.when(j + 1 < count)
        def _():
            x_copy(j + 1, 1 - slot).start()

        @pl.when(j >= 2)
        def _():
            y_copy(j - 2, slot).wait()

        xb = _load_packed_rows(xbuf, EXPERT_BLOCK, (slot,))
        g = jnp.minimum(jnp.dot(xb, wgb[...], preferred_element_type=F32) + bg_ref[0],
                        SWIGLU_LIMIT)
        u = jnp.clip(jnp.dot(xb, wub[...], preferred_element_type=F32) + bu_ref[0],
                     -SWIGLU_LIMIT, SWIGLU_LIMIT)
        act = (u + 1.0) * (g * jax.nn.sigmoid(SWIGLU_ALPHA * g))
        _store_packed_rows(ybuf.at[slot], jnp.dot(act.astype(BF16), wdb[...],
                                                  preferred_element_type=F32) + bd_ref[0])
        y_copy(j, slot).start()
        return c

    lax.fori_loop(0, count, step, 0)

    @pl.when(count >= 2)
    def _():
        y_copy(count - 2, count % 2).wait()

    @pl.when(count >= 1)
    def _():
        y_copy(count - 1, (count - 1) % 2).wait()

    @pl.when(e == n_exp - 1)
    def _():
        spare_first, spare_count = first_ref[n_exp], count_ref[n_exp]
        ybuf[0] = jnp.zeros(ybuf.shape[1:], ybuf.dtype)

        def spare_copy(j):
            dst = pl.ds(pl.multiple_of((spare_first + j) * rows, rows), rows)
            return pltpu.make_async_copy(ybuf.at[0], ys_ref.at[dst, :], ysem.at[0])

        def start(j, c):
            spare_copy(j).start()
            return c

        def wait(j, c):
            spare_copy(j).wait()
            return c

        lax.fori_loop(0, spare_count, start, 0)
        lax.fori_loop(0, spare_count, wait, 0)


def _experts(xs, first_block, block_count, wg, bg, wu, bu, wd, bd):
    NE, D, DE = wg.shape
    cap = xs.shape[0] // PACK_ROWS

    def wsel(e, first, count):
        return (e, 0, 0)

    grid_spec = pltpu.PrefetchScalarGridSpec(
        num_scalar_prefetch=2,
        grid=(NE,),
        in_specs=[pl.BlockSpec(memory_space=pl.ANY),
                  pl.BlockSpec((1, D, DE), wsel), pl.BlockSpec((1, 1, DE), wsel),
                  pl.BlockSpec((1, D, DE), wsel), pl.BlockSpec((1, 1, DE), wsel),
                  pl.BlockSpec((1, DE, D), wsel), pl.BlockSpec((1, 1, D), wsel)],
        out_specs=pl.BlockSpec(memory_space=pl.ANY),
        scratch_shapes=[pltpu.VMEM((D, DE), BF16), pltpu.VMEM((D, DE), BF16),
                        pltpu.VMEM((DE, D), BF16),
                        pltpu.VMEM((2, EXPERT_BLOCK * PACK_ROWS, LANES), jnp.uint32),
                        pltpu.VMEM((2, EXPERT_BLOCK * PACK_ROWS, LANES), jnp.uint32),
                        pltpu.SemaphoreType.DMA((2,)), pltpu.SemaphoreType.DMA((2,))],
    )
    return pl.pallas_call(
        _expert_kernel,
        grid_spec=grid_spec,
        out_shape=jax.ShapeDtypeStruct((cap * PACK_ROWS, LANES), jnp.uint32),
        compiler_params=_params(("arbitrary",)),
        name="experts",
    )(first_block, block_count, xs, wg, bg.reshape(NE, 1, DE), wu, bu.reshape(NE, 1, DE), wd,
      bd.reshape(NE, 1, D))


def _combine_kernel(c0_ref, c1_ref, c2_ref, c3_ref, n0_ref, n1_ref, n2_ref, n3_ref,
                    x1_ref, p_ref, mod_ref, g_ref, ys_ref, o_ref, buf, sem):
    TT = x1_ref.shape[0]
    i = pl.program_id(0)
    R = PACK_ROWS

    def gather(pos_refs, s):
        def issue(g, c):
            for tt in range(ROW_DMA_UNROLL):
                t = g * ROW_DMA_UNROLL + tt
                for r in range(TOP_K):
                    pltpu.make_async_copy(
                        ys_ref.at[pl.ds(pl.multiple_of(pos_refs[r][t], R), R), :],
                        buf.at[s, r, pl.ds(pl.multiple_of(t * R, R), R), :],
                        sem.at[s]).start(priority=(tt * TOP_K + r) % 2)
            return c
        lax.fori_loop(0, TT // ROW_DMA_UNROLL, issue, 0)

    @pl.when(i == 0)
    def _():
        gather((c0_ref, c1_ref, c2_ref, c3_ref), 0)

    def reduce_tile(s):
        @pl.when(i + 1 < pl.num_programs(0))
        def _():
            gather((n0_ref, n1_ref, n2_ref, n3_ref), 1 - s)

        for r in range(TOP_K):
            pltpu.make_async_copy(ys_ref.at[pl.ds(0, TT * R), :], buf.at[s, r],
                                  sem.at[s]).wait()
        p = p_ref[...]
        moe = p[:, 0:1] * _load_packed_rows(buf, TT, (s, 0), F32)
        for r in range(1, TOP_K):
            moe = moe + p[:, r:r + 1] * _load_packed_rows(buf, TT, (s, r), F32)
        x2 = x1_ref[...] + mod_ref[0, 5:6, :] * moe
        o_ref[...] = _rms(x2, g_ref[...])

    for s in range(2):
        pl.when(i % 2 == s)(functools.partial(reduce_tile, s))


def _combine(x1f, probs, pos_flat, mod3, g, ys, tiles_per_batch):
    N, D = x1f.shape
    TT = COMBINE_TILE
    n_tiles = N // TT

    def next_spec(r):
        def index(i):
            j = jnp.minimum(i + 1, n_tiles - 1)
            return ((j // tiles_per_batch * TOP_K + r) * tiles_per_batch + j % tiles_per_batch,)
        return pl.BlockSpec((TT,), index, memory_space=pltpu.SMEM)

    return pl.pallas_call(
        _combine_kernel,
        grid=(n_tiles,),
        in_specs=_slot_specs(TT, tiles_per_batch) + [next_spec(r) for r in range(TOP_K)]
        + [pl.BlockSpec((TT, D), lambda i: (i, 0)),
           pl.BlockSpec((TT, TOP_K), lambda i: (i, 0)),
           pl.BlockSpec((1, N_MOD, D), lambda i: (i // tiles_per_batch, 0, 0)),
           pl.BlockSpec((1, D), lambda i: (0, 0)),
           pl.BlockSpec(memory_space=pl.ANY)],
        out_specs=pl.BlockSpec((TT, D), lambda i: (i, 0)),
        out_shape=jax.ShapeDtypeStruct((N, D), F32),
        scratch_shapes=[pltpu.VMEM((2, TOP_K, TT * PACK_ROWS, LANES), jnp.uint32),
                        pltpu.SemaphoreType.DMA((2,))],
        compiler_params=_params(("arbitrary",)),
        name="combine",
    )(*([pos_flat] * (2 * TOP_K)), x1f, probs, mod3, g.reshape(1, D), ys)


def _routing(idx, rank, counts):
    B, K, S = idx.shape
    n_exp = counts.shape[0]
    TM = EXPERT_BLOCK
    n_assign = B * S * K
    padded = ((counts + TM - 1) // TM) * TM
    pad_ends = jnp.cumsum(padded)
    pad_starts = pad_ends - padded
    experts = jnp.arange(n_exp, dtype=jnp.int32).reshape(n_exp, 1, 1, 1)
    base = jnp.sum(jnp.where(idx[None] == experts, pad_starts.reshape(n_exp, 1, 1, 1), 0),
                   axis=0)
    pos = (base + rank).astype(jnp.int32).reshape(n_assign)
    n_blocks = n_assign // TM + n_exp
    first_block = jnp.concatenate([pad_starts, pad_ends[-1:]]) // TM
    block_count = jnp.concatenate([padded // TM, n_blocks - pad_ends[-1:] // TM])
    tail_start = jnp.where(padded > 0, pad_ends - TM, -1)
    spare = pad_ends[-1] + jnp.arange(n_exp, dtype=jnp.int32) * TM
    fill_start = jnp.concatenate(
        [tail_start, jnp.where(spare < n_blocks * TM, spare, -1)]).astype(jnp.int32)
    return (pos, first_block.astype(jnp.int32), block_count.astype(jnp.int32),
            fill_start, n_blocks * TM)


def kernel(x, c, ada_w, ada_b, mix_norm_g, w_in, conv_w, conv_b, rg_w, rg_b, ig_w, ig_b,
           lru_lambda, attn_out_g, lru_out_g, w_out, ffn_norm_g, router_w, router_b,
           exp_w_gate, exp_b_gate, exp_w_up, exp_b_up, exp_w_down, exp_b_down, final_norm_g):
    B, S, D = x.shape
    depth = ada_w.shape[0]
    assert S % COMBINE_TILE == 0 and S % 512 == 0
    assert D == SUBLANES * LANES
    for l in range(depth):
        mod3 = _ada(c, ada_w[l], ada_b[l]).reshape(B, N_MOD, D)
        q, k, v, yl = _inproj(x, mod3, mix_norm_g[l], w_in[l].astype(BF16), conv_w[l],
                              conv_b[l], rg_w[l], rg_b[l], ig_w[l], ig_b[l], lru_lambda[l],
                              lru_out_g[l])
        ya = _attention(q, k, v)
        x1, h2, idx, prob, rank, cnt = _outproj(
            x, ya, yl, mod3, attn_out_g[l], w_out[l].astype(BF16), ffn_norm_g[l],
            router_w[l].T, router_b[l])
        pos, first_block, block_count, fill_start, cap = _routing(idx, rank, cnt[:, 0])
        xs = _dispatch(h2, pos * PACK_ROWS, fill_start * PACK_ROWS, cap, S)
        ys = _experts(xs, first_block, block_count, exp_w_gate[l], exp_b_gate[l],
                      exp_w_up[l], exp_b_up[l], exp_w_down[l], exp_b_down[l])
        probs = jnp.transpose(prob, (0, 2, 1)).reshape(B * S, TOP_K)
        assert depth == 1
        x = _combine(x1.reshape(B * S, D), probs, pos * PACK_ROWS, mod3, final_norm_g, ys,
                     S // COMBINE_TILE).reshape(B, S, D)
    return x
```
